```python
import math
import jax
import jax.numpy as jnp
from jax import lax
import numpy as np

D_MODEL = 1024
BATCH = 16
SEQ = 2048
DEPTH = 2

GRID_W = 64
CTX_LEN = 256
N_MOD = 6
EPS = 1e-6
ROPE_BASE = 10000.0
ATTN_BLOCK = 128

SWA_HEADS = 4
SWA_KV_HEADS = 2
SWA_HEAD_DIM = 64
SWA_WINDOW = 128

GDN_HEADS = 4
GDN_HEAD_DIM = 128
GDN_CONV = 5
GDN_CHUNK = 64

MLA_HEADS = 4
MLA_Q_RANK = 256
MLA_KV_RANK = 128
MLA_NOPE = 64
MLA_ROPE = 32
MLA_V = 64
MLA_QK = MLA_NOPE + MLA_ROPE

N_EXPERTS = 32
TOP_K = 4
D_EXPERT = 1024
SWIGLU_LIMIT = 7.0
SWIGLU_ALPHA = 1.702
MOE_BLOCK = 256

SWA_Q = SWA_HEADS * SWA_HEAD_DIM
SWA_KV = SWA_KV_HEADS * SWA_HEAD_DIM
GDN_W = GDN_HEADS * GDN_HEAD_DIM
MLA_O = MLA_HEADS * MLA_V
D_MIX = SWA_Q + GDN_W + MLA_O
IN_SPLITS = (SWA_Q, SWA_KV, SWA_KV, GDN_W, GDN_W, GDN_W, GDN_W, 2 * GDN_HEADS, 2 * GDN_HEADS, MLA_Q_RANK, MLA_KV_RANK, MLA_ROPE)
N_IN = sum(IN_SPLITS)

kernel_name = 'hybrid_swa_gdn_mla_moe_dit'

F32 = jnp.float32


def rms_norm(x, gain):
    xf = x.astype(F32)
    y = xf * lax.rsqrt(jnp.mean(xf * xf, axis=-1, keepdims=True) + EPS)
    return (y * gain.astype(F32)).astype(x.dtype)


def l2_norm(x):
    xf = x.astype(F32)
    return xf * lax.rsqrt(jnp.sum(xf * xf, axis=-1, keepdims=True) + EPS)


def modulate(x, gain, shift, scale):
    return rms_norm(x, gain) * (1 + scale) + shift


def axial_rope_tables(n_lat, rot_dim):
    rows = n_lat // GRID_W
    t = jnp.arange(rows * GRID_W)
    row = (t // GRID_W).astype(F32)
    col = (t % GRID_W).astype(F32)
    n_freq = rot_dim // 4
    freq = ROPE_BASE ** (-jnp.arange(n_freq, dtype=F32) / n_freq)
    ang = jnp.stack([row[:, None] * freq, col[:, None] * freq], axis=1)
    return jnp.cos(ang), jnp.sin(ang)


def apply_axial_rope(x, cos, sin):
    b, n, h, r = x.shape
    xr = x.astype(F32).reshape(b, n, h, 2, 2, r // 4)
    x1, x2 = xr[..., 0, :], xr[..., 1, :]
    cs, sn = cos[None, :, None], sin[None, :, None]
    out = jnp.stack([x1 * cs - x2 * sn, x2 * cs + x1 * sn], axis=-2)
    return out.reshape(b, n, h, r).astype(x.dtype)


def swa_latent(q, k, v, kc, vc, sink):
    b, s, hq, dh = q.shape
    hkv = k.shape[2]
    grp = hq // hkv
    nb = s // ATTN_BLOCK
    nw = 3 * ATTN_BLOCK
    scale = dh ** -0.5
    qb = q.reshape(b, nb, ATTN_BLOCK, hkv, grp, dh)
    pad = ((0, 0), (ATTN_BLOCK, ATTN_BLOCK), (0, 0), (0, 0))
    kp = jnp.pad(k, pad).reshape(b, nb + 2, ATTN_BLOCK, hkv, dh)
    vp = jnp.pad(v, pad).reshape(b, nb + 2, ATTN_BLOCK, hkv, dh)
    kw = jnp.concatenate([kp[:, :-2], kp[:, 1:-1], kp[:, 2:]], axis=2)
    vw = jnp.concatenate([vp[:, :-2], vp[:, 1:-1], vp[:, 2:]], axis=2)
    q_off = jnp.arange(ATTN_BLOCK)
    k_off = jnp.arange(nw) - ATTN_BLOCK
    key_pos = jnp.arange(nb)[:, None] * ATTN_BLOCK + k_off[None, :]
    in_band = jnp.abs(k_off[None, :] - q_off[:, None]) <= SWA_WINDOW
    in_seq = (key_pos >= 0) & (key_pos < s)
    valid = in_band[None] & in_seq[:, None, :]
    s_win = jnp.einsum('bnqhgd,bnkhd->bnhgqk', qb, kw, preferred_element_type=F32) * scale
    s_win = jnp.where(valid[None, :, None, None], s_win, -jnp.inf)
    s_ctx = jnp.einsum('bnqhgd,bchd->bnhgqc', qb, kc, preferred_element_type=F32) * scale
    s_sink = jnp.broadcast_to(sink.astype(F32).reshape(hkv, grp)[:, :, None, None], s_win.shape[:-1] + (1,))
    p = jax.nn.softmax(jnp.concatenate([s_win, s_ctx, s_sink], axis=-1), axis=-1).astype(v.dtype)
    o = (jnp.einsum('bnhgqk,bnkhd->bnqhgd', p[..., :nw], vw)
         + jnp.einsum('bnhgqc,bchd->bnqhgd', p[..., nw:-1], vc))
    return o.reshape(b, s, hq * dh)


def context_attention(q, k, v, sink):
    b, n, hq, dq = q.shape
    hkv = k.shape[2]
    grp = hq // hkv
    qg = q.reshape(b, n, hkv, grp, dq)
    sc = jnp.einsum('bqhgd,bkhd->bhgqk', qg, k, preferred_element_type=F32) * dq ** -0.5
    if sink is not None:
        s_sink = jnp.broadcast_to(sink.astype(F32).reshape(hkv, grp)[:, :, None, None], sc.shape[:-1] + (1,))
        sc = jnp.concatenate([sc, s_sink], axis=-1)
    p = jax.nn.softmax(sc, axis=-1).astype(v.dtype)[..., :n]
    o = jnp.einsum('bhgqk,bkhd->bqhgd', p, v)
    return o.reshape(b, n, hq * v.shape[-1])


def dense_block_attention(q, k, v, kc, vc):
    b, s, h, dqk = q.shape
    dv = v.shape[-1]
    nb = s // ATTN_BLOCK
    scale = dqk ** -0.5
    qb = jnp.moveaxis(q.reshape(b, nb, ATTN_BLOCK, h, dqk), 1, 0)

    def attend(qi):
        sl = jnp.einsum('bqhd,bkhd->bhqk', qi, k, preferred_element_type=F32) * scale
        sc = jnp.einsum('bqhd,bchd->bhqc', qi, kc, preferred_element_type=F32) * scale
        p = jax.nn.softmax(jnp.concatenate([sl, sc], axis=-1), axis=-1).astype(v.dtype)
        return (jnp.einsum('bhqk,bkhd->bqhd', p[..., :s], v)
                + jnp.einsum('bhqc,bchd->bqhd', p[..., s:], vc))

    o = lax.map(attend, qb)
    return jnp.moveaxis(o, 0, 1).reshape(b, s, h * dv)


def short_conv(x, w):
    y = lax.conv_general_dilated(
        x, w[:, None, :].astype(x.dtype), window_strides=(1,),
        padding=[(GDN_CONV // 2, GDN_CONV // 2)],
        dimension_numbers=('NWC', 'WIO', 'NWC'), feature_group_count=x.shape[-1])
    return jax.nn.silu(y)


def gdn_chunk_scan(q, k, v, g, beta, state):
    b, n, h, _ = q.shape
    nc = n // GDN_CHUNK

    def chunks(t):
        t = t.reshape((b, nc, GDN_CHUNK, h) + t.shape[3:])
        return jnp.moveaxis(t, 3, 1)

    qc, kc, vc = chunks(q), chunks(k), chunks(v)
    gc = jnp.cumsum(chunks(g), axis=-1)
    bc = chunks(beta)
    idx = jnp.arange(GDN_CHUNK)
    incl = idx[:, None] >= idx[None, :]
    strict = idx[:, None] > idx[None, :]
    decay = jnp.exp(jnp.where(incl, gc[..., :, None] - gc[..., None, :], -jnp.inf))
    kk = jnp.einsum('bhctd,bhcid->bhcti', kc, kc)
    a_mat = jnp.where(strict, bc[..., :, None] * kk * decay, 0.0) + jnp.eye(GDN_CHUNK, dtype=F32)
    u = lax.linalg.triangular_solve(a_mat, vc * bc[..., None], left_side=True, lower=True, unit_diagonal=True)
    w = lax.linalg.triangular_solve(a_mat, kc * (bc * jnp.exp(gc))[..., None], left_side=True, lower=True, unit_diagonal=True)
    qk = jnp.einsum('bhctd,bhcid->bhcti', qc, kc) * decay
    q_dec = qc * jnp.exp(gc)[..., None]
    k_dec = kc * jnp.exp(gc[..., -1:] - gc)[..., None]
    g_end = jnp.exp(gc[..., -1])

    def step(s_prev, xs):
        u_c, w_c, qk_c, qd_c, kd_c, ge_c = xs
        v_new = u_c - jnp.einsum('bhtk,bhkv->bhtv', w_c, s_prev)
        o_c = jnp.einsum('bhtk,bhkv->bhtv', qd_c, s_prev) + jnp.einsum('bhti,bhiv->bhtv', qk_c, v_new)
        s_next = ge_c[..., None, None] * s_prev + jnp.einsum('bhtk,bhtv->bhkv', kd_c, v_new)
        return s_next, o_c

    xs = tuple(jnp.moveaxis(t, 2, 0) for t in (u, w, qk, q_dec, k_dec, g_end))
    state, o = lax.scan(step, state, xs)
    o = jnp.moveaxis(jnp.moveaxis(o, 0, 2), 1, 3)
    return o.reshape(b, n, h, o.shape[-1]), state


def orient(stream, d):
    q, k, v, g, beta = stream
    seq = (q, k, v, g[:, :, d], beta[:, :, d])
    if d == 1:
        return tuple(jnp.flip(t, axis=1) for t in seq)
    return seq


def bidirectional_gdn(lat, ctx):
    b, _, h, dk = lat[0].shape
    dv = lat[2].shape[-1]
    o_lat, o_ctx = [], []
    for d in range(2):
        state0 = jnp.zeros((b, h, dk, dv), F32)
        oc, s_ctx = gdn_chunk_scan(*orient(ctx, d), state0)
        ol, _ = gdn_chunk_scan(*orient(lat, d), s_ctx)
        if d == 1:
            oc, ol = jnp.flip(oc, axis=1), jnp.flip(ol, axis=1)
        o_lat.append(ol)
        o_ctx.append(oc)
    return o_lat[0] + o_lat[1], o_ctx[0] + o_ctx[1]


def token_mixers(p, pc, rope_a, rope_c, swa_q_norm, swa_k_norm, swa_sink, gdn_conv, gdn_a_log, gdn_dt_bias,
                 gdn_out_norm, mla_q_a_norm, mla_w_uq, mla_kv_a_norm, mla_w_ukv, mla_q_norm, mla_k_norm, with_ctx):
    b = p.shape[0]
    offsets = np.cumsum(IN_SPLITS)[:-1].tolist()
    aq, ak, av, gq, gk, gv, gz, ga, gb, cq, ckv, ckr = jnp.split(p, offsets, axis=-1)
    aq_c, ak_c, av_c, gq_c, gk_c, gv_c, gz_c, ga_c, gb_c, cq_c, ckv_c, ckr_c = jnp.split(pc, offsets, axis=-1)

    def swa_proj(q, k, v, rope):
        n = q.shape[1]
        q = rms_norm(q.reshape(b, n, SWA_HEADS, SWA_HEAD_DIM), swa_q_norm)
        k = rms_norm(k.reshape(b, n, SWA_KV_HEADS, SWA_HEAD_DIM), swa_k_norm)
        if rope is not None:
            q, k = apply_axial_rope(q, *rope), apply_axial_rope(k, *rope)
        return q, k, v.reshape(b, n, SWA_KV_HEADS, SWA_HEAD_DIM)

    qa, ka, va = swa_proj(aq, ak, av, rope_a)
    qa_c, ka_c, va_c = swa_proj(aq_c, ak_c, av_c, None)
    o_a = swa_latent(qa, ka, va, ka_c, va_c, swa_sink)

    def gdn_proj(q, k, v, a, bg):
        n = q.shape[1]
        q, k, v = jnp.split(short_conv(jnp.concatenate([q, k, v], axis=-1), gdn_conv), 3, axis=-1)
        hs = (b, n, GDN_HEADS, GDN_HEAD_DIM)
        q = l2_norm(q.reshape(hs)) * GDN_HEAD_DIM ** -0.5
        k = l2_norm(k.reshape(hs))
        v = v.reshape(hs).astype(F32)
        a = a.reshape(b, n, 2, GDN_HEADS).astype(F32)
        log_decay = -jnp.exp(gdn_a_log.astype(F32)) * jax.nn.softplus(a + gdn_dt_bias.astype(F32))
        beta = jax.nn.sigmoid(bg.reshape(b, n, 2, GDN_HEADS).astype(F32))
        return q, k, v, log_decay, beta

    def gdn_out(o, z):
        n = o.shape[1]
        gate = jax.nn.silu(z.reshape(b, n, GDN_HEADS, GDN_HEAD_DIM).astype(F32))
        return (rms_norm(o, gdn_out_norm) * gate).reshape(b, n, GDN_W).astype(p.dtype)

    ob, ob_c = bidirectional_gdn(gdn_proj(gq, gk, gv, ga, gb), gdn_proj(gq_c, gk_c, gv_c, ga_c, gb_c))
    o_b = gdn_out(ob, gz)

    def mla_proj(c_q, c_kv, k_rope, rope):
        n = c_q.shape[1]
        q = (rms_norm(c_q, mla_q_a_norm) @ mla_w_uq).reshape(b, n, MLA_HEADS, MLA_QK)
        kv = (rms_norm(c_kv, mla_kv_a_norm) @ mla_w_ukv).reshape(b, n, MLA_HEADS, MLA_NOPE + MLA_V)
        k_pe = jnp.broadcast_to(k_rope[:, :, None, :], (b, n, MLA_HEADS, MLA_ROPE))
        k = jnp.concatenate([kv[..., :MLA_NOPE], k_pe], axis=-1)
        q, k = rms_norm(q, mla_q_norm), rms_norm(k, mla_k_norm)
        if rope is not None:
            q = jnp.concatenate([q[..., :MLA_NOPE], apply_axial_rope(q[..., MLA_NOPE:], *rope)], axis=-1)
            k = jnp.concatenate([k[..., :MLA_NOPE], apply_axial_rope(k[..., MLA_NOPE:], *rope)], axis=-1)
        return q, k, kv[..., MLA_NOPE:]

    qm, km, vm = mla_proj(cq, ckv, ckr, rope_c)
    qm_c, km_c, vm_c = mla_proj(cq_c, ckv_c, ckr_c, None)
    o_c = dense_block_attention(qm, km, vm, km_c, vm_c)

    mix = jnp.concatenate([o_a, o_b, o_c], axis=-1)
    if not with_ctx:
        return mix, None
    mix_c = jnp.concatenate([context_attention(qa_c, ka_c, va_c, swa_sink), gdn_out(ob_c, gz_c),
                             context_attention(qm_c, km_c, vm_c, None)], axis=-1)
    return mix, mix_c


def clamped_swiglu(gu):
    x_glu = jnp.minimum(gu[..., ::2], SWIGLU_LIMIT)
    x_lin = jnp.clip(gu[..., 1::2], -SWIGLU_LIMIT, SWIGLU_LIMIT)
    return x_glu * jax.nn.sigmoid(SWIGLU_ALPHA * x_glu) * (x_lin + 1)


def moe_ffn(h, router_w, router_b, w_gu, b_gu, w_dn, b_dn):
    n, d = h.shape
    logits = jnp.dot(h, router_w, preferred_element_type=F32) + router_b.astype(F32)
    top_logit, top_e = lax.top_k(logits, TOP_K)
    gate = jax.nn.softmax(top_logit, axis=-1)
    nk = n * TOP_K
    e_flat = top_e.reshape(nk)
    tok_flat = jnp.arange(nk, dtype=jnp.int32) // TOP_K
    order = jnp.argsort(e_flat)
    e_sorted = e_flat[order]
    counts = jnp.bincount(e_flat, length=N_EXPERTS)
    padded = (counts + MOE_BLOCK - 1) // MOE_BLOCK * MOE_BLOCK
    pad_end = jnp.cumsum(padded)
    pad_start = pad_end - padded
    raw_start = jnp.cumsum(counts) - counts
    slot = pad_start[e_sorted] + jnp.arange(nk) - raw_start[e_sorted]
    n_blocks = -(-nk // MOE_BLOCK) + N_EXPERTS
    cap = n_blocks * MOE_BLOCK
    slot_tok = jnp.full((cap,), n, jnp.int32).at[slot].set(tok_flat[order])
    slot_gate = jnp.zeros((cap,), F32).at[slot].set(gate.reshape(nk)[order])
    block_e = jnp.minimum(jnp.searchsorted(pad_end, jnp.arange(n_blocks) * MOE_BLOCK, side='right'), N_EXPERTS - 1)
    h_pad = jnp.concatenate([h, jnp.zeros((1, d), h.dtype)], axis=0)

    def expert_block(args):
        toks, e = args
        xb = h_pad[toks]
        return clamped_swiglu(xb @ w_gu[e] + b_gu[e]) @ w_dn[e] + b_dn[e]

    y = lax.map(expert_block, (slot_tok.reshape(n_blocks, MOE_BLOCK), block_e))
    y = y.reshape(cap, d) * slot_gate[:, None].astype(y.dtype)
    return jnp.zeros((n + 1, d), y.dtype).at[slot_tok].add(y)[:n]


def setup_inputs(seed: int = 0) -> dict:
    key = jax.random.key(seed)
    ks = iter(jax.random.split(key, 32))
    L, d = DEPTH, D_MODEL

    def nrm(shape, scale):
        return jax.random.normal(next(ks), shape, F32) * scale

    def gain(shape):
        return 1.0 + nrm(shape, 0.02)

    a_log = jnp.log(jax.random.uniform(next(ks), (L, 2, GDN_HEADS), F32, 1.0, 16.0))
    dt = jnp.exp(jax.random.uniform(next(ks), (L, 2, GDN_HEADS), F32, math.log(1e-3), math.log(1e-1)))
    dt_bias = dt + jnp.log(-jnp.expm1(-dt))
    return {
        'x': nrm((BATCH, SEQ, d), 1.0),
        'c': nrm((BATCH, d), 1.0),
        'ctx': nrm((BATCH, CTX_LEN, d), 1.0),
        'c_ctx': nrm((d,), 1.0),
        'w_mod': nrm((L, d, N_MOD * d), 0.5 * d ** -0.5),
        'b_mod': nrm((L, N_MOD * d), 0.02),
        'norm1': gain((L, d)),
        'norm2': gain((L, d)),
        'w_in': nrm((L, d, N_IN), d ** -0.5),
        'w_out': nrm((L, D_MIX, d), D_MIX ** -0.5),
        'swa_q_norm': gain((L, SWA_HEAD_DIM)),
        'swa_k_norm': gain((L, SWA_HEAD_DIM)),
        'swa_sink': nrm((L, SWA_HEADS), 0.5),
        'gdn_conv': nrm((L, GDN_CONV, 3 * GDN_W), GDN_CONV ** -0.5),
        'gdn_a_log': a_log,
        'gdn_dt_bias': dt_bias,
        'gdn_out_norm': gain((L, GDN_HEAD_DIM)),
        'mla_q_a_norm': gain((L, MLA_Q_RANK)),
        'mla_w_uq': nrm((L, MLA_Q_RANK, MLA_HEADS * MLA_QK), MLA_Q_RANK ** -0.5),
        'mla_kv_a_norm': gain((L, MLA_KV_RANK)),
        'mla_w_ukv': nrm((L, MLA_KV_RANK, MLA_HEADS * (MLA_NOPE + MLA_V)), MLA_KV_RANK ** -0.5),
        'mla_q_norm': gain((L, MLA_QK)),
        'mla_k_norm': gain((L, MLA_QK)),
        'router_w': nrm((L, d, N_EXPERTS), d ** -0.5),
        'router_b': nrm((L, N_EXPERTS), 0.01),
        'exp_w_gu': nrm((L, N_EXPERTS, d, 2 * D_EXPERT), d ** -0.5),
        'exp_b_gu': nrm((L, N_EXPERTS, 2 * D_EXPERT), 0.02),
        'exp_w_dn': nrm((L, N_EXPERTS, D_EXPERT, d), D_EXPERT ** -0.5),
        'exp_b_dn': nrm((L, N_EXPERTS, d), 0.02),
    }


def reference(x, c, ctx, c_ctx, w_mod, b_mod, norm1, norm2, w_in, w_out, swa_q_norm, swa_k_norm, swa_sink,
              gdn_conv, gdn_a_log, gdn_dt_bias, gdn_out_norm, mla_q_a_norm, mla_w_uq, mla_kv_a_norm, mla_w_ukv,
              mla_q_norm, mla_k_norm, router_w, router_b, exp_w_gu, exp_b_gu, exp_w_dn, exp_b_dn):
    b, s, d = x.shape
    cl = ctx.shape[1]
    rope_a = axial_rope_tables(s, SWA_HEAD_DIM)
    rope_c = axial_rope_tables(s, MLA_ROPE)
    xc = ctx
    for l in range(DEPTH):
        with_ctx = l < DEPTH - 1
        mod = (jax.nn.silu(c) @ w_mod[l] + b_mod[l]).reshape(b, 1, N_MOD, d)
        mod_c = (jax.nn.silu(c_ctx) @ w_mod[l] + b_mod[l]).reshape(1, 1, N_MOD, d)
        h = modulate(x, norm1[l], mod[:, :, 0], mod[:, :, 1])
        hc = modulate(xc, norm1[l], mod_c[:, :, 0], mod_c[:, :, 1])
        mix, mix_c = token_mixers(h @ w_in[l], hc @ w_in[l], rope_a, rope_c, swa_q_norm[l], swa_k_norm[l],
                                  swa_sink[l], gdn_conv[l], gdn_a_log[l], gdn_dt_bias[l], gdn_out_norm[l],
                                  mla_q_a_norm[l], mla_w_uq[l], mla_kv_a_norm[l], mla_w_ukv[l], mla_q_norm[l],
                                  mla_k_norm[l], with_ctx)
        x = x + mod[:, :, 2] * (mix @ w_out[l])
        h = modulate(x, norm2[l], mod[:, :, 3], mod[:, :, 4])
        if with_ctx:
            xc = xc + mod_c[:, :, 2] * (mix_c @ w_out[l])
            hc = modulate(xc, norm2[l], mod_c[:, :, 3], mod_c[:, :, 4])
            y = moe_ffn(jnp.concatenate([h.reshape(b * s, d), hc.reshape(b * cl, d)], axis=0), router_w[l],
                        router_b[l], exp_w_gu[l], exp_b_gu[l], exp_w_dn[l], exp_b_dn[l])
            x = x + mod[:, :, 5] * y[:b * s].reshape(b, s, d)
            xc = xc + mod_c[:, :, 5] * y[b * s:].reshape(b, cl, d)
        else:
            y = moe_ffn(h.reshape(b * s, d), router_w[l], router_b[l], exp_w_gu[l], exp_b_gu[l],
                        exp_w_dn[l], exp_b_dn[l])
            x = x + mod[:, :, 5] * y.reshape(b, s, d)
    return x
```

```python
import functools
import math

import jax
import jax.numpy as jnp
import numpy as np
from jax import lax
from jax.experimental import pallas as pl
from jax.experimental.pallas import tpu as pltpu

F32 = jnp.float32
BF16 = jnp.bfloat16

GRID_W = 64
N_MOD = 6
EPS = 1e-6
ROPE_BASE = 10000.0
ATTN_BLOCK = 128

SWA_HEADS = 4
SWA_KV_HEADS = 2
SWA_HEAD_DIM = 64
SWA_WINDOW = 128

GDN_HEADS = 4
GDN_HEAD_DIM = 128
GDN_CONV = 5
GDN_CHUNK = 64

MLA_HEADS = 4
MLA_Q_RANK = 256
MLA_KV_RANK = 128
MLA_NOPE = 64
MLA_ROPE = 32
MLA_V = 64
MLA_QK = MLA_NOPE + MLA_ROPE

N_EXPERTS = 32
TOP_K = 4
SWIGLU_LIMIT = 7.0
SWIGLU_ALPHA = 1.702

SWA_Q = SWA_HEADS * SWA_HEAD_DIM
SWA_KV = SWA_KV_HEADS * SWA_HEAD_DIM
GDN_W = GDN_HEADS * GDN_HEAD_DIM
MLA_O = MLA_HEADS * MLA_V
D_MIX = SWA_Q + GDN_W + MLA_O
IN_SPLITS = (SWA_Q, SWA_KV, SWA_KV, GDN_W, GDN_W, GDN_W, GDN_W, 2 * GDN_HEADS, 2 * GDN_HEADS, MLA_Q_RANK,
             MLA_KV_RANK, MLA_ROPE)
N_IN = sum(IN_SPLITS)

LANE = 128
SUBLANE = 8
VMEM_LIMIT = 56 * 1024 * 1024

TOK_TILE = 512
MOE_TILE = 512
ROUTE_TILE = 512
MOVE_TILE = 256


def _cparams(*sem):
    return pltpu.CompilerParams(dimension_semantics=sem, vmem_limit_bytes=VMEM_LIMIT)


def _mod_kernel(c_ref, w_ref, b_ref, o_ref):
    a = c_ref[...]
    a = a * jax.nn.sigmoid(a)
    o_ref[0] = jnp.dot(a.astype(BF16), w_ref[0].astype(BF16), preferred_element_type=F32) + b_ref[0]


def _modulation(c_rows, w_mod, b_mod):
    nl, d, n = w_mod.shape
    r = c_rows.shape[0]
    tn = 1536
    out = pl.pallas_call(
        _mod_kernel,
        out_shape=jax.ShapeDtypeStruct((nl, r, n), F32),
        grid=(nl, n // tn),
        in_specs=[pl.BlockSpec((r, d), lambda l, j: (0, 0)),
                  pl.BlockSpec((1, d, tn), lambda l, j: (l, 0, j)),
                  pl.BlockSpec((1, 1, tn), lambda l, j: (l, 0, j))],
        out_specs=pl.BlockSpec((1, r, tn), lambda l, j: (l, 0, j)),
        compiler_params=_cparams("arbitrary", "arbitrary"),
        name="modulation",
    )(c_rows, w_mod, b_mod.reshape(nl, 1, n))
    return out.reshape(nl, r, N_MOD, d)


def _modulated_norm(x, gain, shift, scale):
    y = x * lax.rsqrt(jnp.mean(x * x, axis=-1, keepdims=True) + EPS)
    return (y * gain) * (1.0 + scale) + shift


def _inproj_kernel(x_ref, g_ref, m_ref, w_ref, o_ref):
    m = m_ref[0]
    h = _modulated_norm(x_ref[...], g_ref[...], m[0:1], m[1:2])
    o_ref[...] = jnp.dot(h.astype(BF16), w_ref[...], preferred_element_type=F32)


def _group_of_tile(i, tiles_per_seq, n_batch):
    return jnp.minimum(i // tiles_per_seq, n_batch)


def _in_projection(x_flat, gain, mod_l, w_bf, n_batch, seq):
    n_tok, d = x_flat.shape
    n = w_bf.shape[1]
    tm = TOK_TILE
    tps = seq // tm
    return pl.pallas_call(
        _inproj_kernel,
        out_shape=jax.ShapeDtypeStruct((n_tok, n), F32),
        grid=(n_tok // tm,),
        in_specs=[pl.BlockSpec((tm, d), lambda i: (i, 0)),
                  pl.BlockSpec((1, d), lambda i: (0, 0)),
                  pl.BlockSpec((1, N_MOD, d), lambda i: (_group_of_tile(i, tps, n_batch), 0, 0)),
                  pl.BlockSpec((d, n), lambda i: (0, 0))],
        out_specs=pl.BlockSpec((tm, n), lambda i: (i, 0)),
        compiler_params=_cparams("arbitrary"),
        name="in_projection",
    )(x_flat, gain.reshape(1, d), mod_l, w_bf)


def _outproj_kernel(mix_ref, x_ref, w_ref, g_ref, m_ref, xo_ref, h_ref):
    m = m_ref[0]
    o = jnp.dot(mix_ref[...].astype(BF16), w_ref[...], preferred_element_type=F32)
    xn = x_ref[...] + m[2:3] * o
    xo_ref[...] = xn
    h_ref[...] = _modulated_norm(xn, g_ref[...], m[3:4], m[4:5])


def _out_projection(mix_flat, x_flat, w_bf, gain2, mod_l, n_batch, seq):
    n_tok, d = mix_flat.shape[0], x_flat.shape[1]
    tm = TOK_TILE
    tps = seq // tm
    return pl.pallas_call(
        _outproj_kernel,
        out_shape=(jax.ShapeDtypeStruct((n_tok, d), F32), jax.ShapeDtypeStruct((n_tok, d), F32)),
        grid=(n_tok // tm,),
        in_specs=[pl.BlockSpec((tm, mix_flat.shape[1]), lambda i: (i, 0)),
                  pl.BlockSpec((tm, d), lambda i: (i, 0)),
                  pl.BlockSpec(w_bf.shape, lambda i: (0, 0)),
                  pl.BlockSpec((1, d), lambda i: (0, 0)),
                  pl.BlockSpec((1, N_MOD, d), lambda i: (_group_of_tile(i, tps, n_batch), 0, 0))],
        out_specs=(pl.BlockSpec((tm, d), lambda i: (i, 0)), pl.BlockSpec((tm, d), lambda i: (i, 0))),
        compiler_params=_cparams("arbitrary"),
        name="out_projection",
    )(mix_flat, x_flat, w_bf, gain2.reshape(1, d), mod_l)


def _route_kernel(h_ref, rw_ref, rb_ref, idx_ref, gate_ref, cnt_ref, base_ref):
    step = pl.program_id(0)
    tm = h_ref.shape[0]

    @pl.when(step == 0)
    def _():
        base_ref[...] = jnp.zeros_like(base_ref)

    logits = lax.dot_general(rw_ref[...], h_ref[...].astype(BF16), (((1,), (1,)), ((), ())),
                             preferred_element_type=F32) + rb_ref[...]
    e_iota = lax.broadcasted_iota(jnp.int32, logits.shape, 0)
    work = logits
    tops, picks = [], []
    for _k in range(TOP_K):
        mx = jnp.max(work, axis=0, keepdims=True)
        pick = jnp.min(jnp.where(work == mx, e_iota, N_EXPERTS), axis=0, keepdims=True)
        work = jnp.where(e_iota == pick, -jnp.inf, work)
        tops.append(mx)
        picks.append(pick)
    exps = [jnp.exp(t - tops[0]) for t in tops]
    denom = exps[0] + exps[1] + exps[2] + exps[3]
    sel = jnp.zeros(logits.shape, F32)
    for pick in picks:
        sel = sel + (e_iota == pick).astype(F32)
    row = lax.broadcasted_iota(jnp.int32, (tm, tm), 0)
    col = lax.broadcasted_iota(jnp.int32, (tm, tm), 1)
    before = (row < col).astype(BF16)
    cnt = jnp.dot(sel.astype(BF16), before, preferred_element_type=F32) + base_ref[:, 0:1]
    ranks = [jnp.sum(jnp.where(e_iota == pick, cnt, 0.0), axis=0, keepdims=True) for pick in picks]
    idx_ref[0] = jnp.concatenate(picks + [r.astype(jnp.int32) for r in ranks], axis=0)
    gate_rows = jnp.concatenate([e / denom for e in exps] + [jnp.zeros((LANE - TOP_K, tm), F32)], axis=0)
    gate_ref[...] = jnp.transpose(gate_rows)
    base_ref[...] = base_ref[...] + jnp.sum(sel, axis=1, keepdims=True)
    cnt_ref[...] = base_ref[...]


def _route(h_flat, n_tok, rw_t_bf, rb):
    d = h_flat.shape[1]
    tm = ROUTE_TILE
    nt = n_tok // tm
    return pl.pallas_call(
        _route_kernel,
        out_shape=(jax.ShapeDtypeStruct((nt, 2 * TOP_K, tm), jnp.int32),
                   jax.ShapeDtypeStruct((n_tok, LANE), F32),
                   jax.ShapeDtypeStruct((N_EXPERTS, LANE), F32)),
        grid=(nt,),
        in_specs=[pl.BlockSpec((tm, d), lambda i: (i, 0)),
                  pl.BlockSpec((N_EXPERTS, d), lambda i: (0, 0)),
                  pl.BlockSpec((N_EXPERTS, 1), lambda i: (0, 0))],
        out_specs=(pl.BlockSpec((1, 2 * TOP_K, tm), lambda i: (i, 0, 0)),
                   pl.BlockSpec((tm, LANE), lambda i: (i, 0)),
                   pl.BlockSpec((N_EXPERTS, LANE), lambda i: (0, 0))),
        scratch_shapes=[pltpu.VMEM((N_EXPERTS, LANE), F32)],
        compiler_params=_cparams("arbitrary"),
        name="moe_route",
    )(h_flat, rw_t_bf, rb.reshape(N_EXPERTS, 1))


def _dispatch_kernel(slot_ref, h_hbm, xs_in, xs_out, sem):
    del xs_in
    tm = slot_ref.shape[2]
    row0 = pl.program_id(0) * tm

    def row_copy(t, k):
        return pltpu.make_async_copy(h_hbm.at[pl.ds(row0 + t, 1)], xs_out.at[pl.ds(slot_ref[0, k, t], 1)], sem)

    def issue(t, carry):
        for k in range(TOP_K):
            row_copy(t, k).start()
        return carry

    lax.fori_loop(0, tm, issue, 0)

    def drain(t, carry):
        for k in range(TOP_K):
            row_copy(t, k).wait()
        return carry

    lax.fori_loop(0, tm, drain, 0)


def _dispatch(slots, h_flat, n_tok, cap):
    d = h_flat.shape[1]
    tm = slots.shape[2]
    return pl.pallas_call(
        _dispatch_kernel,
        out_shape=jax.ShapeDtypeStruct((cap, d), F32),
        grid=(n_tok // tm,),
        in_specs=[pl.BlockSpec((1, TOP_K, tm), lambda i: (i, 0, 0), memory_space=pltpu.SMEM),
                  pl.BlockSpec(memory_space=pl.ANY),
                  pl.BlockSpec(memory_space=pl.ANY)],
        out_specs=pl.BlockSpec(memory_space=pl.ANY),
        scratch_shapes=[pltpu.SemaphoreType.DMA(())],
        input_output_aliases={2: 0},
        compiler_params=_cparams("arbitrary"),
        name="moe_dispatch",
    )(slots, h_flat, jnp.zeros((cap, d), F32))


def _expert_kernel(be_ref, nu_ref, xs_ref, wg_ref, wl_ref, bg_ref, bl_ref, wd_ref, bd_ref, ys_ref):
    del be_ref

    @pl.when(pl.program_id(0) < nu_ref[0])
    def _():
        x = xs_ref[...].astype(BF16)
        g = jnp.dot(x, wg_ref[0], preferred_element_type=F32) + bg_ref[0]
        u = jnp.dot(x, wl_ref[0], preferred_element_type=F32) + bl_ref[0]
        g = jnp.minimum(g, SWIGLU_LIMIT)
        u = jnp.clip(u, -SWIGLU_LIMIT, SWIGLU_LIMIT)
        act = g * jax.nn.sigmoid(SWIGLU_ALPHA * g) * (u + 1.0)
        ys_ref[...] = jnp.dot(act.astype(BF16), wd_ref[0], preferred_element_type=F32) + bd_ref[0]

    @pl.when(pl.program_id(0) >= nu_ref[0])
    def _():
        ys_ref[...] = jnp.zeros_like(ys_ref)


def _experts(block_e, n_used, xs, wg, wl, bg, bl, wd, bd):
    cap, d = xs.shape
    f = wg.shape[2]
    tm = MOE_TILE
    nb = cap // tm

    def row_map(i, be, nu):
        return (jnp.minimum(i, nu[0] - 1), 0)

    def w_map(i, be, nu):
        return (be[jnp.minimum(i, nu[0] - 1)], 0, 0)

    return pl.pallas_call(
        _expert_kernel,
        out_shape=jax.ShapeDtypeStruct((cap, d), F32),
        grid_spec=pltpu.PrefetchScalarGridSpec(
            num_scalar_prefetch=2,
            grid=(nb,),
            in_specs=[pl.BlockSpec((tm, d), row_map),
                      pl.BlockSpec((1, d, f), w_map),
                      pl.BlockSpec((1, d, f), w_map),
                      pl.BlockSpec((1, 1, f), w_map),
                      pl.BlockSpec((1, 1, f), w_map),
                      pl.BlockSpec((1, f, d), w_map),
                      pl.BlockSpec((1, 1, d), w_map)],
            out_specs=pl.BlockSpec((tm, d), lambda i, be, nu: (i, 0))),
        compiler_params=_cparams("arbitrary"),
        name="moe_experts",
    )(block_e, n_used, xs, wg, wl, bg, bl, wd, bd)


def _combine_kernel(slot_ref, ys_hbm, x_ref, gate_ref, m_ref, o_ref, buf, sem):
    tm = x_ref.shape[0]

    def row_copy(t, k):
        return pltpu.make_async_copy(ys_hbm.at[pl.ds(slot_ref[0, k, t], 1)], buf.at[k, pl.ds(t, 1)], sem)

    def issue(t, carry):
        for k in range(TOP_K):
            row_copy(t, k).start()
        return carry

    lax.fori_loop(0, tm, issue, 0)

    def drain(t, carry):
        for k in range(TOP_K):
            row_copy(t, k).wait()
        return carry

    lax.fori_loop(0, tm, drain, 0)
    gates = gate_ref[...]
    y = buf[0] * gates[:, 0:1]
    for k in range(1, TOP_K):
        y = y + buf[k] * gates[:, k:k + 1]
    o_ref[...] = x_ref[...] + m_ref[0][5:6] * y


def _combine(slots, ys, x_flat, gates, mod_l, n_tok, n_batch, seq):
    d = x_flat.shape[1]
    tm = slots.shape[2]
    tps = seq // tm
    return pl.pallas_call(
        _combine_kernel,
        out_shape=jax.ShapeDtypeStruct((n_tok, d), F32),
        grid=(n_tok // tm,),
        in_specs=[pl.BlockSpec((1, TOP_K, tm), lambda i: (i, 0, 0), memory_space=pltpu.SMEM),
                  pl.BlockSpec(memory_space=pl.ANY),
                  pl.BlockSpec((tm, d), lambda i: (i, 0)),
                  pl.BlockSpec((tm, LANE), lambda i: (i, 0)),
                  pl.BlockSpec((1, N_MOD, d), lambda i: (_group_of_tile(i, tps, n_batch), 0, 0))],
        out_specs=pl.BlockSpec((tm, d), lambda i: (i, 0)),
        scratch_shapes=[pltpu.VMEM((TOP_K, tm, d), F32), pltpu.SemaphoreType.DMA(())],
        compiler_params=_cparams("arbitrary"),
        name="moe_combine",
    )(slots, ys, x_flat, gates, mod_l)


def _moe(h_flat, x_flat, n_tok, mod_l, rw_t_bf, rb, wg, wl, bg, bl, wd, bd, n_batch, seq):
    idx, gates, counts = _route(h_flat, n_tok, rw_t_bf, rb)
    counts = counts[:, 0].astype(jnp.int32)
    padded = (counts + MOE_TILE - 1) // MOE_TILE * MOE_TILE
    pad_end = jnp.cumsum(padded)
    pad_start = pad_end - padded
    n_blocks = -(-(n_tok * TOP_K) // MOE_TILE) + N_EXPERTS
    cap = n_blocks * MOE_TILE
    block_e = jnp.minimum(jnp.searchsorted(pad_end, jnp.arange(n_blocks, dtype=jnp.int32) * MOE_TILE, side='right'),
                          N_EXPERTS - 1).astype(jnp.int32)
    n_used = (pad_end[-1:] // MOE_TILE).astype(jnp.int32)
    slots = pad_start[idx[:, :TOP_K, :]] + idx[:, TOP_K:, :]
    slots = _retile_slots(slots, MOVE_TILE)
    xs = _dispatch(slots, h_flat, n_tok, cap)
    ys = _experts(block_e, n_used, xs, wg, wl, bg, bl, wd, bd)
    return _combine(slots, ys, x_flat, gates, mod_l, n_tok, n_batch, seq)


def _retile_slots(slots, tm):
    nt, k, t = slots.shape
    return slots.reshape(nt, k, t // tm, tm).transpose(0, 2, 1, 3).reshape(nt * (t // tm), k, tm)


def _rms_norm(x, gain):
    xf = x.astype(F32)
    y = xf * lax.rsqrt(jnp.mean(xf * xf, axis=-1, keepdims=True) + EPS)
    return (y * gain.astype(F32)).astype(x.dtype)


def _l2_norm(x):
    xf = x.astype(F32)
    return xf * lax.rsqrt(jnp.sum(xf * xf, axis=-1, keepdims=True) + EPS)


def _axial_rope_tables(n_lat, rot_dim):
    rows = n_lat // GRID_W
    t = jnp.arange(rows * GRID_W)
    row = (t // GRID_W).astype(F32)
    col = (t % GRID_W).astype(F32)
    n_freq = rot_dim // 4
    freq = ROPE_BASE ** (-jnp.arange(n_freq, dtype=F32) / n_freq)
    ang = jnp.stack([row[:, None] * freq, col[:, None] * freq], axis=1)
    return jnp.cos(ang), jnp.sin(ang)


def _apply_axial_rope(x, cos, sin):
    b, n, h, r = x.shape
    xr = x.astype(F32).reshape(b, n, h, 2, 2, r // 4)
    x1, x2 = xr[..., 0, :], xr[..., 1, :]
    cs, sn = cos[None, :, None], sin[None, :, None]
    out = jnp.stack([x1 * cs - x2 * sn, x2 * cs + x1 * sn], axis=-2)
    return out.reshape(b, n, h, r).astype(x.dtype)


def _swa_latent(q, k, v, kc, vc, sink):
    b, s, hq, dh = q.shape
    hkv = k.shape[2]
    grp = hq // hkv
    nb = s // ATTN_BLOCK
    nw = 3 * ATTN_BLOCK
    scale = dh ** -0.5
    qb = q.reshape(b, nb, ATTN_BLOCK, hkv, grp, dh)
    pad = ((0, 0), (ATTN_BLOCK, ATTN_BLOCK), (0, 0), (0, 0))
    kp = jnp.pad(k, pad).reshape(b, nb + 2, ATTN_BLOCK, hkv, dh)
    vp = jnp.pad(v, pad).reshape(b, nb + 2, ATTN_BLOCK, hkv, dh)
    kw = jnp.concatenate([kp[:, :-2], kp[:, 1:-1], kp[:, 2:]], axis=2)
    vw = jnp.concatenate([vp[:, :-2], vp[:, 1:-1], vp[:, 2:]], axis=2)
    q_off = jnp.arange(ATTN_BLOCK)
    k_off = jnp.arange(nw) - ATTN_BLOCK
    key_pos = jnp.arange(nb)[:, None] * ATTN_BLOCK + k_off[None, :]
    in_band = jnp.abs(k_off[None, :] - q_off[:, None]) <= SWA_WINDOW
    in_seq = (key_pos >= 0) & (key_pos < s)
    valid = in_band[None] & in_seq[:, None, :]
    s_win = jnp.einsum('bnqhgd,bnkhd->bnhgqk', qb, kw, preferred_element_type=F32) * scale
    s_win = jnp.where(valid[None, :, None, None], s_win, -jnp.inf)
    s_ctx = jnp.einsum('bnqhgd,bchd->bnhgqc', qb, kc, preferred_element_type=F32) * scale
    s_sink = jnp.broadcast_to(sink.astype(F32).reshape(hkv, grp)[:, :, None, None], s_win.shape[:-1] + (1,))
    p = jax.nn.softmax(jnp.concatenate([s_win, s_ctx, s_sink], axis=-1), axis=-1).astype(v.dtype)
    o = (jnp.einsum('bnhgqk,bnkhd->bnqhgd', p[..., :nw], vw)
         + jnp.einsum('bnhgqc,bchd->bnqhgd', p[..., nw:-1], vc))
    return o.reshape(b, s, hq * dh)


def _context_attention(q, k, v, sink):
    b, n, hq, dq = q.shape
    hkv = k.shape[2]
    grp = hq // hkv
    qg = q.reshape(b, n, hkv, grp, dq)
    sc = jnp.einsum('bqhgd,bkhd->bhgqk', qg, k, preferred_element_type=F32) * dq ** -0.5
    if sink is not None:
        s_sink = jnp.broadcast_to(sink.astype(F32).reshape(hkv, grp)[:, :, None, None], sc.shape[:-1] + (1,))
        sc = jnp.concatenate([sc, s_sink], axis=-1)
    p = jax.nn.softmax(sc, axis=-1).astype(v.dtype)[..., :n]
    o = jnp.einsum('bhgqk,bkhd->bqhgd', p, v)
    return o.reshape(b, n, hq * v.shape[-1])


def _dense_block_attention(q, k, v, kc, vc):
    b, s, h, dqk = q.shape
    nb = s // ATTN_BLOCK
    scale = dqk ** -0.5
    qb = jnp.moveaxis(q.reshape(b, nb, ATTN_BLOCK, h, dqk), 1, 0)

    def attend(qi):
        sl = jnp.einsum('bqhd,bkhd->bhqk', qi, k, preferred_element_type=F32) * scale
        sc = jnp.einsum('bqhd,bchd->bhqc', qi, kc, preferred_element_type=F32) * scale
        p = jax.nn.softmax(jnp.concatenate([sl, sc], axis=-1), axis=-1).astype(v.dtype)
        return (jnp.einsum('bhqk,bkhd->bqhd', p[..., :s], v)
                + jnp.einsum('bhqc,bchd->bqhd', p[..., s:], vc))

    o = lax.map(attend, qb)
    return jnp.moveaxis(o, 0, 1).reshape(b, s, h * v.shape[-1])


def _short_conv(x, w):
    y = lax.conv_general_dilated(
        x, w[:, None, :].astype(x.dtype), window_strides=(1,),
        padding=[(GDN_CONV // 2, GDN_CONV // 2)],
        dimension_numbers=('NWC', 'WIO', 'NWC'), feature_group_count=x.shape[-1])
    return jax.nn.silu(y)


def _gdn_chunk_scan(q, k, v, g, beta, state):
    b, n, h, _ = q.shape
    nc = n // GDN_CHUNK

    def chunks(t):
        t = t.reshape((b, nc, GDN_CHUNK, h) + t.shape[3:])
        return jnp.moveaxis(t, 3, 1)

    qc, kc, vc = chunks(q), chunks(k), chunks(v)
    gc = jnp.cumsum(chunks(g), axis=-1)
    bc = chunks(beta)
    idx = jnp.arange(GDN_CHUNK)
    incl = idx[:, None] >= idx[None, :]
    strict = idx[:, None] > idx[None, :]
    decay = jnp.exp(jnp.where(incl, gc[..., :, None] - gc[..., None, :], -jnp.inf))
    kk = jnp.einsum('bhctd,bhcid->bhcti', kc, kc)
    a_mat = jnp.where(strict, bc[..., :, None] * kk * decay, 0.0) + jnp.eye(GDN_CHUNK, dtype=F32)
    u = lax.linalg.triangular_solve(a_mat, vc * bc[..., None], left_side=True, lower=True, unit_diagonal=True)
    w = lax.linalg.triangular_solve(a_mat, kc * (bc * jnp.exp(gc))[..., None], left_side=True, lower=True,
                                    unit_diagonal=True)
    qk = jnp.einsum('bhctd,bhcid->bhcti', qc, kc) * decay
    q_dec = qc * jnp.exp(gc)[..., None]
    k_dec = kc * jnp.exp(gc[..., -1:] - gc)[..., None]
    g_end = jnp.exp(gc[..., -1])

    def step(s_prev, xs):
        u_c, w_c, qk_c, qd_c, kd_c, ge_c = xs
        v_new = u_c - jnp.einsum('bhtk,bhkv->bhtv', w_c, s_prev)
        o_c = jnp.einsum('bhtk,bhkv->bhtv', qd_c, s_prev) + jnp.einsum('bhti,bhiv->bhtv', qk_c, v_new)
        s_next = ge_c[..., None, None] * s_prev + jnp.einsum('bhtk,bhtv->bhkv', kd_c, v_new)
        return s_next, o_c

    xs = tuple(jnp.moveaxis(t, 2, 0) for t in (u, w, qk, q_dec, k_dec, g_end))
    state, o = lax.scan(step, state, xs)
    o = jnp.moveaxis(jnp.moveaxis(o, 0, 2), 1, 3)
    return o.reshape(b, n, h, o.shape[-1]), state


def _orient(stream, d):
    q, k, v, g, beta = stream
    seq = (q, k, v, g[:, :, d], beta[:, :, d])
    if d == 1:
        return tuple(jnp.flip(t, axis=1) for t in seq)
    return seq


def _bidirectional_gdn(lat, ctx):
    b, _, h, dk = lat[0].shape
    dv = lat[2].shape[-1]
    o_lat, o_ctx = [], []
    for d in range(2):
        state0 = jnp.zeros((b, h, dk, dv), F32)
        oc, s_ctx = _gdn_chunk_scan(*_orient(ctx, d), state0)
        ol, _ = _gdn_chunk_scan(*_orient(lat, d), s_ctx)
        if d == 1:
            oc, ol = jnp.flip(oc, axis=1), jnp.flip(ol, axis=1)
        o_lat.append(ol)
        o_ctx.append(oc)
    return o_lat[0] + o_lat[1], o_ctx[0] + o_ctx[1]


def _token_mixers(p, pc, rope_a, rope_c, swa_q_norm, swa_k_norm, swa_sink, gdn_conv, gdn_a_log, gdn_dt_bias,
                  gdn_out_norm, mla_q_a_norm, mla_w_uq, mla_kv_a_norm, mla_w_ukv, mla_q_norm, mla_k_norm, with_ctx):
    b = p.shape[0]
    offsets = np.cumsum(IN_SPLITS)[:-1].tolist()
    aq, ak, av, gq, gk, gv, gz, ga, gb, cq, ckv, ckr = jnp.split(p, offsets, axis=-1)
    aq_c, ak_c, av_c, gq_c, gk_c, gv_c, gz_c, ga_c, gb_c, cq_c, ckv_c, ckr_c = jnp.split(pc, offsets, axis=-1)

    def swa_proj(q, k, v, rope):
        n = q.shape[1]
        q = _rms_norm(q.reshape(b, n, SWA_HEADS, SWA_HEAD_DIM), swa_q_norm)
        k = _rms_norm(k.reshape(b, n, SWA_KV_HEADS, SWA_HEAD_DIM), swa_k_norm)
        if rope is not None:
            q, k = _apply_axial_rope(q, *rope), _apply_axial_rope(k, *rope)
        return q, k, v.reshape(b, n, SWA_KV_HEADS, SWA_HEAD_DIM)

    qa, ka, va = swa_proj(aq, ak, av, rope_a)
    qa_c, ka_c, va_c = swa_proj(aq_c, ak_c, av_c, None)
    o_a = _swa_latent(qa, ka, va, ka_c, va_c, swa_sink)

    def gdn_proj(q, k, v, a, bg):
        n = q.shape[1]
        q, k, v = jnp.split(_short_conv(jnp.concatenate([q, k, v], axis=-1), gdn_conv), 3, axis=-1)
        hs = (b, n, GDN_HEADS, GDN_HEAD_DIM)
        q = _l2_norm(q.reshape(hs)) * GDN_HEAD_DIM ** -0.5
        k = _l2_norm(k.reshape(hs))
        v = v.reshape(hs).astype(F32)
        a = a.reshape(b, n, 2, GDN_HEADS).astype(F32)
        log_decay = -jnp.exp(gdn_a_log.astype(F32)) * jax.nn.softplus(a + gdn_dt_bias.astype(F32))
        beta = jax.nn.sigmoid(bg.reshape(b, n, 2, GDN_HEADS).astype(F32))
        return q, k, v, log_decay, beta

    def gdn_out(o, z):
        n = o.shape[1]
        gate = jax.nn.silu(z.reshape(b, n, GDN_HEADS, GDN_HEAD_DIM).astype(F32))
        return (_rms_norm(o, gdn_out_norm) * gate).reshape(b, n, GDN_W).astype(p.dtype)

    ob, ob_c = _bidirectional_gdn(gdn_proj(gq, gk, gv, ga, gb), gdn_proj(gq_c, gk_c, gv_c, ga_c, gb_c))
    o_b = gdn_out(ob, gz)

    def mla_proj(c_q, c_kv, k_rope, rope):
        n = c_q.shape[1]
        q = (_rms_norm(c_q, mla_q_a_norm) @ mla_w_uq).reshape(b, n, MLA_HEADS, MLA_QK)
        kv = (_rms_norm(c_kv, mla_kv_a_norm) @ mla_w_ukv).reshape(b, n, MLA_HEADS, MLA_NOPE + MLA_V)
        k_pe = jnp.broadcast_to(k_rope[:, :, None, :], (b, n, MLA_HEADS, MLA_ROPE))
        k = jnp.concatenate([kv[..., :MLA_NOPE], k_pe], axis=-1)
        q, k = _rms_norm(q, mla_q_norm), _rms_norm(k, mla_k_norm)
        if rope is not None:
            q = jnp.concatenate([q[..., :MLA_NOPE], _apply_axial_rope(q[..., MLA_NOPE:], *rope)], axis=-1)
            k = jnp.concatenate([k[..., :MLA_NOPE], _apply_axial_rope(k[..., MLA_NOPE:], *rope)], axis=-1)
        return q, k, kv[..., MLA_NOPE:]

    qm, km, vm = mla_proj(cq, ckv, ckr, rope_c)
    qm_c, km_c, vm_c = mla_proj(cq_c, ckv_c, ckr_c, None)
    o_c = _dense_block_attention(qm, km, vm, km_c, vm_c)

    mix = jnp.concatenate([o_a, o_b, o_c], axis=-1)
    if not with_ctx:
        return mix, None
    mix_c = jnp.concatenate([_context_attention(qa_c, ka_c, va_c, swa_sink), gdn_out(ob_c, gz_c),
                             _context_attention(qm_c, km_c, vm_c, None)], axis=-1)
    return mix, mix_c


def kernel(x, c, ctx, c_ctx, w_mod, b_mod, norm1, norm2, w_in, w_out, swa_q_norm, swa_k_norm, swa_sink, gdn_conv,
           gdn_a_log, gdn_dt_bias, gdn_out_norm, mla_q_a_norm, mla_w_uq, mla_kv_a_norm, mla_w_ukv, mla_q_norm,
           mla_k_norm, router_w, router_b, exp_w_gu, exp_b_gu, exp_w_dn, exp_b_dn):
    b, s, d = x.shape
    cl = ctx.shape[1]
    depth = w_mod.shape[0]
    n_lat, n_ctx = b * s, b * cl
    assert s % TOK_TILE == 0 and n_ctx % TOK_TILE == 0 and s % MOVE_TILE == 0 and n_ctx % ROUTE_TILE == 0
    rope_a = _axial_rope_tables(s, SWA_HEAD_DIM)
    rope_c = _axial_rope_tables(s, MLA_ROPE)

    n_rows = -(-(b + 1) // SUBLANE) * SUBLANE
    c_rows = jnp.concatenate([c, c_ctx[None, :], jnp.zeros((n_rows - b - 1, d), F32)], axis=0)
    mod_all = _modulation(c_rows, w_mod, b_mod)

    n_in_pad = -(-N_IN // LANE) * LANE
    x_flat = jnp.concatenate([x.reshape(n_lat, d), ctx.reshape(n_ctx, d)], axis=0)
    for l in range(depth):
        with_ctx = l < depth - 1
        mod_l = mod_all[l]
        w_in_bf = jnp.pad(w_in[l], ((0, 0), (0, n_in_pad - N_IN))).astype(BF16)
        p_flat = _in_projection(x_flat, norm1[l], mod_l, w_in_bf, b, s)[:, :N_IN]
        p = p_flat[:n_lat].reshape(b, s, N_IN)
        pc = p_flat[n_lat:].reshape(b, cl, N_IN)
        mix, mix_c = _token_mixers(p, pc, rope_a, rope_c, swa_q_norm[l], swa_k_norm[l], swa_sink[l], gdn_conv[l],
                                   gdn_a_log[l], gdn_dt_bias[l], gdn_out_norm[l], mla_q_a_norm[l], mla_w_uq[l],
                                   mla_kv_a_norm[l], mla_w_ukv[l], mla_q_norm[l], mla_k_norm[l], with_ctx)
        if with_ctx:
            mix_flat = jnp.concatenate([mix.reshape(n_lat, D_MIX), mix_c.reshape(n_ctx, D_MIX)], axis=0)
            n_tok = n_lat + n_ctx
        else:
            mix_flat = mix.reshape(n_lat, D_MIX)
            n_tok = n_lat
        x_mid, h2 = _out_projection(mix_flat, x_flat, w_out[l].astype(BF16), norm2[l], mod_l, b, s)
        wg = exp_w_gu[l][:, :, 0::2].astype(BF16)
        wl = exp_w_gu[l][:, :, 1::2].astype(BF16)
        bg = exp_b_gu[l][:, None, 0::2]
        bl = exp_b_gu[l][:, None, 1::2]
        wd = exp_w_dn[l].astype(BF16)
        bd = exp_b_dn[l][:, None, :]
        x_flat = _moe(h2, x_mid, n_tok, mod_l, router_w[l].T.astype(BF16), router_b[l], wg, wl, bg, bl, wd, bd, b, s)
    return x_flat[:n_lat].reshape(b, s, d)
```

```python
import functools

import jax
import jax.numpy as jnp
import numpy as np
from jax import lax
from jax.experimental import pallas as pl
from jax.experimental.pallas import tpu as pltpu

F32 = jnp.float32
BF16 = jnp.bfloat16

GRID_W = 64
N_MOD = 6
EPS = 1e-6
ROPE_BASE = 10000.0
ATTN_BLOCK = 128

SWA_HEADS = 4
SWA_KV_HEADS = 2
SWA_HEAD_DIM = 64
SWA_WINDOW = 128

GDN_HEADS = 4
GDN_HEAD_DIM = 128
GDN_CONV = 5
GDN_CHUNK = 64

MLA_HEADS = 4
MLA_Q_RANK = 256
MLA_KV_RANK = 128
MLA_NOPE = 64
MLA_ROPE = 32
MLA_V = 64
MLA_QK = MLA_NOPE + MLA_ROPE

N_EXPERTS = 32
TOP_K = 4
SWIGLU_LIMIT = 7.0
SWIGLU_ALPHA = 1.702

SWA_Q = SWA_HEADS * SWA_HEAD_DIM
SWA_KV = SWA_KV_HEADS * SWA_HEAD_DIM
GDN_W = GDN_HEADS * GDN_HEAD_DIM
MLA_O = MLA_HEADS * MLA_V
D_MIX = SWA_Q + GDN_W + MLA_O
IN_SPLITS = (SWA_Q, SWA_KV, SWA_KV, GDN_W, GDN_W, GDN_W, GDN_W, 2 * GDN_HEADS, 2 * GDN_HEADS, MLA_Q_RANK,
             MLA_KV_RANK, MLA_ROPE)
N_IN = sum(IN_SPLITS)

LANE = 128
SUBLANE = 8
VMEM_LIMIT = 56 * 1024 * 1024

TOK_TILE = 512
MOE_TILE = 512
ROUTE_TILE = 512
MOVE_TILE = 256
MLA_Q_TILE = 256

COL_AQ, COL_AK, COL_AV = 0, 256, 384
COL_G = 512
COL_CQ, COL_CKV, COL_SMALL = 2560, 2816, 2944
N_IN_PAD = 3072
SMALL_KR, SMALL_GA, SMALL_GB = 0, MLA_ROPE, MLA_ROPE + 2 * GDN_HEADS
SWA_HEAD_ORDER = (0, 2, 1, 3)
MLA_HEAD_PAD = 128


def _cparams(*sem):
    return pltpu.CompilerParams(dimension_semantics=sem, vmem_limit_bytes=VMEM_LIMIT)


def _in_proj_perm():
    old = np.cumsum((0,) + IN_SPLITS)
    o_aq, o_ak, o_av, o_gq, o_gk, o_gv, o_gz, o_ga, o_gb, o_cq, o_ckv, o_ckr = old[:-1]
    perm = np.full((N_IN_PAD,), N_IN, np.int32)
    perm[COL_AQ:COL_AQ + SWA_Q] = np.concatenate(
        [o_aq + h * SWA_HEAD_DIM + np.arange(SWA_HEAD_DIM) for h in SWA_HEAD_ORDER])
    perm[COL_AK:COL_G + 4 * GDN_W] = np.arange(o_ak, o_ga)
    perm[COL_CQ:COL_CQ + MLA_Q_RANK] = o_cq + np.arange(MLA_Q_RANK)
    perm[COL_CKV:COL_CKV + MLA_KV_RANK] = o_ckv + np.arange(MLA_KV_RANK)
    perm[COL_SMALL + SMALL_KR:COL_SMALL + SMALL_KR + MLA_ROPE] = o_ckr + np.arange(MLA_ROPE)
    perm[COL_SMALL + SMALL_GA:COL_SMALL + SMALL_GA + 2 * GDN_HEADS] = o_ga + np.arange(2 * GDN_HEADS)
    perm[COL_SMALL + SMALL_GB:COL_SMALL + SMALL_GB + 2 * GDN_HEADS] = o_gb + np.arange(2 * GDN_HEADS)
    return perm


def _mod_kernel(c_ref, w_ref, b_ref, o_ref):
    a = c_ref[...]
    a = a * jax.nn.sigmoid(a)
    o_ref[0] = jnp.dot(a.astype(BF16), w_ref[0].astype(BF16), preferred_element_type=F32) + b_ref[0]


def _modulation(c_rows, w_mod, b_mod):
    nl, d, n = w_mod.shape
    r = c_rows.shape[0]
    tn = 1536
    out = pl.pallas_call(
        _mod_kernel,
        out_shape=jax.ShapeDtypeStruct((nl, r, n), F32),
        grid=(nl, n // tn),
        in_specs=[pl.BlockSpec((r, d), lambda l, j: (0, 0)),
                  pl.BlockSpec((1, d, tn), lambda l, j: (l, 0, j)),
                  pl.BlockSpec((1, 1, tn), lambda l, j: (l, 0, j))],
        out_specs=pl.BlockSpec((1, r, tn), lambda l, j: (l, 0, j)),
        compiler_params=_cparams("arbitrary", "arbitrary"),
        name="modulation",
    )(c_rows, w_mod, b_mod.reshape(nl, 1, n))
    return out.reshape(nl, r, N_MOD, d)


def _modulated_norm(x, gain, shift, scale):
    y = x * lax.rsqrt(jnp.mean(x * x, axis=-1, keepdims=True) + EPS)
    return (y * gain) * (1.0 + scale) + shift


def _group_of_tile(i, tiles_per_seq, n_batch):
    return jnp.minimum(i // tiles_per_seq, n_batch)


def _inproj_kernel(x_ref, g_ref, m_ref, w_ref, o_ref):
    m = m_ref[0]
    h = _modulated_norm(x_ref[...], g_ref[...], m[0:1], m[1:2])
    o_ref[...] = jnp.dot(h.astype(BF16), w_ref[...], preferred_element_type=F32)


def _in_projection(x_flat, gain, mod_l, w_bf, n_batch, seq):
    n_tok, d = x_flat.shape
    n = w_bf.shape[1]
    tm = TOK_TILE
    tps = seq // tm
    return pl.pallas_call(
        _inproj_kernel,
        out_shape=jax.ShapeDtypeStruct((n_tok, n), F32),
        grid=(n_tok // tm,),
        in_specs=[pl.BlockSpec((tm, d), lambda i: (i, 0)),
                  pl.BlockSpec((1, d), lambda i: (0, 0)),
                  pl.BlockSpec((1, N_MOD, d), lambda i: (_group_of_tile(i, tps, n_batch), 0, 0)),
                  pl.BlockSpec((d, n), lambda i: (0, 0))],
        out_specs=pl.BlockSpec((tm, n), lambda i: (i, 0)),
        compiler_params=_cparams("arbitrary"),
        name="in_projection",
    )(x_flat, gain.reshape(1, d), mod_l, w_bf)


def _group_sumsq(x, seg):
    x2 = x * x
    hi = x2.astype(BF16)
    lo = (x2 - hi.astype(F32)).astype(BF16)
    return jnp.dot(hi, seg, preferred_element_type=F32) + jnp.dot(lo, seg, preferred_element_type=F32)


def _swap_pairs(x, half):
    w = x.shape[1]
    lane = lax.broadcasted_iota(jnp.int32, (1, w), 1)
    first = (lane % (2 * half)) < half
    return jnp.where(first, pltpu.roll(x, w - half, axis=1), pltpu.roll(x, half, axis=1))


def _prep_kernel(pa_ref, pc_ref, ra_c_ref, ra_s_ref, rc_c_ref, rc_s_ref, seg_a_ref, seg_c_ref,
                 gaq_ref, gak_ref, gcq_ref, gckv_ref, wuq_ref, wuk_ref, wuv_ref, gmq_ref, gmk_ref,
                 qa_ref, kva_ref, qm_ref, km_ref, vm_ref):
    aq = pa_ref[:, 0:SWA_Q]
    ak = pa_ref[:, SWA_Q:SWA_Q + SWA_KV]
    av = pa_ref[:, SWA_Q + SWA_KV:SWA_Q + 2 * SWA_KV]
    ca, sa = ra_c_ref[...], ra_s_ref[...]
    seg_a = seg_a_ref[...]
    qn = aq * lax.rsqrt(_group_sumsq(aq, seg_a) * (1.0 / SWA_HEAD_DIM) + EPS) * gaq_ref[...]
    kn = ak * lax.rsqrt(_group_sumsq(ak, seg_a[:SWA_KV, :SWA_KV]) * (1.0 / SWA_HEAD_DIM) + EPS) * gak_ref[...]
    qn = qn * jnp.concatenate([ca, ca], axis=1) + _swap_pairs(qn, SWA_HEAD_DIM // 4) * jnp.concatenate([sa, sa], axis=1)
    kn = kn * ca + _swap_pairs(kn, SWA_HEAD_DIM // 4) * sa
    qa_ref[...] = (qn * SWA_HEAD_DIM ** -0.5).astype(BF16)
    kva_ref[...] = jnp.concatenate([kn, av], axis=1).astype(BF16)

    cq = pc_ref[:, 0:MLA_Q_RANK]
    ckv = pc_ref[:, MLA_Q_RANK:MLA_Q_RANK + MLA_KV_RANK]
    small = pc_ref[:, MLA_Q_RANK + MLA_KV_RANK:]
    cqn = cq * lax.rsqrt(jnp.mean(cq * cq, axis=-1, keepdims=True) + EPS) * gcq_ref[...]
    ckvn = (ckv * lax.rsqrt(jnp.mean(ckv * ckv, axis=-1, keepdims=True) + EPS) * gckv_ref[...]).astype(BF16)
    q = jnp.dot(cqn.astype(BF16), wuq_ref[...], preferred_element_type=F32)
    k = jnp.dot(ckvn, wuk_ref[...], preferred_element_type=F32)
    v = jnp.dot(ckvn, wuv_ref[...], preferred_element_type=F32)
    lane = lax.broadcasted_iota(jnp.int32, (1, LANE), 1)
    kpe = jnp.where((lane >= MLA_NOPE) & (lane < MLA_QK), pltpu.roll(small, MLA_NOPE, axis=1), 0.0)
    k = k + jnp.concatenate([kpe] * MLA_HEADS, axis=1)
    seg_c = seg_c_ref[...]
    q = q * lax.rsqrt(_group_sumsq(q, seg_c) * (1.0 / MLA_QK) + EPS) * gmq_ref[...]
    k = k * lax.rsqrt(_group_sumsq(k, seg_c) * (1.0 / MLA_QK) + EPS) * gmk_ref[...]
    cc = jnp.concatenate([rc_c_ref[...]] * MLA_HEADS, axis=1)
    sc = jnp.concatenate([rc_s_ref[...]] * MLA_HEADS, axis=1)
    q = q * cc + _swap_pairs(q, MLA_ROPE // 4) * sc
    k = k * cc + _swap_pairs(k, MLA_ROPE // 4) * sc
    qm_ref[...] = (q * MLA_QK ** -0.5).astype(BF16)
    km_ref[...] = k.astype(BF16)
    vm_ref[...] = v.astype(BF16)


def _attention_prep(p_flat, tabs, consts, n_batch, seq):
    n_tok = p_flat.shape[0]
    tm = TOK_TILE
    tps = seq // tm
    n_lat_tiles = n_batch * tps
    row = lambda i: (i, 0)
    const = lambda i: (0, 0)
    tab = lambda i: (jnp.where(i < n_lat_tiles, i % tps, tps), 0)
    wide = MLA_HEADS * MLA_HEAD_PAD
    in_specs = [pl.BlockSpec((tm, 512), lambda i: (i, COL_AQ // 512)),
                pl.BlockSpec((tm, 512), lambda i: (i, COL_CQ // 512))]
    in_specs += [pl.BlockSpec((tm, LANE), tab)] * 4
    in_specs += [pl.BlockSpec(a.shape, const) for a in consts]
    return pl.pallas_call(
        _prep_kernel,
        out_shape=(jax.ShapeDtypeStruct((n_tok, SWA_Q), BF16), jax.ShapeDtypeStruct((n_tok, 2 * SWA_KV), BF16),
                   jax.ShapeDtypeStruct((n_tok, wide), BF16), jax.ShapeDtypeStruct((n_tok, wide), BF16),
                   jax.ShapeDtypeStruct((n_tok, wide), BF16)),
        grid=(n_tok // tm,),
        in_specs=in_specs,
        out_specs=(pl.BlockSpec((tm, SWA_Q), row), pl.BlockSpec((tm, 2 * SWA_KV), row),
                   pl.BlockSpec((tm, wide), row), pl.BlockSpec((tm, wide), row), pl.BlockSpec((tm, wide), row)),
        compiler_params=_cparams("arbitrary"),
        name="attention_prep",
    )(p_flat, p_flat, *tabs, *consts)


def _rope_tables(seq):
    t = jnp.arange(seq)
    row = (t // GRID_W).astype(F32)
    col = (t % GRID_W).astype(F32)

    def cos_sin(rot_dim):
        n_freq = rot_dim // 4
        freq = ROPE_BASE ** (-jnp.arange(n_freq, dtype=F32) / n_freq)
        ar, ac = row[:, None] * freq, col[:, None] * freq
        c = jnp.concatenate([jnp.cos(ar), jnp.cos(ar), jnp.cos(ac), jnp.cos(ac)], axis=1)
        s = jnp.concatenate([-jnp.sin(ar), jnp.sin(ar), -jnp.sin(ac), jnp.sin(ac)], axis=1)
        return c, s

    ca, sa = cos_sin(SWA_HEAD_DIM)
    ca, sa = jnp.tile(ca, (1, 2)), jnp.tile(sa, (1, 2))
    cc, sc = cos_sin(MLA_ROPE)
    ones_l = jnp.ones((seq, MLA_NOPE), F32)
    ones_r = jnp.ones((seq, MLA_HEAD_PAD - MLA_QK), F32)
    cc = jnp.concatenate([ones_l, cc, ones_r], axis=1)
    sc = jnp.concatenate([0 * ones_l, sc, 0 * ones_r], axis=1)
    ident_c = jnp.ones((TOK_TILE, LANE), F32)
    ident_s = jnp.zeros((TOK_TILE, LANE), F32)
    return tuple(jnp.concatenate([a, i], axis=0) for a, i in ((ca, ident_c), (sa, ident_s), (cc, ident_c), (sc, ident_s)))


def _block_diag_ones(n, blk):
    i = np.arange(n) // blk
    return jnp.asarray((i[:, None] == i[None, :]).astype(np.float32), BF16)


def _dot_nt(a, b):
    return lax.dot_general(a, b, (((1,), (1,)), ((), ())), preferred_element_type=F32)


def _swa_heads(sink_ref, q, k_list, v_list, mask_list, o_ref):
    lane = lax.broadcasted_iota(jnp.int32, (1, LANE), 1)
    lower = lane < SWA_HEAD_DIM
    cols = []
    for cgrp in range(2):
        qc = q[:, cgrp * LANE:(cgrp + 1) * LANE]
        halves = []
        for half in range(2):
            head = SWA_HEAD_ORDER[2 * cgrp + half]
            qh = jnp.where(lower if half == 0 else ~lower, qc, jnp.zeros_like(qc))
            sink = sink_ref[head]
            scores = []
            m = jnp.full((q.shape[0], 1), sink, F32)
            for kb, mk in zip(k_list, mask_list):
                s = _dot_nt(qh, kb)
                if mk is not None:
                    s = jnp.where(mk, s, -jnp.inf)
                scores.append(s)
                m = jnp.maximum(m, jnp.max(s, axis=-1, keepdims=True))
            l = jnp.exp(sink - m)
            acc = jnp.zeros((q.shape[0], LANE), F32)
            for s, vb in zip(scores, v_list):
                p = jnp.exp(s - m)
                l = l + jnp.sum(p, axis=-1, keepdims=True)
                acc = acc + jnp.dot(p.astype(BF16), vb, preferred_element_type=F32)
            halves.append(acc * (1.0 / l))
        cols.append(jnp.where(lower, halves[0], halves[1]))
    o_ref[...] = jnp.concatenate(cols, axis=1).astype(o_ref.dtype)


def _swa_kernel(sink_ref, q_ref, kvm_ref, kv0_ref, kvp_ref, kvc_ref, o_ref, *, n_blocks, seq):
    j = pl.program_id(1)
    kc, vc = kvc_ref[:, 0:SWA_KV], kvc_ref[:, SWA_KV:]

    @pl.when(j < n_blocks)
    def _():
        r = lax.broadcasted_iota(jnp.int32, (ATTN_BLOCK, ATTN_BLOCK), 0)
        c = lax.broadcasted_iota(jnp.int32, (ATTN_BLOCK, ATTN_BLOCK), 1)
        out_of_seq = -2 * ATTN_BLOCK
        m_prev = (r - c) <= jnp.where(j > 0, SWA_WINDOW - ATTN_BLOCK, out_of_seq)
        m_next = (c - r) <= jnp.where(j < n_blocks - 1, SWA_WINDOW - ATTN_BLOCK, out_of_seq)
        ks = [kvm_ref[:, 0:SWA_KV], kv0_ref[:, 0:SWA_KV], kvp_ref[:, 0:SWA_KV], kc]
        vs = [kvm_ref[:, SWA_KV:], kv0_ref[:, SWA_KV:], kvp_ref[:, SWA_KV:], vc]
        _swa_heads(sink_ref, q_ref[...], ks, vs, [m_prev, None, m_next, None], o_ref)

    @pl.when(j >= n_blocks)
    def _():
        _swa_heads(sink_ref, q_ref[...], [kc], [vc], [None], o_ref)


def _swa_attention(sink, qa, kva, n_batch, seq, ctx_len, with_ctx):
    nb = seq // ATTN_BLOCK
    ncb = ctx_len // ATTN_BLOCK
    n_lat = n_batch * seq
    steps = nb + (ncb if with_ctx else 0)
    n_tok = n_lat + (n_batch * ctx_len if with_ctx else 0)
    lat_blocks = n_lat // ATTN_BLOCK

    def q_map(b, j, s):
        return (jnp.where(j < nb, b * nb + j, lat_blocks + b * ncb + (j - nb)), 0)

    def kv_map(off):
        def f(b, j, s):
            jj = jnp.clip(jnp.minimum(j, nb - 1) + off, 0, nb - 1)
            return (b * nb + jj, 0)
        return f

    return pl.pallas_call(
        functools.partial(_swa_kernel, n_blocks=nb, seq=seq),
        out_shape=jax.ShapeDtypeStruct((n_tok, SWA_Q), BF16),
        grid_spec=pltpu.PrefetchScalarGridSpec(
            num_scalar_prefetch=1,
            grid=(n_batch, steps),
            in_specs=[pl.BlockSpec((ATTN_BLOCK, SWA_Q), q_map),
                      pl.BlockSpec((ATTN_BLOCK, 2 * SWA_KV), kv_map(-1)),
                      pl.BlockSpec((ATTN_BLOCK, 2 * SWA_KV), kv_map(0)),
                      pl.BlockSpec((ATTN_BLOCK, 2 * SWA_KV), kv_map(1)),
                      pl.BlockSpec((ctx_len, 2 * SWA_KV), lambda b, j, s: (n_lat // ctx_len + b, 0))],
            out_specs=pl.BlockSpec((ATTN_BLOCK, SWA_Q), q_map)),
        compiler_params=_cparams("arbitrary", "arbitrary"),
        name="swa_attention",
    )(sink, qa, kva, kva, kva, kva)


def _mla_heads(q_ref, k_refs, v_refs, o_ref):
    for h in range(MLA_HEADS):
        sl = slice(h * MLA_HEAD_PAD, (h + 1) * MLA_HEAD_PAD)
        qh = q_ref[:, sl]
        scores = [_dot_nt(qh, k_ref[:, sl]) for k_ref in k_refs]
        m = jnp.max(scores[0], axis=-1, keepdims=True)
        for s in scores[1:]:
            m = jnp.maximum(m, jnp.max(s, axis=-1, keepdims=True))
        l = jnp.zeros_like(m)
        acc = jnp.zeros((qh.shape[0], MLA_HEAD_PAD), F32)
        for s, v_ref in zip(scores, v_refs):
            p = jnp.exp(s - m)
            l = l + jnp.sum(p, axis=-1, keepdims=True)
            acc = acc + jnp.dot(p.astype(BF16), v_ref[:, sl], preferred_element_type=F32)
        o_ref[:, sl] = (acc * (1.0 / l)).astype(o_ref.dtype)


def _mla_kernel(q_ref, kl_ref, vl_ref, kc_ref, vc_ref, o_ref, *, n_lat_steps):
    j = pl.program_id(1)

    @pl.when(j < n_lat_steps)
    def _():
        _mla_heads(q_ref, [kl_ref, kc_ref], [vl_ref, vc_ref], o_ref)

    @pl.when(j >= n_lat_steps)
    def _():
        _mla_heads(q_ref, [kc_ref], [vc_ref], o_ref)


def _mla_attention(qm, km, vm, n_batch, seq, ctx_len, with_ctx):
    tq = MLA_Q_TILE
    nq = seq // tq
    ncq = ctx_len // tq
    n_lat = n_batch * seq
    steps = nq + (ncq if with_ctx else 0)
    n_tok = n_lat + (n_batch * ctx_len if with_ctx else 0)
    wide = MLA_HEADS * MLA_HEAD_PAD

    def q_map(b, j):
        return (jnp.where(j < nq, b * nq + j, n_lat // tq + b * ncq + (j - nq)), 0)

    lat_map = lambda b, j: (b, 0)
    ctx_map = lambda b, j: (n_lat // ctx_len + b, 0)
    return pl.pallas_call(
        functools.partial(_mla_kernel, n_lat_steps=nq),
        out_shape=jax.ShapeDtypeStruct((n_tok, wide), BF16),
        grid=(n_batch, steps),
        in_specs=[pl.BlockSpec((tq, wide), q_map),
                  pl.BlockSpec((seq, wide), lat_map),
                  pl.BlockSpec((seq, wide), lat_map),
                  pl.BlockSpec((ctx_len, wide), ctx_map),
                  pl.BlockSpec((ctx_len, wide), ctx_map)],
        out_specs=pl.BlockSpec((tq, wide), q_map),
        compiler_params=_cparams("arbitrary", "arbitrary"),
        name="mla_attention",
    )(qm, km, vm, km, vm)


def _outproj_kernel(oa_ref, ob_ref, oc_ref, x_ref, wa_ref, wb_ref, wc_ref, g_ref, m_ref, xo_ref, h_ref):
    m = m_ref[0]
    o = (jnp.dot(oa_ref[...].astype(BF16), wa_ref[...], preferred_element_type=F32)
         + jnp.dot(ob_ref[...].astype(BF16), wb_ref[...], preferred_element_type=F32)
         + jnp.dot(oc_ref[...].astype(BF16), wc_ref[...], preferred_element_type=F32))
    xn = x_ref[...] + m[2:3] * o
    xo_ref[...] = xn
    h_ref[...] = _modulated_norm(xn, g_ref[...], m[3:4], m[4:5])


def _out_projection(o_a, o_b, o_c, x_flat, wa, wb, wc, gain2, mod_l, n_tok, n_batch, seq):
    d = x_flat.shape[1]
    tm = TOK_TILE
    tps = seq // tm
    row = lambda i: (i, 0)
    const = lambda i: (0, 0)
    return pl.pallas_call(
        _outproj_kernel,
        out_shape=(jax.ShapeDtypeStruct((n_tok, d), F32), jax.ShapeDtypeStruct((n_tok, d), F32)),
        grid=(n_tok // tm,),
        in_specs=[pl.BlockSpec((tm, o_a.shape[1]), row),
                  pl.BlockSpec((tm, o_b.shape[1]), row),
                  pl.BlockSpec((tm, o_c.shape[1]), row),
                  pl.BlockSpec((tm, d), row),
                  pl.BlockSpec(wa.shape, const),
                  pl.BlockSpec(wb.shape, const),
                  pl.BlockSpec(wc.shape, const),
                  pl.BlockSpec((1, d), const),
                  pl.BlockSpec((1, N_MOD, d), lambda i: (_group_of_tile(i, tps, n_batch), 0, 0))],
        out_specs=(pl.BlockSpec((tm, d), row), pl.BlockSpec((tm, d), row)),
        compiler_params=_cparams("arbitrary"),
        name="out_projection",
    )(o_a, o_b, o_c, x_flat, wa, wb, wc, gain2.reshape(1, d), mod_l)


def _route_kernel(h_ref, rw_ref, rb_ref, idx_ref, gate_ref, cnt_ref, base_ref):
    step = pl.program_id(0)
    tm = h_ref.shape[0]

    @pl.when(step == 0)
    def _():
        base_ref[...] = jnp.zeros_like(base_ref)

    logits = lax.dot_general(rw_ref[...], h_ref[...].astype(BF16), (((1,), (1,)), ((), ())),
                             preferred_element_type=F32) + rb_ref[...]
    e_iota = lax.broadcasted_iota(jnp.int32, logits.shape, 0)
    work = logits
    tops, picks = [], []
    for _k in range(TOP_K):
        mx = jnp.max(work, axis=0, keepdims=True)
        pick = jnp.min(jnp.where(work == mx, e_iota, N_EXPERTS), axis=0, keepdims=True)
        work = jnp.where(e_iota == pick, -jnp.inf, work)
        tops.append(mx)
        picks.append(pick)
    exps = [jnp.exp(t - tops[0]) for t in tops]
    denom = exps[0] + exps[1] + exps[2] + exps[3]
    sel = jnp.zeros(logits.shape, F32)
    for pick in picks:
        sel = sel + (e_iota == pick).astype(F32)
    row = lax.broadcasted_iota(jnp.int32, (tm, tm), 0)
    col = lax.broadcasted_iota(jnp.int32, (tm, tm), 1)
    before = (row < col).astype(BF16)
    cnt = jnp.dot(sel.astype(BF16), before, preferred_element_type=F32) + base_ref[:, 0:1]
    ranks = [jnp.sum(jnp.where(e_iota == pick, cnt, 0.0), axis=0, keepdims=True) for pick in picks]
    idx_ref[0] = jnp.concatenate(picks + [r.astype(jnp.int32) for r in ranks], axis=0)
    gate_rows = jnp.concatenate([e / denom for e in exps] + [jnp.zeros((LANE - TOP_K, tm), F32)], axis=0)
    gate_ref[...] = jnp.transpose(gate_rows)
    base_ref[...] = base_ref[...] + jnp.sum(sel, axis=1, keepdims=True)
    cnt_ref[...] = base_ref[...]


def _route(h_flat, n_tok, rw_t_bf, rb):
    d = h_flat.shape[1]
    tm = ROUTE_TILE
    nt = n_tok // tm
    return pl.pallas_call(
        _route_kernel,
        out_shape=(jax.ShapeDtypeStruct((nt, 2 * TOP_K, tm), jnp.int32),
                   jax.ShapeDtypeStruct((n_tok, LANE), F32),
                   jax.ShapeDtypeStruct((N_EXPERTS, LANE), F32)),
        grid=(nt,),
        in_specs=[pl.BlockSpec((tm, d), lambda i: (i, 0)),
                  pl.BlockSpec((N_EXPERTS, d), lambda i: (0, 0)),
                  pl.BlockSpec((N_EXPERTS, 1), lambda i: (0, 0))],
        out_specs=(pl.BlockSpec((1, 2 * TOP_K, tm), lambda i: (i, 0, 0)),
                   pl.BlockSpec((tm, LANE), lambda i: (i, 0)),
                   pl.BlockSpec((N_EXPERTS, LANE), lambda i: (0, 0))),
        scratch_shapes=[pltpu.VMEM((N_EXPERTS, LANE), F32)],
        compiler_params=_cparams("arbitrary"),
        name="moe_route",
    )(h_flat, rw_t_bf, rb.reshape(N_EXPERTS, 1))


def _dispatch_kernel(slot_ref, h_ref, xs_in, xs_out, sem):
    del xs_in
    tm = slot_ref.shape[2]

    def row_copy(t, k):
        return pltpu.make_async_copy(h_ref.at[pl.ds(t, 1)], xs_out.at[pl.ds(slot_ref[0, k, t], 1)], sem)

    def issue(t, carry):
        for k in range(TOP_K):
            row_copy(t, k).start()
        return carry

    lax.fori_loop(0, tm, issue, 0)

    def drain(t, carry):
        for k in range(TOP_K):
            row_copy(t, k).wait()
        return carry

    lax.fori_loop(0, tm, drain, 0)


def _dispatch(slots, h_flat, n_tok, cap):
    d = h_flat.shape[1]
    tm = slots.shape[2]
    return pl.pallas_call(
        _dispatch_kernel,
        out_shape=jax.ShapeDtypeStruct((cap, d), F32),
        grid=(n_tok // tm,),
        in_specs=[pl.BlockSpec((1, TOP_K, tm), lambda i: (i, 0, 0), memory_space=pltpu.SMEM),
                  pl.BlockSpec((tm, d), lambda i: (i, 0)),
                  pl.BlockSpec(memory_space=pl.ANY)],
        out_specs=pl.BlockSpec(memory_space=pl.ANY),
        scratch_shapes=[pltpu.SemaphoreType.DMA(())],
        input_output_aliases={2: 0},
        compiler_params=_cparams("arbitrary"),
        name="moe_dispatch",
    )(slots, h_flat, jnp.zeros((cap, d), F32))


def _expert_kernel(be_ref, nu_ref, xs_ref, wg_ref, wl_ref, bg_ref, bl_ref, wd_ref, bd_ref, ys_ref):
    del be_ref

    @pl.when(pl.program_id(0) < nu_ref[0])
    def _():
        x = xs_ref[...].astype(BF16)
        g = jnp.dot(x, wg_ref[0], preferred_element_type=F32) + bg_ref[0]
        u = jnp.dot(x, wl_ref[0], preferred_element_type=F32) + bl_ref[0]
        g = jnp.minimum(g, SWIGLU_LIMIT)
        u = jnp.clip(u, -SWIGLU_LIMIT, SWIGLU_LIMIT)
        act = g * jax.nn.sigmoid(SWIGLU_ALPHA * g) * (u + 1.0)
        ys_ref[...] = jnp.dot(act.astype(BF16), wd_ref[0], preferred_element_type=F32) + bd_ref[0]

    @pl.when(pl.program_id(0) >= nu_ref[0])
    def _():
        ys_ref[...] = jnp.zeros_like(ys_ref)


def _experts(block_e, n_used, xs, wg, wl, bg, bl, wd, bd):
    cap, d = xs.shape
    f = wg.shape[2]
    tm = MOE_TILE
    nb = cap // tm

    def row_map(i, be, nu):
        return (jnp.minimum(i, nu[0] - 1), 0)

    def w_map(i, be, nu):
        return (be[jnp.minimum(i, nu[0] - 1)], 0, 0)

    return pl.pallas_call(
        _expert_kernel,
        out_shape=jax.ShapeDtypeStruct((cap, d), F32),
        grid_spec=pltpu.PrefetchScalarGridSpec(
            num_scalar_prefetch=2,
            grid=(nb,),
            in_specs=[pl.BlockSpec((tm, d), row_map),
                      pl.BlockSpec((1, d, f), w_map),
                      pl.BlockSpec((1, d, f), w_map),
                      pl.BlockSpec((1, 1, f), w_map),
                      pl.BlockSpec((1, 1, f), w_map),
                      pl.BlockSpec((1, f, d), w_map),
                      pl.BlockSpec((1, 1, d), w_map)],
            out_specs=pl.BlockSpec((tm, d), lambda i, be, nu: (i, 0))),
        compiler_params=_cparams("arbitrary"),
        name="moe_experts",
    )(block_e, n_used, xs, wg, wl, bg, bl, wd, bd)


def _combine_kernel(slot_ref, ys_hbm, x_ref, gate_ref, m_ref, o_ref, buf, sem):
    tm = x_ref.shape[0]

    def row_copy(t, k):
        return pltpu.make_async_copy(ys_hbm.at[pl.ds(slot_ref[0, k, t], 1)], buf.at[k, pl.ds(t, 1)], sem)

    def issue(t, carry):
        for k in range(TOP_K):
            row_copy(t, k).start()
        return carry

    lax.fori_loop(0, tm, issue, 0)

    def drain(t, carry):
        for k in range(TOP_K):
            row_copy(t, k).wait()
        return carry

    lax.fori_loop(0, tm, drain, 0)
    gates = gate_ref[...]
    y = buf[0] * gates[:, 0:1]
    for k in range(1, TOP_K):
        y = y + buf[k] * gates[:, k:k + 1]
    o_ref[...] = x_ref[...] + m_ref[0][5:6] * y


def _combine(slots, ys, x_flat, gates, mod_l, n_tok, n_batch, seq):
    d = x_flat.shape[1]
    tm = slots.shape[2]
    tps = seq // tm
    return pl.pallas_call(
        _combine_kernel,
        out_shape=jax.ShapeDtypeStruct((n_tok, d), F32),
        grid=(n_tok // tm,),
        in_specs=[pl.BlockSpec((1, TOP_K, tm), lambda i: (i, 0, 0), memory_space=pltpu.SMEM),
                  pl.BlockSpec(memory_space=pl.ANY),
                  pl.BlockSpec((tm, d), lambda i: (i, 0)),
                  pl.BlockSpec((tm, LANE), lambda i: (i, 0)),
                  pl.BlockSpec((1, N_MOD, d), lambda i: (_group_of_tile(i, tps, n_batch), 0, 0))],
        out_specs=pl.BlockSpec((tm, d), lambda i: (i, 0)),
        scratch_shapes=[pltpu.VMEM((TOP_K, tm, d), F32), pltpu.SemaphoreType.DMA(())],
        compiler_params=_cparams("arbitrary"),
        name="moe_combine",
    )(slots, ys, x_flat, gates, mod_l)


def _retile_slots(slots, tm):
    nt, k, t = slots.shape
    return slots.reshape(nt, k, t // tm, tm).transpose(0, 2, 1, 3).reshape(nt * (t // tm), k, tm)


def _moe(h_flat, x_flat, n_tok, mod_l, rw_t_bf, rb, wg, wl, bg, bl, wd, bd, n_batch, seq):
    idx, gates, counts = _route(h_flat, n_tok, rw_t_bf, rb)
    counts = counts[:, 0].astype(jnp.int32)
    padded = (counts + MOE_TILE - 1) // MOE_TILE * MOE_TILE
    pad_end = jnp.cumsum(padded)
    pad_start = pad_end - padded
    n_blocks = -(-(n_tok * TOP_K) // MOE_TILE) + N_EXPERTS
    cap = n_blocks * MOE_TILE
    block_e = jnp.minimum(jnp.searchsorted(pad_end, jnp.arange(n_blocks, dtype=jnp.int32) * MOE_TILE, side='right'),
                          N_EXPERTS - 1).astype(jnp.int32)
    n_used = (pad_end[-1:] // MOE_TILE).astype(jnp.int32)
    slots = pad_start[idx[:, :TOP_K, :]] + idx[:, TOP_K:, :]
    slots = _retile_slots(slots, MOVE_TILE)
    xs = _dispatch(slots, h_flat, n_tok, cap)
    ys = _experts(block_e, n_used, xs, wg, wl, bg, bl, wd, bd)
    return _combine(slots, ys, x_flat, gates, mod_l, n_tok, n_batch, seq)


def _rms_norm(x, gain):
    xf = x.astype(F32)
    y = xf * lax.rsqrt(jnp.mean(xf * xf, axis=-1, keepdims=True) + EPS)
    return (y * gain.astype(F32)).astype(x.dtype)


def _l2_norm(x):
    xf = x.astype(F32)
    return xf * lax.rsqrt(jnp.sum(xf * xf, axis=-1, keepdims=True) + EPS)


def _short_conv(x, w):
    y = lax.conv_general_dilated(
        x, w[:, None, :].astype(x.dtype), window_strides=(1,),
        padding=[(GDN_CONV // 2, GDN_CONV // 2)],
        dimension_numbers=('NWC', 'WIO', 'NWC'), feature_group_count=x.shape[-1])
    return jax.nn.silu(y)


def _gdn_chunk_scan(q, k, v, g, beta, state):
    b, n, h, _ = q.shape
    nc = n // GDN_CHUNK

    def chunks(t):
        t = t.reshape((b, nc, GDN_CHUNK, h) + t.shape[3:])
        return jnp.moveaxis(t, 3, 1)

    qc, kc, vc = chunks(q), chunks(k), chunks(v)
    gc = jnp.cumsum(chunks(g), axis=-1)
    bc = chunks(beta)
    idx = jnp.arange(GDN_CHUNK)
    incl = idx[:, None] >= idx[None, :]
    strict = idx[:, None] > idx[None, :]
    decay = jnp.exp(jnp.where(incl, gc[..., :, None] - gc[..., None, :], -jnp.inf))
    kk = jnp.einsum('bhctd,bhcid->bhcti', kc, kc)
    a_mat = jnp.where(strict, bc[..., :, None] * kk * decay, 0.0) + jnp.eye(GDN_CHUNK, dtype=F32)
    u = lax.linalg.triangular_solve(a_mat, vc * bc[..., None], left_side=True, lower=True, unit_diagonal=True)
    w = lax.linalg.triangular_solve(a_mat, kc * (bc * jnp.exp(gc))[..., None], left_side=True, lower=True,
                                    unit_diagonal=True)
    qk = jnp.einsum('bhctd,bhcid->bhcti', qc, kc) * decay
    q_dec = qc * jnp.exp(gc)[..., None]
    k_dec = kc * jnp.exp(gc[..., -1:] - gc)[..., None]
    g_end = jnp.exp(gc[..., -1])

    def step(s_prev, xs):
        u_c, w_c, qk_c, qd_c, kd_c, ge_c = xs
        v_new = u_c - jnp.einsum('bhtk,bhkv->bhtv', w_c, s_prev)
        o_c = jnp.einsum('bhtk,bhkv->bhtv', qd_c, s_prev) + jnp.einsum('bhti,bhiv->bhtv', qk_c, v_new)
        s_next = ge_c[..., None, None] * s_prev + jnp.einsum('bhtk,bhtv->bhkv', kd_c, v_new)
        return s_next, o_c

    xs = tuple(jnp.moveaxis(t, 2, 0) for t in (u, w, qk, q_dec, k_dec, g_end))
    state, o = lax.scan(step, state, xs)
    o = jnp.moveaxis(jnp.moveaxis(o, 0, 2), 1, 3)
    return o.reshape(b, n, h, o.shape[-1]), state


def _orient(stream, d):
    q, k, v, g, beta = stream
    seq = (q, k, v, g[:, :, d], beta[:, :, d])
    if d == 1:
        return tuple(jnp.flip(t, axis=1) for t in seq)
    return seq


def _bidirectional_gdn(lat, ctx):
    b, _, h, dk = lat[0].shape
    dv = lat[2].shape[-1]
    o_lat, o_ctx = [], []
    for d in range(2):
        state0 = jnp.zeros((b, h, dk, dv), F32)
        oc, s_ctx = _gdn_chunk_scan(*_orient(ctx, d), state0)
        ol, _ = _gdn_chunk_scan(*_orient(lat, d), s_ctx)
        if d == 1:
            oc, ol = jnp.flip(oc, axis=1), jnp.flip(ol, axis=1)
        o_lat.append(ol)
        o_ctx.append(oc)
    return o_lat[0] + o_lat[1], o_ctx[0] + o_ctx[1]


def _gdn_mixer(p_flat, n_batch, seq, ctx_len, gdn_conv, gdn_a_log, gdn_dt_bias, gdn_out_norm, with_ctx):
    n_lat = n_batch * seq

    def split(rows, n):
        g4 = rows[:, COL_G:COL_G + 4 * GDN_W].reshape(n_batch, n, 4 * GDN_W)
        small = rows[:, COL_SMALL:].reshape(n_batch, n, LANE)
        q, k, v, z = jnp.split(g4, 4, axis=-1)
        return q, k, v, z, small[..., SMALL_GA:SMALL_GA + 2 * GDN_HEADS], small[..., SMALL_GB:SMALL_GB + 2 * GDN_HEADS]

    def proj(q, k, v, a, bg):
        n = q.shape[1]
        q, k, v = jnp.split(_short_conv(jnp.concatenate([q, k, v], axis=-1), gdn_conv), 3, axis=-1)
        hs = (n_batch, n, GDN_HEADS, GDN_HEAD_DIM)
        q = _l2_norm(q.reshape(hs)) * GDN_HEAD_DIM ** -0.5
        k = _l2_norm(k.reshape(hs))
        v = v.reshape(hs)
        a = a.reshape(n_batch, n, 2, GDN_HEADS)
        log_decay = -jnp.exp(gdn_a_log) * jax.nn.softplus(a + gdn_dt_bias)
        beta = jax.nn.sigmoid(bg.reshape(n_batch, n, 2, GDN_HEADS))
        return q, k, v, log_decay, beta

    def out(o, z):
        n = o.shape[1]
        gate = jax.nn.silu(z.reshape(n_batch, n, GDN_HEADS, GDN_HEAD_DIM))
        return (_rms_norm(o, gdn_out_norm) * gate).reshape(n_batch * n, GDN_W)

    q, k, v, z, a, bg = split(p_flat[:n_lat], seq)
    qc, kc, vc, zc, ac, bgc = split(p_flat[n_lat:], ctx_len)
    ob, ob_c = _bidirectional_gdn(proj(q, k, v, a, bg), proj(qc, kc, vc, ac, bgc))
    if with_ctx:
        return jnp.concatenate([out(ob, z), out(ob_c, zc)], axis=0)
    return out(ob, z)


def _pad_heads(w, n_heads, width, start, take):
    lead = w.shape[:-1]
    w = w.reshape(lead + (n_heads, width))[..., start:start + take]
    w = jnp.pad(w, [(0, 0)] * len(lead) + [(0, 0), (0, MLA_HEAD_PAD - take)])
    return w.reshape(lead + (n_heads * MLA_HEAD_PAD,))


def kernel(x, c, ctx, c_ctx, w_mod, b_mod, norm1, norm2, w_in, w_out, swa_q_norm, swa_k_norm, swa_sink, gdn_conv,
           gdn_a_log, gdn_dt_bias, gdn_out_norm, mla_q_a_norm, mla_w_uq, mla_kv_a_norm, mla_w_ukv, mla_q_norm,
           mla_k_norm, router_w, router_b, exp_w_gu, exp_b_gu, exp_w_dn, exp_b_dn):
    b, s, d = x.shape
    cl = ctx.shape[1]
    depth = w_mod.shape[0]
    n_lat, n_ctx = b * s, b * cl
    assert s % TOK_TILE == 0 and n_ctx % TOK_TILE == 0 and s % MOVE_TILE == 0 and n_ctx % ROUTE_TILE == 0
    assert cl % MLA_Q_TILE == 0 and cl % ATTN_BLOCK == 0 and n_lat % cl == 0

    n_rows = -(-(b + 1) // SUBLANE) * SUBLANE
    c_rows = jnp.concatenate([c, c_ctx[None, :], jnp.zeros((n_rows - b - 1, d), F32)], axis=0)
    mod_all = _modulation(c_rows, w_mod, b_mod)

    perm = _in_proj_perm()
    tabs = _rope_tables(s)
    seg_a = _block_diag_ones(SWA_Q, SWA_HEAD_DIM)
    seg_c = _block_diag_ones(MLA_HEADS * MLA_HEAD_PAD, MLA_HEAD_PAD)
    head_rows = np.concatenate([h * SWA_HEAD_DIM + np.arange(SWA_HEAD_DIM) for h in SWA_HEAD_ORDER])

    x_flat = jnp.concatenate([x.reshape(n_lat, d), ctx.reshape(n_ctx, d)], axis=0)
    for l in range(depth):
        with_ctx = l < depth - 1
        n_tok = n_lat + n_ctx if with_ctx else n_lat
        mod_l = mod_all[l]
        w_in_bf = jnp.pad(w_in[l], ((0, 0), (0, 1)))[:, perm].astype(BF16)
        p_flat = _in_projection(x_flat, norm1[l], mod_l, w_in_bf, b, s)

        consts = (seg_a, seg_c,
                  jnp.tile(swa_q_norm[l], SWA_HEADS)[None], jnp.tile(swa_k_norm[l], SWA_KV_HEADS)[None],
                  mla_q_a_norm[l][None], mla_kv_a_norm[l][None],
                  _pad_heads(mla_w_uq[l], MLA_HEADS, MLA_QK, 0, MLA_QK).astype(BF16),
                  _pad_heads(mla_w_ukv[l], MLA_HEADS, MLA_NOPE + MLA_V, 0, MLA_NOPE).astype(BF16),
                  _pad_heads(mla_w_ukv[l], MLA_HEADS, MLA_NOPE + MLA_V, MLA_NOPE, MLA_V).astype(BF16),
                  _pad_heads(jnp.tile(mla_q_norm[l], MLA_HEADS), MLA_HEADS, MLA_QK, 0, MLA_QK)[None],
                  _pad_heads(jnp.tile(mla_k_norm[l], MLA_HEADS), MLA_HEADS, MLA_QK, 0, MLA_QK)[None])
        qa, kva, qm, km, vm = _attention_prep(p_flat, tabs, consts, b, s)
        o_a = _swa_attention(swa_sink[l], qa, kva, b, s, cl, with_ctx)
        o_c = _mla_attention(qm, km, vm, b, s, cl, with_ctx)
        o_b = _gdn_mixer(p_flat, b, s, cl, gdn_conv[l], gdn_a_log[l], gdn_dt_bias[l], gdn_out_norm[l], with_ctx)

        wa = w_out[l][:SWA_Q][head_rows].astype(BF16)
        wb = w_out[l][SWA_Q:SWA_Q + GDN_W].astype(BF16)
        wc = w_out[l][SWA_Q + GDN_W:].reshape(MLA_HEADS, MLA_V, d)
        wc = jnp.pad(wc, ((0, 0), (0, MLA_HEAD_PAD - MLA_V), (0, 0))).reshape(MLA_HEADS * MLA_HEAD_PAD, d).astype(BF16)
        x_mid, h2 = _out_projection(o_a, o_b, o_c, x_flat, wa, wb, wc, norm2[l], mod_l, n_tok, b, s)

        wg = exp_w_gu[l][:, :, 0::2].astype(BF16)
        wl = exp_w_gu[l][:, :, 1::2].astype(BF16)
        bg = exp_b_gu[l][:, None, 0::2]
        bl = exp_b_gu[l][:, None, 1::2]
        wd = exp_w_dn[l].astype(BF16)
        bd = exp_b_dn[l][:, None, :]
        x_flat = _moe(h2, x_mid, n_tok, mod_l, router_w[l].T.astype(BF16), router_b[l], wg, wl, bg, bl, wd, bd, b, s)
    return x_flat[:n_lat].reshape(b, s, d)
```

```python
import functools

import jax
import jax.numpy as jnp
import numpy as np
from jax import lax
from jax.experimental import pallas as pl
from jax.experimental.pallas import tpu as pltpu

F32 = jnp.float32
BF16 = jnp.bfloat16

GRID_W = 64
N_MOD = 6
EPS = 1e-6
ROPE_BASE = 10000.0
ATTN_BLOCK = 128

SWA_HEADS = 4
SWA_KV_HEADS = 2
SWA_HEAD_DIM = 64
SWA_WINDOW = 128

GDN_HEADS = 4
GDN_HEAD_DIM = 128
GDN_CONV = 5
GDN_CHUNK = 64

MLA_HEADS = 4
MLA_Q_RANK = 256
MLA_KV_RANK = 128
MLA_NOPE = 64
MLA_ROPE = 32
MLA_V = 64
MLA_QK = MLA_NOPE + MLA_ROPE

N_EXPERTS = 32
TOP_K = 4
SWIGLU_LIMIT = 7.0
SWIGLU_ALPHA = 1.702

SWA_Q = SWA_HEADS * SWA_HEAD_DIM
SWA_KV = SWA_KV_HEADS * SWA_HEAD_DIM
GDN_W = GDN_HEADS * GDN_HEAD_DIM
MLA_O = MLA_HEADS * MLA_V
D_MIX = SWA_Q + GDN_W + MLA_O
IN_SPLITS = (SWA_Q, SWA_KV, SWA_KV, GDN_W, GDN_W, GDN_W, GDN_W, 2 * GDN_HEADS, 2 * GDN_HEADS, MLA_Q_RANK,
             MLA_KV_RANK, MLA_ROPE)
N_IN = sum(IN_SPLITS)

LANE = 128
SUBLANE = 8
VMEM_LIMIT = 56 * 1024 * 1024

TOK_TILE = 512
MOE_TILE = 512
ROUTE_TILE = 512
MOVE_TILE = 256
MLA_Q_TILE = 256

COL_AQ, COL_AK, COL_AV = 0, 256, 384
COL_G = 512
COL_CQ, COL_CKV, COL_SMALL = 2560, 2816, 2944
N_IN_PAD = 3072
SMALL_KR, SMALL_GA, SMALL_GB = 0, MLA_ROPE, MLA_ROPE + 2 * GDN_HEADS
SWA_HEAD_ORDER = (0, 2, 1, 3)
MLA_HEAD_PAD = 128


def _cparams(*sem):
    return pltpu.CompilerParams(dimension_semantics=sem, vmem_limit_bytes=VMEM_LIMIT)


def _in_proj_perm():
    old = np.cumsum((0,) + IN_SPLITS)
    o_aq, o_ak, o_av, o_gq, o_gk, o_gv, o_gz, o_ga, o_gb, o_cq, o_ckv, o_ckr = old[:-1]
    perm = np.full((N_IN_PAD,), N_IN, np.int32)
    perm[COL_AQ:COL_AQ + SWA_Q] = np.concatenate(
        [o_aq + h * SWA_HEAD_DIM + np.arange(SWA_HEAD_DIM) for h in SWA_HEAD_ORDER])
    perm[COL_AK:COL_G + 4 * GDN_W] = np.arange(o_ak, o_ga)
    perm[COL_CQ:COL_CQ + MLA_Q_RANK] = o_cq + np.arange(MLA_Q_RANK)
    perm[COL_CKV:COL_CKV + MLA_KV_RANK] = o_ckv + np.arange(MLA_KV_RANK)
    perm[COL_SMALL + SMALL_KR:COL_SMALL + SMALL_KR + MLA_ROPE] = o_ckr + np.arange(MLA_ROPE)
    perm[COL_SMALL + SMALL_GA:COL_SMALL + SMALL_GA + 2 * GDN_HEADS] = o_ga + np.arange(2 * GDN_HEADS)
    perm[COL_SMALL + SMALL_GB:COL_SMALL + SMALL_GB + 2 * GDN_HEADS] = o_gb + np.arange(2 * GDN_HEADS)
    return perm


def _mod_kernel(c_ref, w_ref, b_ref, o_ref):
    a = c_ref[...]
    a = a * jax.nn.sigmoid(a)
    o_ref[0] = jnp.dot(a.astype(BF16), w_ref[0].astype(BF16), preferred_element_type=F32) + b_ref[0]


def _modulation(c_rows, w_mod, b_mod):
    nl, d, n = w_mod.shape
    r = c_rows.shape[0]
    tn = 1536
    out = pl.pallas_call(
        _mod_kernel,
        out_shape=jax.ShapeDtypeStruct((nl, r, n), F32),
        grid=(nl, n // tn),
        in_specs=[pl.BlockSpec((r, d), lambda l, j: (0, 0)),
                  pl.BlockSpec((1, d, tn), lambda l, j: (l, 0, j)),
                  pl.BlockSpec((1, 1, tn), lambda l, j: (l, 0, j))],
        out_specs=pl.BlockSpec((1, r, tn), lambda l, j: (l, 0, j)),
        compiler_params=_cparams("arbitrary", "arbitrary"),
        name="modulation",
    )(c_rows, w_mod, b_mod.reshape(nl, 1, n))
    return out.reshape(nl, r, N_MOD, d)


def _modulated_norm(x, gain, shift, scale):
    y = x * lax.rsqrt(jnp.mean(x * x, axis=-1, keepdims=True) + EPS)
    return (y * gain) * (1.0 + scale) + shift


def _group_of_tile(i, tiles_per_seq, n_batch):
    return jnp.minimum(i // tiles_per_seq, n_batch)


def _inproj_kernel(x_ref, g_ref, m_ref, w_ref, o_ref):
    m = m_ref[0]
    h = _modulated_norm(x_ref[...], g_ref[...], m[0:1], m[1:2])
    o_ref[...] = jnp.dot(h.astype(BF16), w_ref[...], preferred_element_type=F32)


def _in_projection(x_flat, gain, mod_l, w_bf, n_batch, seq):
    n_tok, d = x_flat.shape
    n = w_bf.shape[1]
    tm = TOK_TILE
    tps = seq // tm
    return pl.pallas_call(
        _inproj_kernel,
        out_shape=jax.ShapeDtypeStruct((n_tok, n), F32),
        grid=(n_tok // tm,),
        in_specs=[pl.BlockSpec((tm, d), lambda i: (i, 0)),
                  pl.BlockSpec((1, d), lambda i: (0, 0)),
                  pl.BlockSpec((1, N_MOD, d), lambda i: (_group_of_tile(i, tps, n_batch), 0, 0)),
                  pl.BlockSpec((d, n), lambda i: (0, 0))],
        out_specs=pl.BlockSpec((tm, n), lambda i: (i, 0)),
        compiler_params=_cparams("arbitrary"),
        name="in_projection",
    )(x_flat, gain.reshape(1, d), mod_l, w_bf)


def _group_sumsq(x, seg):
    x2 = x * x
    hi = x2.astype(BF16)
    lo = (x2 - hi.astype(F32)).astype(BF16)
    return jnp.dot(hi, seg, preferred_element_type=F32) + jnp.dot(lo, seg, preferred_element_type=F32)


def _swap_pairs(x, half):
    w = x.shape[1]
    lane = lax.broadcasted_iota(jnp.int32, (1, w), 1)
    first = (lane % (2 * half)) < half
    return jnp.where(first, pltpu.roll(x, w - half, axis=1), pltpu.roll(x, half, axis=1))


def _prep_kernel(pa_ref, pc_ref, ra_c_ref, ra_s_ref, rc_c_ref, rc_s_ref, seg_a_ref, seg_c_ref,
                 gaq_ref, gak_ref, gcq_ref, gckv_ref, wuq_ref, wuk_ref, wuv_ref, gmq_ref, gmk_ref,
                 qa_ref, kva_ref, qm_ref, km_ref, vm_ref):
    aq = pa_ref[:, 0:SWA_Q]
    ak = pa_ref[:, SWA_Q:SWA_Q + SWA_KV]
    av = pa_ref[:, SWA_Q + SWA_KV:SWA_Q + 2 * SWA_KV]
    ca, sa = ra_c_ref[...], ra_s_ref[...]
    seg_a = seg_a_ref[...]
    qn = aq * lax.rsqrt(_group_sumsq(aq, seg_a) * (1.0 / SWA_HEAD_DIM) + EPS) * gaq_ref[...]
    kn = ak * lax.rsqrt(_group_sumsq(ak, seg_a[:SWA_KV, :SWA_KV]) * (1.0 / SWA_HEAD_DIM) + EPS) * gak_ref[...]
    qn = qn * jnp.concatenate([ca, ca], axis=1) + _swap_pairs(qn, SWA_HEAD_DIM // 4) * jnp.concatenate([sa, sa], axis=1)
    kn = kn * ca + _swap_pairs(kn, SWA_HEAD_DIM // 4) * sa
    qa_ref[...] = (qn * SWA_HEAD_DIM ** -0.5).astype(BF16)
    kva_ref[...] = jnp.concatenate([kn, av], axis=1).astype(BF16)

    cq = pc_ref[:, 0:MLA_Q_RANK]
    ckv = pc_ref[:, MLA_Q_RANK:MLA_Q_RANK + MLA_KV_RANK]
    small = pc_ref[:, MLA_Q_RANK + MLA_KV_RANK:]
    cqn = cq * lax.rsqrt(jnp.mean(cq * cq, axis=-1, keepdims=True) + EPS) * gcq_ref[...]
    ckvn = (ckv * lax.rsqrt(jnp.mean(ckv * ckv, axis=-1, keepdims=True) + EPS) * gckv_ref[...]).astype(BF16)
    q = jnp.dot(cqn.astype(BF16), wuq_ref[...], preferred_element_type=F32)
    k = jnp.dot(ckvn, wuk_ref[...], preferred_element_type=F32)
    v = jnp.dot(ckvn, wuv_ref[...], preferred_element_type=F32)
    lane = lax.broadcasted_iota(jnp.int32, (1, LANE), 1)
    kpe = jnp.where((lane >= MLA_NOPE) & (lane < MLA_QK), pltpu.roll(small, MLA_NOPE, axis=1), 0.0)
    k = k + jnp.concatenate([kpe] * MLA_HEADS, axis=1)
    seg_c = seg_c_ref[...]
    q = q * lax.rsqrt(_group_sumsq(q, seg_c) * (1.0 / MLA_QK) + EPS) * gmq_ref[...]
    k = k * lax.rsqrt(_group_sumsq(k, seg_c) * (1.0 / MLA_QK) + EPS) * gmk_ref[...]
    cc = jnp.concatenate([rc_c_ref[...]] * MLA_HEADS, axis=1)
    sc = jnp.concatenate([rc_s_ref[...]] * MLA_HEADS, axis=1)
    q = q * cc + _swap_pairs(q, MLA_ROPE // 4) * sc
    k = k * cc + _swap_pairs(k, MLA_ROPE // 4) * sc
    qm_ref[...] = (q * MLA_QK ** -0.5).astype(BF16)
    km_ref[...] = k.astype(BF16)
    vm_ref[...] = v.astype(BF16)


def _attention_prep(p_flat, tabs, consts, n_batch, seq):
    n_tok = p_flat.shape[0]
    tm = TOK_TILE
    tps = seq // tm
    n_lat_tiles = n_batch * tps
    row = lambda i: (i, 0)
    const = lambda i: (0, 0)
    tab = lambda i: (jnp.where(i < n_lat_tiles, i % tps, tps), 0)
    wide = MLA_HEADS * MLA_HEAD_PAD
    in_specs = [pl.BlockSpec((tm, 512), lambda i: (i, COL_AQ // 512)),
                pl.BlockSpec((tm, 512), lambda i: (i, COL_CQ // 512))]
    in_specs += [pl.BlockSpec((tm, LANE), tab)] * 4
    in_specs += [pl.BlockSpec(a.shape, const) for a in consts]
    return pl.pallas_call(
        _prep_kernel,
        out_shape=(jax.ShapeDtypeStruct((n_tok, SWA_Q), BF16), jax.ShapeDtypeStruct((n_tok, 2 * SWA_KV), BF16),
                   jax.ShapeDtypeStruct((n_tok, wide), BF16), jax.ShapeDtypeStruct((n_tok, wide), BF16),
                   jax.ShapeDtypeStruct((n_tok, wide), BF16)),
        grid=(n_tok // tm,),
        in_specs=in_specs,
        out_specs=(pl.BlockSpec((tm, SWA_Q), row), pl.BlockSpec((tm, 2 * SWA_KV), row),
                   pl.BlockSpec((tm, wide), row), pl.BlockSpec((tm, wide), row), pl.BlockSpec((tm, wide), row)),
        compiler_params=_cparams("arbitrary"),
        name="attention_prep",
    )(p_flat, p_flat, *tabs, *consts)


def _rope_tables(seq):
    t = jnp.arange(seq)
    row = (t // GRID_W).astype(F32)
    col = (t % GRID_W).astype(F32)

    def cos_sin(rot_dim):
        n_freq = rot_dim // 4
        freq = ROPE_BASE ** (-jnp.arange(n_freq, dtype=F32) / n_freq)
        ar, ac = row[:, None] * freq, col[:, None] * freq
        c = jnp.concatenate([jnp.cos(ar), jnp.cos(ar), jnp.cos(ac), jnp.cos(ac)], axis=1)
        s = jnp.concatenate([-jnp.sin(ar), jnp.sin(ar), -jnp.sin(ac), jnp.sin(ac)], axis=1)
        return c, s

    ca, sa = cos_sin(SWA_HEAD_DIM)
    ca, sa = jnp.tile(ca, (1, 2)), jnp.tile(sa, (1, 2))
    cc, sc = cos_sin(MLA_ROPE)
    ones_l = jnp.ones((seq, MLA_NOPE), F32)
    ones_r = jnp.ones((seq, MLA_HEAD_PAD - MLA_QK), F32)
    cc = jnp.concatenate([ones_l, cc, ones_r], axis=1)
    sc = jnp.concatenate([0 * ones_l, sc, 0 * ones_r], axis=1)
    ident_c = jnp.ones((TOK_TILE, LANE), F32)
    ident_s = jnp.zeros((TOK_TILE, LANE), F32)
    return tuple(jnp.concatenate([a, i], axis=0) for a, i in ((ca, ident_c), (sa, ident_s), (cc, ident_c), (sc, ident_s)))


def _block_diag_ones(n, blk):
    i = np.arange(n) // blk
    return jnp.asarray((i[:, None] == i[None, :]).astype(np.float32), BF16)


def _dot_nt(a, b):
    return lax.dot_general(a, b, (((1,), (1,)), ((), ())), preferred_element_type=F32)


def _swa_heads(sink_ref, q, k_list, v_list, mask_list, o_ref):
    lane = lax.broadcasted_iota(jnp.int32, (1, LANE), 1)
    lower = lane < SWA_HEAD_DIM
    cols = []
    for cgrp in range(2):
        qc = q[:, cgrp * LANE:(cgrp + 1) * LANE]
        halves = []
        for half in range(2):
            head = SWA_HEAD_ORDER[2 * cgrp + half]
            qh = jnp.where(lower if half == 0 else ~lower, qc, jnp.zeros_like(qc))
            sink = sink_ref[head]
            scores = []
            m = jnp.full((q.shape[0], 1), sink, F32)
            for kb, mk in zip(k_list, mask_list):
                s = _dot_nt(qh, kb)
                if mk is not None:
                    s = jnp.where(mk, s, -jnp.inf)
                scores.append(s)
                m = jnp.maximum(m, jnp.max(s, axis=-1, keepdims=True))
            l = jnp.exp(sink - m)
            acc = jnp.zeros((q.shape[0], LANE), F32)
            for s, vb in zip(scores, v_list):
                p = jnp.exp(s - m)
                l = l + jnp.sum(p, axis=-1, keepdims=True)
                acc = acc + jnp.dot(p.astype(BF16), vb, preferred_element_type=F32)
            halves.append(acc * (1.0 / l))
        cols.append(jnp.where(lower, halves[0], halves[1]))
    o_ref[...] = jnp.concatenate(cols, axis=1).astype(o_ref.dtype)


def _swa_kernel(sink_ref, q_ref, kvm_ref, kv0_ref, kvp_ref, kvc_ref, o_ref, *, n_blocks, seq):
    j = pl.program_id(1)
    kc, vc = kvc_ref[:, 0:SWA_KV], kvc_ref[:, SWA_KV:]

    @pl.when(j < n_blocks)
    def _():
        r = lax.broadcasted_iota(jnp.int32, (ATTN_BLOCK, ATTN_BLOCK), 0)
        c = lax.broadcasted_iota(jnp.int32, (ATTN_BLOCK, ATTN_BLOCK), 1)
        out_of_seq = -2 * ATTN_BLOCK
        m_prev = (r - c) <= jnp.where(j > 0, SWA_WINDOW - ATTN_BLOCK, out_of_seq)
        m_next = (c - r) <= jnp.where(j < n_blocks - 1, SWA_WINDOW - ATTN_BLOCK, out_of_seq)
        ks = [kvm_ref[:, 0:SWA_KV], kv0_ref[:, 0:SWA_KV], kvp_ref[:, 0:SWA_KV], kc]
        vs = [kvm_ref[:, SWA_KV:], kv0_ref[:, SWA_KV:], kvp_ref[:, SWA_KV:], vc]
        _swa_heads(sink_ref, q_ref[...], ks, vs, [m_prev, None, m_next, None], o_ref)

    @pl.when(j >= n_blocks)
    def _():
        _swa_heads(sink_ref, q_ref[...], [kc], [vc], [None], o_ref)


def _swa_attention(sink, qa, kva, n_batch, seq, ctx_len, with_ctx):
    nb = seq // ATTN_BLOCK
    ncb = ctx_len // ATTN_BLOCK
    n_lat = n_batch * seq
    steps = nb + (ncb if with_ctx else 0)
    n_tok = n_lat + (n_batch * ctx_len if with_ctx else 0)
    lat_blocks = n_lat // ATTN_BLOCK

    def q_map(b, j, s):
        return (jnp.where(j < nb, b * nb + j, lat_blocks + b * ncb + (j - nb)), 0)

    def kv_map(off):
        def f(b, j, s):
            jj = jnp.clip(jnp.minimum(j, nb - 1) + off, 0, nb - 1)
            return (b * nb + jj, 0)
        return f

    return pl.pallas_call(
        functools.partial(_swa_kernel, n_blocks=nb, seq=seq),
        out_shape=jax.ShapeDtypeStruct((n_tok, SWA_Q), BF16),
        grid_spec=pltpu.PrefetchScalarGridSpec(
            num_scalar_prefetch=1,
            grid=(n_batch, steps),
            in_specs=[pl.BlockSpec((ATTN_BLOCK, SWA_Q), q_map),
                      pl.BlockSpec((ATTN_BLOCK, 2 * SWA_KV), kv_map(-1)),
                      pl.BlockSpec((ATTN_BLOCK, 2 * SWA_KV), kv_map(0)),
                      pl.BlockSpec((ATTN_BLOCK, 2 * SWA_KV), kv_map(1)),
                      pl.BlockSpec((ctx_len, 2 * SWA_KV), lambda b, j, s: (n_lat // ctx_len + b, 0))],
            out_specs=pl.BlockSpec((ATTN_BLOCK, SWA_Q), q_map)),
        compiler_params=_cparams("arbitrary", "arbitrary"),
        name="swa_attention",
    )(sink, qa, kva, kva, kva, kva)


def _mla_heads(q_ref, k_refs, v_refs, o_ref):
    for h in range(MLA_HEADS):
        sl = slice(h * MLA_HEAD_PAD, (h + 1) * MLA_HEAD_PAD)
        qh = q_ref[:, sl]
        scores = [_dot_nt(qh, k_ref[:, sl]) for k_ref in k_refs]
        m = jnp.max(scores[0], axis=-1, keepdims=True)
        for s in scores[1:]:
            m = jnp.maximum(m, jnp.max(s, axis=-1, keepdims=True))
        l = jnp.zeros_like(m)
        acc = jnp.zeros((qh.shape[0], MLA_HEAD_PAD), F32)
        for s, v_ref in zip(scores, v_refs):
            p = jnp.exp(s - m)
            l = l + jnp.sum(p, axis=-1, keepdims=True)
            acc = acc + jnp.dot(p.astype(BF16), v_ref[:, sl], preferred_element_type=F32)
        o_ref[:, sl] = (acc * (1.0 / l)).astype(o_ref.dtype)


def _mla_kernel(q_ref, kl_ref, vl_ref, kc_ref, vc_ref, o_ref, *, n_lat_steps):
    j = pl.program_id(1)

    @pl.when(j < n_lat_steps)
    def _():
        _mla_heads(q_ref, [kl_ref, kc_ref], [vl_ref, vc_ref], o_ref)

    @pl.when(j >= n_lat_steps)
    def _():
        _mla_heads(q_ref, [kc_ref], [vc_ref], o_ref)


def _mla_attention(qm, km, vm, n_batch, seq, ctx_len, with_ctx):
    tq = MLA_Q_TILE
    nq = seq // tq
    ncq = ctx_len // tq
    n_lat = n_batch * seq
    steps = nq + (ncq if with_ctx else 0)
    n_tok = n_lat + (n_batch * ctx_len if with_ctx else 0)
    wide = MLA_HEADS * MLA_HEAD_PAD

    def q_map(b, j):
        return (jnp.where(j < nq, b * nq + j, n_lat // tq + b * ncq + (j - nq)), 0)

    lat_map = lambda b, j: (b, 0)
    ctx_map = lambda b, j: (n_lat // ctx_len + b, 0)
    return pl.pallas_call(
        functools.partial(_mla_kernel, n_lat_steps=nq),
        out_shape=jax.ShapeDtypeStruct((n_tok, wide), BF16),
        grid=(n_batch, steps),
        in_specs=[pl.BlockSpec((tq, wide), q_map),
                  pl.BlockSpec((seq, wide), lat_map),
                  pl.BlockSpec((seq, wide), lat_map),
                  pl.BlockSpec((ctx_len, wide), ctx_map),
                  pl.BlockSpec((ctx_len, wide), ctx_map)],
        out_specs=pl.BlockSpec((tq, wide), q_map),
        compiler_params=_cparams("arbitrary", "arbitrary"),
        name="mla_attention",
    )(qm, km, vm, km, vm)


def _outproj_kernel(oa_ref, of_ref, ob_ref, z_ref, oc_ref, x_ref, wa_ref, wb_ref, wc_ref, gg_ref, g_ref, m_ref,
                    xo_ref, h_ref):
    m = m_ref[0]
    ob = of_ref[...] + ob_ref[...]
    z = z_ref[...]
    heads = []
    for h in range(GDN_HEADS):
        oh = ob[:, h * GDN_HEAD_DIM:(h + 1) * GDN_HEAD_DIM]
        heads.append(oh * lax.rsqrt(jnp.mean(oh * oh, axis=-1, keepdims=True) + EPS))
    gated = jnp.concatenate(heads, axis=1) * gg_ref[...] * (z * jax.nn.sigmoid(z))
    o = (jnp.dot(oa_ref[...], wa_ref[...], preferred_element_type=F32)
         + jnp.dot(gated.astype(BF16), wb_ref[...], preferred_element_type=F32)
         + jnp.dot(oc_ref[...], wc_ref[...], preferred_element_type=F32))
    xn = x_ref[...] + m[2:3] * o
    xo_ref[...] = xn
    h_ref[...] = _modulated_norm(xn, g_ref[...], m[3:4], m[4:5])


def _out_projection(o_a, o_fwd, o_bwd, p_flat, o_c, x_flat, wa, wb, wc, gdn_gain, gain2, mod_l, n_tok, n_batch, seq):
    d = x_flat.shape[1]
    tm = TOK_TILE
    tps = seq // tm
    row = lambda i: (i, 0)
    const = lambda i: (0, 0)
    return pl.pallas_call(
        _outproj_kernel,
        out_shape=(jax.ShapeDtypeStruct((n_tok, d), F32), jax.ShapeDtypeStruct((n_tok, d), F32)),
        grid=(n_tok // tm,),
        in_specs=[pl.BlockSpec((tm, o_a.shape[1]), row),
                  pl.BlockSpec((tm, GDN_W), row),
                  pl.BlockSpec((tm, GDN_W), row),
                  pl.BlockSpec((tm, GDN_W), lambda i: (i, COL_G // GDN_W + 3)),
                  pl.BlockSpec((tm, o_c.shape[1]), row),
                  pl.BlockSpec((tm, d), row),
                  pl.BlockSpec(wa.shape, const),
                  pl.BlockSpec(wb.shape, const),
                  pl.BlockSpec(wc.shape, const),
                  pl.BlockSpec((1, GDN_W), const),
                  pl.BlockSpec((1, d), const),
                  pl.BlockSpec((1, N_MOD, d), lambda i: (_group_of_tile(i, tps, n_batch), 0, 0))],
        out_specs=(pl.BlockSpec((tm, d), row), pl.BlockSpec((tm, d), row)),
        compiler_params=_cparams("arbitrary"),
        name="out_projection",
    )(o_a, o_fwd, o_bwd, p_flat, o_c, x_flat, wa, wb, wc, gdn_gain, gain2.reshape(1, d), mod_l)


def _route_kernel(h_ref, rw_ref, rb_ref, idx_ref, gate_ref, cnt_ref, base_ref):
    step = pl.program_id(0)
    tm = h_ref.shape[0]

    @pl.when(step == 0)
    def _():
        base_ref[...] = jnp.zeros_like(base_ref)

    logits = lax.dot_general(rw_ref[...], h_ref[...].astype(BF16), (((1,), (1,)), ((), ())),
                             preferred_element_type=F32) + rb_ref[...]
    e_iota = lax.broadcasted_iota(jnp.int32, logits.shape, 0)
    work = logits
    tops, picks = [], []
    for _k in range(TOP_K):
        mx = jnp.max(work, axis=0, keepdims=True)
        pick = jnp.min(jnp.where(work == mx, e_iota, N_EXPERTS), axis=0, keepdims=True)
        work = jnp.where(e_iota == pick, -jnp.inf, work)
        tops.append(mx)
        picks.append(pick)
    exps = [jnp.exp(t - tops[0]) for t in tops]
    denom = exps[0] + exps[1] + exps[2] + exps[3]
    sel = jnp.zeros(logits.shape, F32)
    for pick in picks:
        sel = sel + (e_iota == pick).astype(F32)
    row = lax.broadcasted_iota(jnp.int32, (tm, tm), 0)
    col = lax.broadcasted_iota(jnp.int32, (tm, tm), 1)
    before = (row < col).astype(BF16)
    cnt = jnp.dot(sel.astype(BF16), before, preferred_element_type=F32) + base_ref[:, 0:1]
    ranks = [jnp.sum(jnp.where(e_iota == pick, cnt, 0.0), axis=0, keepdims=True) for pick in picks]
    idx_ref[0] = jnp.concatenate(picks + [r.astype(jnp.int32) for r in ranks], axis=0)
    gate_rows = jnp.concatenate([e / denom for e in exps] + [jnp.zeros((LANE - TOP_K, tm), F32)], axis=0)
    gate_ref[...] = jnp.transpose(gate_rows)
    base_ref[...] = base_ref[...] + jnp.sum(sel, axis=1, keepdims=True)
    cnt_ref[...] = base_ref[...]


def _route(h_flat, n_tok, rw_t_bf, rb):
    d = h_flat.shape[1]
    tm = ROUTE_TILE
    nt = n_tok // tm
    return pl.pallas_call(
        _route_kernel,
        out_shape=(jax.ShapeDtypeStruct((nt, 2 * TOP_K, tm), jnp.int32),
                   jax.ShapeDtypeStruct((n_tok, LANE), F32),
                   jax.ShapeDtypeStruct((N_EXPERTS, LANE), F32)),
        grid=(nt,),
        in_specs=[pl.BlockSpec((tm, d), lambda i: (i, 0)),
                  pl.BlockSpec((N_EXPERTS, d), lambda i: (0, 0)),
                  pl.BlockSpec((N_EXPERTS, 1), lambda i: (0, 0))],
        out_specs=(pl.BlockSpec((1, 2 * TOP_K, tm), lambda i: (i, 0, 0)),
                   pl.BlockSpec((tm, LANE), lambda i: (i, 0)),
                   pl.BlockSpec((N_EXPERTS, LANE), lambda i: (0, 0))),
        scratch_shapes=[pltpu.VMEM((N_EXPERTS, LANE), F32)],
        compiler_params=_cparams("arbitrary"),
        name="moe_route",
    )(h_flat, rw_t_bf, rb.reshape(N_EXPERTS, 1))


def _dispatch_kernel(slot_ref, h_ref, xs_in, xs_out, sem):
    del xs_in
    tm = slot_ref.shape[2]

    def row_copy(t, k):
        return pltpu.make_async_copy(h_ref.at[pl.ds(t, 1)], xs_out.at[pl.ds(slot_ref[0, k, t], 1)], sem)

    def issue(t, carry):
        for k in range(TOP_K):
            row_copy(t, k).start()
        return carry

    lax.fori_loop(0, tm, issue, 0)

    def drain(t, carry):
        for k in range(TOP_K):
            row_copy(t, k).wait()
        return carry

    lax.fori_loop(0, tm, drain, 0)


def _dispatch(slots, h_flat, n_tok, cap):
    d = h_flat.shape[1]
    tm = slots.shape[2]
    return pl.pallas_call(
        _dispatch_kernel,
        out_shape=jax.ShapeDtypeStruct((cap, d), F32),
        grid=(n_tok // tm,),
        in_specs=[pl.BlockSpec((1, TOP_K, tm), lambda i: (i, 0, 0), memory_space=pltpu.SMEM),
                  pl.BlockSpec((tm, d), lambda i: (i, 0)),
                  pl.BlockSpec(memory_space=pl.ANY)],
        out_specs=pl.BlockSpec(memory_space=pl.ANY),
        scratch_shapes=[pltpu.SemaphoreType.DMA(())],
        input_output_aliases={2: 0},
        compiler_params=_cparams("arbitrary"),
        name="moe_dispatch",
    )(slots, h_flat, jnp.zeros((cap, d), F32))


def _expert_kernel(be_ref, nu_ref, xs_ref, wg_ref, wl_ref, bg_ref, bl_ref, wd_ref, bd_ref, ys_ref):
    del be_ref

    @pl.when(pl.program_id(0) < nu_ref[0])
    def _():
        x = xs_ref[...].astype(BF16)
        g = jnp.dot(x, wg_ref[0], preferred_element_type=F32) + bg_ref[0]
        u = jnp.dot(x, wl_ref[0], preferred_element_type=F32) + bl_ref[0]
        g = jnp.minimum(g, SWIGLU_LIMIT)
        u = jnp.clip(u, -SWIGLU_LIMIT, SWIGLU_LIMIT)
        act = g * jax.nn.sigmoid(SWIGLU_ALPHA * g) * (u + 1.0)
        ys_ref[...] = jnp.dot(act.astype(BF16), wd_ref[0], preferred_element_type=F32) + bd_ref[0]

    @pl.when(pl.program_id(0) >= nu_ref[0])
    def _():
        ys_ref[...] = jnp.zeros_like(ys_ref)


def _experts(block_e, n_used, xs, wg, wl, bg, bl, wd, bd):
    cap, d = xs.shape
    f = wg.shape[2]
    tm = MOE_TILE
    nb = cap // tm

    def row_map(i, be, nu):
        return (jnp.minimum(i, nu[0] - 1), 0)

    def w_map(i, be, nu):
        return (be[jnp.minimum(i, nu[0] - 1)], 0, 0)

    return pl.pallas_call(
        _expert_kernel,
        out_shape=jax.ShapeDtypeStruct((cap, d), F32),
        grid_spec=pltpu.PrefetchScalarGridSpec(
            num_scalar_prefetch=2,
            grid=(nb,),
            in_specs=[pl.BlockSpec((tm, d), row_map),
                      pl.BlockSpec((1, d, f), w_map),
                      pl.BlockSpec((1, d, f), w_map),
                      pl.BlockSpec((1, 1, f), w_map),
                      pl.BlockSpec((1, 1, f), w_map),
                      pl.BlockSpec((1, f, d), w_map),
                      pl.BlockSpec((1, 1, d), w_map)],
            out_specs=pl.BlockSpec((tm, d), lambda i, be, nu: (i, 0))),
        compiler_params=_cparams("arbitrary"),
        name="moe_experts",
    )(block_e, n_used, xs, wg, wl, bg, bl, wd, bd)


def _combine_kernel(slot_ref, ys_hbm, x_ref, gate_ref, m_ref, o_ref, buf, sem):
    tm = x_ref.shape[0]

    def row_copy(t, k):
        return pltpu.make_async_copy(ys_hbm.at[pl.ds(slot_ref[0, k, t], 1)], buf.at[k, pl.ds(t, 1)], sem)

    def issue(t, carry):
        for k in range(TOP_K):
            row_copy(t, k).start()
        return carry

    lax.fori_loop(0, tm, issue, 0)

    def drain(t, carry):
        for k in range(TOP_K):
            row_copy(t, k).wait()
        return carry

    lax.fori_loop(0, tm, drain, 0)
    gates = gate_ref[...]
    y = buf[0] * gates[:, 0:1]
    for k in range(1, TOP_K):
        y = y + buf[k] * gates[:, k:k + 1]
    o_ref[...] = x_ref[...] + m_ref[0][5:6] * y


def _combine(slots, ys, x_flat, gates, mod_l, n_tok, n_batch, seq):
    d = x_flat.shape[1]
    tm = slots.shape[2]
    tps = seq // tm
    return pl.pallas_call(
        _combine_kernel,
        out_shape=jax.ShapeDtypeStruct((n_tok, d), F32),
        grid=(n_tok // tm,),
        in_specs=[pl.BlockSpec((1, TOP_K, tm), lambda i: (i, 0, 0), memory_space=pltpu.SMEM),
                  pl.BlockSpec(memory_space=pl.ANY),
                  pl.BlockSpec((tm, d), lambda i: (i, 0)),
                  pl.BlockSpec((tm, LANE), lambda i: (i, 0)),
                  pl.BlockSpec((1, N_MOD, d), lambda i: (_group_of_tile(i, tps, n_batch), 0, 0))],
        out_specs=pl.BlockSpec((tm, d), lambda i: (i, 0)),
        scratch_shapes=[pltpu.VMEM((TOP_K, tm, d), F32), pltpu.SemaphoreType.DMA(())],
        compiler_params=_cparams("arbitrary"),
        name="moe_combine",
    )(slots, ys, x_flat, gates, mod_l)


def _retile_slots(slots, tm):
    nt, k, t = slots.shape
    return slots.reshape(nt, k, t // tm, tm).transpose(0, 2, 1, 3).reshape(nt * (t // tm), k, tm)


def _moe(h_flat, x_flat, n_tok, mod_l, rw_t_bf, rb, wg, wl, bg, bl, wd, bd, n_batch, seq):
    idx, gates, counts = _route(h_flat, n_tok, rw_t_bf, rb)
    counts = counts[:, 0].astype(jnp.int32)
    padded = (counts + MOE_TILE - 1) // MOE_TILE * MOE_TILE
    pad_end = jnp.cumsum(padded)
    pad_start = pad_end - padded
    n_blocks = -(-(n_tok * TOP_K) // MOE_TILE) + N_EXPERTS
    cap = n_blocks * MOE_TILE
    block_e = jnp.minimum(jnp.searchsorted(pad_end, jnp.arange(n_blocks, dtype=jnp.int32) * MOE_TILE, side='right'),
                          N_EXPERTS - 1).astype(jnp.int32)
    n_used = (pad_end[-1:] // MOE_TILE).astype(jnp.int32)
    experts = idx[:, :TOP_K, :]
    start = jnp.sum(jnp.where(experts[..., None] == jnp.arange(N_EXPERTS), pad_start, 0), axis=-1)
    slots = start + idx[:, TOP_K:, :]
    slots = _retile_slots(slots, MOVE_TILE)
    xs = _dispatch(slots, h_flat, n_tok, cap)
    ys = _experts(block_e, n_used, xs, wg, wl, bg, bl, wd, bd)
    return _combine(slots, ys, x_flat, gates, mod_l, n_tok, n_batch, seq)


def _rms_norm(x, gain):
    xf = x.astype(F32)
    y = xf * lax.rsqrt(jnp.mean(xf * xf, axis=-1, keepdims=True) + EPS)
    return (y * gain.astype(F32)).astype(x.dtype)


def _l2_norm(x):
    xf = x.astype(F32)
    return xf * lax.rsqrt(jnp.sum(xf * xf, axis=-1, keepdims=True) + EPS)


def _short_conv(x, w):
    y = lax.conv_general_dilated(
        x, w[:, None, :].astype(x.dtype), window_strides=(1,),
        padding=[(GDN_CONV // 2, GDN_CONV // 2)],
        dimension_numbers=('NWC', 'WIO', 'NWC'), feature_group_count=x.shape[-1])
    return jax.nn.silu(y)


def _gdn_chunk_scan(q, k, v, g, beta, state):
    b, n, h, _ = q.shape
    nc = n // GDN_CHUNK

    def chunks(t):
        t = t.reshape((b, nc, GDN_CHUNK, h) + t.shape[3:])
        return jnp.moveaxis(t, 3, 1)

    qc, kc, vc = chunks(q), chunks(k), chunks(v)
    gc = jnp.cumsum(chunks(g), axis=-1)
    bc = chunks(beta)
    idx = jnp.arange(GDN_CHUNK)
    incl = idx[:, None] >= idx[None, :]
    strict = idx[:, None] > idx[None, :]
    decay = jnp.exp(jnp.where(incl, gc[..., :, None] - gc[..., None, :], -jnp.inf))
    kk = jnp.einsum('bhctd,bhcid->bhcti', kc, kc)
    a_mat = jnp.where(strict, bc[..., :, None] * kk * decay, 0.0) + jnp.eye(GDN_CHUNK, dtype=F32)
    u = lax.linalg.triangular_solve(a_mat, vc * bc[..., None], left_side=True, lower=True, unit_diagonal=True)
    w = lax.linalg.triangular_solve(a_mat, kc * (bc * jnp.exp(gc))[..., None], left_side=True, lower=True,
                                    unit_diagonal=True)
    qk = jnp.einsum('bhctd,bhcid->bhcti', qc, kc) * decay
    q_dec = qc * jnp.exp(gc)[..., None]
    k_dec = kc * jnp.exp(gc[..., -1:] - gc)[..., None]
    g_end = jnp.exp(gc[..., -1])

    def step(s_prev, xs):
        u_c, w_c, qk_c, qd_c, kd_c, ge_c = xs
        v_new = u_c - jnp.einsum('bhtk,bhkv->bhtv', w_c, s_prev)
        o_c = jnp.einsum('bhtk,bhkv->bhtv', qd_c, s_prev) + jnp.einsum('bhti,bhiv->bhtv', qk_c, v_new)
        s_next = ge_c[..., None, None] * s_prev + jnp.einsum('bhtk,bhtv->bhkv', kd_c, v_new)
        return s_next, o_c

    xs = tuple(jnp.moveaxis(t, 2, 0) for t in (u, w, qk, q_dec, k_dec, g_end))
    state, o = lax.scan(step, state, xs)
    o = jnp.moveaxis(jnp.moveaxis(o, 0, 2), 1, 3)
    return o.reshape(b, n, h, o.shape[-1]), state


def _orient(stream, d):
    q, k, v, g, beta = stream
    seq = (q, k, v, g[:, :, d], beta[:, :, d])
    if d == 1:
        return tuple(jnp.flip(t, axis=1) for t in seq)
    return seq


def _bidirectional_gdn(lat, ctx):
    b, _, h, dk = lat[0].shape
    dv = lat[2].shape[-1]
    o_lat, o_ctx = [], []
    for d in range(2):
        state0 = jnp.zeros((b, h, dk, dv), F32)
        oc, s_ctx = _gdn_chunk_scan(*_orient(ctx, d), state0)
        ol, _ = _gdn_chunk_scan(*_orient(lat, d), s_ctx)
        if d == 1:
            oc, ol = jnp.flip(oc, axis=1), jnp.flip(ol, axis=1)
        o_lat.append(ol)
        o_ctx.append(oc)
    return o_lat[0] + o_lat[1], o_ctx[0] + o_ctx[1]


def _gdn_mixer(p_flat, n_batch, seq, ctx_len, gdn_conv, gdn_a_log, gdn_dt_bias, gdn_out_norm, with_ctx):
    n_lat = n_batch * seq

    def split(rows, n):
        g4 = rows[:, COL_G:COL_G + 4 * GDN_W].reshape(n_batch, n, 4 * GDN_W)
        small = rows[:, COL_SMALL:].reshape(n_batch, n, LANE)
        q, k, v, z = jnp.split(g4, 4, axis=-1)
        return q, k, v, z, small[..., SMALL_GA:SMALL_GA + 2 * GDN_HEADS], small[..., SMALL_GB:SMALL_GB + 2 * GDN_HEADS]

    def proj(q, k, v, a, bg):
        n = q.shape[1]
        q, k, v = jnp.split(_short_conv(jnp.concatenate([q, k, v], axis=-1), gdn_conv), 3, axis=-1)
        hs = (n_batch, n, GDN_HEADS, GDN_HEAD_DIM)
        q = _l2_norm(q.reshape(hs)) * GDN_HEAD_DIM ** -0.5
        k = _l2_norm(k.reshape(hs))
        v = v.reshape(hs)
        a = a.reshape(n_batch, n, 2, GDN_HEADS)
        log_decay = -jnp.exp(gdn_a_log) * jax.nn.softplus(a + gdn_dt_bias)
        beta = jax.nn.sigmoid(bg.reshape(n_batch, n, 2, GDN_HEADS))
        return q, k, v, log_decay, beta

    def out(o, z):
        n = o.shape[1]
        gate = jax.nn.silu(z.reshape(n_batch, n, GDN_HEADS, GDN_HEAD_DIM))
        return (_rms_norm(o, gdn_out_norm) * gate).reshape(n_batch * n, GDN_W)

    q, k, v, z, a, bg = split(p_flat[:n_lat], seq)
    qc, kc, vc, zc, ac, bgc = split(p_flat[n_lat:], ctx_len)
    ob, ob_c = _bidirectional_gdn(proj(q, k, v, a, bg), proj(qc, kc, vc, ac, bgc))
    if with_ctx:
        return jnp.concatenate([out(ob, z), out(ob_c, zc)], axis=0)
    return out(ob, z)


GDN_TILE = 256
HALO = SUBLANE


def _gdn_prep_kernel(q_ref, k_ref, v_ref, qp_ref, kp_ref, vp_ref, qn_ref, kn_ref, vn_ref, small_ref, w_ref,
                     a_ref, bias_ref, qkv_ref, gb_ref, xe_ref, *, lat_tiles, lat_tps, ctx_tps):
    i = pl.program_id(0)
    tm = q_ref.shape[0]
    in_lat = i < lat_tiles
    pos = jnp.where(in_lat, i % lat_tps, (i - lat_tiles) % ctx_tps)
    last = jnp.where(in_lat, lat_tps - 1, ctx_tps - 1)
    keep_prev = jnp.where(pos > 0, 1.0, 0.0)
    keep_next = jnp.where(pos < last, 1.0, 0.0)
    half = GDN_CONV // 2
    for g, (x_ref, p_ref, n_ref) in enumerate(((q_ref, qp_ref, qn_ref), (k_ref, kp_ref, kn_ref),
                                               (v_ref, vp_ref, vn_ref))):
        xe_ref[g, 0:HALO, :] = p_ref[...] * keep_prev
        xe_ref[g, HALO:HALO + tm, :] = x_ref[...]
        xe_ref[g, HALO + tm:2 * HALO + tm, :] = n_ref[...] * keep_next
        w = w_ref[:, g * GDN_W:(g + 1) * GDN_W]
        y = xe_ref[g, HALO - half:HALO - half + tm, :] * w[0:1]
        for j in range(1, GDN_CONV):
            y = y + xe_ref[g, HALO - half + j:HALO - half + j + tm, :] * w[j:j + 1]
        y = y * jax.nn.sigmoid(y)
        if g < 2:
            heads = []
            for h in range(GDN_HEADS):
                yh = y[:, h * GDN_HEAD_DIM:(h + 1) * GDN_HEAD_DIM]
                yh = yh * lax.rsqrt(jnp.sum(yh * yh, axis=-1, keepdims=True) + EPS)
                heads.append(yh * GDN_HEAD_DIM ** -0.5 if g == 0 else yh)
            y = jnp.concatenate(heads, axis=1)
        qkv_ref[:, g * GDN_W:(g + 1) * GDN_W] = y.astype(BF16)
    sm = small_ref[...]
    z = sm + bias_ref[...]
    softplus = jnp.maximum(z, 0.0) + jnp.log(1.0 + jnp.exp(-jnp.abs(z)))
    lane = lax.broadcasted_iota(jnp.int32, (1, LANE), 1)
    is_g = (lane >= SMALL_GA) & (lane < SMALL_GB)
    is_b = (lane >= SMALL_GB) & (lane < SMALL_GB + 2 * GDN_HEADS)
    gb_ref[...] = jnp.where(is_g, a_ref[...] * softplus, jnp.where(is_b, jax.nn.sigmoid(sm), 0.0))


def _gdn_prep(p_flat, conv_w, neg_a, dt_bias, n_batch, seq, ctx_len):
    n_tok = p_flat.shape[0]
    tm = GDN_TILE
    lat_tiles = n_batch * seq // tm
    per_tile = tm // HALO
    n_halo_blocks = n_tok // HALO
    col0 = COL_G // GDN_W

    def main(g):
        return pl.BlockSpec((tm, GDN_W), lambda i: (i, col0 + g))

    def prev(g):
        return pl.BlockSpec((HALO, GDN_W), lambda i: (jnp.maximum(i * per_tile - 1, 0), col0 + g))

    def nxt(g):
        return pl.BlockSpec((HALO, GDN_W), lambda i: (jnp.minimum((i + 1) * per_tile, n_halo_blocks - 1), col0 + g))

    const = lambda i: (0, 0)
    return pl.pallas_call(
        functools.partial(_gdn_prep_kernel, lat_tiles=lat_tiles, lat_tps=seq // tm, ctx_tps=ctx_len // tm),
        out_shape=(jax.ShapeDtypeStruct((n_tok, 3 * GDN_W), BF16), jax.ShapeDtypeStruct((n_tok, LANE), F32)),
        grid=(n_tok // tm,),
        in_specs=[main(0), main(1), main(2), prev(0), prev(1), prev(2), nxt(0), nxt(1), nxt(2),
                  pl.BlockSpec((tm, LANE), lambda i: (i, COL_SMALL // LANE)),
                  pl.BlockSpec(conv_w.shape, const), pl.BlockSpec((1, LANE), const), pl.BlockSpec((1, LANE), const)],
        out_specs=(pl.BlockSpec((tm, 3 * GDN_W), lambda i: (i, 0)), pl.BlockSpec((tm, LANE), lambda i: (i, 0))),
        scratch_shapes=[pltpu.VMEM((3, tm + 2 * HALO, GDN_W), F32)],
        compiler_params=_cparams("arbitrary"),
        name="gdn_prep",
    )(*([p_flat] * 10), conv_w, neg_a, dt_bias)


GDN_SUB = 16


def _bdot(a, b):
    return jnp.dot(a.astype(BF16), b.astype(BF16), preferred_element_type=F32)


def _dot_tn(a, b):
    return lax.dot_general(a, b, (((0,), (0,)), ((), ())), preferred_element_type=F32)


def _each(f, *lists):
    return [f(*args) for args in zip(*lists)]


def _unit_tri_solve(n, rhs, same_blk, eye):
    nd = _each(lambda a: jnp.where(same_blk, a, 0.0).astype(BF16), n)
    nl = _each(lambda a: jnp.where(same_blk, 0.0, a).astype(BF16), n)
    nd2 = _each(lambda a: jnp.dot(a, a, preferred_element_type=F32).astype(BF16), nd)
    nd4 = _each(lambda a: jnp.dot(a, a, preferred_element_type=F32).astype(BF16), nd2)
    nd8 = _each(lambda a: jnp.dot(a, a, preferred_element_type=F32).astype(BF16), nd4)
    m = _each(lambda a: eye - a.astype(F32), nd)
    m = _each(lambda a, b: a + _bdot(a, b), m, nd2)
    m = _each(lambda a, b: a + _bdot(a, b), m, nd4)
    dinv = _each(lambda a, b: (a + _bdot(a, b)).astype(BF16), m, nd8)
    p = _each(lambda a, b: jnp.dot(a, b, preferred_element_type=F32).astype(BF16), dinv, nl)
    p2 = _each(lambda a: jnp.dot(a, a, preferred_element_type=F32).astype(BF16), p)
    y = _each(lambda a, b: _bdot(a, b), dinv, rhs)
    y = _each(lambda a, b: a + _bdot(b, a), y, p2)
    return _each(lambda a, b: a - _bdot(b, a), y, p)


def _gdn_scan_kernel(x0_ref, gb0_ref, x1_ref, gb1_ref, o0_ref, o1_ref, state_ref):
    assert GDN_CHUNK == 4 * GDN_SUB

    @pl.when(pl.program_id(1) == 0)
    def _():
        state_ref[...] = jnp.zeros_like(state_ref)

    n = GDN_CHUNK
    r = lax.broadcasted_iota(jnp.int32, (n, n), 0)
    c = lax.broadcasted_iota(jnp.int32, (n, n), 1)
    same_blk = (r // GDN_SUB) == (c // GDN_SUB)
    eye = (r == c).astype(F32)
    dirs = ((x0_ref, gb0_ref, o0_ref), (x1_ref, gb1_ref, o1_ref))
    probs = [(d, h) for d in range(2) for h in range(GDN_HEADS)]
    incl = [r >= c, r <= c]
    strict = [r > c, r < c]
    end_row = [n - 1, 0]

    gbv, gc_all, gc_t = [], [], []
    for d in range(2):
        g = dirs[d][1][...]
        tri = incl[d].astype(BF16)
        hi = g.astype(BF16)
        rem = g - hi.astype(F32)
        mid = rem.astype(BF16)
        lo = (rem - mid.astype(F32)).astype(BF16)
        gc = (jnp.dot(tri, hi, preferred_element_type=F32) + jnp.dot(tri, mid, preferred_element_type=F32)
              + jnp.dot(tri, lo, preferred_element_type=F32))
        gbv.append(g)
        gc_all.append(gc)
        gc_t.append(jnp.transpose(gc))

    def head_cols(ref, part, h):
        lo_col = part * GDN_W + h * GDN_HEAD_DIM
        return ref[:, lo_col:lo_col + GDN_HEAD_DIM]

    lane_g = [SMALL_GA + GDN_HEADS * d + h for d, h in probs]
    lane_b = [SMALL_GB + GDN_HEADS * d + h for d, h in probs]
    gcol = [gc_all[d][:, lg:lg + 1] for (d, h), lg in zip(probs, lane_g)]
    grow = [gc_t[d][lg:lg + 1, :] for (d, h), lg in zip(probs, lane_g)]
    gend = [gc_all[d][end_row[d]:end_row[d] + 1, lg:lg + 1] for (d, h), lg in zip(probs, lane_g)]
    beta = [gbv[d][:, lb:lb + 1] for (d, h), lb in zip(probs, lane_b)]
    q = [head_cols(dirs[d][0], 0, h) for d, h in probs]
    k = [head_cols(dirs[d][0], 1, h) for d, h in probs]
    v = [head_cols(dirs[d][0], 2, h) for d, h in probs]
    kf = _each(lambda a: a.astype(F32), k)
    dec = [jnp.where(incl[d], jnp.exp(jnp.where(incl[d], gc_ - gr_, 0.0)), 0.0)
           for (d, h), gc_, gr_ in zip(probs, gcol, grow)]
    kk = _each(_dot_nt, k, k)
    qk = _each(_dot_nt, q, k)
    nmat = [jnp.where(strict[d], b_ * kk_ * dec_, 0.0) for (d, h), b_, kk_, dec_ in zip(probs, beta, kk, dec)]
    egc = _each(jnp.exp, gcol)
    rhs = _each(lambda v_, kf_, b_, e_: jnp.concatenate([v_.astype(F32) * b_, kf_ * (b_ * e_)], axis=1).astype(BF16),
                v, kf, beta, egc)
    uw = _unit_tri_solve(nmat, rhs, same_blk, eye)
    s_prev = [state_ref[i] for i in range(len(probs))]
    s_bf = _each(lambda a: a.astype(BF16), s_prev)
    v_new = _each(lambda uw_, s_: uw_[:, :GDN_HEAD_DIM] - _bdot(uw_[:, GDN_HEAD_DIM:], s_), uw, s_bf)
    o_state = _each(lambda q_, e_, s_: _bdot(q_.astype(F32) * e_, s_), q, egc, s_bf)
    o_local = _each(lambda qk_, dec_, vn_: _bdot(qk_ * dec_, vn_), qk, dec, v_new)
    k_dec = _each(lambda kf_, ge_, gc_: (kf_ * jnp.exp(ge_ - gc_)).astype(BF16), kf, gend, gcol)
    s_add = _each(lambda kd_, vn_: _dot_tn(kd_, vn_.astype(BF16)), k_dec, v_new)
    for i, (d, h) in enumerate(probs):
        state_ref[i] = jnp.exp(gend[i]) * s_prev[i] + s_add[i]
        dirs[d][2][:, h * GDN_HEAD_DIM:(h + 1) * GDN_HEAD_DIM] = o_state[i] + o_local[i]


def _gdn_scan(qkv, gbeta, n_batch, seq, ctx_len):
    n_tok = qkv.shape[0]
    n = GDN_CHUNK
    lat_c, ctx_c = seq // n, ctx_len // n
    ctx0 = n_batch * seq // n

    def fwd(b, j):
        return (jnp.where(j < ctx_c, ctx0 + b * ctx_c + j, b * lat_c + (j - ctx_c)), 0)

    def bwd(b, j):
        return (jnp.where(j < ctx_c, ctx0 + b * ctx_c + (ctx_c - 1 - j), b * lat_c + (lat_c - 1 - (j - ctx_c))), 0)

    return pl.pallas_call(
        _gdn_scan_kernel,
        out_shape=(jax.ShapeDtypeStruct((n_tok, GDN_W), F32), jax.ShapeDtypeStruct((n_tok, GDN_W), F32)),
        grid=(n_batch, ctx_c + lat_c),
        in_specs=[pl.BlockSpec((n, 3 * GDN_W), fwd), pl.BlockSpec((n, LANE), fwd),
                  pl.BlockSpec((n, 3 * GDN_W), bwd), pl.BlockSpec((n, LANE), bwd)],
        out_specs=(pl.BlockSpec((n, GDN_W), fwd), pl.BlockSpec((n, GDN_W), bwd)),
        scratch_shapes=[pltpu.VMEM((2 * GDN_HEADS, GDN_HEAD_DIM, GDN_HEAD_DIM), F32)],
        compiler_params=_cparams("arbitrary", "arbitrary"),
        name="gdn_scan",
    )(qkv, gbeta, qkv, gbeta)


def _pad_heads(w, n_heads, width, start, take):
    lead = w.shape[:-1]
    w = w.reshape(lead + (n_heads, width))[..., start:start + take]
    w = jnp.pad(w, [(0, 0)] * len(lead) + [(0, 0), (0, MLA_HEAD_PAD - take)])
    return w.reshape(lead + (n_heads * MLA_HEAD_PAD,))


def kernel(x, c, ctx, c_ctx, w_mod, b_mod, norm1, norm2, w_in, w_out, swa_q_norm, swa_k_norm, swa_sink, gdn_conv,
           gdn_a_log, gdn_dt_bias, gdn_out_norm, mla_q_a_norm, mla_w_uq, mla_kv_a_norm, mla_w_ukv, mla_q_norm,
           mla_k_norm, router_w, router_b, exp_w_gu, exp_b_gu, exp_w_dn, exp_b_dn):
    b, s, d = x.shape
    cl = ctx.shape[1]
    depth = w_mod.shape[0]
    n_lat, n_ctx = b * s, b * cl
    assert s % TOK_TILE == 0 and n_ctx % TOK_TILE == 0 and s % MOVE_TILE == 0 and n_ctx % ROUTE_TILE == 0
    assert cl % MLA_Q_TILE == 0 and cl % ATTN_BLOCK == 0 and n_lat % cl == 0

    n_rows = -(-(b + 1) // SUBLANE) * SUBLANE
    c_rows = jnp.concatenate([c, c_ctx[None, :], jnp.zeros((n_rows - b - 1, d), F32)], axis=0)
    mod_all = _modulation(c_rows, w_mod, b_mod)

    perm = _in_proj_perm()
    tabs = _rope_tables(s)
    seg_a = _block_diag_ones(SWA_Q, SWA_HEAD_DIM)
    seg_c = _block_diag_ones(MLA_HEADS * MLA_HEAD_PAD, MLA_HEAD_PAD)
    head_rows = np.concatenate([h * SWA_HEAD_DIM + np.arange(SWA_HEAD_DIM) for h in SWA_HEAD_ORDER])

    x_flat = jnp.concatenate([x.reshape(n_lat, d), ctx.reshape(n_ctx, d)], axis=0)
    for l in range(depth):
        with_ctx = l < depth - 1
        n_tok = n_lat + n_ctx if with_ctx else n_lat
        mod_l = mod_all[l]
        w_in_bf = jnp.pad(w_in[l], ((0, 0), (0, 1)))[:, perm].astype(BF16)
        p_flat = _in_projection(x_flat, norm1[l], mod_l, w_in_bf, b, s)

        consts = (seg_a, seg_c,
                  jnp.tile(swa_q_norm[l], SWA_HEADS)[None], jnp.tile(swa_k_norm[l], SWA_KV_HEADS)[None],
                  mla_q_a_norm[l][None], mla_kv_a_norm[l][None],
                  _pad_heads(mla_w_uq[l], MLA_HEADS, MLA_QK, 0, MLA_QK).astype(BF16),
                  _pad_heads(mla_w_ukv[l], MLA_HEADS, MLA_NOPE + MLA_V, 0, MLA_NOPE).astype(BF16),
                  _pad_heads(mla_w_ukv[l], MLA_HEADS, MLA_NOPE + MLA_V, MLA_NOPE, MLA_V).astype(BF16),
                  _pad_heads(jnp.tile(mla_q_norm[l], MLA_HEADS), MLA_HEADS, MLA_QK, 0, MLA_QK)[None],
                  _pad_heads(jnp.tile(mla_k_norm[l], MLA_HEADS), MLA_HEADS, MLA_QK, 0, MLA_QK)[None])
        qa, kva, qm, km, vm = _attention_prep(p_flat, tabs, consts, b, s)
        o_a = _swa_attention(swa_sink[l], qa, kva, b, s, cl, with_ctx)
        o_c = _mla_attention(qm, km, vm, b, s, cl, with_ctx)
        lanes_g = slice(SMALL_GA, SMALL_GA + 2 * GDN_HEADS)
        neg_a = jnp.zeros((1, LANE), F32).at[0, lanes_g].set(-jnp.exp(gdn_a_log[l]).reshape(-1))
        dt_bias = jnp.zeros((1, LANE), F32).at[0, lanes_g].set(gdn_dt_bias[l].reshape(-1))
        qkv, gbeta = _gdn_prep(p_flat, gdn_conv[l], neg_a, dt_bias, b, s, cl)
        o_fwd, o_bwd = _gdn_scan(qkv, gbeta, b, s, cl)

        wa = w_out[l][:SWA_Q][head_rows].astype(BF16)
        wb = w_out[l][SWA_Q:SWA_Q + GDN_W].astype(BF16)
        wc = w_out[l][SWA_Q + GDN_W:].reshape(MLA_HEADS, MLA_V, d)
        wc = jnp.pad(wc, ((0, 0), (0, MLA_HEAD_PAD - MLA_V), (0, 0))).reshape(MLA_HEADS * MLA_HEAD_PAD, d).astype(BF16)
        x_mid, h2 = _out_projection(o_a, o_fwd, o_bwd, p_flat, o_c, x_flat, wa, wb, wc,
                                    jnp.tile(gdn_out_norm[l], GDN_HEADS)[None], norm2[l], mod_l, n_tok, b, s)

        wg = exp_w_gu[l][:, :, 0::2].astype(BF16)
        wl = exp_w_gu[l][:, :, 1::2].astype(BF16)
        bg = exp_b_gu[l][:, None, 0::2]
        bl = exp_b_gu[l][:, None, 1::2]
        wd = exp_w_dn[l].astype(BF16)
        bd = exp_b_dn[l][:, None, :]
        x_flat = _moe(h2, x_mid, n_tok, mod_l, router_w[l].T.astype(BF16), router_b[l], wg, wl, bg, bl, wd, bd, b, s)
    return x_flat[:n_lat].reshape(b, s, d)
```

```python
import functools

import jax
import jax.numpy as jnp
import numpy as np
from jax import lax
from jax.experimental import pallas as pl
from jax.experimental.pallas import tpu as pltpu

F32 = jnp.float32
BF16 = jnp.bfloat16

GRID_W = 64
N_MOD = 6
EPS = 1e-6
ROPE_BASE = 10000.0
ATTN_BLOCK = 128

SWA_HEADS = 4
SWA_KV_HEADS = 2
SWA_HEAD_DIM = 64
SWA_WINDOW = 128

GDN_HEADS = 4
GDN_HEAD_DIM = 128
GDN_CONV = 5
GDN_CHUNK = 64

MLA_HEADS = 4
MLA_Q_RANK = 256
MLA_KV_RANK = 128
MLA_NOPE = 64
MLA_ROPE = 32
MLA_V = 64
MLA_QK = MLA_NOPE + MLA_ROPE

N_EXPERTS = 32
TOP_K = 4
SWIGLU_LIMIT = 7.0
SWIGLU_ALPHA = 1.702

SWA_Q = SWA_HEADS * SWA_HEAD_DIM
SWA_KV = SWA_KV_HEADS * SWA_HEAD_DIM
GDN_W = GDN_HEADS * GDN_HEAD_DIM
MLA_O = MLA_HEADS * MLA_V
D_MIX = SWA_Q + GDN_W + MLA_O
IN_SPLITS = (SWA_Q, SWA_KV, SWA_KV, GDN_W, GDN_W, GDN_W, GDN_W, 2 * GDN_HEADS, 2 * GDN_HEADS, MLA_Q_RANK,
             MLA_KV_RANK, MLA_ROPE)
N_IN = sum(IN_SPLITS)

LANE = 128
SUBLANE = 8
VMEM_LIMIT = 56 * 1024 * 1024

TOK_TILE = 512
MOE_TILE = 512
ROUTE_TILE = 512
MOVE_TILE = 256
MLA_Q_TILE = 256

COL_AQ, COL_AK, COL_AV = 0, 256, 384
COL_G = 512
COL_CQ, COL_CKV, COL_SMALL = 2560, 2816, 2944
N_IN_PAD = 3072
SMALL_KR, SMALL_GA, SMALL_GB = 0, MLA_ROPE, MLA_ROPE + 2 * GDN_HEADS
SWA_HEAD_ORDER = (0, 2, 1, 3)
MLA_HEAD_PAD = 128


def _cparams(*sem):
    return pltpu.CompilerParams(dimension_semantics=sem, vmem_limit_bytes=VMEM_LIMIT)


def _in_proj_perm():
    old = np.cumsum((0,) + IN_SPLITS)
    o_aq, o_ak, o_av, o_gq, o_gk, o_gv, o_gz, o_ga, o_gb, o_cq, o_ckv, o_ckr = old[:-1]
    perm = np.full((N_IN_PAD,), N_IN, np.int32)
    perm[COL_AQ:COL_AQ + SWA_Q] = np.concatenate(
        [o_aq + h * SWA_HEAD_DIM + np.arange(SWA_HEAD_DIM) for h in SWA_HEAD_ORDER])
    perm[COL_AK:COL_G + 4 * GDN_W] = np.arange(o_ak, o_ga)
    perm[COL_CQ:COL_CQ + MLA_Q_RANK] = o_cq + np.arange(MLA_Q_RANK)
    perm[COL_CKV:COL_CKV + MLA_KV_RANK] = o_ckv + np.arange(MLA_KV_RANK)
    perm[COL_SMALL + SMALL_KR:COL_SMALL + SMALL_KR + MLA_ROPE] = o_ckr + np.arange(MLA_ROPE)
    perm[COL_SMALL + SMALL_GA:COL_SMALL + SMALL_GA + 2 * GDN_HEADS] = o_ga + np.arange(2 * GDN_HEADS)
    perm[COL_SMALL + SMALL_GB:COL_SMALL + SMALL_GB + 2 * GDN_HEADS] = o_gb + np.arange(2 * GDN_HEADS)
    return perm


def _mod_kernel(c_ref, w_ref, b_ref, o_ref):
    a = c_ref[...]
    a = a * jax.nn.sigmoid(a)
    o_ref[0] = jnp.dot(a.astype(BF16), w_ref[0].astype(BF16), preferred_element_type=F32) + b_ref[0]


def _modulation(c_rows, w_mod, b_mod):
    nl, d, n = w_mod.shape
    r = c_rows.shape[0]
    tn = 1536
    out = pl.pallas_call(
        _mod_kernel,
        out_shape=jax.ShapeDtypeStruct((nl, r, n), F32),
        grid=(nl, n // tn),
        in_specs=[pl.BlockSpec((r, d), lambda l, j: (0, 0)),
                  pl.BlockSpec((1, d, tn), lambda l, j: (l, 0, j)),
                  pl.BlockSpec((1, 1, tn), lambda l, j: (l, 0, j))],
        out_specs=pl.BlockSpec((1, r, tn), lambda l, j: (l, 0, j)),
        compiler_params=_cparams("arbitrary", "arbitrary"),
        name="modulation",
    )(c_rows, w_mod, b_mod.reshape(nl, 1, n))
    return out.reshape(nl, r, N_MOD, d)


def _modulated_norm(x, gain, shift, scale):
    y = x * lax.rsqrt(jnp.mean(x * x, axis=-1, keepdims=True) + EPS)
    return (y * gain) * (1.0 + scale) + shift


def _group_of_tile(i, tiles_per_seq, n_batch):
    return jnp.minimum(i // tiles_per_seq, n_batch)


def _inproj_kernel(x_ref, g_ref, m_ref, w_ref, o_ref):
    m = m_ref[0]
    h = _modulated_norm(x_ref[...], g_ref[...], m[0:1], m[1:2])
    o_ref[...] = jnp.dot(h.astype(BF16), w_ref[...], preferred_element_type=F32)


def _in_projection(x_flat, gain, mod_l, w_bf, n_batch, seq):
    n_tok, d = x_flat.shape
    n = w_bf.shape[1]
    tm = TOK_TILE
    tps = seq // tm
    return pl.pallas_call(
        _inproj_kernel,
        out_shape=jax.ShapeDtypeStruct((n_tok, n), F32),
        grid=(n_tok // tm,),
        in_specs=[pl.BlockSpec((tm, d), lambda i: (i, 0)),
                  pl.BlockSpec((1, d), lambda i: (0, 0)),
                  pl.BlockSpec((1, N_MOD, d), lambda i: (_group_of_tile(i, tps, n_batch), 0, 0)),
                  pl.BlockSpec((d, n), lambda i: (0, 0))],
        out_specs=pl.BlockSpec((tm, n), lambda i: (i, 0)),
        compiler_params=_cparams("arbitrary"),
        name="in_projection",
    )(x_flat, gain.reshape(1, d), mod_l, w_bf)


def _group_sumsq(x, seg):
    x2 = x * x
    hi = x2.astype(BF16)
    lo = (x2 - hi.astype(F32)).astype(BF16)
    return jnp.dot(hi, seg, preferred_element_type=F32) + jnp.dot(lo, seg, preferred_element_type=F32)


def _swap_pairs(x, half):
    w = x.shape[1]
    lane = lax.broadcasted_iota(jnp.int32, (1, w), 1)
    first = (lane % (2 * half)) < half
    return jnp.where(first, pltpu.roll(x, w - half, axis=1), pltpu.roll(x, half, axis=1))


def _prep_kernel(pa_ref, pc_ref, ra_c_ref, ra_s_ref, rc_c_ref, rc_s_ref, seg_a_ref, seg_c_ref,
                 gaq_ref, gak_ref, gcq_ref, gckv_ref, wuq_ref, wuk_ref, wuv_ref, gmq_ref, gmk_ref,
                 qa_ref, kva_ref, qm_ref, km_ref, vm_ref):
    aq = pa_ref[:, 0:SWA_Q]
    ak = pa_ref[:, SWA_Q:SWA_Q + SWA_KV]
    av = pa_ref[:, SWA_Q + SWA_KV:SWA_Q + 2 * SWA_KV]
    ca, sa = ra_c_ref[...], ra_s_ref[...]
    seg_a = seg_a_ref[...]
    qn = aq * lax.rsqrt(_group_sumsq(aq, seg_a) * (1.0 / SWA_HEAD_DIM) + EPS) * gaq_ref[...]
    kn = ak * lax.rsqrt(_group_sumsq(ak, seg_a[:SWA_KV, :SWA_KV]) * (1.0 / SWA_HEAD_DIM) + EPS) * gak_ref[...]
    qn = qn * jnp.concatenate([ca, ca], axis=1) + _swap_pairs(qn, SWA_HEAD_DIM // 4) * jnp.concatenate([sa, sa], axis=1)
    kn = kn * ca + _swap_pairs(kn, SWA_HEAD_DIM // 4) * sa
    qa_ref[...] = (qn * SWA_HEAD_DIM ** -0.5).astype(BF16)
    kva_ref[...] = jnp.concatenate([kn, av], axis=1).astype(BF16)

    cq = pc_ref[:, 0:MLA_Q_RANK]
    ckv = pc_ref[:, MLA_Q_RANK:MLA_Q_RANK + MLA_KV_RANK]
    small = pc_ref[:, MLA_Q_RANK + MLA_KV_RANK:]
    cqn = cq * lax.rsqrt(jnp.mean(cq * cq, axis=-1, keepdims=True) + EPS) * gcq_ref[...]
    ckvn = (ckv * lax.rsqrt(jnp.mean(ckv * ckv, axis=-1, keepdims=True) + EPS) * gckv_ref[...]).astype(BF16)
    q = jnp.dot(cqn.astype(BF16), wuq_ref[...], preferred_element_type=F32)
    k = jnp.dot(ckvn, wuk_ref[...], preferred_element_type=F32)
    v = jnp.dot(ckvn, wuv_ref[...], preferred_element_type=F32)
    lane = lax.broadcasted_iota(jnp.int32, (1, LANE), 1)
    kpe = jnp.where((lane >= MLA_NOPE) & (lane < MLA_QK), pltpu.roll(small, MLA_NOPE, axis=1), 0.0)
    k = k + jnp.concatenate([kpe] * MLA_HEADS, axis=1)
    seg_c = seg_c_ref[...]
    q = q * lax.rsqrt(_group_sumsq(q, seg_c) * (1.0 / MLA_QK) + EPS) * gmq_ref[...]
    k = k * lax.rsqrt(_group_sumsq(k, seg_c) * (1.0 / MLA_QK) + EPS) * gmk_ref[...]
    cc = jnp.concatenate([rc_c_ref[...]] * MLA_HEADS, axis=1)
    sc = jnp.concatenate([rc_s_ref[...]] * MLA_HEADS, axis=1)
    q = q * cc + _swap_pairs(q, MLA_ROPE // 4) * sc
    k = k * cc + _swap_pairs(k, MLA_ROPE // 4) * sc
    qm_ref[...] = (q * MLA_QK ** -0.5).astype(BF16)
    km_ref[...] = k.astype(BF16)
    vm_ref[...] = v.astype(BF16)


def _attention_prep(p_flat, tabs, consts, n_batch, seq):
    n_tok = p_flat.shape[0]
    tm = TOK_TILE
    tps = seq // tm
    n_lat_tiles = n_batch * tps
    row = lambda i: (i, 0)
    const = lambda i: (0, 0)
    tab = lambda i: (jnp.where(i < n_lat_tiles, i % tps, tps), 0)
    wide = MLA_HEADS * MLA_HEAD_PAD
    in_specs = [pl.BlockSpec((tm, 512), lambda i: (i, COL_AQ // 512)),
                pl.BlockSpec((tm, 512), lambda i: (i, COL_CQ // 512))]
    in_specs += [pl.BlockSpec((tm, LANE), tab)] * 4
    in_specs += [pl.BlockSpec(a.shape, const) for a in consts]
    return pl.pallas_call(
        _prep_kernel,
        out_shape=(jax.ShapeDtypeStruct((n_tok, SWA_Q), BF16), jax.ShapeDtypeStruct((n_tok, 2 * SWA_KV), BF16),
                   jax.ShapeDtypeStruct((n_tok, wide), BF16), jax.ShapeDtypeStruct((n_tok, wide), BF16),
                   jax.ShapeDtypeStruct((n_tok, wide), BF16)),
        grid=(n_tok // tm,),
        in_specs=in_specs,
        out_specs=(pl.BlockSpec((tm, SWA_Q), row), pl.BlockSpec((tm, 2 * SWA_KV), row),
                   pl.BlockSpec((tm, wide), row), pl.BlockSpec((tm, wide), row), pl.BlockSpec((tm, wide), row)),
        compiler_params=_cparams("arbitrary"),
        name="attention_prep",
    )(p_flat, p_flat, *tabs, *consts)


def _rope_tables(seq):
    t = jnp.arange(seq)
    row = (t // GRID_W).astype(F32)
    col = (t % GRID_W).astype(F32)

    def cos_sin(rot_dim):
        n_freq = rot_dim // 4
        freq = ROPE_BASE ** (-jnp.arange(n_freq, dtype=F32) / n_freq)
        ar, ac = row[:, None] * freq, col[:, None] * freq
        c = jnp.concatenate([jnp.cos(ar), jnp.cos(ar), jnp.cos(ac), jnp.cos(ac)], axis=1)
        s = jnp.concatenate([-jnp.sin(ar), jnp.sin(ar), -jnp.sin(ac), jnp.sin(ac)], axis=1)
        return c, s

    ca, sa = cos_sin(SWA_HEAD_DIM)
    ca, sa = jnp.tile(ca, (1, 2)), jnp.tile(sa, (1, 2))
    cc, sc = cos_sin(MLA_ROPE)
    ones_l = jnp.ones((seq, MLA_NOPE), F32)
    ones_r = jnp.ones((seq, MLA_HEAD_PAD - MLA_QK), F32)
    cc = jnp.concatenate([ones_l, cc, ones_r], axis=1)
    sc = jnp.concatenate([0 * ones_l, sc, 0 * ones_r], axis=1)
    ident_c = jnp.ones((TOK_TILE, LANE), F32)
    ident_s = jnp.zeros((TOK_TILE, LANE), F32)
    return tuple(jnp.concatenate([a, i], axis=0) for a, i in ((ca, ident_c), (sa, ident_s), (cc, ident_c), (sc, ident_s)))


def _block_diag_ones(n, blk):
    i = np.arange(n) // blk
    return jnp.asarray((i[:, None] == i[None, :]).astype(np.float32), BF16)


def _dot_nt(a, b):
    return lax.dot_general(a, b, (((1,), (1,)), ((), ())), preferred_element_type=F32)


def _swa_heads(sink_ref, q, k_list, v_list, mask_list, o_ref):
    lane = lax.broadcasted_iota(jnp.int32, (1, LANE), 1)
    lower = lane < SWA_HEAD_DIM
    cols = []
    for cgrp in range(2):
        qc = q[:, cgrp * LANE:(cgrp + 1) * LANE]
        halves = []
        for half in range(2):
            head = SWA_HEAD_ORDER[2 * cgrp + half]
            qh = jnp.where(lower if half == 0 else ~lower, qc, jnp.zeros_like(qc))
            sink = sink_ref[head]
            scores = []
            m = jnp.full((q.shape[0], 1), sink, F32)
            for kb, mk in zip(k_list, mask_list):
                s = _dot_nt(qh, kb)
                if mk is not None:
                    s = jnp.where(mk, s, -jnp.inf)
                scores.append(s)
                m = jnp.maximum(m, jnp.max(s, axis=-1, keepdims=True))
            l = jnp.exp(sink - m)
            acc = jnp.zeros((q.shape[0], LANE), F32)
            for s, vb in zip(scores, v_list):
                p = jnp.exp(s - m)
                l = l + jnp.sum(p, axis=-1, keepdims=True)
                acc = acc + jnp.dot(p.astype(BF16), vb, preferred_element_type=F32)
            halves.append(acc * (1.0 / l))
        cols.append(jnp.where(lower, halves[0], halves[1]))
    o_ref[...] = jnp.concatenate(cols, axis=1).astype(o_ref.dtype)


def _swa_kernel(sink_ref, q_ref, kvm_ref, kv0_ref, kvp_ref, kvc_ref, o_ref, *, n_blocks, seq):
    j = pl.program_id(1)
    kc, vc = kvc_ref[:, 0:SWA_KV], kvc_ref[:, SWA_KV:]

    @pl.when(j < n_blocks)
    def _():
        r = lax.broadcasted_iota(jnp.int32, (ATTN_BLOCK, ATTN_BLOCK), 0)
        c = lax.broadcasted_iota(jnp.int32, (ATTN_BLOCK, ATTN_BLOCK), 1)
        out_of_seq = -2 * ATTN_BLOCK
        m_prev = (r - c) <= jnp.where(j > 0, SWA_WINDOW - ATTN_BLOCK, out_of_seq)
        m_next = (c - r) <= jnp.where(j < n_blocks - 1, SWA_WINDOW - ATTN_BLOCK, out_of_seq)
        ks = [kvm_ref[:, 0:SWA_KV], kv0_ref[:, 0:SWA_KV], kvp_ref[:, 0:SWA_KV], kc]
        vs = [kvm_ref[:, SWA_KV:], kv0_ref[:, SWA_KV:], kvp_ref[:, SWA_KV:], vc]
        _swa_heads(sink_ref, q_ref[...], ks, vs, [m_prev, None, m_next, None], o_ref)

    @pl.when(j >= n_blocks)
    def _():
        _swa_heads(sink_ref, q_ref[...], [kc], [vc], [None], o_ref)


def _swa_attention(sink, qa, kva, n_batch, seq, ctx_len, with_ctx):
    nb = seq // ATTN_BLOCK
    ncb = ctx_len // ATTN_BLOCK
    n_lat = n_batch * seq
    steps = nb + (ncb if with_ctx else 0)
    n_tok = n_lat + (n_batch * ctx_len if with_ctx else 0)
    lat_blocks = n_lat // ATTN_BLOCK

    def q_map(b, j, s):
        return (jnp.where(j < nb, b * nb + j, lat_blocks + b * ncb + (j - nb)), 0)

    def kv_map(off):
        def f(b, j, s):
            jj = jnp.clip(jnp.minimum(j, nb - 1) + off, 0, nb - 1)
            return (b * nb + jj, 0)
        return f

    return pl.pallas_call(
        functools.partial(_swa_kernel, n_blocks=nb, seq=seq),
        out_shape=jax.ShapeDtypeStruct((n_tok, SWA_Q), BF16),
        grid_spec=pltpu.PrefetchScalarGridSpec(
            num_scalar_prefetch=1,
            grid=(n_batch, steps),
            in_specs=[pl.BlockSpec((ATTN_BLOCK, SWA_Q), q_map),
                      pl.BlockSpec((ATTN_BLOCK, 2 * SWA_KV), kv_map(-1)),
                      pl.BlockSpec((ATTN_BLOCK, 2 * SWA_KV), kv_map(0)),
                      pl.BlockSpec((ATTN_BLOCK, 2 * SWA_KV), kv_map(1)),
                      pl.BlockSpec((ctx_len, 2 * SWA_KV), lambda b, j, s: (n_lat // ctx_len + b, 0))],
            out_specs=pl.BlockSpec((ATTN_BLOCK, SWA_Q), q_map)),
        compiler_params=_cparams("arbitrary", "arbitrary"),
        name="swa_attention",
    )(sink, qa, kva, kva, kva, kva)


def _mla_heads(q_ref, k_refs, v_refs, o_ref):
    for h in range(MLA_HEADS):
        sl = slice(h * MLA_HEAD_PAD, (h + 1) * MLA_HEAD_PAD)
        qh = q_ref[:, sl]
        scores = [_dot_nt(qh, k_ref[:, sl]) for k_ref in k_refs]
        m = jnp.max(scores[0], axis=-1, keepdims=True)
        for s in scores[1:]:
            m = jnp.maximum(m, jnp.max(s, axis=-1, keepdims=True))
        l = jnp.zeros_like(m)
        acc = jnp.zeros((qh.shape[0], MLA_HEAD_PAD), F32)
        for s, v_ref in zip(scores, v_refs):
            p = jnp.exp(s - m)
            l = l + jnp.sum(p, axis=-1, keepdims=True)
            acc = acc + jnp.dot(p.astype(BF16), v_ref[:, sl], preferred_element_type=F32)
        o_ref[:, sl] = (acc * (1.0 / l)).astype(o_ref.dtype)


def _mla_kernel(q_ref, kl_ref, vl_ref, kc_ref, vc_ref, o_ref, *, n_lat_steps):
    j = pl.program_id(1)

    @pl.when(j < n_lat_steps)
    def _():
        _mla_heads(q_ref, [kl_ref, kc_ref], [vl_ref, vc_ref], o_ref)

    @pl.when(j >= n_lat_steps)
    def _():
        _mla_heads(q_ref, [kc_ref], [vc_ref], o_ref)


def _mla_attention(qm, km, vm, n_batch, seq, ctx_len, with_ctx):
    tq = MLA_Q_TILE
    nq = seq // tq
    ncq = ctx_len // tq
    n_lat = n_batch * seq
    steps = nq + (ncq if with_ctx else 0)
    n_tok = n_lat + (n_batch * ctx_len if with_ctx else 0)
    wide = MLA_HEADS * MLA_HEAD_PAD

    def q_map(b, j):
        return (jnp.where(j < nq, b * nq + j, n_lat // tq + b * ncq + (j - nq)), 0)

    lat_map = lambda b, j: (b, 0)
    ctx_map = lambda b, j: (n_lat // ctx_len + b, 0)
    return pl.pallas_call(
        functools.partial(_mla_kernel, n_lat_steps=nq),
        out_shape=jax.ShapeDtypeStruct((n_tok, wide), BF16),
        grid=(n_batch, steps),
        in_specs=[pl.BlockSpec((tq, wide), q_map),
                  pl.BlockSpec((seq, wide), lat_map),
                  pl.BlockSpec((seq, wide), lat_map),
                  pl.BlockSpec((ctx_len, wide), ctx_map),
                  pl.BlockSpec((ctx_len, wide), ctx_map)],
        out_specs=pl.BlockSpec((tq, wide), q_map),
        compiler_params=_cparams("arbitrary", "arbitrary"),
        name="mla_attention",
    )(qm, km, vm, km, vm)


def _outproj_kernel(oa_ref, of_ref, ob_ref, z_ref, oc_ref, x_ref, wa_ref, wb_ref, wc_ref, gg_ref, g_ref, m_ref,
                    xo_ref, h_ref):
    m = m_ref[0]
    ob = of_ref[...] + ob_ref[...]
    z = z_ref[...]
    heads = []
    for h in range(GDN_HEADS):
        oh = ob[:, h * GDN_HEAD_DIM:(h + 1) * GDN_HEAD_DIM]
        heads.append(oh * lax.rsqrt(jnp.mean(oh * oh, axis=-1, keepdims=True) + EPS))
    gated = jnp.concatenate(heads, axis=1) * gg_ref[...] * (z * jax.nn.sigmoid(z))
    o = (jnp.dot(oa_ref[...], wa_ref[...], preferred_element_type=F32)
         + jnp.dot(gated.astype(BF16), wb_ref[...], preferred_element_type=F32)
         + jnp.dot(oc_ref[...], wc_ref[...], preferred_element_type=F32))
    xn = x_ref[...] + m[2:3] * o
    xo_ref[...] = xn
    h_ref[...] = _modulated_norm(xn, g_ref[...], m[3:4], m[4:5])


def _out_projection(o_a, o_fwd, o_bwd, p_flat, o_c, x_flat, wa, wb, wc, gdn_gain, gain2, mod_l, n_tok, n_batch, seq):
    d = x_flat.shape[1]
    tm = TOK_TILE
    tps = seq // tm
    row = lambda i: (i, 0)
    const = lambda i: (0, 0)
    return pl.pallas_call(
        _outproj_kernel,
        out_shape=(jax.ShapeDtypeStruct((n_tok, d), F32), jax.ShapeDtypeStruct((n_tok, d), F32)),
        grid=(n_tok // tm,),
        in_specs=[pl.BlockSpec((tm, o_a.shape[1]), row),
                  pl.BlockSpec((tm, GDN_W), row),
                  pl.BlockSpec((tm, GDN_W), row),
                  pl.BlockSpec((tm, GDN_W), lambda i: (i, COL_G // GDN_W + 3)),
                  pl.BlockSpec((tm, o_c.shape[1]), row),
                  pl.BlockSpec((tm, d), row),
                  pl.BlockSpec(wa.shape, const),
                  pl.BlockSpec(wb.shape, const),
                  pl.BlockSpec(wc.shape, const),
                  pl.BlockSpec((1, GDN_W), const),
                  pl.BlockSpec((1, d), const),
                  pl.BlockSpec((1, N_MOD, d), lambda i: (_group_of_tile(i, tps, n_batch), 0, 0))],
        out_specs=(pl.BlockSpec((tm, d), row), pl.BlockSpec((tm, d), row)),
        compiler_params=_cparams("arbitrary"),
        name="out_projection",
    )(o_a, o_fwd, o_bwd, p_flat, o_c, x_flat, wa, wb, wc, gdn_gain, gain2.reshape(1, d), mod_l)


def _route_kernel(h_ref, rw_ref, rb_ref, idx_ref, gate_ref, cnt_ref, base_ref):
    step = pl.program_id(0)
    tm = h_ref.shape[0]

    @pl.when(step == 0)
    def _():
        base_ref[...] = jnp.zeros_like(base_ref)

    logits = lax.dot_general(rw_ref[...], h_ref[...].astype(BF16), (((1,), (1,)), ((), ())),
                             preferred_element_type=F32) + rb_ref[...]
    e_iota = lax.broadcasted_iota(jnp.int32, logits.shape, 0)
    work = logits
    tops, picks = [], []
    for _k in range(TOP_K):
        mx = jnp.max(work, axis=0, keepdims=True)
        pick = jnp.min(jnp.where(work == mx, e_iota, N_EXPERTS), axis=0, keepdims=True)
        work = jnp.where(e_iota == pick, -jnp.inf, work)
        tops.append(mx)
        picks.append(pick)
    exps = [jnp.exp(t - tops[0]) for t in tops]
    denom = exps[0] + exps[1] + exps[2] + exps[3]
    sel = jnp.zeros(logits.shape, F32)
    for pick in picks:
        sel = sel + (e_iota == pick).astype(F32)
    row = lax.broadcasted_iota(jnp.int32, (tm, tm), 0)
    col = lax.broadcasted_iota(jnp.int32, (tm, tm), 1)
    before = (row < col).astype(BF16)
    cnt = jnp.dot(sel.astype(BF16), before, preferred_element_type=F32) + base_ref[:, 0:1]
    ranks = [jnp.sum(jnp.where(e_iota == pick, cnt, 0.0), axis=0, keepdims=True) for pick in picks]
    idx_ref[0] = jnp.concatenate(picks + [r.astype(jnp.int32) for r in ranks], axis=0)
    gate_rows = jnp.concatenate([e / denom for e in exps] + [jnp.zeros((LANE - TOP_K, tm), F32)], axis=0)
    gate_ref[...] = jnp.transpose(gate_rows)
    base_ref[...] = base_ref[...] + jnp.sum(sel, axis=1, keepdims=True)
    cnt_ref[...] = base_ref[...]


def _route(h_flat, n_tok, rw_t_bf, rb):
    d = h_flat.shape[1]
    tm = ROUTE_TILE
    nt = n_tok // tm
    return pl.pallas_call(
        _route_kernel,
        out_shape=(jax.ShapeDtypeStruct((nt, 2 * TOP_K, tm), jnp.int32),
                   jax.ShapeDtypeStruct((n_tok, LANE), F32),
                   jax.ShapeDtypeStruct((N_EXPERTS, LANE), F32)),
        grid=(nt,),
        in_specs=[pl.BlockSpec((tm, d), lambda i: (i, 0)),
                  pl.BlockSpec((N_EXPERTS, d), lambda i: (0, 0)),
                  pl.BlockSpec((N_EXPERTS, 1), lambda i: (0, 0))],
        out_specs=(pl.BlockSpec((1, 2 * TOP_K, tm), lambda i: (i, 0, 0)),
                   pl.BlockSpec((tm, LANE), lambda i: (i, 0)),
                   pl.BlockSpec((N_EXPERTS, LANE), lambda i: (0, 0))),
        scratch_shapes=[pltpu.VMEM((N_EXPERTS, LANE), F32)],
        compiler_params=_cparams("arbitrary"),
        name="moe_route",
    )(h_flat, rw_t_bf, rb.reshape(N_EXPERTS, 1))


def _dispatch_kernel(slot_ref, h_ref, xs_in, xs_out, sem):
    del xs_in
    tm = slot_ref.shape[2]

    def row_copy(t, k):
        return pltpu.make_async_copy(h_ref.at[pl.ds(t, 1)], xs_out.at[pl.ds(slot_ref[0, k, t], 1)], sem)

    def issue(t, carry):
        for k in range(TOP_K):
            row_copy(t, k).start()
        return carry

    lax.fori_loop(0, tm, issue, 0)

    def drain(t, carry):
        for k in range(TOP_K):
            row_copy(t, k).wait()
        return carry

    lax.fori_loop(0, tm, drain, 0)


def _dispatch(slots, h_flat, n_tok, cap):
    d = h_flat.shape[1]
    tm = slots.shape[2]
    return pl.pallas_call(
        _dispatch_kernel,
        out_shape=jax.ShapeDtypeStruct((cap, d), F32),
        grid=(n_tok // tm,),
        in_specs=[pl.BlockSpec((1, TOP_K, tm), lambda i: (i, 0, 0), memory_space=pltpu.SMEM),
                  pl.BlockSpec((tm, d), lambda i: (i, 0)),
                  pl.BlockSpec(memory_space=pl.ANY)],
        out_specs=pl.BlockSpec(memory_space=pl.ANY),
        scratch_shapes=[pltpu.SemaphoreType.DMA(())],
        input_output_aliases={2: 0},
        compiler_params=_cparams("arbitrary"),
        name="moe_dispatch",
    )(slots, h_flat, jnp.zeros((cap, d), F32))


SPLIT_COLS = 1024


def _split_kernel(w_ref, sel_ref, wg_ref, wl_ref):
    for j in range(w_ref.shape[2] // (2 * LANE)):
        blk = w_ref[0, :, 2 * LANE * j:2 * LANE * (j + 1)].astype(BF16)
        r = jnp.dot(blk, sel_ref[...], preferred_element_type=F32)
        wg_ref[0, :, LANE * j:LANE * (j + 1)] = r[:, :LANE].astype(BF16)
        wl_ref[0, :, LANE * j:LANE * (j + 1)] = r[:, LANE:].astype(BF16)


def _split_glu_columns(w_gu):
    nl, ne, d, f2 = w_gu.shape
    src = np.arange(2 * LANE)
    dst = np.where(src % 2 == 0, src // 2, LANE + src // 2)
    sel = jnp.asarray((dst[:, None] == np.arange(2 * LANE)[None, :]).astype(np.float32), BF16)
    tn = SPLIT_COLS
    out = jax.ShapeDtypeStruct((nl * ne, d, f2 // 2), BF16)
    wg, wl = pl.pallas_call(
        _split_kernel,
        out_shape=(out, out),
        grid=(nl * ne, f2 // tn),
        in_specs=[pl.BlockSpec((1, d, tn), lambda e, j: (e, 0, j)),
                  pl.BlockSpec((2 * LANE, 2 * LANE), lambda e, j: (0, 0))],
        out_specs=(pl.BlockSpec((1, d, tn // 2), lambda e, j: (e, 0, j)),
                   pl.BlockSpec((1, d, tn // 2), lambda e, j: (e, 0, j))),
        compiler_params=_cparams("arbitrary", "arbitrary"),
        name="split_glu_columns",
    )(w_gu.reshape(nl * ne, d, f2), sel)
    return wg.reshape(nl, ne, d, f2 // 2), wl.reshape(nl, ne, d, f2 // 2)


def _expert_kernel(be_ref, nu_ref, xs_ref, wg_ref, wl_ref, bg_ref, bl_ref, wd_ref, bd_ref, ys_ref):
    del be_ref

    @pl.when(pl.program_id(0) < nu_ref[0])
    def _():
        x = xs_ref[...].astype(BF16)
        g = jnp.dot(x, wg_ref[0], preferred_element_type=F32) + bg_ref[0]
        u = jnp.dot(x, wl_ref[0], preferred_element_type=F32) + bl_ref[0]
        g = jnp.minimum(g, SWIGLU_LIMIT)
        u = jnp.clip(u, -SWIGLU_LIMIT, SWIGLU_LIMIT)
        act = g * jax.nn.sigmoid(SWIGLU_ALPHA * g) * (u + 1.0)
        ys_ref[...] = jnp.dot(act.astype(BF16), wd_ref[0], preferred_element_type=F32) + bd_ref[0]

    @pl.when(pl.program_id(0) >= nu_ref[0])
    def _():
        ys_ref[...] = jnp.zeros_like(ys_ref)


def _experts(block_e, n_used, xs, wg, wl, bg, bl, wd, bd):
    cap, d = xs.shape
    f = wg.shape[2]
    tm = MOE_TILE
    nb = cap // tm

    def row_map(i, be, nu):
        return (jnp.minimum(i, nu[0] - 1), 0)

    def w_map(i, be, nu):
        return (be[jnp.minimum(i, nu[0] - 1)], 0, 0)

    return pl.pallas_call(
        _expert_kernel,
        out_shape=jax.ShapeDtypeStruct((cap, d), F32),
        grid_spec=pltpu.PrefetchScalarGridSpec(
            num_scalar_prefetch=2,
            grid=(nb,),
            in_specs=[pl.BlockSpec((tm, d), row_map),
                      pl.BlockSpec((1, d, f), w_map),
                      pl.BlockSpec((1, d, f), w_map),
                      pl.BlockSpec((1, 1, f), w_map),
                      pl.BlockSpec((1, 1, f), w_map),
                      pl.BlockSpec((1, f, d), w_map),
                      pl.BlockSpec((1, 1, d), w_map)],
            out_specs=pl.BlockSpec((tm, d), lambda i, be, nu: (i, 0))),
        compiler_params=_cparams("arbitrary"),
        name="moe_experts",
    )(block_e, n_used, xs, wg, wl, bg, bl, wd, bd)


def _combine_kernel(slot_ref, ys_hbm, x_ref, gate_ref, m_ref, o_ref, buf, sem):
    tm = x_ref.shape[0]

    def row_copy(t, k):
        return pltpu.make_async_copy(ys_hbm.at[pl.ds(slot_ref[0, k, t], 1)], buf.at[k, pl.ds(t, 1)], sem)

    def issue(t, carry):
        for k in range(TOP_K):
            row_copy(t, k).start()
        return carry

    lax.fori_loop(0, tm, issue, 0)

    def drain(t, carry):
        for k in range(TOP_K):
            row_copy(t, k).wait()
        return carry

    lax.fori_loop(0, tm, drain, 0)
    gates = gate_ref[...]
    y = buf[0] * gates[:, 0:1]
    for k in range(1, TOP_K):
        y = y + buf[k] * gates[:, k:k + 1]
    o_ref[...] = x_ref[...] + m_ref[0][5:6] * y


def _combine(slots, ys, x_flat, gates, mod_l, n_tok, n_batch, seq):
    d = x_flat.shape[1]
    tm = slots.shape[2]
    tps = seq // tm
    return pl.pallas_call(
        _combine_kernel,
        out_shape=jax.ShapeDtypeStruct((n_tok, d), F32),
        grid=(n_tok // tm,),
        in_specs=[pl.BlockSpec((1, TOP_K, tm), lambda i: (i, 0, 0), memory_space=pltpu.SMEM),
                  pl.BlockSpec(memory_space=pl.ANY),
                  pl.BlockSpec((tm, d), lambda i: (i, 0)),
                  pl.BlockSpec((tm, LANE), lambda i: (i, 0)),
                  pl.BlockSpec((1, N_MOD, d), lambda i: (_group_of_tile(i, tps, n_batch), 0, 0))],
        out_specs=pl.BlockSpec((tm, d), lambda i: (i, 0)),
        scratch_shapes=[pltpu.VMEM((TOP_K, tm, d), F32), pltpu.SemaphoreType.DMA(())],
        compiler_params=_cparams("arbitrary"),
        name="moe_combine",
    )(slots, ys, x_flat, gates, mod_l)


def _retile_slots(slots, tm):
    nt, k, t = slots.shape
    return slots.reshape(nt, k, t // tm, tm).transpose(0, 2, 1, 3).reshape(nt * (t // tm), k, tm)


def _moe(h_flat, x_flat, n_tok, mod_l, rw_t_bf, rb, wg, wl, bg, bl, wd, bd, n_batch, seq):
    idx, gates, counts = _route(h_flat, n_tok, rw_t_bf, rb)
    counts = counts[:, 0].astype(jnp.int32)
    padded = (counts + MOE_TILE - 1) // MOE_TILE * MOE_TILE
    pad_end = jnp.cumsum(padded)
    pad_start = pad_end - padded
    n_blocks = -(-(n_tok * TOP_K) // MOE_TILE) + N_EXPERTS
    cap = n_blocks * MOE_TILE
    block_row = jnp.arange(n_blocks, dtype=jnp.int32) * MOE_TILE
    block_e = jnp.minimum(jnp.sum(block_row[:, None] >= pad_end[None, :], axis=1), N_EXPERTS - 1).astype(jnp.int32)
    n_used = (pad_end[-1:] // MOE_TILE).astype(jnp.int32)
    experts = idx[:, :TOP_K, :]
    start = jnp.sum(jnp.where(experts[..., None] == jnp.arange(N_EXPERTS), pad_start, 0), axis=-1)
    slots = start + idx[:, TOP_K:, :]
    slots = _retile_slots(slots, MOVE_TILE)
    xs = _dispatch(slots, h_flat, n_tok, cap)
    ys = _experts(block_e, n_used, xs, wg, wl, bg, bl, wd, bd)
    return _combine(slots, ys, x_flat, gates, mod_l, n_tok, n_batch, seq)


def _rms_norm(x, gain):
    xf = x.astype(F32)
    y = xf * lax.rsqrt(jnp.mean(xf * xf, axis=-1, keepdims=True) + EPS)
    return (y * gain.astype(F32)).astype(x.dtype)


def _l2_norm(x):
    xf = x.astype(F32)
    return xf * lax.rsqrt(jnp.sum(xf * xf, axis=-1, keepdims=True) + EPS)


def _short_conv(x, w):
    y = lax.conv_general_dilated(
        x, w[:, None, :].astype(x.dtype), window_strides=(1,),
        padding=[(GDN_CONV // 2, GDN_CONV // 2)],
        dimension_numbers=('NWC', 'WIO', 'NWC'), feature_group_count=x.shape[-1])
    return jax.nn.silu(y)


def _gdn_chunk_scan(q, k, v, g, beta, state):
    b, n, h, _ = q.shape
    nc = n // GDN_CHUNK

    def chunks(t):
        t = t.reshape((b, nc, GDN_CHUNK, h) + t.shape[3:])
        return jnp.moveaxis(t, 3, 1)

    qc, kc, vc = chunks(q), chunks(k), chunks(v)
    gc = jnp.cumsum(chunks(g), axis=-1)
    bc = chunks(beta)
    idx = jnp.arange(GDN_CHUNK)
    incl = idx[:, None] >= idx[None, :]
    strict = idx[:, None] > idx[None, :]
    decay = jnp.exp(jnp.where(incl, gc[..., :, None] - gc[..., None, :], -jnp.inf))
    kk = jnp.einsum('bhctd,bhcid->bhcti', kc, kc)
    a_mat = jnp.where(strict, bc[..., :, None] * kk * decay, 0.0) + jnp.eye(GDN_CHUNK, dtype=F32)
    u = lax.linalg.triangular_solve(a_mat, vc * bc[..., None], left_side=True, lower=True, unit_diagonal=True)
    w = lax.linalg.triangular_solve(a_mat, kc * (bc * jnp.exp(gc))[..., None], left_side=True, lower=True,
                                    unit_diagonal=True)
    qk = jnp.einsum('bhctd,bhcid->bhcti', qc, kc) * decay
    q_dec = qc * jnp.exp(gc)[..., None]
    k_dec = kc * jnp.exp(gc[..., -1:] - gc)[..., None]
    g_end = jnp.exp(gc[..., -1])

    def step(s_prev, xs):
        u_c, w_c, qk_c, qd_c, kd_c, ge_c = xs
        v_new = u_c - jnp.einsum('bhtk,bhkv->bhtv', w_c, s_prev)
        o_c = jnp.einsum('bhtk,bhkv->bhtv', qd_c, s_prev) + jnp.einsum('bhti,bhiv->bhtv', qk_c, v_new)
        s_next = ge_c[..., None, None] * s_prev + jnp.einsum('bhtk,bhtv->bhkv', kd_c, v_new)
        return s_next, o_c

    xs = tuple(jnp.moveaxis(t, 2, 0) for t in (u, w, qk, q_dec, k_dec, g_end))
    state, o = lax.scan(step, state, xs)
    o = jnp.moveaxis(jnp.moveaxis(o, 0, 2), 1, 3)
    return o.reshape(b, n, h, o.shape[-1]), state


def _orient(stream, d):
    q, k, v, g, beta = stream
    seq = (q, k, v, g[:, :, d], beta[:, :, d])
    if d == 1:
        return tuple(jnp.flip(t, axis=1) for t in seq)
    return seq


def _bidirectional_gdn(lat, ctx):
    b, _, h, dk = lat[0].shape
    dv = lat[2].shape[-1]
    o_lat, o_ctx = [], []
    for d in range(2):
        state0 = jnp.zeros((b, h, dk, dv), F32)
        oc, s_ctx = _gdn_chunk_scan(*_orient(ctx, d), state0)
        ol, _ = _gdn_chunk_scan(*_orient(lat, d), s_ctx)
        if d == 1:
            oc, ol = jnp.flip(oc, axis=1), jnp.flip(ol, axis=1)
        o_lat.append(ol)
        o_ctx.append(oc)
    return o_lat[0] + o_lat[1], o_ctx[0] + o_ctx[1]


def _gdn_mixer(p_flat, n_batch, seq, ctx_len, gdn_conv, gdn_a_log, gdn_dt_bias, gdn_out_norm, with_ctx):
    n_lat = n_batch * seq

    def split(rows, n):
        g4 = rows[:, COL_G:COL_G + 4 * GDN_W].reshape(n_batch, n, 4 * GDN_W)
        small = rows[:, COL_SMALL:].reshape(n_batch, n, LANE)
        q, k, v, z = jnp.split(g4, 4, axis=-1)
        return q, k, v, z, small[..., SMALL_GA:SMALL_GA + 2 * GDN_HEADS], small[..., SMALL_GB:SMALL_GB + 2 * GDN_HEADS]

    def proj(q, k, v, a, bg):
        n = q.shape[1]
        q, k, v = jnp.split(_short_conv(jnp.concatenate([q, k, v], axis=-1), gdn_conv), 3, axis=-1)
        hs = (n_batch, n, GDN_HEADS, GDN_HEAD_DIM)
        q = _l2_norm(q.reshape(hs)) * GDN_HEAD_DIM ** -0.5
        k = _l2_norm(k.reshape(hs))
        v = v.reshape(hs)
        a = a.reshape(n_batch, n, 2, GDN_HEADS)
        log_decay = -jnp.exp(gdn_a_log) * jax.nn.softplus(a + gdn_dt_bias)
        beta = jax.nn.sigmoid(bg.reshape(n_batch, n, 2, GDN_HEADS))
        return q, k, v, log_decay, beta

    def out(o, z):
        n = o.shape[1]
        gate = jax.nn.silu(z.reshape(n_batch, n, GDN_HEADS, GDN_HEAD_DIM))
        return (_rms_norm(o, gdn_out_norm) * gate).reshape(n_batch * n, GDN_W)

    q, k, v, z, a, bg = split(p_flat[:n_lat], seq)
    qc, kc, vc, zc, ac, bgc = split(p_flat[n_lat:], ctx_len)
    ob, ob_c = _bidirectional_gdn(proj(q, k, v, a, bg), proj(qc, kc, vc, ac, bgc))
    if with_ctx:
        return jnp.concatenate([out(ob, z), out(ob_c, zc)], axis=0)
    return out(ob, z)


GDN_TILE = 256
HALO = SUBLANE


def _gdn_prep_kernel(q_ref, k_ref, v_ref, qp_ref, kp_ref, vp_ref, qn_ref, kn_ref, vn_ref, small_ref, w_ref,
                     a_ref, bias_ref, qkv_ref, gb_ref, xe_ref, *, lat_tiles, lat_tps, ctx_tps):
    i = pl.program_id(0)
    tm = q_ref.shape[0]
    in_lat = i < lat_tiles
    pos = jnp.where(in_lat, i % lat_tps, (i - lat_tiles) % ctx_tps)
    last = jnp.where(in_lat, lat_tps - 1, ctx_tps - 1)
    keep_prev = jnp.where(pos > 0, 1.0, 0.0)
    keep_next = jnp.where(pos < last, 1.0, 0.0)
    half = GDN_CONV // 2
    for g, (x_ref, p_ref, n_ref) in enumerate(((q_ref, qp_ref, qn_ref), (k_ref, kp_ref, kn_ref),
                                               (v_ref, vp_ref, vn_ref))):
        xe_ref[g, 0:HALO, :] = p_ref[...] * keep_prev
        xe_ref[g, HALO:HALO + tm, :] = x_ref[...]
        xe_ref[g, HALO + tm:2 * HALO + tm, :] = n_ref[...] * keep_next
        w = w_ref[:, g * GDN_W:(g + 1) * GDN_W]
        y = xe_ref[g, HALO - half:HALO - half + tm, :] * w[0:1]
        for j in range(1, GDN_CONV):
            y = y + xe_ref[g, HALO - half + j:HALO - half + j + tm, :] * w[j:j + 1]
        y = y * jax.nn.sigmoid(y)
        if g < 2:
            heads = []
            for h in range(GDN_HEADS):
                yh = y[:, h * GDN_HEAD_DIM:(h + 1) * GDN_HEAD_DIM]
                yh = yh * lax.rsqrt(jnp.sum(yh * yh, axis=-1, keepdims=True) + EPS)
                heads.append(yh * GDN_HEAD_DIM ** -0.5 if g == 0 else yh)
            y = jnp.concatenate(heads, axis=1)
        qkv_ref[:, g * GDN_W:(g + 1) * GDN_W] = y.astype(BF16)
    sm = small_ref[...]
    z = sm + bias_ref[...]
    softplus = jnp.maximum(z, 0.0) + jnp.log(1.0 + jnp.exp(-jnp.abs(z)))
    lane = lax.broadcasted_iota(jnp.int32, (1, LANE), 1)
    is_g = (lane >= SMALL_GA) & (lane < SMALL_GB)
    is_b = (lane >= SMALL_GB) & (lane < SMALL_GB + 2 * GDN_HEADS)
    gb_ref[...] = jnp.where(is_g, a_ref[...] * softplus, jnp.where(is_b, jax.nn.sigmoid(sm), 0.0))


def _gdn_prep(p_flat, conv_w, neg_a, dt_bias, n_batch, seq, ctx_len):
    n_tok = p_flat.shape[0]
    tm = GDN_TILE
    lat_tiles = n_batch * seq // tm
    per_tile = tm // HALO
    n_halo_blocks = n_tok // HALO
    col0 = COL_G // GDN_W

    def main(g):
        return pl.BlockSpec((tm, GDN_W), lambda i: (i, col0 + g))

    def prev(g):
        return pl.BlockSpec((HALO, GDN_W), lambda i: (jnp.maximum(i * per_tile - 1, 0), col0 + g))

    def nxt(g):
        return pl.BlockSpec((HALO, GDN_W), lambda i: (jnp.minimum((i + 1) * per_tile, n_halo_blocks - 1), col0 + g))

    const = lambda i: (0, 0)
    return pl.pallas_call(
        functools.partial(_gdn_prep_kernel, lat_tiles=lat_tiles, lat_tps=seq // tm, ctx_tps=ctx_len // tm),
        out_shape=(jax.ShapeDtypeStruct((n_tok, 3 * GDN_W), BF16), jax.ShapeDtypeStruct((n_tok, LANE), F32)),
        grid=(n_tok // tm,),
        in_specs=[main(0), main(1), main(2), prev(0), prev(1), prev(2), nxt(0), nxt(1), nxt(2),
                  pl.BlockSpec((tm, LANE), lambda i: (i, COL_SMALL // LANE)),
                  pl.BlockSpec(conv_w.shape, const), pl.BlockSpec((1, LANE), const), pl.BlockSpec((1, LANE), const)],
        out_specs=(pl.BlockSpec((tm, 3 * GDN_W), lambda i: (i, 0)), pl.BlockSpec((tm, LANE), lambda i: (i, 0))),
        scratch_shapes=[pltpu.VMEM((3, tm + 2 * HALO, GDN_W), F32)],
        compiler_params=_cparams("arbitrary"),
        name="gdn_prep",
    )(*([p_flat] * 10), conv_w, neg_a, dt_bias)


GDN_SUB = 16


def _bdot(a, b):
    return jnp.dot(a.astype(BF16), b.astype(BF16), preferred_element_type=F32)


def _dot_tn(a, b):
    return lax.dot_general(a, b, (((0,), (0,)), ((), ())), preferred_element_type=F32)


def _each(f, *lists):
    return [f(*args) for args in zip(*lists)]


def _unit_tri_solve(n, rhs, same_blk, eye):
    nd = _each(lambda a: jnp.where(same_blk, a, 0.0).astype(BF16), n)
    nl = _each(lambda a: jnp.where(same_blk, 0.0, a).astype(BF16), n)
    nd2 = _each(lambda a: jnp.dot(a, a, preferred_element_type=F32).astype(BF16), nd)
    nd4 = _each(lambda a: jnp.dot(a, a, preferred_element_type=F32).astype(BF16), nd2)
    nd8 = _each(lambda a: jnp.dot(a, a, preferred_element_type=F32).astype(BF16), nd4)
    m = _each(lambda a: eye - a.astype(F32), nd)
    m = _each(lambda a, b: a + _bdot(a, b), m, nd2)
    m = _each(lambda a, b: a + _bdot(a, b), m, nd4)
    dinv = _each(lambda a, b: (a + _bdot(a, b)).astype(BF16), m, nd8)
    p = _each(lambda a, b: jnp.dot(a, b, preferred_element_type=F32).astype(BF16), dinv, nl)
    p2 = _each(lambda a: jnp.dot(a, a, preferred_element_type=F32).astype(BF16), p)
    y = _each(lambda a, b: _bdot(a, b), dinv, rhs)
    y = _each(lambda a, b: a + _bdot(b, a), y, p2)
    return _each(lambda a, b: a - _bdot(b, a), y, p)


def _gdn_scan_kernel(x0_ref, gb0_ref, x1_ref, gb1_ref, o0_ref, o1_ref, state_ref):
    assert GDN_CHUNK == 4 * GDN_SUB

    @pl.when(pl.program_id(1) == 0)
    def _():
        state_ref[...] = jnp.zeros_like(state_ref)

    n = GDN_CHUNK
    r = lax.broadcasted_iota(jnp.int32, (n, n), 0)
    c = lax.broadcasted_iota(jnp.int32, (n, n), 1)
    same_blk = (r // GDN_SUB) == (c // GDN_SUB)
    eye = (r == c).astype(F32)
    dirs = ((x0_ref, gb0_ref, o0_ref), (x1_ref, gb1_ref, o1_ref))
    probs = [(d, h) for d in range(2) for h in range(GDN_HEADS)]
    incl = [r >= c, r <= c]
    strict = [r > c, r < c]
    end_row = [n - 1, 0]

    gbv, gc_all, gc_t = [], [], []
    for d in range(2):
        g = dirs[d][1][...]
        tri = incl[d].astype(BF16)
        hi = g.astype(BF16)
        rem = g - hi.astype(F32)
        mid = rem.astype(BF16)
        lo = (rem - mid.astype(F32)).astype(BF16)
        gc = (jnp.dot(tri, hi, preferred_element_type=F32) + jnp.dot(tri, mid, preferred_element_type=F32)
              + jnp.dot(tri, lo, preferred_element_type=F32))
        gbv.append(g)
        gc_all.append(gc)
        gc_t.append(jnp.transpose(gc))

    def head_cols(ref, part, h):
        lo_col = part * GDN_W + h * GDN_HEAD_DIM
        return ref[:, lo_col:lo_col + GDN_HEAD_DIM]

    lane_g = [SMALL_GA + GDN_HEADS * d + h for d, h in probs]
    lane_b = [SMALL_GB + GDN_HEADS * d + h for d, h in probs]
    gcol = [gc_all[d][:, lg:lg + 1] for (d, h), lg in zip(probs, lane_g)]
    grow = [gc_t[d][lg:lg + 1, :] for (d, h), lg in zip(probs, lane_g)]
    gend = [gc_all[d][end_row[d]:end_row[d] + 1, lg:lg + 1] for (d, h), lg in zip(probs, lane_g)]
    beta = [gbv[d][:, lb:lb + 1] for (d, h), lb in zip(probs, lane_b)]
    q = [head_cols(dirs[d][0], 0, h) for d, h in probs]
    k = [head_cols(dirs[d][0], 1, h) for d, h in probs]
    v = [head_cols(dirs[d][0], 2, h) for d, h in probs]
    kf = _each(lambda a: a.astype(F32), k)
    dec = [jnp.where(incl[d], jnp.exp(jnp.where(incl[d], gc_ - gr_, 0.0)), 0.0)
           for (d, h), gc_, gr_ in zip(probs, gcol, grow)]
    qk_kk = _each(lambda q_, k_: _dot_nt(jnp.concatenate([q_, k_], axis=0), k_), q, k)
    qk = [a[:n] for a in qk_kk]
    kk = [a[n:] for a in qk_kk]
    nmat = [jnp.where(strict[d], b_ * kk_ * dec_, 0.0) for (d, h), b_, kk_, dec_ in zip(probs, beta, kk, dec)]
    egc = _each(jnp.exp, gcol)
    rhs = _each(lambda v_, kf_, b_, e_: jnp.concatenate([v_.astype(F32) * b_, kf_ * (b_ * e_)], axis=1).astype(BF16),
                v, kf, beta, egc)
    uw = _unit_tri_solve(nmat, rhs, same_blk, eye)
    s_prev = [state_ref[i] for i in range(len(probs))]
    s_bf = _each(lambda a: a.astype(BF16), s_prev)
    ws_qs = _each(lambda uw_, q_, e_, s_: _bdot(jnp.concatenate([uw_[:, GDN_HEAD_DIM:], q_.astype(F32) * e_], axis=0), s_),
                  uw, q, egc, s_bf)
    v_new = _each(lambda uw_, a: uw_[:, :GDN_HEAD_DIM] - a[:n], uw, ws_qs)
    o_state = [a[n:] for a in ws_qs]
    o_local = _each(lambda qk_, dec_, vn_: _bdot(qk_ * dec_, vn_), qk, dec, v_new)
    k_dec = _each(lambda kf_, ge_, gc_: (kf_ * jnp.exp(ge_ - gc_)).astype(BF16), kf, gend, gcol)
    s_add = _each(lambda kd_, vn_: _dot_tn(kd_, vn_.astype(BF16)), k_dec, v_new)
    for i, (d, h) in enumerate(probs):
        state_ref[i] = jnp.exp(gend[i]) * s_prev[i] + s_add[i]
        dirs[d][2][:, h * GDN_HEAD_DIM:(h + 1) * GDN_HEAD_DIM] = o_state[i] + o_local[i]


def _gdn_scan(qkv, gbeta, n_batch, seq, ctx_len):
    n_tok = qkv.shape[0]
    n = GDN_CHUNK
    lat_c, ctx_c = seq // n, ctx_len // n
    ctx0 = n_batch * seq // n

    def fwd(b, j):
        return (jnp.where(j < ctx_c, ctx0 + b * ctx_c + j, b * lat_c + (j - ctx_c)), 0)

    def bwd(b, j):
        return (jnp.where(j < ctx_c, ctx0 + b * ctx_c + (ctx_c - 1 - j), b * lat_c + (lat_c - 1 - (j - ctx_c))), 0)

    return pl.pallas_call(
        _gdn_scan_kernel,
        out_shape=(jax.ShapeDtypeStruct((n_tok, GDN_W), F32), jax.ShapeDtypeStruct((n_tok, GDN_W), F32)),
        grid=(n_batch, ctx_c + lat_c),
        in_specs=[pl.BlockSpec((n, 3 * GDN_W), fwd), pl.BlockSpec((n, LANE), fwd),
                  pl.BlockSpec((n, 3 * GDN_W), bwd), pl.BlockSpec((n, LANE), bwd)],
        out_specs=(pl.BlockSpec((n, GDN_W), fwd), pl.BlockSpec((n, GDN_W), bwd)),
        scratch_shapes=[pltpu.VMEM((2 * GDN_HEADS, GDN_HEAD_DIM, GDN_HEAD_DIM), F32)],
        compiler_params=_cparams("arbitrary", "arbitrary"),
        name="gdn_scan",
    )(qkv, gbeta, qkv, gbeta)


def _pad_heads(w, n_heads, width, start, take):
    lead = w.shape[:-1]
    w = w.reshape(lead + (n_heads, width))[..., start:start + take]
    w = jnp.pad(w, [(0, 0)] * len(lead) + [(0, 0), (0, MLA_HEAD_PAD - take)])
    return w.reshape(lead + (n_heads * MLA_HEAD_PAD,))


def kernel(x, c, ctx, c_ctx, w_mod, b_mod, norm1, norm2, w_in, w_out, swa_q_norm, swa_k_norm, swa_sink, gdn_conv,
           gdn_a_log, gdn_dt_bias, gdn_out_norm, mla_q_a_norm, mla_w_uq, mla_kv_a_norm, mla_w_ukv, mla_q_norm,
           mla_k_norm, router_w, router_b, exp_w_gu, exp_b_gu, exp_w_dn, exp_b_dn):
    b, s, d = x.shape
    cl = ctx.shape[1]
    depth = w_mod.shape[0]
    n_lat, n_ctx = b * s, b * cl
    assert s % TOK_TILE == 0 and n_ctx % TOK_TILE == 0 and s % MOVE_TILE == 0 and n_ctx % ROUTE_TILE == 0
    assert cl % MLA_Q_TILE == 0 and cl % ATTN_BLOCK == 0 and n_lat % cl == 0

    n_rows = -(-(b + 1) // SUBLANE) * SUBLANE
    c_rows = jnp.concatenate([c, c_ctx[None, :], jnp.zeros((n_rows - b - 1, d), F32)], axis=0)
    mod_all = _modulation(c_rows, w_mod, b_mod)

    perm = _in_proj_perm()
    tabs = _rope_tables(s)
    seg_a = _block_diag_ones(SWA_Q, SWA_HEAD_DIM)
    seg_c = _block_diag_ones(MLA_HEADS * MLA_HEAD_PAD, MLA_HEAD_PAD)
    head_rows = np.concatenate([h * SWA_HEAD_DIM + np.arange(SWA_HEAD_DIM) for h in SWA_HEAD_ORDER])

    wg_all, wl_all = _split_glu_columns(exp_w_gu)
    x_flat = jnp.concatenate([x.reshape(n_lat, d), ctx.reshape(n_ctx, d)], axis=0)
    for l in range(depth):
        with_ctx = l < depth - 1
        n_tok = n_lat + n_ctx if with_ctx else n_lat
        mod_l = mod_all[l]
        w_in_bf = jnp.pad(w_in[l], ((0, 0), (0, 1)))[:, perm].astype(BF16)
        p_flat = _in_projection(x_flat, norm1[l], mod_l, w_in_bf, b, s)

        consts = (seg_a, seg_c,
                  jnp.tile(swa_q_norm[l], SWA_HEADS)[None], jnp.tile(swa_k_norm[l], SWA_KV_HEADS)[None],
                  mla_q_a_norm[l][None], mla_kv_a_norm[l][None],
                  _pad_heads(mla_w_uq[l], MLA_HEADS, MLA_QK, 0, MLA_QK).astype(BF16),
                  _pad_heads(mla_w_ukv[l], MLA_HEADS, MLA_NOPE + MLA_V, 0, MLA_NOPE).astype(BF16),
                  _pad_heads(mla_w_ukv[l], MLA_HEADS, MLA_NOPE + MLA_V, MLA_NOPE, MLA_V).astype(BF16),
                  _pad_heads(jnp.tile(mla_q_norm[l], MLA_HEADS), MLA_HEADS, MLA_QK, 0, MLA_QK)[None],
                  _pad_heads(jnp.tile(mla_k_norm[l], MLA_HEADS), MLA_HEADS, MLA_QK, 0, MLA_QK)[None])
        qa, kva, qm, km, vm = _attention_prep(p_flat, tabs, consts, b, s)
        o_a = _swa_attention(swa_sink[l], qa, kva, b, s, cl, with_ctx)
        o_c = _mla_attention(qm, km, vm, b, s, cl, with_ctx)
        lanes_g = slice(SMALL_GA, SMALL_GA + 2 * GDN_HEADS)
        neg_a = jnp.zeros((1, LANE), F32).at[0, lanes_g].set(-jnp.exp(gdn_a_log[l]).reshape(-1))
        dt_bias = jnp.zeros((1, LANE), F32).at[0, lanes_g].set(gdn_dt_bias[l].reshape(-1))
        qkv, gbeta = _gdn_prep(p_flat, gdn_conv[l], neg_a, dt_bias, b, s, cl)
        o_fwd, o_bwd = _gdn_scan(qkv, gbeta, b, s, cl)

        wa = w_out[l][:SWA_Q][head_rows].astype(BF16)
        wb = w_out[l][SWA_Q:SWA_Q + GDN_W].astype(BF16)
        wc = w_out[l][SWA_Q + GDN_W:].reshape(MLA_HEADS, MLA_V, d)
        wc = jnp.pad(wc, ((0, 0), (0, MLA_HEAD_PAD - MLA_V), (0, 0))).reshape(MLA_HEADS * MLA_HEAD_PAD, d).astype(BF16)
        x_mid, h2 = _out_projection(o_a, o_fwd, o_bwd, p_flat, o_c, x_flat, wa, wb, wc,
                                    jnp.tile(gdn_out_norm[l], GDN_HEADS)[None], norm2[l], mod_l, n_tok, b, s)

        wg, wl = wg_all[l], wl_all[l]
        bg = exp_b_gu[l][:, None, 0::2]
        bl = exp_b_gu[l][:, None, 1::2]
        wd = exp_w_dn[l].astype(BF16)
        bd = exp_b_dn[l][:, None, :]
        x_flat = _moe(h2, x_mid, n_tok, mod_l, router_w[l].T.astype(BF16), router_b[l], wg, wl, bg, bl, wd, bd, b, s)
    return x_flat[:n_lat].reshape(b, s, d)
```

```python
import functools

import jax
import jax.numpy as jnp
import numpy as np
from jax import lax
from jax.experimental import pallas as pl
from jax.experimental.pallas import tpu as pltpu

F32 = jnp.float32
BF16 = jnp.bfloat16

GRID_W = 64
N_MOD = 6
EPS = 1e-6
ROPE_BASE = 10000.0
ATTN_BLOCK = 128

SWA_HEADS = 4
SWA_KV_HEADS = 2
SWA_HEAD_DIM = 64
SWA_WINDOW = 128

GDN_HEADS = 4
GDN_HEAD_DIM = 128
GDN_CONV = 5
GDN_CHUNK = 64

MLA_HEADS = 4
MLA_Q_RANK = 256
MLA_KV_RANK = 128
MLA_NOPE = 64
MLA_ROPE = 32
MLA_V = 64
MLA_QK = MLA_NOPE + MLA_ROPE

N_EXPERTS = 32
TOP_K = 4
SWIGLU_LIMIT = 7.0
SWIGLU_ALPHA = 1.702

SWA_Q = SWA_HEADS * SWA_HEAD_DIM
SWA_KV = SWA_KV_HEADS * SWA_HEAD_DIM
GDN_W = GDN_HEADS * GDN_HEAD_DIM
MLA_O = MLA_HEADS * MLA_V
D_MIX = SWA_Q + GDN_W + MLA_O
IN_SPLITS = (SWA_Q, SWA_KV, SWA_KV, GDN_W, GDN_W, GDN_W, GDN_W, 2 * GDN_HEADS, 2 * GDN_HEADS, MLA_Q_RANK,
             MLA_KV_RANK, MLA_ROPE)
N_IN = sum(IN_SPLITS)

LANE = 128
SUBLANE = 8
VMEM_LIMIT = 56 * 1024 * 1024

TOK_TILE = 512
MOE_TILE = 512
ROUTE_TILE = 512
MOVE_TILE = 256
ROW_UNROLL = 4
MLA_Q_TILE = 256

COL_AQ, COL_AK, COL_AV = 0, 256, 384
COL_G = 512
COL_CQ, COL_CKV, COL_SMALL = 2560, 2816, 2944
N_IN_PAD = 3072
SMALL_KR, SMALL_GA, SMALL_GB = 0, MLA_ROPE, MLA_ROPE + 2 * GDN_HEADS
SWA_HEAD_ORDER = (0, 2, 1, 3)
MLA_HEAD_PAD = 128


def _cparams(*sem):
    return pltpu.CompilerParams(dimension_semantics=sem, vmem_limit_bytes=VMEM_LIMIT)


def _in_proj_perm():
    old = np.cumsum((0,) + IN_SPLITS)
    o_aq, o_ak, o_av, o_gq, o_gk, o_gv, o_gz, o_ga, o_gb, o_cq, o_ckv, o_ckr = old[:-1]
    perm = np.full((N_IN_PAD,), N_IN, np.int32)
    perm[COL_AQ:COL_AQ + SWA_Q] = np.concatenate(
        [o_aq + h * SWA_HEAD_DIM + np.arange(SWA_HEAD_DIM) for h in SWA_HEAD_ORDER])
    perm[COL_AK:COL_G + 4 * GDN_W] = np.arange(o_ak, o_ga)
    perm[COL_CQ:COL_CQ + MLA_Q_RANK] = o_cq + np.arange(MLA_Q_RANK)
    perm[COL_CKV:COL_CKV + MLA_KV_RANK] = o_ckv + np.arange(MLA_KV_RANK)
    perm[COL_SMALL + SMALL_KR:COL_SMALL + SMALL_KR + MLA_ROPE] = o_ckr + np.arange(MLA_ROPE)
    perm[COL_SMALL + SMALL_GA:COL_SMALL + SMALL_GA + 2 * GDN_HEADS] = o_ga + np.arange(2 * GDN_HEADS)
    perm[COL_SMALL + SMALL_GB:COL_SMALL + SMALL_GB + 2 * GDN_HEADS] = o_gb + np.arange(2 * GDN_HEADS)
    return perm


def _mod_kernel(c_ref, w_ref, b_ref, o_ref):
    a = c_ref[...]
    a = a * jax.nn.sigmoid(a)
    o_ref[0] = jnp.dot(a.astype(BF16), w_ref[0].astype(BF16), preferred_element_type=F32) + b_ref[0]


def _modulation(c_rows, w_mod, b_mod):
    nl, d, n = w_mod.shape
    r = c_rows.shape[0]
    tn = 1536
    out = pl.pallas_call(
        _mod_kernel,
        out_shape=jax.ShapeDtypeStruct((nl, r, n), F32),
        grid=(nl, n // tn),
        in_specs=[pl.BlockSpec((r, d), lambda l, j: (0, 0)),
                  pl.BlockSpec((1, d, tn), lambda l, j: (l, 0, j)),
                  pl.BlockSpec((1, 1, tn), lambda l, j: (l, 0, j))],
        out_specs=pl.BlockSpec((1, r, tn), lambda l, j: (l, 0, j)),
        compiler_params=_cparams("arbitrary", "arbitrary"),
        name="modulation",
    )(c_rows, w_mod, b_mod.reshape(nl, 1, n))
    return out.reshape(nl, r, N_MOD, d)


def _modulated_norm(x, gain, shift, scale):
    y = x * lax.rsqrt(jnp.mean(x * x, axis=-1, keepdims=True) + EPS)
    return (y * gain) * (1.0 + scale) + shift


def _group_of_tile(i, tiles_per_seq, n_batch):
    return jnp.minimum(i // tiles_per_seq, n_batch)


def _inproj_kernel(x_ref, g_ref, m_ref, w_ref, o_ref):
    m = m_ref[0]
    h = _modulated_norm(x_ref[...], g_ref[...], m[0:1], m[1:2])
    o_ref[...] = jnp.dot(h.astype(BF16), w_ref[...], preferred_element_type=F32)


def _in_projection(x_flat, gain, mod_l, w_bf, n_batch, seq):
    n_tok, d = x_flat.shape
    n = w_bf.shape[1]
    tm = TOK_TILE
    tps = seq // tm
    return pl.pallas_call(
        _inproj_kernel,
        out_shape=jax.ShapeDtypeStruct((n_tok, n), F32),
        grid=(n_tok // tm,),
        in_specs=[pl.BlockSpec((tm, d), lambda i: (i, 0)),
                  pl.BlockSpec((1, d), lambda i: (0, 0)),
                  pl.BlockSpec((1, N_MOD, d), lambda i: (_group_of_tile(i, tps, n_batch), 0, 0)),
                  pl.BlockSpec((d, n), lambda i: (0, 0))],
        out_specs=pl.BlockSpec((tm, n), lambda i: (i, 0)),
        compiler_params=_cparams("arbitrary"),
        name="in_projection",
    )(x_flat, gain.reshape(1, d), mod_l, w_bf)


def _group_sumsq(x, seg):
    x2 = x * x
    hi = x2.astype(BF16)
    lo = (x2 - hi.astype(F32)).astype(BF16)
    return jnp.dot(hi, seg, preferred_element_type=F32) + jnp.dot(lo, seg, preferred_element_type=F32)


def _swap_pairs(x, half):
    w = x.shape[1]
    lane = lax.broadcasted_iota(jnp.int32, (1, w), 1)
    first = (lane % (2 * half)) < half
    return jnp.where(first, pltpu.roll(x, w - half, axis=1), pltpu.roll(x, half, axis=1))


def _prep_kernel(pa_ref, pc_ref, ra_c_ref, ra_s_ref, rc_c_ref, rc_s_ref, seg_a_ref, seg_c_ref,
                 gaq_ref, gak_ref, gcq_ref, gckv_ref, wuq_ref, wuk_ref, wuv_ref, gmq_ref, gmk_ref,
                 qa_ref, kva_ref, qm_ref, km_ref, vm_ref):
    aq = pa_ref[:, 0:SWA_Q]
    ak = pa_ref[:, SWA_Q:SWA_Q + SWA_KV]
    av = pa_ref[:, SWA_Q + SWA_KV:SWA_Q + 2 * SWA_KV]
    ca, sa = ra_c_ref[...], ra_s_ref[...]
    seg_a = seg_a_ref[...]
    qn = aq * lax.rsqrt(_group_sumsq(aq, seg_a) * (1.0 / SWA_HEAD_DIM) + EPS) * gaq_ref[...]
    kn = ak * lax.rsqrt(_group_sumsq(ak, seg_a[:SWA_KV, :SWA_KV]) * (1.0 / SWA_HEAD_DIM) + EPS) * gak_ref[...]
    qn = qn * jnp.concatenate([ca, ca], axis=1) + _swap_pairs(qn, SWA_HEAD_DIM // 4) * jnp.concatenate([sa, sa], axis=1)
    kn = kn * ca + _swap_pairs(kn, SWA_HEAD_DIM // 4) * sa
    qa_ref[...] = (qn * SWA_HEAD_DIM ** -0.5).astype(BF16)
    kva_ref[...] = jnp.concatenate([kn, av], axis=1).astype(BF16)

    cq = pc_ref[:, 0:MLA_Q_RANK]
    ckv = pc_ref[:, MLA_Q_RANK:MLA_Q_RANK + MLA_KV_RANK]
    small = pc_ref[:, MLA_Q_RANK + MLA_KV_RANK:]
    cqn = cq * lax.rsqrt(jnp.mean(cq * cq, axis=-1, keepdims=True) + EPS) * gcq_ref[...]
    ckvn = (ckv * lax.rsqrt(jnp.mean(ckv * ckv, axis=-1, keepdims=True) + EPS) * gckv_ref[...]).astype(BF16)
    q = jnp.dot(cqn.astype(BF16), wuq_ref[...], preferred_element_type=F32)
    k = jnp.dot(ckvn, wuk_ref[...], preferred_element_type=F32)
    v = jnp.dot(ckvn, wuv_ref[...], preferred_element_type=F32)
    lane = lax.broadcasted_iota(jnp.int32, (1, LANE), 1)
    kpe = jnp.where((lane >= MLA_NOPE) & (lane < MLA_QK), pltpu.roll(small, MLA_NOPE, axis=1), 0.0)
    k = k + jnp.concatenate([kpe] * MLA_HEADS, axis=1)
    seg_c = seg_c_ref[...]
    q = q * lax.rsqrt(_group_sumsq(q, seg_c) * (1.0 / MLA_QK) + EPS) * gmq_ref[...]
    k = k * lax.rsqrt(_group_sumsq(k, seg_c) * (1.0 / MLA_QK) + EPS) * gmk_ref[...]
    cc = jnp.concatenate([rc_c_ref[...]] * MLA_HEADS, axis=1)
    sc = jnp.concatenate([rc_s_ref[...]] * MLA_HEADS, axis=1)
    q = q * cc + _swap_pairs(q, MLA_ROPE // 4) * sc
    k = k * cc + _swap_pairs(k, MLA_ROPE // 4) * sc
    qm_ref[...] = (q * MLA_QK ** -0.5).astype(BF16)
    km_ref[...] = k.astype(BF16)
    vm_ref[...] = v.astype(BF16)


def _attention_prep(p_flat, tabs, consts, n_batch, seq):
    n_tok = p_flat.shape[0]
    tm = TOK_TILE
    tps = seq // tm
    n_lat_tiles = n_batch * tps
    row = lambda i: (i, 0)
    const = lambda i: (0, 0)
    tab = lambda i: (jnp.where(i < n_lat_tiles, i % tps, tps), 0)
    wide = MLA_HEADS * MLA_HEAD_PAD
    in_specs = [pl.BlockSpec((tm, 512), lambda i: (i, COL_AQ // 512)),
                pl.BlockSpec((tm, 512), lambda i: (i, COL_CQ // 512))]
    in_specs += [pl.BlockSpec((tm, LANE), tab)] * 4
    in_specs += [pl.BlockSpec(a.shape, const) for a in consts]
    return pl.pallas_call(
        _prep_kernel,
        out_shape=(jax.ShapeDtypeStruct((n_tok, SWA_Q), BF16), jax.ShapeDtypeStruct((n_tok, 2 * SWA_KV), BF16),
                   jax.ShapeDtypeStruct((n_tok, wide), BF16), jax.ShapeDtypeStruct((n_tok, wide), BF16),
                   jax.ShapeDtypeStruct((n_tok, wide), BF16)),
        grid=(n_tok // tm,),
        in_specs=in_specs,
        out_specs=(pl.BlockSpec((tm, SWA_Q), row), pl.BlockSpec((tm, 2 * SWA_KV), row),
                   pl.BlockSpec((tm, wide), row), pl.BlockSpec((tm, wide), row), pl.BlockSpec((tm, wide), row)),
        compiler_params=_cparams("arbitrary"),
        name="attention_prep",
    )(p_flat, p_flat, *tabs, *consts)


def _rope_tables(seq):
    t = jnp.arange(seq)
    row = (t // GRID_W).astype(F32)
    col = (t % GRID_W).astype(F32)

    def cos_sin(rot_dim):
        n_freq = rot_dim // 4
        freq = ROPE_BASE ** (-jnp.arange(n_freq, dtype=F32) / n_freq)
        ar, ac = row[:, None] * freq, col[:, None] * freq
        c = jnp.concatenate([jnp.cos(ar), jnp.cos(ar), jnp.cos(ac), jnp.cos(ac)], axis=1)
        s = jnp.concatenate([-jnp.sin(ar), jnp.sin(ar), -jnp.sin(ac), jnp.sin(ac)], axis=1)
        return c, s

    ca, sa = cos_sin(SWA_HEAD_DIM)
    ca, sa = jnp.tile(ca, (1, 2)), jnp.tile(sa, (1, 2))
    cc, sc = cos_sin(MLA_ROPE)
    ones_l = jnp.ones((seq, MLA_NOPE), F32)
    ones_r = jnp.ones((seq, MLA_HEAD_PAD - MLA_QK), F32)
    cc = jnp.concatenate([ones_l, cc, ones_r], axis=1)
    sc = jnp.concatenate([0 * ones_l, sc, 0 * ones_r], axis=1)
    ident_c = jnp.ones((TOK_TILE, LANE), F32)
    ident_s = jnp.zeros((TOK_TILE, LANE), F32)
    return tuple(jnp.concatenate([a, i], axis=0) for a, i in ((ca, ident_c), (sa, ident_s), (cc, ident_c), (sc, ident_s)))


def _block_diag_ones(n, blk):
    i = np.arange(n) // blk
    return jnp.asarray((i[:, None] == i[None, :]).astype(np.float32), BF16)


def _dot_nt(a, b):
    return lax.dot_general(a, b, (((1,), (1,)), ((), ())), preferred_element_type=F32)


def _swa_heads(sink_ref, q, k_list, v_list, mask_list, o_ref):
    lane = lax.broadcasted_iota(jnp.int32, (1, LANE), 1)
    lower = lane < SWA_HEAD_DIM
    cols = []
    for cgrp in range(2):
        qc = q[:, cgrp * LANE:(cgrp + 1) * LANE]
        halves = []
        for half in range(2):
            head = SWA_HEAD_ORDER[2 * cgrp + half]
            qh = jnp.where(lower if half == 0 else ~lower, qc, jnp.zeros_like(qc))
            sink = sink_ref[head]
            scores = []
            m = jnp.full((q.shape[0], 1), sink, F32)
            for kb, mk in zip(k_list, mask_list):
                s = _dot_nt(qh, kb)
                if mk is not None:
                    s = jnp.where(mk, s, -jnp.inf)
                scores.append(s)
                m = jnp.maximum(m, jnp.max(s, axis=-1, keepdims=True))
            l = jnp.exp(sink - m)
            acc = jnp.zeros((q.shape[0], LANE), F32)
            for s, vb in zip(scores, v_list):
                p = jnp.exp(s - m)
                l = l + jnp.sum(p, axis=-1, keepdims=True)
                acc = acc + jnp.dot(p.astype(BF16), vb, preferred_element_type=F32)
            halves.append(acc * (1.0 / l))
        cols.append(jnp.where(lower, halves[0], halves[1]))
    o_ref[...] = jnp.concatenate(cols, axis=1).astype(o_ref.dtype)


SWA_Q_BLOCKS = 2
SWA_KEY_OFFSETS = tuple(range(-1, SWA_Q_BLOCKS + 1))


def _swa_kernel(sink_ref, q_ref, *refs, n_blocks):
    kv_refs, kvc_ref, o_ref = refs[:len(SWA_KEY_OFFSETS)], refs[-2], refs[-1]
    j = pl.program_id(1)
    n_steps = n_blocks // SWA_Q_BLOCKS
    kc, vc = kvc_ref[:, 0:SWA_KV], kvc_ref[:, SWA_KV:]

    @pl.when(j < n_steps)
    def _():
        rows = SWA_Q_BLOCKS * ATTN_BLOCK
        r = lax.broadcasted_iota(jnp.int32, (rows, ATTN_BLOCK), 0)
        c = lax.broadcasted_iota(jnp.int32, (rows, ATTN_BLOCK), 1)
        masks = []
        for off in SWA_KEY_OFFSETS:
            blk = j * SWA_Q_BLOCKS + off
            in_seq = (blk >= 0) & (blk < n_blocks)
            masks.append(jnp.abs(off * ATTN_BLOCK + c - r) <= jnp.where(in_seq, SWA_WINDOW, -1))
        ks = [ref[:, 0:SWA_KV] for ref in kv_refs] + [kc]
        vs = [ref[:, SWA_KV:] for ref in kv_refs] + [vc]
        _swa_heads(sink_ref, q_ref[...], ks, vs, masks + [None], o_ref)

    @pl.when(j >= n_steps)
    def _():
        _swa_heads(sink_ref, q_ref[...], [kc], [vc], [None], o_ref)


def _swa_attention(sink, qa, kva, n_batch, seq, ctx_len, with_ctx):
    rows = SWA_Q_BLOCKS * ATTN_BLOCK
    nb = seq // ATTN_BLOCK
    nq = seq // rows
    ncq = ctx_len // rows
    n_lat = n_batch * seq
    steps = nq + (ncq if with_ctx else 0)
    n_tok = n_lat + (n_batch * ctx_len if with_ctx else 0)

    def q_map(b, j, s):
        return (jnp.where(j < nq, b * nq + j, n_lat // rows + b * ncq + (j - nq)), 0)

    def kv_map(off):
        def f(b, j, s):
            jj = jnp.clip(jnp.minimum(j, nq - 1) * SWA_Q_BLOCKS + off, 0, nb - 1)
            return (b * nb + jj, 0)
        return f

    return pl.pallas_call(
        functools.partial(_swa_kernel, n_blocks=nb),
        out_shape=jax.ShapeDtypeStruct((n_tok, SWA_Q), BF16),
        grid_spec=pltpu.PrefetchScalarGridSpec(
            num_scalar_prefetch=1,
            grid=(n_batch, steps),
            in_specs=[pl.BlockSpec((rows, SWA_Q), q_map)]
            + [pl.BlockSpec((ATTN_BLOCK, 2 * SWA_KV), kv_map(off)) for off in SWA_KEY_OFFSETS]
            + [pl.BlockSpec((ctx_len, 2 * SWA_KV), lambda b, j, s: (n_lat // ctx_len + b, 0))],
            out_specs=pl.BlockSpec((rows, SWA_Q), q_map)),
        compiler_params=_cparams("arbitrary", "arbitrary"),
        name="swa_attention",
    )(sink, qa, *([kva] * len(SWA_KEY_OFFSETS)), kva)


def _mla_heads(q_ref, k_refs, v_refs, o_ref):
    for h in range(MLA_HEADS):
        sl = slice(h * MLA_HEAD_PAD, (h + 1) * MLA_HEAD_PAD)
        qh = q_ref[:, sl]
        scores = [_dot_nt(qh, k_ref[:, sl]) for k_ref in k_refs]
        m = jnp.max(scores[0], axis=-1, keepdims=True)
        for s in scores[1:]:
            m = jnp.maximum(m, jnp.max(s, axis=-1, keepdims=True))
        l = jnp.zeros_like(m)
        acc = jnp.zeros((qh.shape[0], MLA_HEAD_PAD), F32)
        for s, v_ref in zip(scores, v_refs):
            p = jnp.exp(s - m)
            l = l + jnp.sum(p, axis=-1, keepdims=True)
            acc = acc + jnp.dot(p.astype(BF16), v_ref[:, sl], preferred_element_type=F32)
        o_ref[:, sl] = (acc * (1.0 / l)).astype(o_ref.dtype)


def _mla_kernel(q_ref, kl_ref, vl_ref, kc_ref, vc_ref, o_ref, *, n_lat_steps):
    j = pl.program_id(1)

    @pl.when(j < n_lat_steps)
    def _():
        _mla_heads(q_ref, [kl_ref, kc_ref], [vl_ref, vc_ref], o_ref)

    @pl.when(j >= n_lat_steps)
    def _():
        _mla_heads(q_ref, [kc_ref], [vc_ref], o_ref)


def _mla_attention(qm, km, vm, n_batch, seq, ctx_len, with_ctx):
    tq = MLA_Q_TILE
    nq = seq // tq
    ncq = ctx_len // tq
    n_lat = n_batch * seq
    steps = nq + (ncq if with_ctx else 0)
    n_tok = n_lat + (n_batch * ctx_len if with_ctx else 0)
    wide = MLA_HEADS * MLA_HEAD_PAD

    def q_map(b, j):
        return (jnp.where(j < nq, b * nq + j, n_lat // tq + b * ncq + (j - nq)), 0)

    lat_map = lambda b, j: (b, 0)
    ctx_map = lambda b, j: (n_lat // ctx_len + b, 0)
    return pl.pallas_call(
        functools.partial(_mla_kernel, n_lat_steps=nq),
        out_shape=jax.ShapeDtypeStruct((n_tok, wide), BF16),
        grid=(n_batch, steps),
        in_specs=[pl.BlockSpec((tq, wide), q_map),
                  pl.BlockSpec((seq, wide), lat_map),
                  pl.BlockSpec((seq, wide), lat_map),
                  pl.BlockSpec((ctx_len, wide), ctx_map),
                  pl.BlockSpec((ctx_len, wide), ctx_map)],
        out_specs=pl.BlockSpec((tq, wide), q_map),
        compiler_params=_cparams("arbitrary", "arbitrary"),
        name="mla_attention",
    )(qm, km, vm, km, vm)


def _outproj_kernel(oa_ref, of_ref, ob_ref, z_ref, oc_ref, x_ref, wa_ref, wb_ref, wc_ref, gg_ref, g_ref, m_ref,
                    xo_ref, h_ref):
    m = m_ref[0]
    ob = of_ref[...] + ob_ref[...]
    z = z_ref[...]
    heads = []
    for h in range(GDN_HEADS):
        oh = ob[:, h * GDN_HEAD_DIM:(h + 1) * GDN_HEAD_DIM]
        heads.append(oh * lax.rsqrt(jnp.mean(oh * oh, axis=-1, keepdims=True) + EPS))
    gated = jnp.concatenate(heads, axis=1) * gg_ref[...] * (z * jax.nn.sigmoid(z))
    o = (jnp.dot(oa_ref[...], wa_ref[...], preferred_element_type=F32)
         + jnp.dot(gated.astype(BF16), wb_ref[...], preferred_element_type=F32)
         + jnp.dot(oc_ref[...], wc_ref[...], preferred_element_type=F32))
    xn = x_ref[...] + m[2:3] * o
    xo_ref[...] = xn
    h_ref[...] = _modulated_norm(xn, g_ref[...], m[3:4], m[4:5])


def _out_projection(o_a, o_fwd, o_bwd, p_flat, o_c, x_flat, wa, wb, wc, gdn_gain, gain2, mod_l, n_tok, n_batch, seq):
    d = x_flat.shape[1]
    tm = TOK_TILE
    tps = seq // tm
    row = lambda i: (i, 0)
    const = lambda i: (0, 0)
    return pl.pallas_call(
        _outproj_kernel,
        out_shape=(jax.ShapeDtypeStruct((n_tok, d), F32), jax.ShapeDtypeStruct((n_tok, d), F32)),
        grid=(n_tok // tm,),
        in_specs=[pl.BlockSpec((tm, o_a.shape[1]), row),
                  pl.BlockSpec((tm, GDN_W), row),
                  pl.BlockSpec((tm, GDN_W), row),
                  pl.BlockSpec((tm, GDN_W), lambda i: (i, COL_G // GDN_W + 3)),
                  pl.BlockSpec((tm, o_c.shape[1]), row),
                  pl.BlockSpec((tm, d), row),
                  pl.BlockSpec(wa.shape, const),
                  pl.BlockSpec(wb.shape, const),
                  pl.BlockSpec(wc.shape, const),
                  pl.BlockSpec((1, GDN_W), const),
                  pl.BlockSpec((1, d), const),
                  pl.BlockSpec((1, N_MOD, d), lambda i: (_group_of_tile(i, tps, n_batch), 0, 0))],
        out_specs=(pl.BlockSpec((tm, d), row), pl.BlockSpec((tm, d), row)),
        compiler_params=_cparams("arbitrary"),
        name="out_projection",
    )(o_a, o_fwd, o_bwd, p_flat, o_c, x_flat, wa, wb, wc, gdn_gain, gain2.reshape(1, d), mod_l)


def _route_kernel(h_ref, rw_ref, rb_ref, idx_ref, gate_ref, cnt_ref, base_ref):
    step = pl.program_id(0)
    tm = h_ref.shape[0]

    @pl.when(step == 0)
    def _():
        base_ref[...] = jnp.zeros_like(base_ref)

    logits = lax.dot_general(rw_ref[...], h_ref[...].astype(BF16), (((1,), (1,)), ((), ())),
                             preferred_element_type=F32) + rb_ref[...]
    e_iota = lax.broadcasted_iota(jnp.int32, logits.shape, 0)
    work = logits
    tops, picks = [], []
    for _k in range(TOP_K):
        mx = jnp.max(work, axis=0, keepdims=True)
        pick = jnp.min(jnp.where(work == mx, e_iota, N_EXPERTS), axis=0, keepdims=True)
        work = jnp.where(e_iota == pick, -jnp.inf, work)
        tops.append(mx)
        picks.append(pick)
    exps = [jnp.exp(t - tops[0]) for t in tops]
    denom = exps[0] + exps[1] + exps[2] + exps[3]
    sel = jnp.zeros(logits.shape, F32)
    for pick in picks:
        sel = sel + (e_iota == pick).astype(F32)
    row = lax.broadcasted_iota(jnp.int32, (tm, tm), 0)
    col = lax.broadcasted_iota(jnp.int32, (tm, tm), 1)
    before = (row < col).astype(BF16)
    cnt = jnp.dot(sel.astype(BF16), before, preferred_element_type=F32) + base_ref[:, 0:1]
    ranks = [jnp.sum(jnp.where(e_iota == pick, cnt, 0.0), axis=0, keepdims=True) for pick in picks]
    idx_ref[0] = jnp.concatenate(picks + [r.astype(jnp.int32) for r in ranks], axis=0)
    gate_rows = jnp.concatenate([e / denom for e in exps] + [jnp.zeros((LANE - TOP_K, tm), F32)], axis=0)
    gate_ref[...] = jnp.transpose(gate_rows)
    base_ref[...] = base_ref[...] + jnp.sum(sel, axis=1, keepdims=True)
    cnt_ref[...] = base_ref[...]


def _route(h_flat, n_tok, rw_t_bf, rb):
    d = h_flat.shape[1]
    tm = ROUTE_TILE
    nt = n_tok // tm
    return pl.pallas_call(
        _route_kernel,
        out_shape=(jax.ShapeDtypeStruct((nt, 2 * TOP_K, tm), jnp.int32),
                   jax.ShapeDtypeStruct((n_tok, LANE), F32),
                   jax.ShapeDtypeStruct((N_EXPERTS, LANE), F32)),
        grid=(nt,),
        in_specs=[pl.BlockSpec((tm, d), lambda i: (i, 0)),
                  pl.BlockSpec((N_EXPERTS, d), lambda i: (0, 0)),
                  pl.BlockSpec((N_EXPERTS, 1), lambda i: (0, 0))],
        out_specs=(pl.BlockSpec((1, 2 * TOP_K, tm), lambda i: (i, 0, 0)),
                   pl.BlockSpec((tm, LANE), lambda i: (i, 0)),
                   pl.BlockSpec((N_EXPERTS, LANE), lambda i: (0, 0))),
        scratch_shapes=[pltpu.VMEM((N_EXPERTS, LANE), F32)],
        compiler_params=_cparams("arbitrary"),
        name="moe_route",
    )(h_flat, rw_t_bf, rb.reshape(N_EXPERTS, 1))


def _dispatch_kernel(slot_ref, h_ref, xs_in, xs_out, sem):
    del xs_in
    tm = slot_ref.shape[2]

    def row_copy(t, k):
        return pltpu.make_async_copy(h_ref.at[pl.ds(t, 1)], xs_out.at[pl.ds(slot_ref[0, k, t], 1)], sem)

    def issue(i, carry):
        for u in range(ROW_UNROLL):
            for k in range(TOP_K):
                row_copy(i * ROW_UNROLL + u, k).start(priority=k % 2)
        return carry

    lax.fori_loop(0, tm // ROW_UNROLL, issue, 0)

    def drain(i, carry):
        for u in range(ROW_UNROLL):
            for k in range(TOP_K):
                row_copy(i * ROW_UNROLL + u, k).wait()
        return carry

    lax.fori_loop(0, tm // ROW_UNROLL, drain, 0)


def _dispatch(slots, h_flat, n_tok, cap):
    d = h_flat.shape[1]
    tm = slots.shape[2]
    return pl.pallas_call(
        _dispatch_kernel,
        out_shape=jax.ShapeDtypeStruct((cap, d), F32),
        grid=(n_tok // tm,),
        in_specs=[pl.BlockSpec((1, TOP_K, tm), lambda i: (i, 0, 0), memory_space=pltpu.SMEM),
                  pl.BlockSpec((tm, d), lambda i: (i, 0)),
                  pl.BlockSpec(memory_space=pl.ANY)],
        out_specs=pl.BlockSpec(memory_space=pl.ANY),
        scratch_shapes=[pltpu.SemaphoreType.DMA(())],
        input_output_aliases={2: 0},
        compiler_params=_cparams("arbitrary"),
        name="moe_dispatch",
    )(slots, h_flat, jnp.zeros((cap, d), F32))


SPLIT_COLS = 1024


def _split_kernel(w_ref, sel_ref, wg_ref, wl_ref):
    for j in range(w_ref.shape[2] // (2 * LANE)):
        blk = w_ref[0, :, 2 * LANE * j:2 * LANE * (j + 1)].astype(BF16)
        r = jnp.dot(blk, sel_ref[...], preferred_element_type=F32)
        wg_ref[0, :, LANE * j:LANE * (j + 1)] = r[:, :LANE].astype(BF16)
        wl_ref[0, :, LANE * j:LANE * (j + 1)] = r[:, LANE:].astype(BF16)


def _split_glu_columns(w_gu):
    nl, ne, d, f2 = w_gu.shape
    src = np.arange(2 * LANE)
    dst = np.where(src % 2 == 0, src // 2, LANE + src // 2)
    sel = jnp.asarray((dst[:, None] == np.arange(2 * LANE)[None, :]).astype(np.float32), BF16)
    tn = SPLIT_COLS
    out = jax.ShapeDtypeStruct((nl * ne, d, f2 // 2), BF16)
    wg, wl = pl.pallas_call(
        _split_kernel,
        out_shape=(out, out),
        grid=(nl * ne, f2 // tn),
        in_specs=[pl.BlockSpec((1, d, tn), lambda e, j: (e, 0, j)),
                  pl.BlockSpec((2 * LANE, 2 * LANE), lambda e, j: (0, 0))],
        out_specs=(pl.BlockSpec((1, d, tn // 2), lambda e, j: (e, 0, j)),
                   pl.BlockSpec((1, d, tn // 2), lambda e, j: (e, 0, j))),
        compiler_params=_cparams("arbitrary", "arbitrary"),
        name="split_glu_columns",
    )(w_gu.reshape(nl * ne, d, f2), sel)
    return wg.reshape(nl, ne, d, f2 // 2), wl.reshape(nl, ne, d, f2 // 2)


def _expert_kernel(be_ref, nu_ref, xs_ref, wg_ref, wl_ref, bg_ref, bl_ref, wd_ref, bd_ref, ys_ref):
    del be_ref

    @pl.when(pl.program_id(0) < nu_ref[0])
    def _():
        x = xs_ref[...].astype(BF16)
        g = jnp.dot(x, wg_ref[0], preferred_element_type=F32) + bg_ref[0]
        u = jnp.dot(x, wl_ref[0], preferred_element_type=F32) + bl_ref[0]
        g = jnp.minimum(g, SWIGLU_LIMIT)
        u = jnp.clip(u, -SWIGLU_LIMIT, SWIGLU_LIMIT)
        act = g * jax.nn.sigmoid(SWIGLU_ALPHA * g) * (u + 1.0)
        ys_ref[...] = jnp.dot(act.astype(BF16), wd_ref[0], preferred_element_type=F32) + bd_ref[0]

    @pl.when(pl.program_id(0) >= nu_ref[0])
    def _():
        ys_ref[...] = jnp.zeros_like(ys_ref)


def _experts(block_e, n_used, xs, wg, wl, bg, bl, wd, bd):
    cap, d = xs.shape
    f = wg.shape[2]
    tm = MOE_TILE
    nb = cap // tm

    def row_map(i, be, nu):
        return (jnp.minimum(i, nu[0] - 1), 0)

    def w_map(i, be, nu):
        return (be[jnp.minimum(i, nu[0] - 1)], 0, 0)

    return pl.pallas_call(
        _expert_kernel,
        out_shape=jax.ShapeDtypeStruct((cap, d), F32),
        grid_spec=pltpu.PrefetchScalarGridSpec(
            num_scalar_prefetch=2,
            grid=(nb,),
            in_specs=[pl.BlockSpec((tm, d), row_map),
                      pl.BlockSpec((1, d, f), w_map),
                      pl.BlockSpec((1, d, f), w_map),
                      pl.BlockSpec((1, 1, f), w_map),
                      pl.BlockSpec((1, 1, f), w_map),
                      pl.BlockSpec((1, f, d), w_map),
                      pl.BlockSpec((1, 1, d), w_map)],
            out_specs=pl.BlockSpec((tm, d), lambda i, be, nu: (i, 0))),
        compiler_params=_cparams("arbitrary"),
        name="moe_experts",
    )(block_e, n_used, xs, wg, wl, bg, bl, wd, bd)


def _combine_kernel(slot_ref, ys_hbm, x_ref, gate_ref, m_ref, o_ref, buf, sem):
    tm = x_ref.shape[0]

    def row_copy(t, k):
        return pltpu.make_async_copy(ys_hbm.at[pl.ds(slot_ref[0, k, t], 1)], buf.at[k, pl.ds(t, 1)], sem)

    def issue(i, carry):
        for u in range(ROW_UNROLL):
            for k in range(TOP_K):
                row_copy(i * ROW_UNROLL + u, k).start(priority=k % 2)
        return carry

    lax.fori_loop(0, tm // ROW_UNROLL, issue, 0)

    def drain(i, carry):
        for u in range(ROW_UNROLL):
            for k in range(TOP_K):
                row_copy(i * ROW_UNROLL + u, k).wait()
        return carry

    lax.fori_loop(0, tm // ROW_UNROLL, drain, 0)
    gates = gate_ref[...]
    y = buf[0] * gates[:, 0:1]
    for k in range(1, TOP_K):
        y = y + buf[k] * gates[:, k:k + 1]
    o_ref[...] = x_ref[...] + m_ref[0][5:6] * y


def _combine(slots, ys, x_flat, gates, mod_l, n_tok, n_batch, seq):
    d = x_flat.shape[1]
    tm = slots.shape[2]
    tps = seq // tm
    return pl.pallas_call(
        _combine_kernel,
        out_shape=jax.ShapeDtypeStruct((n_tok, d), F32),
        grid=(n_tok // tm,),
        in_specs=[pl.BlockSpec((1, TOP_K, tm), lambda i: (i, 0, 0), memory_space=pltpu.SMEM),
                  pl.BlockSpec(memory_space=pl.ANY),
                  pl.BlockSpec((tm, d), lambda i: (i, 0)),
                  pl.BlockSpec((tm, LANE), lambda i: (i, 0)),
                  pl.BlockSpec((1, N_MOD, d), lambda i: (_group_of_tile(i, tps, n_batch), 0, 0))],
        out_specs=pl.BlockSpec((tm, d), lambda i: (i, 0)),
        scratch_shapes=[pltpu.VMEM((TOP_K, tm, d), F32), pltpu.SemaphoreType.DMA(())],
        compiler_params=_cparams("arbitrary"),
        name="moe_combine",
    )(slots, ys, x_flat, gates, mod_l)


def _retile_slots(slots, tm):
    nt, k, t = slots.shape
    return slots.reshape(nt, k, t // tm, tm).transpose(0, 2, 1, 3).reshape(nt * (t // tm), k, tm)


def _moe(h_flat, x_flat, n_tok, mod_l, rw_t_bf, rb, wg, wl, bg, bl, wd, bd, n_batch, seq):
    idx, gates, counts = _route(h_flat, n_tok, rw_t_bf, rb)
    counts = counts[:, 0].astype(jnp.int32)
    padded = (counts + MOE_TILE - 1) // MOE_TILE * MOE_TILE
    pad_end = jnp.cumsum(padded)
    pad_start = pad_end - padded
    n_blocks = -(-(n_tok * TOP_K) // MOE_TILE) + N_EXPERTS
    cap = n_blocks * MOE_TILE
    block_row = jnp.arange(n_blocks, dtype=jnp.int32) * MOE_TILE
    block_e = jnp.minimum(jnp.sum(block_row[:, None] >= pad_end[None, :], axis=1), N_EXPERTS - 1).astype(jnp.int32)
    n_used = (pad_end[-1:] // MOE_TILE).astype(jnp.int32)
    experts = idx[:, :TOP_K, :]
    start = jnp.sum(jnp.where(experts[..., None] == jnp.arange(N_EXPERTS), pad_start, 0), axis=-1)
    slots = start + idx[:, TOP_K:, :]
    slots = _retile_slots(slots, MOVE_TILE)
    xs = _dispatch(slots, h_flat, n_tok, cap)
    ys = _experts(block_e, n_used, xs, wg, wl, bg, bl, wd, bd)
    return _combine(slots, ys, x_flat, gates, mod_l, n_tok, n_batch, seq)


def _rms_norm(x, gain):
    xf = x.astype(F32)
    y = xf * lax.rsqrt(jnp.mean(xf * xf, axis=-1, keepdims=True) + EPS)
    return (y * gain.astype(F32)).astype(x.dtype)


def _l2_norm(x):
    xf = x.astype(F32)
    return xf * lax.rsqrt(jnp.sum(xf * xf, axis=-1, keepdims=True) + EPS)


def _short_conv(x, w):
    y = lax.conv_general_dilated(
        x, w[:, None, :].astype(x.dtype), window_strides=(1,),
        padding=[(GDN_CONV // 2, GDN_CONV // 2)],
        dimension_numbers=('NWC', 'WIO', 'NWC'), feature_group_count=x.shape[-1])
    return jax.nn.silu(y)


def _gdn_chunk_scan(q, k, v, g, beta, state):
    b, n, h, _ = q.shape
    nc = n // GDN_CHUNK

    def chunks(t):
        t = t.reshape((b, nc, GDN_CHUNK, h) + t.shape[3:])
        return jnp.moveaxis(t, 3, 1)

    qc, kc, vc = chunks(q), chunks(k), chunks(v)
    gc = jnp.cumsum(chunks(g), axis=-1)
    bc = chunks(beta)
    idx = jnp.arange(GDN_CHUNK)
    incl = idx[:, None] >= idx[None, :]
    strict = idx[:, None] > idx[None, :]
    decay = jnp.exp(jnp.where(incl, gc[..., :, None] - gc[..., None, :], -jnp.inf))
    kk = jnp.einsum('bhctd,bhcid->bhcti', kc, kc)
    a_mat = jnp.where(strict, bc[..., :, None] * kk * decay, 0.0) + jnp.eye(GDN_CHUNK, dtype=F32)
    u = lax.linalg.triangular_solve(a_mat, vc * bc[..., None], left_side=True, lower=True, unit_diagonal=True)
    w = lax.linalg.triangular_solve(a_mat, kc * (bc * jnp.exp(gc))[..., None], left_side=True, lower=True,
                                    unit_diagonal=True)
    qk = jnp.einsum('bhctd,bhcid->bhcti', qc, kc) * decay
    q_dec = qc * jnp.exp(gc)[..., None]
    k_dec = kc * jnp.exp(gc[..., -1:] - gc)[..., None]
    g_end = jnp.exp(gc[..., -1])

    def step(s_prev, xs):
        u_c, w_c, qk_c, qd_c, kd_c, ge_c = xs
        v_new = u_c - jnp.einsum('bhtk,bhkv->bhtv', w_c, s_prev)
        o_c = jnp.einsum('bhtk,bhkv->bhtv', qd_c, s_prev) + jnp.einsum('bhti,bhiv->bhtv', qk_c, v_new)
        s_next = ge_c[..., None, None] * s_prev + jnp.einsum('bhtk,bhtv->bhkv', kd_c, v_new)
        return s_next, o_c

    xs = tuple(jnp.moveaxis(t, 2, 0) for t in (u, w, qk, q_dec, k_dec, g_end))
    state, o = lax.scan(step, state, xs)
    o = jnp.moveaxis(jnp.moveaxis(o, 0, 2), 1, 3)
    return o.reshape(b, n, h, o.shape[-1]), state


def _orient(stream, d):
    q, k, v, g, beta = stream
    seq = (q, k, v, g[:, :, d], beta[:, :, d])
    if d == 1:
        return tuple(jnp.flip(t, axis=1) for t in seq)
    return seq


def _bidirectional_gdn(lat, ctx):
    b, _, h, dk = lat[0].shape
    dv = lat[2].shape[-1]
    o_lat, o_ctx = [], []
    for d in range(2):
        state0 = jnp.zeros((b, h, dk, dv), F32)
        oc, s_ctx = _gdn_chunk_scan(*_orient(ctx, d), state0)
        ol, _ = _gdn_chunk_scan(*_orient(lat, d), s_ctx)
        if d == 1:
            oc, ol = jnp.flip(oc, axis=1), jnp.flip(ol, axis=1)
        o_lat.append(ol)
        o_ctx.append(oc)
    return o_lat[0] + o_lat[1], o_ctx[0] + o_ctx[1]


def _gdn_mixer(p_flat, n_batch, seq, ctx_len, gdn_conv, gdn_a_log, gdn_dt_bias, gdn_out_norm, with_ctx):
    n_lat = n_batch * seq

    def split(rows, n):
        g4 = rows[:, COL_G:COL_G + 4 * GDN_W].reshape(n_batch, n, 4 * GDN_W)
        small = rows[:, COL_SMALL:].reshape(n_batch, n, LANE)
        q, k, v, z = jnp.split(g4, 4, axis=-1)
        return q, k, v, z, small[..., SMALL_GA:SMALL_GA + 2 * GDN_HEADS], small[..., SMALL_GB:SMALL_GB + 2 * GDN_HEADS]

    def proj(q, k, v, a, bg):
        n = q.shape[1]
        q, k, v = jnp.split(_short_conv(jnp.concatenate([q, k, v], axis=-1), gdn_conv), 3, axis=-1)
        hs = (n_batch, n, GDN_HEADS, GDN_HEAD_DIM)
        q = _l2_norm(q.reshape(hs)) * GDN_HEAD_DIM ** -0.5
        k = _l2_norm(k.reshape(hs))
        v = v.reshape(hs)
        a = a.reshape(n_batch, n, 2, GDN_HEADS)
        log_decay = -jnp.exp(gdn_a_log) * jax.nn.softplus(a + gdn_dt_bias)
        beta = jax.nn.sigmoid(bg.reshape(n_batch, n, 2, GDN_HEADS))
        return q, k, v, log_decay, beta

    def out(o, z):
        n = o.shape[1]
        gate = jax.nn.silu(z.reshape(n_batch, n, GDN_HEADS, GDN_HEAD_DIM))
        return (_rms_norm(o, gdn_out_norm) * gate).reshape(n_batch * n, GDN_W)

    q, k, v, z, a, bg = split(p_flat[:n_lat], seq)
    qc, kc, vc, zc, ac, bgc = split(p_flat[n_lat:], ctx_len)
    ob, ob_c = _bidirectional_gdn(proj(q, k, v, a, bg), proj(qc, kc, vc, ac, bgc))
    if with_ctx:
        return jnp.concatenate([out(ob, z), out(ob_c, zc)], axis=0)
    return out(ob, z)


GDN_TILE = 256
HALO = SUBLANE


def _gdn_prep_kernel(q_ref, k_ref, v_ref, qp_ref, kp_ref, vp_ref, qn_ref, kn_ref, vn_ref, small_ref, w_ref,
                     a_ref, bias_ref, qkv_ref, gb_ref, xe_ref, *, lat_tiles, lat_tps, ctx_tps):
    i = pl.program_id(0)
    tm = q_ref.shape[0]
    in_lat = i < lat_tiles
    pos = jnp.where(in_lat, i % lat_tps, (i - lat_tiles) % ctx_tps)
    last = jnp.where(in_lat, lat_tps - 1, ctx_tps - 1)
    keep_prev = jnp.where(pos > 0, 1.0, 0.0)
    keep_next = jnp.where(pos < last, 1.0, 0.0)
    half = GDN_CONV // 2
    for g, (x_ref, p_ref, n_ref) in enumerate(((q_ref, qp_ref, qn_ref), (k_ref, kp_ref, kn_ref),
                                               (v_ref, vp_ref, vn_ref))):
        xe_ref[g, 0:HALO, :] = p_ref[...] * keep_prev
        xe_ref[g, HALO:HALO + tm, :] = x_ref[...]
        xe_ref[g, HALO + tm:2 * HALO + tm, :] = n_ref[...] * keep_next
        w = w_ref[:, g * GDN_W:(g + 1) * GDN_W]
        y = xe_ref[g, HALO - half:HALO - half + tm, :] * w[0:1]
        for j in range(1, GDN_CONV):
            y = y + xe_ref[g, HALO - half + j:HALO - half + j + tm, :] * w[j:j + 1]
        y = y * jax.nn.sigmoid(y)
        if g < 2:
            heads = []
            for h in range(GDN_HEADS):
                yh = y[:, h * GDN_HEAD_DIM:(h + 1) * GDN_HEAD_DIM]
                yh = yh * lax.rsqrt(jnp.sum(yh * yh, axis=-1, keepdims=True) + EPS)
                heads.append(yh * GDN_HEAD_DIM ** -0.5 if g == 0 else yh)
            y = jnp.concatenate(heads, axis=1)
        qkv_ref[:, g * GDN_W:(g + 1) * GDN_W] = y.astype(BF16)
    sm = small_ref[...]
    z = sm + bias_ref[...]
    softplus = jnp.maximum(z, 0.0) + jnp.log(1.0 + jnp.exp(-jnp.abs(z)))
    lane = lax.broadcasted_iota(jnp.int32, (1, LANE), 1)
    is_g = (lane >= SMALL_GA) & (lane < SMALL_GB)
    is_b = (lane >= SMALL_GB) & (lane < SMALL_GB + 2 * GDN_HEADS)
    gb_ref[...] = jnp.where(is_g, a_ref[...] * softplus, jnp.where(is_b, jax.nn.sigmoid(sm), 0.0))


def _gdn_prep(p_flat, conv_w, neg_a, dt_bias, n_batch, seq, ctx_len):
    n_tok = p_flat.shape[0]
    tm = GDN_TILE
    lat_tiles = n_batch * seq // tm
    per_tile = tm // HALO
    n_halo_blocks = n_tok // HALO
    col0 = COL_G // GDN_W

    def main(g):
        return pl.BlockSpec((tm, GDN_W), lambda i: (i, col0 + g))

    def prev(g):
        return pl.BlockSpec((HALO, GDN_W), lambda i: (jnp.maximum(i * per_tile - 1, 0), col0 + g))

    def nxt(g):
        return pl.BlockSpec((HALO, GDN_W), lambda i: (jnp.minimum((i + 1) * per_tile, n_halo_blocks - 1), col0 + g))

    const = lambda i: (0, 0)
    return pl.pallas_call(
        functools.partial(_gdn_prep_kernel, lat_tiles=lat_tiles, lat_tps=seq // tm, ctx_tps=ctx_len // tm),
        out_shape=(jax.ShapeDtypeStruct((n_tok, 3 * GDN_W), BF16), jax.ShapeDtypeStruct((n_tok, LANE), F32)),
        grid=(n_tok // tm,),
        in_specs=[main(0), main(1), main(2), prev(0), prev(1), prev(2), nxt(0), nxt(1), nxt(2),
                  pl.BlockSpec((tm, LANE), lambda i: (i, COL_SMALL // LANE)),
                  pl.BlockSpec(conv_w.shape, const), pl.BlockSpec((1, LANE), const), pl.BlockSpec((1, LANE), const)],
        out_specs=(pl.BlockSpec((tm, 3 * GDN_W), lambda i: (i, 0)), pl.BlockSpec((tm, LANE), lambda i: (i, 0))),
        scratch_shapes=[pltpu.VMEM((3, tm + 2 * HALO, GDN_W), F32)],
        compiler_params=_cparams("arbitrary"),
        name="gdn_prep",
    )(*([p_flat] * 10), conv_w, neg_a, dt_bias)


GDN_SUB = 16
GDN_CHUNKS_PER_STEP = 4


def _bdot(a, b):
    return jnp.dot(a.astype(BF16), b.astype(BF16), preferred_element_type=F32)


def _dot_tn(a, b):
    return lax.dot_general(a, b, (((0,), (0,)), ((), ())), preferred_element_type=F32)


def _each(f, *lists):
    return [f(*args) for args in zip(*lists)]


def _unit_tri_solve(n, rhs, same_blk, eye):
    nd = _each(lambda a: jnp.where(same_blk, a, 0.0).astype(BF16), n)
    nl = _each(lambda a: jnp.where(same_blk, 0.0, a).astype(BF16), n)
    nd2 = _each(lambda a: jnp.dot(a, a, preferred_element_type=F32).astype(BF16), nd)
    nd4 = _each(lambda a: jnp.dot(a, a, preferred_element_type=F32).astype(BF16), nd2)
    nd8 = _each(lambda a: jnp.dot(a, a, preferred_element_type=F32).astype(BF16), nd4)
    m = _each(lambda a: eye - a.astype(F32), nd)
    m = _each(lambda a, b: a + _bdot(a, b), m, nd2)
    m = _each(lambda a, b: a + _bdot(a, b), m, nd4)
    dinv = _each(lambda a, b: (a + _bdot(a, b)).astype(BF16), m, nd8)
    p = _each(lambda a, b: jnp.dot(a, b, preferred_element_type=F32).astype(BF16), dinv, nl)
    p2 = _each(lambda a: jnp.dot(a, a, preferred_element_type=F32).astype(BF16), p)
    y = _each(lambda a, b: _bdot(a, b), dinv, rhs)
    y = _each(lambda a, b: a + _bdot(b, a), y, p2)
    return _each(lambda a, b: a - _bdot(b, a), y, p)


def _gdn_scan_kernel(x0_ref, gb0_ref, x1_ref, gb1_ref, o0_ref, o1_ref, state_ref, *, n_sub):
    assert GDN_CHUNK == 4 * GDN_SUB

    @pl.when(pl.program_id(1) == 0)
    def _():
        state_ref[...] = jnp.zeros_like(state_ref)

    n = GDN_CHUNK
    r = lax.broadcasted_iota(jnp.int32, (n, n), 0)
    c = lax.broadcasted_iota(jnp.int32, (n, n), 1)
    same_blk = (r // GDN_SUB) == (c // GDN_SUB)
    eye = (r == c).astype(F32)
    dirs = ((x0_ref, gb0_ref, o0_ref), (x1_ref, gb1_ref, o1_ref))
    heads = [(d, h) for d in range(2) for h in range(GDN_HEADS)]
    chunk_at = [list(range(n_sub)), list(range(n_sub - 1, -1, -1))]
    probs = [(d, h, chunk_at[d][s]) for s in range(n_sub) for d, h in heads]
    incl = [r >= c, r <= c]
    strict = [r > c, r < c]
    end_row = [n - 1, 0]

    gbv, gc_all, gc_t = {}, {}, {}
    for d in range(2):
        tri = incl[d].astype(BF16)
        for ch in range(n_sub):
            g = dirs[d][1][ch * n:(ch + 1) * n, :]
            hi = g.astype(BF16)
            rem = g - hi.astype(F32)
            mid = rem.astype(BF16)
            lo = (rem - mid.astype(F32)).astype(BF16)
            gc = (jnp.dot(tri, hi, preferred_element_type=F32) + jnp.dot(tri, mid, preferred_element_type=F32)
                  + jnp.dot(tri, lo, preferred_element_type=F32))
            gbv[d, ch], gc_all[d, ch], gc_t[d, ch] = g, gc, jnp.transpose(gc)

    def head_cols(d, part, h, ch):
        lo_col = part * GDN_W + h * GDN_HEAD_DIM
        return dirs[d][0][ch * n:(ch + 1) * n, lo_col:lo_col + GDN_HEAD_DIM]

    lane_g = [SMALL_GA + GDN_HEADS * d + h for d, h, ch in probs]
    lane_b = [SMALL_GB + GDN_HEADS * d + h for d, h, ch in probs]
    gcol = [gc_all[d, ch][:, lg:lg + 1] for (d, h, ch), lg in zip(probs, lane_g)]
    grow = [gc_t[d, ch][lg:lg + 1, :] for (d, h, ch), lg in zip(probs, lane_g)]
    gend = [gc_all[d, ch][end_row[d]:end_row[d] + 1, lg:lg + 1] for (d, h, ch), lg in zip(probs, lane_g)]
    beta = [gbv[d, ch][:, lb:lb + 1] for (d, h, ch), lb in zip(probs, lane_b)]
    q = [head_cols(d, 0, h, ch) for d, h, ch in probs]
    k = [head_cols(d, 1, h, ch) for d, h, ch in probs]
    v = [head_cols(d, 2, h, ch) for d, h, ch in probs]
    kf = _each(lambda a: a.astype(F32), k)
    dec = [jnp.where(incl[d], jnp.exp(jnp.where(incl[d], gc_ - gr_, 0.0)), 0.0)
           for (d, h, ch), gc_, gr_ in zip(probs, gcol, grow)]
    qk_kk = _each(lambda q_, k_: _dot_nt(jnp.concatenate([q_, k_], axis=0), k_), q, k)
    qkd = _each(lambda a, dec_: (a[:n] * dec_).astype(BF16), qk_kk, dec)
    nmat = [jnp.where(strict[d], b_ * a[n:] * dec_, 0.0) for (d, h, ch), b_, a, dec_ in zip(probs, beta, qk_kk, dec)]
    egc = _each(jnp.exp, gcol)
    rhs = _each(lambda v_, kf_, b_, e_: jnp.concatenate([v_.astype(F32) * b_, kf_ * (b_ * e_)], axis=1).astype(BF16),
                v, kf, beta, egc)
    uw = _unit_tri_solve(nmat, rhs, same_blk, eye)
    wq = _each(lambda uw_, q_, e_: jnp.concatenate([uw_[:, GDN_HEAD_DIM:], q_.astype(F32) * e_], axis=0).astype(BF16),
               uw, q, egc)
    k_dec = _each(lambda kf_, ge_, gc_: (kf_ * jnp.exp(ge_ - gc_)).astype(BF16), kf, gend, gcol)
    g_end = _each(jnp.exp, gend)

    state = [state_ref[i] for i in range(len(heads))]
    for s in range(n_sub):
        sl = slice(s * len(heads), (s + 1) * len(heads))
        s_bf = _each(lambda a: a.astype(BF16), state)
        ws_qs = _each(lambda a, s_: jnp.dot(a, s_, preferred_element_type=F32), wq[sl], s_bf)
        v_new = _each(lambda uw_, a: (uw_[:, :GDN_HEAD_DIM] - a[:n]).astype(BF16), uw[sl], ws_qs)
        o_local = _each(lambda a, vn_: jnp.dot(a, vn_, preferred_element_type=F32), qkd[sl], v_new)
        s_add = _each(_dot_tn, k_dec[sl], v_new)
        state = _each(lambda ge_, st_, add_: ge_ * st_ + add_, g_end[sl], state, s_add)
        for (d, h, ch), a, ol in zip(probs[sl], ws_qs, o_local):
            dirs[d][2][ch * n:(ch + 1) * n, h * GDN_HEAD_DIM:(h + 1) * GDN_HEAD_DIM] = a[n:] + ol
    for i in range(len(heads)):
        state_ref[i] = state[i]


def _gdn_scan(qkv, gbeta, n_batch, seq, ctx_len):
    n_tok = qkv.shape[0]
    n = GDN_CHUNK * GDN_CHUNKS_PER_STEP
    lat_c, ctx_c = seq // n, ctx_len // n
    ctx0 = n_batch * seq // n

    def fwd(b, j):
        return (jnp.where(j < ctx_c, ctx0 + b * ctx_c + j, b * lat_c + (j - ctx_c)), 0)

    def bwd(b, j):
        return (jnp.where(j < ctx_c, ctx0 + b * ctx_c + (ctx_c - 1 - j), b * lat_c + (lat_c - 1 - (j - ctx_c))), 0)

    return pl.pallas_call(
        functools.partial(_gdn_scan_kernel, n_sub=GDN_CHUNKS_PER_STEP),
        out_shape=(jax.ShapeDtypeStruct((n_tok, GDN_W), F32), jax.ShapeDtypeStruct((n_tok, GDN_W), F32)),
        grid=(n_batch, ctx_c + lat_c),
        in_specs=[pl.BlockSpec((n, 3 * GDN_W), fwd), pl.BlockSpec((n, LANE), fwd),
                  pl.BlockSpec((n, 3 * GDN_W), bwd), pl.BlockSpec((n, LANE), bwd)],
        out_specs=(pl.BlockSpec((n, GDN_W), fwd), pl.BlockSpec((n, GDN_W), bwd)),
        scratch_shapes=[pltpu.VMEM((2 * GDN_HEADS, GDN_HEAD_DIM, GDN_HEAD_DIM), F32)],
        compiler_params=_cparams("arbitrary", "arbitrary"),
        name="gdn_scan",
    )(qkv, gbeta, qkv, gbeta)


def _pad_heads(w, n_heads, width, start, take):
    lead = w.shape[:-1]
    w = w.reshape(lead + (n_heads, width))[..., start:start + take]
    w = jnp.pad(w, [(0, 0)] * len(lead) + [(0, 0), (0, MLA_HEAD_PAD - take)])
    return w.reshape(lead + (n_heads * MLA_HEAD_PAD,))


def kernel(x, c, ctx, c_ctx, w_mod, b_mod, norm1, norm2, w_in, w_out, swa_q_norm, swa_k_norm, swa_sink, gdn_conv,
           gdn_a_log, gdn_dt_bias, gdn_out_norm, mla_q_a_norm, mla_w_uq, mla_kv_a_norm, mla_w_ukv, mla_q_norm,
           mla_k_norm, router_w, router_b, exp_w_gu, exp_b_gu, exp_w_dn, exp_b_dn):
    b, s, d = x.shape
    cl = ctx.shape[1]
    depth = w_mod.shape[0]
    n_lat, n_ctx = b * s, b * cl
    assert s % TOK_TILE == 0 and n_ctx % TOK_TILE == 0 and s % MOVE_TILE == 0 and n_ctx % ROUTE_TILE == 0
    assert cl % MLA_Q_TILE == 0 and cl % ATTN_BLOCK == 0 and n_lat % cl == 0

    n_rows = -(-(b + 1) // SUBLANE) * SUBLANE
    c_rows = jnp.concatenate([c, c_ctx[None, :], jnp.zeros((n_rows - b - 1, d), F32)], axis=0)
    mod_all = _modulation(c_rows, w_mod, b_mod)

    perm = _in_proj_perm()
    tabs = _rope_tables(s)
    seg_a = _block_diag_ones(SWA_Q, SWA_HEAD_DIM)
    seg_c = _block_diag_ones(MLA_HEADS * MLA_HEAD_PAD, MLA_HEAD_PAD)
    head_rows = np.concatenate([h * SWA_HEAD_DIM + np.arange(SWA_HEAD_DIM) for h in SWA_HEAD_ORDER])

    wg_all, wl_all = _split_glu_columns(exp_w_gu)
    x_flat = jnp.concatenate([x.reshape(n_lat, d), ctx.reshape(n_ctx, d)], axis=0)
    for l in range(depth):
        with_ctx = l < depth - 1
        n_tok = n_lat + n_ctx if with_ctx else n_lat
        mod_l = mod_all[l]
        w_in_bf = jnp.pad(w_in[l], ((0, 0), (0, 1)))[:, perm].astype(BF16)
        p_flat = _in_projection(x_flat, norm1[l], mod_l, w_in_bf, b, s)

        consts = (seg_a, seg_c,
                  jnp.tile(swa_q_norm[l], SWA_HEADS)[None], jnp.tile(swa_k_norm[l], SWA_KV_HEADS)[None],
                  mla_q_a_norm[l][None], mla_kv_a_norm[l][None],
                  _pad_heads(mla_w_uq[l], MLA_HEADS, MLA_QK, 0, MLA_QK).astype(BF16),
                  _pad_heads(mla_w_ukv[l], MLA_HEADS, MLA_NOPE + MLA_V, 0, MLA_NOPE).astype(BF16),
                  _pad_heads(mla_w_ukv[l], MLA_HEADS, MLA_NOPE + MLA_V, MLA_NOPE, MLA_V).astype(BF16),
                  _pad_heads(jnp.tile(mla_q_norm[l], MLA_HEADS), MLA_HEADS, MLA_QK, 0, MLA_QK)[None],
                  _pad_heads(jnp.tile(mla_k_norm[l], MLA_HEADS), MLA_HEADS, MLA_QK, 0, MLA_QK)[None])
        qa, kva, qm, km, vm = _attention_prep(p_flat, tabs, consts, b, s)
        o_a = _swa_attention(swa_sink[l], qa, kva, b, s, cl, with_ctx)
        o_c = _mla_attention(qm, km, vm, b, s, cl, with_ctx)
        lanes_g = slice(SMALL_GA, SMALL_GA + 2 * GDN_HEADS)
        neg_a = jnp.zeros((1, LANE), F32).at[0, lanes_g].set(-jnp.exp(gdn_a_log[l]).reshape(-1))
        dt_bias = jnp.zeros((1, LANE), F32).at[0, lanes_g].set(gdn_dt_bias[l].reshape(-1))
        qkv, gbeta = _gdn_prep(p_flat, gdn_conv[l], neg_a, dt_bias, b, s, cl)
        o_fwd, o_bwd = _gdn_scan(qkv, gbeta, b, s, cl)

        wa = w_out[l][:SWA_Q][head_rows].astype(BF16)
        wb = w_out[l][SWA_Q:SWA_Q + GDN_W].astype(BF16)
        wc = w_out[l][SWA_Q + GDN_W:].reshape(MLA_HEADS, MLA_V, d)
        wc = jnp.pad(wc, ((0, 0), (0, MLA_HEAD_PAD - MLA_V), (0, 0))).reshape(MLA_HEADS * MLA_HEAD_PAD, d).astype(BF16)
        x_mid, h2 = _out_projection(o_a, o_fwd, o_bwd, p_flat, o_c, x_flat, wa, wb, wc,
                                    jnp.tile(gdn_out_norm[l], GDN_HEADS)[None], norm2[l], mod_l, n_tok, b, s)

        wg, wl = wg_all[l], wl_all[l]
        bg = exp_b_gu[l][:, None, 0::2]
        bl = exp_b_gu[l][:, None, 1::2]
        wd = exp_w_dn[l].astype(BF16)
        bd = exp_b_dn[l][:, None, :]
        x_flat = _moe(h2, x_mid, n_tok, mod_l, router_w[l].T.astype(BF16), router_b[l], wg, wl, bg, bl, wd, bd, b, s)
    return x_flat[:n_lat].reshape(b, s, d)
```

```python
import functools

import jax
import jax.numpy as jnp
import numpy as np
from jax import lax
from jax.experimental import pallas as pl
from jax.experimental.pallas import tpu as pltpu

F32 = jnp.float32
BF16 = jnp.bfloat16

GRID_W = 64
N_MOD = 6
EPS = 1e-6
ROPE_BASE = 10000.0
ATTN_BLOCK = 128

SWA_HEADS = 4
SWA_KV_HEADS = 2
SWA_HEAD_DIM = 64
SWA_WINDOW = 128

GDN_HEADS = 4
GDN_HEAD_DIM = 128
GDN_CONV = 5
GDN_CHUNK = 64

MLA_HEADS = 4
MLA_Q_RANK = 256
MLA_KV_RANK = 128
MLA_NOPE = 64
MLA_ROPE = 32
MLA_V = 64
MLA_QK = MLA_NOPE + MLA_ROPE

N_EXPERTS = 32
TOP_K = 4
SWIGLU_LIMIT = 7.0
SWIGLU_ALPHA = 1.702

SWA_Q = SWA_HEADS * SWA_HEAD_DIM
SWA_KV = SWA_KV_HEADS * SWA_HEAD_DIM
GDN_W = GDN_HEADS * GDN_HEAD_DIM
MLA_O = MLA_HEADS * MLA_V
D_MIX = SWA_Q + GDN_W + MLA_O
IN_SPLITS = (SWA_Q, SWA_KV, SWA_KV, GDN_W, GDN_W, GDN_W, GDN_W, 2 * GDN_HEADS, 2 * GDN_HEADS, MLA_Q_RANK,
             MLA_KV_RANK, MLA_ROPE)
N_IN = sum(IN_SPLITS)

LANE = 128
SUBLANE = 8
VMEM_LIMIT = 56 * 1024 * 1024

TOK_TILE = 512
MOE_TILE = 512
ROUTE_TILE = 512
MOVE_TILE = 512
ROW_UNROLL = 4
MLA_Q_TILE = 256

COL_AQ, COL_AK, COL_AV = 0, 256, 384
COL_G = 512
COL_CQ, COL_CKV, COL_SMALL = 2560, 2816, 2944
N_IN_PAD = 3072
SMALL_KR, SMALL_GA, SMALL_GB = 0, MLA_ROPE, MLA_ROPE + 2 * GDN_HEADS
SWA_HEAD_ORDER = (0, 2, 1, 3)
MLA_HEAD_PAD = 128


def _cparams(*sem):
    return pltpu.CompilerParams(dimension_semantics=sem, vmem_limit_bytes=VMEM_LIMIT)


def _in_proj_perm():
    old = np.cumsum((0,) + IN_SPLITS)
    o_aq, o_ak, o_av, o_gq, o_gk, o_gv, o_gz, o_ga, o_gb, o_cq, o_ckv, o_ckr = old[:-1]
    perm = np.full((N_IN_PAD,), N_IN, np.int32)
    perm[COL_AQ:COL_AQ + SWA_Q] = np.concatenate(
        [o_aq + h * SWA_HEAD_DIM + np.arange(SWA_HEAD_DIM) for h in SWA_HEAD_ORDER])
    perm[COL_AK:COL_G + 4 * GDN_W] = np.arange(o_ak, o_ga)
    perm[COL_CQ:COL_CQ + MLA_Q_RANK] = o_cq + np.arange(MLA_Q_RANK)
    perm[COL_CKV:COL_CKV + MLA_KV_RANK] = o_ckv + np.arange(MLA_KV_RANK)
    perm[COL_SMALL + SMALL_KR:COL_SMALL + SMALL_KR + MLA_ROPE] = o_ckr + np.arange(MLA_ROPE)
    perm[COL_SMALL + SMALL_GA:COL_SMALL + SMALL_GA + 2 * GDN_HEADS] = o_ga + np.arange(2 * GDN_HEADS)
    perm[COL_SMALL + SMALL_GB:COL_SMALL + SMALL_GB + 2 * GDN_HEADS] = o_gb + np.arange(2 * GDN_HEADS)
    return perm


def _mod_kernel(c_ref, w_ref, b_ref, o_ref):
    a = c_ref[...]
    a = a * jax.nn.sigmoid(a)
    o_ref[0] = jnp.dot(a.astype(BF16), w_ref[0].astype(BF16), preferred_element_type=F32) + b_ref[0]


def _modulation(c_rows, w_mod, b_mod):
    nl, d, n = w_mod.shape
    r = c_rows.shape[0]
    tn = 1536
    out = pl.pallas_call(
        _mod_kernel,
        out_shape=jax.ShapeDtypeStruct((nl, r, n), F32),
        grid=(nl, n // tn),
        in_specs=[pl.BlockSpec((r, d), lambda l, j: (0, 0)),
                  pl.BlockSpec((1, d, tn), lambda l, j: (l, 0, j)),
                  pl.BlockSpec((1, 1, tn), lambda l, j: (l, 0, j))],
        out_specs=pl.BlockSpec((1, r, tn), lambda l, j: (l, 0, j)),
        compiler_params=_cparams("arbitrary", "arbitrary"),
        name="modulation",
    )(c_rows, w_mod, b_mod.reshape(nl, 1, n))
    return out.reshape(nl, r, N_MOD, d)


def _modulated_norm(x, gain, shift, scale):
    y = x * lax.rsqrt(jnp.mean(x * x, axis=-1, keepdims=True) + EPS)
    return (y * gain) * (1.0 + scale) + shift


def _group_of_tile(i, tiles_per_seq, n_batch):
    return jnp.minimum(i // tiles_per_seq, n_batch)


def _group_sumsq(x, seg):
    x2 = x * x
    hi = x2.astype(BF16)
    lo = (x2 - hi.astype(F32)).astype(BF16)
    return jnp.dot(hi, seg, preferred_element_type=F32) + jnp.dot(lo, seg, preferred_element_type=F32)


def _swap_pairs(x, half):
    w = x.shape[1]
    lane = lax.broadcasted_iota(jnp.int32, (1, w), 1)
    first = (lane % (2 * half)) < half
    return jnp.where(first, pltpu.roll(x, w - half, axis=1), pltpu.roll(x, half, axis=1))


def _inproj_kernel(x_ref, g_ref, m_ref, w_ref, ra_c_ref, ra_s_ref, rc_c_ref, rc_s_ref, seg_a_ref, seg_c_ref,
                   gaq_ref, gak_ref, gcq_ref, gckv_ref, wuq_ref, wuk_ref, wuv_ref, gmq_ref, gmk_ref,
                   pg_ref, small_ref, qa_ref, kva_ref, qm_ref, km_ref, vm_ref):
    m = m_ref[0]
    h = _modulated_norm(x_ref[...], g_ref[...], m[0:1], m[1:2]).astype(BF16)
    pa = jnp.dot(h, w_ref[:, COL_AQ:COL_G], preferred_element_type=F32)
    pg_ref[...] = jnp.dot(h, w_ref[:, COL_G:COL_CQ], preferred_element_type=F32)
    pc = jnp.dot(h, w_ref[:, COL_CQ:], preferred_element_type=F32)
    small = pc[:, MLA_Q_RANK + MLA_KV_RANK:]
    small_ref[...] = small

    aq = pa[:, 0:SWA_Q]
    ak = pa[:, SWA_Q:SWA_Q + SWA_KV]
    av = pa[:, SWA_Q + SWA_KV:SWA_Q + 2 * SWA_KV]
    ca, sa = ra_c_ref[...], ra_s_ref[...]
    seg_a = seg_a_ref[...]
    qn = aq * lax.rsqrt(_group_sumsq(aq, seg_a) * (1.0 / SWA_HEAD_DIM) + EPS) * gaq_ref[...]
    kn = ak * lax.rsqrt(_group_sumsq(ak, seg_a[:SWA_KV, :SWA_KV]) * (1.0 / SWA_HEAD_DIM) + EPS) * gak_ref[...]
    qn = qn * jnp.concatenate([ca, ca], axis=1) + _swap_pairs(qn, SWA_HEAD_DIM // 4) * jnp.concatenate([sa, sa], axis=1)
    kn = kn * ca + _swap_pairs(kn, SWA_HEAD_DIM // 4) * sa
    qa_ref[...] = (qn * SWA_HEAD_DIM ** -0.5).astype(BF16)
    kva_ref[...] = jnp.concatenate([kn, av], axis=1).astype(BF16)

    cq = pc[:, 0:MLA_Q_RANK]
    ckv = pc[:, MLA_Q_RANK:MLA_Q_RANK + MLA_KV_RANK]
    cqn =cq * lax.rsqrt(jnp.mean(cq * cq, axis=-1, keepdims=True) + EPS) * gcq_ref[...]
    ckvn = (ckv * lax.rsqrt(jnp.mean(ckv * ckv, axis=-1, keepdims=True) + EPS) * gckv_ref[...]).astype(BF16)
    q = jnp.dot(cqn.astype(BF16), wuq_ref[...], preferred_element_type=F32)
    k = jnp.dot(ckvn, wuk_ref[...], preferred_element_type=F32)
    v = jnp.dot(ckvn, wuv_ref[...], preferred_element_type=F32)
    lane = lax.broadcasted_iota(jnp.int32, (1, LANE), 1)
    kpe = jnp.where((lane >= MLA_NOPE) & (lane < MLA_QK), pltpu.roll(small, MLA_NOPE, axis=1), 0.0)
    k = k + jnp.concatenate([kpe] * MLA_HEADS, axis=1)
    seg_c = seg_c_ref[...]
    q = q * lax.rsqrt(_group_sumsq(q, seg_c) * (1.0 / MLA_QK) + EPS) * gmq_ref[...]
    k = k * lax.rsqrt(_group_sumsq(k, seg_c) * (1.0 / MLA_QK) + EPS) * gmk_ref[...]
    cc = jnp.concatenate([rc_c_ref[...]] * MLA_HEADS, axis=1)
    sc = jnp.concatenate([rc_s_ref[...]] * MLA_HEADS, axis=1)
    q = q * cc + _swap_pairs(q, MLA_ROPE // 4) * sc
    k = k * cc + _swap_pairs(k, MLA_ROPE // 4) * sc
    qm_ref[...] = (q * MLA_QK ** -0.5).astype(BF16)
    km_ref[...] = k.astype(BF16)
    vm_ref[...] = v.astype(BF16)


def _in_projection(x_flat, gain, mod_l, w_bf, tabs, consts, n_batch, seq):
    n_tok, d = x_flat.shape
    tm = TOK_TILE
    tps = seq // tm
    n_lat_tiles = n_batch * tps
    row = lambda i: (i, 0)
    const = lambda i: (0, 0)
    tab = lambda i: (jnp.where(i < n_lat_tiles, i % tps, tps), 0)
    wide = MLA_HEADS * MLA_HEAD_PAD
    in_specs = [pl.BlockSpec((tm, d), row),
                pl.BlockSpec((1, d), const),
                pl.BlockSpec((1, N_MOD, d), lambda i: (_group_of_tile(i, tps, n_batch), 0, 0)),
                pl.BlockSpec(w_bf.shape, const)]
    in_specs += [pl.BlockSpec((tm, LANE), tab)] * 4
    in_specs += [pl.BlockSpec(a.shape, const) for a in consts]
    widths = (4 * GDN_W, LANE, SWA_Q, 2 * SWA_KV, wide, wide, wide)
    dtypes = (F32, F32, BF16, BF16, BF16, BF16, BF16)
    return pl.pallas_call(
        _inproj_kernel,
        out_shape=tuple(jax.ShapeDtypeStruct((n_tok, w), t) for w, t in zip(widths, dtypes)),
        grid=(n_tok // tm,),
        in_specs=in_specs,
        out_specs=tuple(pl.BlockSpec((tm, w), row) for w in widths),
        compiler_params=_cparams("arbitrary"),
        name="in_projection",
    )(x_flat, gain.reshape(1, d), mod_l, w_bf, *tabs, *consts)


def _rope_tables(seq):
    t = jnp.arange(seq)
    row = (t // GRID_W).astype(F32)
    col = (t % GRID_W).astype(F32)

    def cos_sin(rot_dim):
        n_freq = rot_dim // 4
        freq = ROPE_BASE ** (-jnp.arange(n_freq, dtype=F32) / n_freq)
        ar, ac = row[:, None] * freq, col[:, None] * freq
        c = jnp.concatenate([jnp.cos(ar), jnp.cos(ar), jnp.cos(ac), jnp.cos(ac)], axis=1)
        s = jnp.concatenate([-jnp.sin(ar), jnp.sin(ar), -jnp.sin(ac), jnp.sin(ac)], axis=1)
        return c, s

    ca, sa = cos_sin(SWA_HEAD_DIM)
    ca, sa = jnp.tile(ca, (1, 2)), jnp.tile(sa, (1, 2))
    cc, sc = cos_sin(MLA_ROPE)
    ones_l = jnp.ones((seq, MLA_NOPE), F32)
    ones_r = jnp.ones((seq, MLA_HEAD_PAD - MLA_QK), F32)
    cc = jnp.concatenate([ones_l, cc, ones_r], axis=1)
    sc = jnp.concatenate([0 * ones_l, sc, 0 * ones_r], axis=1)
    ident_c = jnp.ones((TOK_TILE, LANE), F32)
    ident_s = jnp.zeros((TOK_TILE, LANE), F32)
    return tuple(jnp.concatenate([a, i], axis=0) for a, i in ((ca, ident_c), (sa, ident_s), (cc, ident_c), (sc, ident_s)))


def _block_diag_ones(n, blk):
    i = np.arange(n) // blk
    return jnp.asarray((i[:, None] == i[None, :]).astype(np.float32), BF16)


def _dot_nt(a, b):
    return lax.dot_general(a, b, (((1,), (1,)), ((), ())), preferred_element_type=F32)


def _swa_heads(sink_ref, q, k_list, v_list, mask_list, o_ref):
    lane = lax.broadcasted_iota(jnp.int32, (1, LANE), 1)
    lower = lane < SWA_HEAD_DIM
    cols = []
    for cgrp in range(2):
        qc = q[:, cgrp * LANE:(cgrp + 1) * LANE]
        halves = []
        for half in range(2):
            head = SWA_HEAD_ORDER[2 * cgrp + half]
            qh = jnp.where(lower if half == 0 else ~lower, qc, jnp.zeros_like(qc))
            sink = sink_ref[head]
            scores = []
            m = jnp.full((q.shape[0], 1), sink, F32)
            for kb, mk in zip(k_list, mask_list):
                s = _dot_nt(qh, kb)
                if mk is not None:
                    s = jnp.where(mk, s, -jnp.inf)
                scores.append(s)
                m = jnp.maximum(m, jnp.max(s, axis=-1, keepdims=True))
            l = jnp.exp(sink - m)
            acc = jnp.zeros((q.shape[0], LANE), F32)
            for s, vb in zip(scores, v_list):
                p = jnp.exp(s - m)
                l = l + jnp.sum(p, axis=-1, keepdims=True)
                acc = acc + jnp.dot(p.astype(BF16), vb, preferred_element_type=F32)
            halves.append(acc * (1.0 / l))
        cols.append(jnp.where(lower, halves[0], halves[1]))
    o_ref[...] = jnp.concatenate(cols, axis=1).astype(o_ref.dtype)


SWA_Q_BLOCKS = 2
SWA_KEY_OFFSETS = tuple(range(-1, SWA_Q_BLOCKS + 1))


def _swa_kernel(sink_ref, q_ref, *refs, n_blocks):
    kv_refs, kvc_ref, o_ref = refs[:len(SWA_KEY_OFFSETS)], refs[-2], refs[-1]
    j = pl.program_id(1)
    n_steps = n_blocks // SWA_Q_BLOCKS
    kc, vc = kvc_ref[:, 0:SWA_KV], kvc_ref[:, SWA_KV:]

    @pl.when(j < n_steps)
    def _():
        rows = SWA_Q_BLOCKS * ATTN_BLOCK
        r = lax.broadcasted_iota(jnp.int32, (rows, ATTN_BLOCK), 0)
        c = lax.broadcasted_iota(jnp.int32, (rows, ATTN_BLOCK), 1)
        masks = []
        for off in SWA_KEY_OFFSETS:
            blk = j * SWA_Q_BLOCKS + off
            in_seq = (blk >= 0) & (blk < n_blocks)
            masks.append(jnp.abs(off * ATTN_BLOCK + c - r) <= jnp.where(in_seq, SWA_WINDOW, -1))
        ks = [ref[:, 0:SWA_KV] for ref in kv_refs] + [kc]
        vs = [ref[:, SWA_KV:] for ref in kv_refs] + [vc]
        _swa_heads(sink_ref, q_ref[...], ks, vs, masks + [None], o_ref)

    @pl.when(j >= n_steps)
    def _():
        _swa_heads(sink_ref, q_ref[...], [kc], [vc], [None], o_ref)


def _swa_attention(sink, qa, kva, n_batch, seq, ctx_len, with_ctx):
    rows = SWA_Q_BLOCKS * ATTN_BLOCK
    nb = seq // ATTN_BLOCK
    nq = seq // rows
    ncq = ctx_len // rows
    n_lat = n_batch * seq
    steps = nq + (ncq if with_ctx else 0)
    n_tok = n_lat + (n_batch * ctx_len if with_ctx else 0)

    def q_map(b, j, s):
        return (jnp.where(j < nq, b * nq + j, n_lat // rows + b * ncq + (j - nq)), 0)

    def kv_map(off):
        def f(b, j, s):
            jj = jnp.clip(jnp.minimum(j, nq - 1) * SWA_Q_BLOCKS + off, 0, nb - 1)
            return (b * nb + jj, 0)
        return f

    return pl.pallas_call(
        functools.partial(_swa_kernel, n_blocks=nb),
        out_shape=jax.ShapeDtypeStruct((n_tok, SWA_Q), BF16),
        grid_spec=pltpu.PrefetchScalarGridSpec(
            num_scalar_prefetch=1,
            grid=(n_batch, steps),
            in_specs=[pl.BlockSpec((rows, SWA_Q), q_map)]
            + [pl.BlockSpec((ATTN_BLOCK, 2 * SWA_KV), kv_map(off)) for off in SWA_KEY_OFFSETS]
            + [pl.BlockSpec((ctx_len, 2 * SWA_KV), lambda b, j, s: (n_lat // ctx_len + b, 0))],
            out_specs=pl.BlockSpec((rows, SWA_Q), q_map)),
        compiler_params=_cparams("arbitrary", "arbitrary"),
        name="swa_attention",
    )(sink, qa, *([kva] * len(SWA_KEY_OFFSETS)), kva)


def _mla_heads(q_ref, k_refs, v_refs, o_ref):
    for h in range(MLA_HEADS):
        sl = slice(h * MLA_HEAD_PAD, (h + 1) * MLA_HEAD_PAD)
        qh = q_ref[:, sl]
        scores = [_dot_nt(qh, k_ref[:, sl]) for k_ref in k_refs]
        m = jnp.max(scores[0], axis=-1, keepdims=True)
        for s in scores[1:]:
            m = jnp.maximum(m, jnp.max(s, axis=-1, keepdims=True))
        l = jnp.zeros_like(m)
        acc = jnp.zeros((qh.shape[0], MLA_HEAD_PAD), F32)
        for s, v_ref in zip(scores, v_refs):
            p = jnp.exp(s - m)
            l = l + jnp.sum(p, axis=-1, keepdims=True)
            acc = acc + jnp.dot(p.astype(BF16), v_ref[:, sl], preferred_element_type=F32)
        o_ref[:, sl] = (acc * (1.0 / l)).astype(o_ref.dtype)


def _mla_kernel(q_ref, kl_ref, vl_ref, kc_ref, vc_ref, o_ref, *, n_lat_steps):
    j = pl.program_id(1)

    @pl.when(j < n_lat_steps)
    def _():
        _mla_heads(q_ref, [kl_ref, kc_ref], [vl_ref, vc_ref], o_ref)

    @pl.when(j >= n_lat_steps)
    def _():
        _mla_heads(q_ref, [kc_ref], [vc_ref], o_ref)


def _mla_attention(qm, km, vm, n_batch, seq, ctx_len, with_ctx):
    tq = MLA_Q_TILE
    nq = seq // tq
    ncq = ctx_len // tq
    n_lat = n_batch * seq
    steps = nq + (ncq if with_ctx else 0)
    n_tok = n_lat + (n_batch * ctx_len if with_ctx else 0)
    wide = MLA_HEADS * MLA_HEAD_PAD

    def q_map(b, j):
        return (jnp.where(j < nq, b * nq + j, n_lat // tq + b * ncq + (j - nq)), 0)

    lat_map = lambda b, j: (b, 0)
    ctx_map = lambda b, j: (n_lat // ctx_len + b, 0)
    return pl.pallas_call(
        functools.partial(_mla_kernel, n_lat_steps=nq),
        out_shape=jax.ShapeDtypeStruct((n_tok, wide), BF16),
        grid=(n_batch, steps),
        in_specs=[pl.BlockSpec((tq, wide), q_map),
                  pl.BlockSpec((seq, wide), lat_map),
                  pl.BlockSpec((seq, wide), lat_map),
                  pl.BlockSpec((ctx_len, wide), ctx_map),
                  pl.BlockSpec((ctx_len, wide), ctx_map)],
        out_specs=pl.BlockSpec((tq, wide), q_map),
        compiler_params=_cparams("arbitrary", "arbitrary"),
        name="mla_attention",
    )(qm, km, vm, km, vm)


def _outproj_kernel(oa_ref, of_ref, ob_ref, z_ref, oc_ref, x_ref, wa_ref, wb_ref, wc_ref, gg_ref, g_ref, m_ref,
                    xo_ref, h_ref):
    m = m_ref[0]
    ob = of_ref[...] + ob_ref[...]
    z = z_ref[...]
    heads = []
    for h in range(GDN_HEADS):
        oh = ob[:, h * GDN_HEAD_DIM:(h + 1) * GDN_HEAD_DIM]
        heads.append(oh * lax.rsqrt(jnp.mean(oh * oh, axis=-1, keepdims=True) + EPS))
    gated = jnp.concatenate(heads, axis=1) * gg_ref[...] * (z * jax.nn.sigmoid(z))
    o = (jnp.dot(oa_ref[...], wa_ref[...], preferred_element_type=F32)
         + jnp.dot(gated.astype(BF16), wb_ref[...], preferred_element_type=F32)
         + jnp.dot(oc_ref[...], wc_ref[...], preferred_element_type=F32))
    xn = x_ref[...] + m[2:3] * o
    xo_ref[...] = xn
    h_ref[...] = _modulated_norm(xn, g_ref[...], m[3:4], m[4:5])


def _out_projection(o_a, o_fwd, o_bwd, p_flat, o_c, x_flat, wa, wb, wc, gdn_gain, gain2, mod_l, n_tok, n_batch, seq):
    d = x_flat.shape[1]
    tm = TOK_TILE
    tps = seq // tm
    row = lambda i: (i, 0)
    const = lambda i: (0, 0)
    return pl.pallas_call(
        _outproj_kernel,
        out_shape=(jax.ShapeDtypeStruct((n_tok, d), F32), jax.ShapeDtypeStruct((n_tok, d), F32)),
        grid=(n_tok // tm,),
        in_specs=[pl.BlockSpec((tm, o_a.shape[1]), row),
                  pl.BlockSpec((tm, GDN_W), row),
                  pl.BlockSpec((tm, GDN_W), row),
                  pl.BlockSpec((tm, GDN_W), lambda i: (i, 3)),
                  pl.BlockSpec((tm, o_c.shape[1]), row),
                  pl.BlockSpec((tm, d), row),
                  pl.BlockSpec(wa.shape, const),
                  pl.BlockSpec(wb.shape, const),
                  pl.BlockSpec(wc.shape, const),
                  pl.BlockSpec((1, GDN_W), const),
                  pl.BlockSpec((1, d), const),
                  pl.BlockSpec((1, N_MOD, d), lambda i: (_group_of_tile(i, tps, n_batch), 0, 0))],
        out_specs=(pl.BlockSpec((tm, d), row), pl.BlockSpec((tm, d), row)),
        compiler_params=_cparams("arbitrary"),
        name="out_projection",
    )(o_a, o_fwd, o_bwd, p_flat, o_c, x_flat, wa, wb, wc, gdn_gain, gain2.reshape(1, d), mod_l)


def _route_kernel(h_ref, rw_ref, rb_ref, idx_ref, gate_ref, cnt_ref, base_ref):
    step = pl.program_id(0)
    tm = h_ref.shape[0]

    @pl.when(step == 0)
    def _():
        base_ref[...] = jnp.zeros_like(base_ref)

    logits = lax.dot_general(rw_ref[...], h_ref[...].astype(BF16), (((1,), (1,)), ((), ())),
                             preferred_element_type=F32) + rb_ref[...]
    e_iota = lax.broadcasted_iota(jnp.int32, logits.shape, 0)
    work = logits
    tops, picks = [], []
    for _k in range(TOP_K):
        mx = jnp.max(work, axis=0, keepdims=True)
        pick = jnp.min(jnp.where(work == mx, e_iota, N_EXPERTS), axis=0, keepdims=True)
        work = jnp.where(e_iota == pick, -jnp.inf, work)
        tops.append(mx)
        picks.append(pick)
    exps = [jnp.exp(t - tops[0]) for t in tops]
    denom = exps[0] + exps[1] + exps[2] + exps[3]
    sel = jnp.zeros(logits.shape, F32)
    for pick in picks:
        sel = sel + (e_iota == pick).astype(F32)
    row = lax.broadcasted_iota(jnp.int32, (tm, tm), 0)
    col = lax.broadcasted_iota(jnp.int32, (tm, tm), 1)
    before = (row < col).astype(BF16)
    cnt = jnp.dot(sel.astype(BF16), before, preferred_element_type=F32) + base_ref[:, 0:1]
    ranks = [jnp.sum(jnp.where(e_iota == pick, cnt, 0.0), axis=0, keepdims=True) for pick in picks]
    idx_ref[0] = jnp.concatenate(picks + [r.astype(jnp.int32) for r in ranks], axis=0)
    gate_rows = jnp.concatenate([e / denom for e in exps] + [jnp.zeros((LANE - TOP_K, tm), F32)], axis=0)
    gate_ref[...] = jnp.transpose(gate_rows)
    base_ref[...] = base_ref[...] + jnp.sum(sel, axis=1, keepdims=True)
    cnt_ref[...] = base_ref[...]


def _route(h_flat, n_tok, rw_t_bf, rb):
    d = h_flat.shape[1]
    tm = ROUTE_TILE
    nt = n_tok // tm
    return pl.pallas_call(
        _route_kernel,
        out_shape=(jax.ShapeDtypeStruct((nt, 2 * TOP_K, tm), jnp.int32),
                   jax.ShapeDtypeStruct((n_tok, LANE), F32),
                   jax.ShapeDtypeStruct((N_EXPERTS, LANE), F32)),
        grid=(nt,),
        in_specs=[pl.BlockSpec((tm, d), lambda i: (i, 0)),
                  pl.BlockSpec((N_EXPERTS, d), lambda i: (0, 0)),
                  pl.BlockSpec((N_EXPERTS, 1), lambda i: (0, 0))],
        out_specs=(pl.BlockSpec((1, 2 * TOP_K, tm), lambda i: (i, 0, 0)),
                   pl.BlockSpec((tm, LANE), lambda i: (i, 0)),
                   pl.BlockSpec((N_EXPERTS, LANE), lambda i: (0, 0))),
        scratch_shapes=[pltpu.VMEM((N_EXPERTS, LANE), F32)],
        compiler_params=_cparams("arbitrary"),
        name="moe_route",
    )(h_flat, rw_t_bf, rb.reshape(N_EXPERTS, 1))


def _dispatch_kernel(lo_ref, hi_ref, slot_ref, h_ref, xs_out, zero_ref, sem, zero_sem):
    tm = slot_ref.shape[2]

    @pl.when(pl.program_id(0) == pl.num_programs(0) - 1)
    def _():
        zero_ref[...] = jnp.zeros_like(zero_ref)

        def zero_copy(r):
            return pltpu.make_async_copy(zero_ref.at[pl.ds(0, 1)], xs_out.at[pl.ds(r, 1)], zero_sem)

        def start(r, carry):
            zero_copy(r).start()
            return carry

        def wait(r, carry):
            zero_copy(r).wait()
            return carry

        def per_expert(e, carry):
            lax.fori_loop(lo_ref[e], hi_ref[e], start, 0)
            lax.fori_loop(lo_ref[e], hi_ref[e], wait, 0)
            return carry

        lax.fori_loop(0, N_EXPERTS, per_expert, 0)

    def row_copy(t, k):
        return pltpu.make_async_copy(h_ref.at[pl.ds(t, 1)], xs_out.at[pl.ds(slot_ref[0, k, t], 1)], sem)

    def issue(i, carry):
        for u in range(ROW_UNROLL):
            for k in range(TOP_K):
                row_copy(i * ROW_UNROLL + u, k).start(priority=k % 2)
        return carry

    lax.fori_loop(0, tm // ROW_UNROLL, issue, 0)

    def drain(i, carry):
        for u in range(ROW_UNROLL):
            for k in range(TOP_K):
                row_copy(i * ROW_UNROLL + u, k).wait()
        return carry

    lax.fori_loop(0, tm // ROW_UNROLL, drain, 0)


def _dispatch(pad_lo, pad_hi, slots, h_flat, n_tok, cap):
    d = h_flat.shape[1]
    tm = slots.shape[2]
    return pl.pallas_call(
        _dispatch_kernel,
        out_shape=jax.ShapeDtypeStruct((cap, d), F32),
        grid_spec=pltpu.PrefetchScalarGridSpec(
            num_scalar_prefetch=2,
            grid=(n_tok // tm,),
            in_specs=[pl.BlockSpec((1, TOP_K, tm), lambda i, lo, hi: (i, 0, 0), memory_space=pltpu.SMEM),
                      pl.BlockSpec((tm, d), lambda i, lo, hi: (i, 0))],
            out_specs=pl.BlockSpec(memory_space=pl.ANY),
            scratch_shapes=[pltpu.VMEM((SUBLANE, d), F32), pltpu.SemaphoreType.DMA(()), pltpu.SemaphoreType.DMA(())]),
        compiler_params=_cparams("arbitrary"),
        name="moe_dispatch",
    )(pad_lo, pad_hi, slots, h_flat)


SPLIT_COLS = 1024


def _split_kernel(w_ref, sel_ref, wg_ref, wl_ref):
    for j in range(w_ref.shape[2] // (2 * LANE)):
        blk = w_ref[0, :, 2 * LANE * j:2 * LANE * (j + 1)].astype(BF16)
        r = jnp.dot(blk, sel_ref[...], preferred_element_type=F32)
        wg_ref[0, :, LANE * j:LANE * (j + 1)] = r[:, :LANE].astype(BF16)
        wl_ref[0, :, LANE * j:LANE * (j + 1)] = r[:, LANE:].astype(BF16)


def _split_glu_columns(w_gu):
    nl, ne, d, f2 = w_gu.shape
    src = np.arange(2 * LANE)
    dst = np.where(src % 2 == 0, src // 2, LANE + src // 2)
    sel = jnp.asarray((dst[:, None] == np.arange(2 * LANE)[None, :]).astype(np.float32), BF16)
    tn = SPLIT_COLS
    out = jax.ShapeDtypeStruct((nl * ne, d, f2 // 2), BF16)
    wg, wl = pl.pallas_call(
        _split_kernel,
        out_shape=(out, out),
        grid=(nl * ne, f2 // tn),
        in_specs=[pl.BlockSpec((1, d, tn), lambda e, j: (e, 0, j)),
                  pl.BlockSpec((2 * LANE, 2 * LANE), lambda e, j: (0, 0))],
        out_specs=(pl.BlockSpec((1, d, tn // 2), lambda e, j: (e, 0, j)),
                   pl.BlockSpec((1, d, tn // 2), lambda e, j: (e, 0, j))),
        compiler_params=_cparams("arbitrary", "arbitrary"),
        name="split_glu_columns",
    )(w_gu.reshape(nl * ne, d, f2), sel)
    return wg.reshape(nl, ne, d, f2 // 2), wl.reshape(nl, ne, d, f2 // 2)


def _expert_kernel(be_ref, nu_ref, xs_ref, wg_ref, wl_ref, bg_ref, bl_ref, wd_ref, bd_ref, ys_ref, wd_bf_ref):
    i = pl.program_id(0)
    last = nu_ref[0] - 1
    expert = be_ref[jnp.minimum(i, last)]
    prev_expert = be_ref[jnp.minimum(jnp.maximum(i - 1, 0), last)]

    @pl.when(jnp.logical_or(i == 0, expert != prev_expert))
    def _():
        wd_bf_ref[...] = wd_ref[0].astype(BF16)

    @pl.when(i < nu_ref[0])
    def _():
        x = xs_ref[...].astype(BF16)
        g = jnp.dot(x, wg_ref[0], preferred_element_type=F32) + bg_ref[0]
        u = jnp.dot(x, wl_ref[0], preferred_element_type=F32) + bl_ref[0]
        g = jnp.minimum(g, SWIGLU_LIMIT)
        u = jnp.clip(u, -SWIGLU_LIMIT, SWIGLU_LIMIT)
        act = g * jax.nn.sigmoid(SWIGLU_ALPHA * g) * (u + 1.0)
        ys_ref[...] = jnp.dot(act.astype(BF16), wd_bf_ref[...], preferred_element_type=F32) + bd_ref[0]

    @pl.when(pl.program_id(0) >= nu_ref[0])
    def _():
        ys_ref[...] = jnp.zeros_like(ys_ref)


def _experts(block_e, n_used, xs, wg, wl, bg, bl, wd, bd):
    cap, d = xs.shape
    f = wg.shape[2]
    tm = MOE_TILE
    nb = cap // tm

    def row_map(i, be, nu):
        return (jnp.minimum(i, nu[0] - 1), 0)

    def w_map(i, be, nu):
        return (be[jnp.minimum(i, nu[0] - 1)], 0, 0)

    return pl.pallas_call(
        _expert_kernel,
        out_shape=jax.ShapeDtypeStruct((cap, d), F32),
        grid_spec=pltpu.PrefetchScalarGridSpec(
            num_scalar_prefetch=2,
            grid=(nb,),
            in_specs=[pl.BlockSpec((tm, d), row_map),
                      pl.BlockSpec((1, d, f), w_map),
                      pl.BlockSpec((1, d, f), w_map),
                      pl.BlockSpec((1, 1, f), w_map),
                      pl.BlockSpec((1, 1, f), w_map),
                      pl.BlockSpec((1, f, d), w_map),
                      pl.BlockSpec((1, 1, d), w_map)],
            out_specs=pl.BlockSpec((tm, d), lambda i, be, nu: (i, 0)),
            scratch_shapes=[pltpu.VMEM((f, d), BF16)]),
        compiler_params=_cparams("arbitrary"),
        name="moe_experts",
    )(block_e, n_used, xs, wg, wl, bg, bl, wd, bd)


def _combine_kernel(slot_ref, ys_hbm, x_ref, gate_ref, m_ref, o_ref, buf, sem):
    tm = x_ref.shape[0]

    def row_copy(t, k):
        return pltpu.make_async_copy(ys_hbm.at[pl.ds(slot_ref[0, k, t], 1)], buf.at[k, pl.ds(t, 1)], sem)

    def issue(i, carry):
        for u in range(ROW_UNROLL):
            for k in range(TOP_K):
                row_copy(i * ROW_UNROLL + u, k).start(priority=k % 2)
        return carry

    lax.fori_loop(0, tm // ROW_UNROLL, issue, 0)

    def drain(i, carry):
        for u in range(ROW_UNROLL):
            for k in range(TOP_K):
                row_copy(i * ROW_UNROLL + u, k).wait()
        return carry

    lax.fori_loop(0, tm // ROW_UNROLL, drain, 0)
    gates = gate_ref[...]
    y = buf[0] * gates[:, 0:1]
    for k in range(1, TOP_K):
        y = y + buf[k] * gates[:, k:k + 1]
    o_ref[...] = x_ref[...] + m_ref[0][5:6] * y


def _combine(slots, ys, x_flat, gates, mod_l, n_tok, n_batch, seq):
    d = x_flat.shape[1]
    tm = slots.shape[2]
    tps = seq // tm
    return pl.pallas_call(
        _combine_kernel,
        out_shape=jax.ShapeDtypeStruct((n_tok, d), F32),
        grid=(n_tok // tm,),
        in_specs=[pl.BlockSpec((1, TOP_K, tm), lambda i: (i, 0, 0), memory_space=pltpu.SMEM),
                  pl.BlockSpec(memory_space=pl.ANY),
                  pl.BlockSpec((tm, d), lambda i: (i, 0)),
                  pl.BlockSpec((tm, LANE), lambda i: (i, 0)),
                  pl.BlockSpec((1, N_MOD, d), lambda i: (_group_of_tile(i, tps, n_batch), 0, 0))],
        out_specs=pl.BlockSpec((tm, d), lambda i: (i, 0)),
        scratch_shapes=[pltpu.VMEM((TOP_K, tm, d), F32), pltpu.SemaphoreType.DMA(())],
        compiler_params=_cparams("arbitrary"),
        name="moe_combine",
    )(slots, ys, x_flat, gates, mod_l)


def _retile_slots(slots, tm):
    nt, k, t = slots.shape
    return slots.reshape(nt, k, t // tm, tm).transpose(0, 2, 1, 3).reshape(nt * (t // tm), k, tm)


def _moe(h_flat, x_flat, n_tok, mod_l, rw_t_bf, rb, wg, wl, bg, bl, wd, bd, n_batch, seq):
    idx, gates, counts = _route(h_flat, n_tok, rw_t_bf, rb)
    counts = counts[:, 0].astype(jnp.int32)
    padded = (counts + MOE_TILE - 1) // MOE_TILE * MOE_TILE
    pad_end = jnp.cumsum(padded)
    pad_start = pad_end - padded
    n_blocks = -(-(n_tok * TOP_K) // MOE_TILE) + N_EXPERTS
    cap = n_blocks * MOE_TILE
    block_row = jnp.arange(n_blocks, dtype=jnp.int32) * MOE_TILE
    block_e = jnp.minimum(jnp.sum(block_row[:, None] >= pad_end[None, :], axis=1), N_EXPERTS - 1).astype(jnp.int32)
    n_used = (pad_end[-1:] // MOE_TILE).astype(jnp.int32)
    experts = idx[:, :TOP_K, :]
    start = jnp.sum(jnp.where(experts[..., None] == jnp.arange(N_EXPERTS), pad_start, 0), axis=-1)
    slots = start + idx[:, TOP_K:, :]
    slots = _retile_slots(slots, MOVE_TILE)
    xs = _dispatch(pad_start + counts, pad_end, slots, h_flat, n_tok, cap)
    ys = _experts(block_e, n_used, xs, wg, wl, bg, bl, wd, bd)
    return _combine(slots, ys, x_flat, gates, mod_l, n_tok, n_batch, seq)


GDN_TILE = 256
HALO = SUBLANE


def _gdn_prep_kernel(q_ref, k_ref, v_ref, qp_ref, kp_ref, vp_ref, qn_ref, kn_ref, vn_ref, small_ref, w_ref,
                     a_ref, bias_ref, qkv_ref, gb_ref, xe_ref, *, lat_tiles, lat_tps, ctx_tps):
    i = pl.program_id(0)
    tm = q_ref.shape[0]
    in_lat = i < lat_tiles
    pos = jnp.where(in_lat, i % lat_tps, (i - lat_tiles) % ctx_tps)
    last = jnp.where(in_lat, lat_tps - 1, ctx_tps - 1)
    keep_prev = jnp.where(pos > 0, 1.0, 0.0)
    keep_next = jnp.where(pos < last, 1.0, 0.0)
    half = GDN_CONV // 2
    for g, (x_ref, p_ref, n_ref) in enumerate(((q_ref, qp_ref, qn_ref), (k_ref, kp_ref, kn_ref),
                                               (v_ref, vp_ref, vn_ref))):
        xe_ref[g, 0:HALO, :] = p_ref[...] * keep_prev
        xe_ref[g, HALO:HALO + tm, :] = x_ref[...]
        xe_ref[g, HALO + tm:2 * HALO + tm, :] = n_ref[...] * keep_next
        w = w_ref[:, g * GDN_W:(g + 1) * GDN_W]
        y = xe_ref[g, HALO - half:HALO - half + tm, :] * w[0:1]
        for j in range(1, GDN_CONV):
            y = y + xe_ref[g, HALO - half + j:HALO - half + j + tm, :] * w[j:j + 1]
        y = y * jax.nn.sigmoid(y)
        if g < 2:
            heads = []
            for h in range(GDN_HEADS):
                yh = y[:, h * GDN_HEAD_DIM:(h + 1) * GDN_HEAD_DIM]
                yh = yh * lax.rsqrt(jnp.sum(yh * yh, axis=-1, keepdims=True) + EPS)
                heads.append(yh * GDN_HEAD_DIM ** -0.5 if g == 0 else yh)
            y = jnp.concatenate(heads, axis=1)
        qkv_ref[:, g * GDN_W:(g + 1) * GDN_W] = y.astype(BF16)
    sm = small_ref[...]
    z = sm + bias_ref[...]
    softplus = jnp.maximum(z, 0.0) + jnp.log(1.0 + jnp.exp(-jnp.abs(z)))
    lane = lax.broadcasted_iota(jnp.int32, (1, LANE), 1)
    is_g = (lane >= SMALL_GA) & (lane < SMALL_GB)
    is_b = (lane >= SMALL_GB) & (lane < SMALL_GB + 2 * GDN_HEADS)
    gb_ref[...] = jnp.where(is_g, a_ref[...] * softplus, jnp.where(is_b, jax.nn.sigmoid(sm), 0.0))


def _gdn_prep(pg, small, conv_w, neg_a, dt_bias, n_batch, seq, ctx_len):
    n_tok = pg.shape[0]
    tm = GDN_TILE
    lat_tiles = n_batch * seq // tm
    per_tile = tm // HALO
    n_halo_blocks = n_tok // HALO

    def main(g):
        return pl.BlockSpec((tm, GDN_W), lambda i: (i, g))

    def prev(g):
        return pl.BlockSpec((HALO, GDN_W), lambda i: (jnp.maximum(i * per_tile - 1, 0), g))

    def nxt(g):
        return pl.BlockSpec((HALO, GDN_W), lambda i: (jnp.minimum((i + 1) * per_tile, n_halo_blocks - 1), g))

    const = lambda i: (0, 0)
    return pl.pallas_call(
        functools.partial(_gdn_prep_kernel, lat_tiles=lat_tiles, lat_tps=seq // tm, ctx_tps=ctx_len // tm),
        out_shape=(jax.ShapeDtypeStruct((n_tok, 3 * GDN_W), BF16), jax.ShapeDtypeStruct((n_tok, LANE), F32)),
        grid=(n_tok // tm,),
        in_specs=[main(0), main(1), main(2), prev(0), prev(1), prev(2), nxt(0), nxt(1), nxt(2),
                  pl.BlockSpec((tm, LANE), lambda i: (i, 0)),
                  pl.BlockSpec(conv_w.shape, const), pl.BlockSpec((1, LANE), const), pl.BlockSpec((1, LANE), const)],
        out_specs=(pl.BlockSpec((tm, 3 * GDN_W), lambda i: (i, 0)), pl.BlockSpec((tm, LANE), lambda i: (i, 0))),
        scratch_shapes=[pltpu.VMEM((3, tm + 2 * HALO, GDN_W), F32)],
        compiler_params=_cparams("arbitrary"),
        name="gdn_prep",
    )(*([pg] * 9), small, conv_w, neg_a, dt_bias)


GDN_SUB = 16
GDN_CHUNKS_PER_STEP = 4


def _bdot(a, b):
    return jnp.dot(a.astype(BF16), b.astype(BF16), preferred_element_type=F32)


def _dot_tn(a, b):
    return lax.dot_general(a, b, (((0,), (0,)), ((), ())), preferred_element_type=F32)


def _each(f, *lists):
    return [f(*args) for args in zip(*lists)]


def _unit_tri_solve(n, rhs, same_blk, eye):
    nd = _each(lambda a: jnp.where(same_blk, a, 0.0).astype(BF16), n)
    nl = _each(lambda a: jnp.where(same_blk, 0.0, a).astype(BF16), n)
    nd2 = _each(lambda a: jnp.dot(a, a, preferred_element_type=F32).astype(BF16), nd)
    nd4 = _each(lambda a: jnp.dot(a, a, preferred_element_type=F32).astype(BF16), nd2)
    nd8 = _each(lambda a: jnp.dot(a, a, preferred_element_type=F32).astype(BF16), nd4)
    m = _each(lambda a: eye - a.astype(F32), nd)
    m = _each(lambda a, b: a + _bdot(a, b), m, nd2)
    m = _each(lambda a, b: a + _bdot(a, b), m, nd4)
    dinv = _each(lambda a, b: (a + _bdot(a, b)).astype(BF16), m, nd8)
    p = _each(lambda a, b: jnp.dot(a, b, preferred_element_type=F32).astype(BF16), dinv, nl)
    p2 = _each(lambda a: jnp.dot(a, a, preferred_element_type=F32).astype(BF16), p)
    y = _each(lambda a, b: _bdot(a, b), dinv, rhs)
    y = _each(lambda a, b: a + _bdot(b, a), y, p2)
    return _each(lambda a, b: a - _bdot(b, a), y, p)


def _gdn_scan_kernel(x0_ref, gb0_ref, x1_ref, gb1_ref, o0_ref, o1_ref, state_ref, *, n_sub):
    assert GDN_CHUNK == 4 * GDN_SUB

    @pl.when(pl.program_id(1) == 0)
    def _():
        state_ref[...] = jnp.zeros_like(state_ref)

    n = GDN_CHUNK
    r = lax.broadcasted_iota(jnp.int32, (n, n), 0)
    c = lax.broadcasted_iota(jnp.int32, (n, n), 1)
    same_blk = (r // GDN_SUB) == (c // GDN_SUB)
    eye = (r == c).astype(F32)
    dirs = ((x0_ref, gb0_ref, o0_ref), (x1_ref, gb1_ref, o1_ref))
    heads = [(d, h) for d in range(2) for h in range(GDN_HEADS)]
    chunk_at = [list(range(n_sub)), list(range(n_sub - 1, -1, -1))]
    probs = [(d, h, chunk_at[d][s]) for s in range(n_sub) for d, h in heads]
    incl = [r >= c, r <= c]
    strict = [r > c, r < c]
    end_row = [n - 1, 0]

    gbv, gc_all, gc_t = {}, {}, {}
    for d in range(2):
        tri = incl[d].astype(BF16)
        for ch in range(n_sub):
            g = dirs[d][1][ch * n:(ch + 1) * n, :]
            hi = g.astype(BF16)
            rem = g - hi.astype(F32)
            mid = rem.astype(BF16)
            lo = (rem - mid.astype(F32)).astype(BF16)
            gc = (jnp.dot(tri, hi, preferred_element_type=F32) + jnp.dot(tri, mid, preferred_element_type=F32)
                  + jnp.dot(tri, lo, preferred_element_type=F32))
            gbv[d, ch], gc_all[d, ch], gc_t[d, ch] = g, gc, jnp.transpose(gc)

    def head_cols(d, part, h, ch):
        lo_col = part * GDN_W + h * GDN_HEAD_DIM
        return dirs[d][0][ch * n:(ch + 1) * n, lo_col:lo_col + GDN_HEAD_DIM]

    lane_g = [SMALL_GA + GDN_HEADS * d + h for d, h, ch in probs]
    lane_b = [SMALL_GB + GDN_HEADS * d + h for d, h, ch in probs]
    gcol = [gc_all[d, ch][:, lg:lg + 1] for (d, h, ch), lg in zip(probs, lane_g)]
    grow = [gc_t[d, ch][lg:lg + 1, :] for (d, h, ch), lg in zip(probs, lane_g)]
    gend = [gc_all[d, ch][end_row[d]:end_row[d] + 1, lg:lg + 1] for (d, h, ch), lg in zip(probs, lane_g)]
    beta = [gbv[d, ch][:, lb:lb + 1] for (d, h, ch), lb in zip(probs, lane_b)]
    q = [head_cols(d, 0, h, ch) for d, h, ch in probs]
    k = [head_cols(d, 1, h, ch) for d, h, ch in probs]
    v = [head_cols(d, 2, h, ch) for d, h, ch in probs]
    kf = _each(lambda a: a.astype(F32), k)
    dec = [jnp.where(incl[d], jnp.exp(jnp.where(incl[d], gc_ - gr_, 0.0)), 0.0)
           for (d, h, ch), gc_, gr_ in zip(probs, gcol, grow)]
    qk_kk = _each(lambda q_, k_: _dot_nt(jnp.concatenate([q_, k_], axis=0), k_), q, k)
    qkd = _each(lambda a, dec_: (a[:n] * dec_).astype(BF16), qk_kk, dec)
    nmat = [jnp.where(strict[d], b_ * a[n:] * dec_, 0.0) for (d, h, ch), b_, a, dec_ in zip(probs, beta, qk_kk, dec)]
    egc = _each(jnp.exp, gcol)
    rhs = _each(lambda v_, kf_, b_, e_: jnp.concatenate([v_.astype(F32) * b_, kf_ * (b_ * e_)], axis=1).astype(BF16),
                v, kf, beta, egc)
    uw = _unit_tri_solve(nmat, rhs, same_blk, eye)
    wq = _each(lambda uw_, q_, e_: jnp.concatenate([uw_[:, GDN_HEAD_DIM:], q_.astype(F32) * e_], axis=0).astype(BF16),
               uw, q, egc)
    k_dec = _each(lambda kf_, ge_, gc_: (kf_ * jnp.exp(ge_ - gc_)).astype(BF16), kf, gend, gcol)
    g_end = _each(jnp.exp, gend)

    state = [state_ref[i] for i in range(len(heads))]
    for s in range(n_sub):
        sl = slice(s * len(heads), (s + 1) * len(heads))
        s_bf = _each(lambda a: a.astype(BF16), state)
        ws_qs = _each(lambda a, s_: jnp.dot(a, s_, preferred_element_type=F32), wq[sl], s_bf)
        v_new = _each(lambda uw_, a: (uw_[:, :GDN_HEAD_DIM] - a[:n]).astype(BF16), uw[sl], ws_qs)
        o_local = _each(lambda a, vn_: jnp.dot(a, vn_, preferred_element_type=F32), qkd[sl], v_new)
        s_add = _each(_dot_tn, k_dec[sl], v_new)
        state = _each(lambda ge_, st_, add_: ge_ * st_ + add_, g_end[sl], state, s_add)
        for (d, h, ch), a, ol in zip(probs[sl], ws_qs, o_local):
            dirs[d][2][ch * n:(ch + 1) * n, h * GDN_HEAD_DIM:(h + 1) * GDN_HEAD_DIM] = a[n:] + ol
    for i in range(len(heads)):
        state_ref[i] = state[i]


def _gdn_scan(qkv, gbeta, n_batch, seq, ctx_len):
    n_tok = qkv.shape[0]
    n = GDN_CHUNK * GDN_CHUNKS_PER_STEP
    lat_c, ctx_c = seq // n, ctx_len // n
    ctx0 = n_batch * seq // n

    def fwd(b, j):
        return (jnp.where(j < ctx_c, ctx0 + b * ctx_c + j, b * lat_c + (j - ctx_c)), 0)

    def bwd(b, j):
        return (jnp.where(j < ctx_c, ctx0 + b * ctx_c + (ctx_c - 1 - j), b * lat_c + (lat_c - 1 - (j - ctx_c))), 0)

    return pl.pallas_call(
        functools.partial(_gdn_scan_kernel, n_sub=GDN_CHUNKS_PER_STEP),
        out_shape=(jax.ShapeDtypeStruct((n_tok, GDN_W), F32), jax.ShapeDtypeStruct((n_tok, GDN_W), F32)),
        grid=(n_batch, ctx_c + lat_c),
        in_specs=[pl.BlockSpec((n, 3 * GDN_W), fwd), pl.BlockSpec((n, LANE), fwd),
                  pl.BlockSpec((n, 3 * GDN_W), bwd), pl.BlockSpec((n, LANE), bwd)],
        out_specs=(pl.BlockSpec((n, GDN_W), fwd), pl.BlockSpec((n, GDN_W), bwd)),
        scratch_shapes=[pltpu.VMEM((2 * GDN_HEADS, GDN_HEAD_DIM, GDN_HEAD_DIM), F32)],
        compiler_params=_cparams("arbitrary", "arbitrary"),
        name="gdn_scan",
    )(qkv, gbeta, qkv, gbeta)


def _pad_heads(w, n_heads, width, start, take):
    lead = w.shape[:-1]
    w = w.reshape(lead + (n_heads, width))[..., start:start + take]
    w = jnp.pad(w, [(0, 0)] * len(lead) + [(0, 0), (0, MLA_HEAD_PAD - take)])
    return w.reshape(lead + (n_heads * MLA_HEAD_PAD,))


def kernel(x, c, ctx, c_ctx, w_mod, b_mod, norm1, norm2, w_in, w_out, swa_q_norm, swa_k_norm, swa_sink, gdn_conv,
           gdn_a_log, gdn_dt_bias, gdn_out_norm, mla_q_a_norm, mla_w_uq, mla_kv_a_norm, mla_w_ukv, mla_q_norm,
           mla_k_norm, router_w, router_b, exp_w_gu, exp_b_gu, exp_w_dn, exp_b_dn):
    b, s, d = x.shape
    cl = ctx.shape[1]
    depth = w_mod.shape[0]
    n_lat, n_ctx = b * s, b * cl
    assert s % TOK_TILE == 0 and n_ctx % TOK_TILE == 0 and s % MOVE_TILE == 0 and n_ctx % ROUTE_TILE == 0
    assert cl % MLA_Q_TILE == 0 and cl % ATTN_BLOCK == 0 and n_lat % cl == 0

    n_rows = -(-(b + 1) // SUBLANE) * SUBLANE
    c_rows = jnp.concatenate([c, c_ctx[None, :], jnp.zeros((n_rows - b - 1, d), F32)], axis=0)
    mod_all = _modulation(c_rows, w_mod, b_mod)

    perm = _in_proj_perm()
    tabs = _rope_tables(s)
    seg_a = _block_diag_ones(SWA_Q, SWA_HEAD_DIM)
    seg_c = _block_diag_ones(MLA_HEADS * MLA_HEAD_PAD, MLA_HEAD_PAD)
    head_rows = np.concatenate([h * SWA_HEAD_DIM + np.arange(SWA_HEAD_DIM) for h in SWA_HEAD_ORDER])

    wg_all, wl_all = _split_glu_columns(exp_w_gu)
    x_flat = jnp.concatenate([x.reshape(n_lat, d), ctx.reshape(n_ctx, d)], axis=0)
    for l in range(depth):
        with_ctx = l < depth - 1
        n_tok = n_lat + n_ctx if with_ctx else n_lat
        mod_l = mod_all[l]
        w_in_bf = jnp.pad(w_in[l], ((0, 0), (0, 1)))[:, perm].astype(BF16)
        consts = (seg_a, seg_c,
                  jnp.tile(swa_q_norm[l], SWA_HEADS)[None], jnp.tile(swa_k_norm[l], SWA_KV_HEADS)[None],
                  mla_q_a_norm[l][None], mla_kv_a_norm[l][None],
                  _pad_heads(mla_w_uq[l], MLA_HEADS, MLA_QK, 0, MLA_QK).astype(BF16),
                  _pad_heads(mla_w_ukv[l], MLA_HEADS, MLA_NOPE + MLA_V, 0, MLA_NOPE).astype(BF16),
                  _pad_heads(mla_w_ukv[l], MLA_HEADS, MLA_NOPE + MLA_V, MLA_NOPE, MLA_V).astype(BF16),
                  _pad_heads(jnp.tile(mla_q_norm[l], MLA_HEADS), MLA_HEADS, MLA_QK, 0, MLA_QK)[None],
                  _pad_heads(jnp.tile(mla_k_norm[l], MLA_HEADS), MLA_HEADS, MLA_QK, 0, MLA_QK)[None])
        pg, small, qa, kva, qm, km, vm = _in_projection(x_flat, norm1[l], mod_l, w_in_bf, tabs, consts, b, s)
        o_a = _swa_attention(swa_sink[l], qa, kva, b, s, cl, with_ctx)
        o_c = _mla_attention(qm, km, vm, b, s, cl, with_ctx)
        lanes_g = slice(SMALL_GA, SMALL_GA + 2 * GDN_HEADS)
        neg_a = jnp.zeros((1, LANE), F32).at[0, lanes_g].set(-jnp.exp(gdn_a_log[l]).reshape(-1))
        dt_bias = jnp.zeros((1, LANE), F32).at[0, lanes_g].set(gdn_dt_bias[l].reshape(-1))
        qkv, gbeta = _gdn_prep(pg, small, gdn_conv[l], neg_a, dt_bias, b, s, cl)
        o_fwd, o_bwd = _gdn_scan(qkv, gbeta, b, s, cl)

        wa = w_out[l][:SWA_Q][head_rows].astype(BF16)
        wb = w_out[l][SWA_Q:SWA_Q + GDN_W].astype(BF16)
        wc = w_out[l][SWA_Q + GDN_W:].reshape(MLA_HEADS, MLA_V, d)
        wc = jnp.pad(wc, ((0, 0), (0, MLA_HEAD_PAD - MLA_V), (0, 0))).reshape(MLA_HEADS * MLA_HEAD_PAD, d).astype(BF16)
        x_mid, h2 = _out_projection(o_a, o_fwd, o_bwd, pg, o_c, x_flat, wa, wb, wc,
                                    jnp.tile(gdn_out_norm[l], GDN_HEADS)[None], norm2[l], mod_l, n_tok, b, s)

        wg, wl = wg_all[l], wl_all[l]
        bg = exp_b_gu[l][:, None, 0::2]
        bl = exp_b_gu[l][:, None, 1::2]
        wd = exp_w_dn[l]
        bd = exp_b_dn[l][:, None, :]
        x_flat = _moe(h2, x_mid, n_tok, mod_l, router_w[l].T.astype(BF16), router_b[l], wg, wl, bg, bl, wd, bd, b, s)
    return x_flat[:n_lat].reshape(b, s, d)
```

```python
import functools

import jax
import jax.numpy as jnp
import numpy as np
from jax import lax
from jax.experimental import pallas as pl
from jax.experimental.pallas import tpu as pltpu

F32 = jnp.float32
BF16 = jnp.bfloat16

GRID_W = 64
N_MOD = 6
EPS = 1e-6
ROPE_BASE = 10000.0
ATTN_BLOCK = 128

SWA_HEADS = 4
SWA_KV_HEADS = 2
SWA_HEAD_DIM = 64
SWA_WINDOW = 128

GDN_HEADS = 4
GDN_HEAD_DIM = 128
GDN_CONV = 5
GDN_CHUNK = 64

MLA_HEADS = 4
MLA_Q_RANK = 256
MLA_KV_RANK = 128
MLA_NOPE = 64
MLA_ROPE = 32
MLA_V = 64
MLA_QK = MLA_NOPE + MLA_ROPE

N_EXPERTS = 32
TOP_K = 4
SWIGLU_LIMIT = 7.0
SWIGLU_ALPHA = 1.702

SWA_Q = SWA_HEADS * SWA_HEAD_DIM
SWA_KV = SWA_KV_HEADS * SWA_HEAD_DIM
GDN_W = GDN_HEADS * GDN_HEAD_DIM
MLA_O = MLA_HEADS * MLA_V
D_MIX = SWA_Q + GDN_W + MLA_O
IN_SPLITS = (SWA_Q, SWA_KV, SWA_KV, GDN_W, GDN_W, GDN_W, GDN_W, 2 * GDN_HEADS, 2 * GDN_HEADS, MLA_Q_RANK,
             MLA_KV_RANK, MLA_ROPE)
N_IN = sum(IN_SPLITS)

LANE = 128
SUBLANE = 8
VMEM_LIMIT = 56 * 1024 * 1024

TOK_TILE = 512
MOE_TILE = 512
ROUTE_TILE = 512
MOVE_TILE = 512
ROW_UNROLL = 4
ROW_GROUP = 256
MLA_Q_TILE = 256

COL_AQ, COL_AK, COL_AV = 0, 256, 384
COL_G = 512
COL_CQ, COL_CKV, COL_SMALL = 2560, 2816, 2944
N_IN_PAD = 3072
SMALL_KR, SMALL_GA, SMALL_GB = 0, MLA_ROPE, MLA_ROPE + 2 * GDN_HEADS
SWA_HEAD_ORDER = (0, 2, 1, 3)
MLA_HEAD_PAD = 128


def _cparams(*sem):
    return pltpu.CompilerParams(dimension_semantics=sem, vmem_limit_bytes=VMEM_LIMIT)


def _in_proj_perm():
    old = np.cumsum((0,) + IN_SPLITS)
    o_aq, o_ak, o_av, o_gq, o_gk, o_gv, o_gz, o_ga, o_gb, o_cq, o_ckv, o_ckr = old[:-1]
    perm = np.full((N_IN_PAD,), N_IN, np.int32)
    perm[COL_AQ:COL_AQ + SWA_Q] = np.concatenate(
        [o_aq + h * SWA_HEAD_DIM + np.arange(SWA_HEAD_DIM) for h in SWA_HEAD_ORDER])
    perm[COL_AK:COL_G + 4 * GDN_W] = np.arange(o_ak, o_ga)
    perm[COL_CQ:COL_CQ + MLA_Q_RANK] = o_cq + np.arange(MLA_Q_RANK)
    perm[COL_CKV:COL_CKV + MLA_KV_RANK] = o_ckv + np.arange(MLA_KV_RANK)
    perm[COL_SMALL + SMALL_KR:COL_SMALL + SMALL_KR + MLA_ROPE] = o_ckr + np.arange(MLA_ROPE)
    perm[COL_SMALL + SMALL_GA:COL_SMALL + SMALL_GA + 2 * GDN_HEADS] = o_ga + np.arange(2 * GDN_HEADS)
    perm[COL_SMALL + SMALL_GB:COL_SMALL + SMALL_GB + 2 * GDN_HEADS] = o_gb + np.arange(2 * GDN_HEADS)
    return perm


def _mod_kernel(c_ref, w_ref, b_ref, o_ref):
    a = c_ref[...]
    a = a * jax.nn.sigmoid(a)
    o_ref[0] = jnp.dot(a.astype(BF16), w_ref[0].astype(BF16), preferred_element_type=F32) + b_ref[0]


def _modulation(c_rows, w_mod, b_mod):
    nl, d, n = w_mod.shape
    r = c_rows.shape[0]
    tn = 1536
    out = pl.pallas_call(
        _mod_kernel,
        out_shape=jax.ShapeDtypeStruct((nl, r, n), F32),
        grid=(nl, n // tn),
        in_specs=[pl.BlockSpec((r, d), lambda l, j: (0, 0)),
                  pl.BlockSpec((1, d, tn), lambda l, j: (l, 0, j)),
                  pl.BlockSpec((1, 1, tn), lambda l, j: (l, 0, j))],
        out_specs=pl.BlockSpec((1, r, tn), lambda l, j: (l, 0, j)),
        compiler_params=_cparams("arbitrary", "arbitrary"),
        name="modulation",
    )(c_rows, w_mod, b_mod.reshape(nl, 1, n))
    return out.reshape(nl, r, N_MOD, d)


def _modulated_norm(x, gain, shift, scale):
    y = x * lax.rsqrt(jnp.mean(x * x, axis=-1, keepdims=True) + EPS)
    return (y * gain) * (1.0 + scale) + shift


def _group_of_tile(i, tiles_per_seq, n_batch):
    return jnp.minimum(i // tiles_per_seq, n_batch)


def _group_sumsq(x, seg):
    x2 = x * x
    hi = x2.astype(BF16)
    lo = (x2 - hi.astype(F32)).astype(BF16)
    return jnp.dot(hi, seg, preferred_element_type=F32) + jnp.dot(lo, seg, preferred_element_type=F32)


def _swap_pairs(x, half):
    w = x.shape[1]
    lane = lax.broadcasted_iota(jnp.int32, (1, w), 1)
    first = (lane % (2 * half)) < half
    return jnp.where(first, pltpu.roll(x, w - half, axis=1), pltpu.roll(x, half, axis=1))


def _inproj_kernel(x_ref, g_ref, m_ref, w_ref, ra_c_ref, ra_s_ref, rc_c_ref, rc_s_ref, seg_a_ref, seg_c_ref,
                   gaq_ref, gak_ref, gcq_ref, gckv_ref, wuq_ref, wuk_ref, wuv_ref, gmq_ref, gmk_ref,
                   pg_ref, small_ref, qa_ref, kva_ref, qm_ref, km_ref, vm_ref):
    m = m_ref[0]
    groups = [pl.ds(i * ROW_GROUP, ROW_GROUP) for i in range(x_ref.shape[0] // ROW_GROUP)]
    projected = []
    for rows in groups:
        h = _modulated_norm(x_ref[rows, :], g_ref[...], m[0:1], m[1:2]).astype(BF16)
        pa = jnp.dot(h, w_ref[:, COL_AQ:COL_G], preferred_element_type=F32)
        pg_ref[rows, :] = jnp.dot(h, w_ref[:, COL_G:COL_CQ], preferred_element_type=F32)
        pc = jnp.dot(h, w_ref[:, COL_CQ:], preferred_element_type=F32)
        small_ref[rows, :] = pc[:, MLA_Q_RANK + MLA_KV_RANK:]
        projected.append((pa, pc))
    for rows, (pa, pc) in zip(groups, projected):
        _attention_prep(rows, pa, pc, ra_c_ref, ra_s_ref, rc_c_ref, rc_s_ref, seg_a_ref, seg_c_ref, gaq_ref, gak_ref,
                        gcq_ref, gckv_ref, wuq_ref, wuk_ref, wuv_ref, gmq_ref, gmk_ref,
                        qa_ref, kva_ref, qm_ref, km_ref, vm_ref)


def _attention_prep(rows, pa, pc, ra_c_ref, ra_s_ref, rc_c_ref, rc_s_ref, seg_a_ref, seg_c_ref, gaq_ref, gak_ref,
                    gcq_ref, gckv_ref, wuq_ref, wuk_ref, wuv_ref, gmq_ref, gmk_ref,
                    qa_ref, kva_ref, qm_ref, km_ref, vm_ref):
    small = pc[:, MLA_Q_RANK + MLA_KV_RANK:]

    aq = pa[:, 0:SWA_Q]
    ak = pa[:, SWA_Q:SWA_Q + SWA_KV]
    av = pa[:, SWA_Q + SWA_KV:SWA_Q + 2 * SWA_KV]
    ca, sa = ra_c_ref[rows, :], ra_s_ref[rows, :]
    seg_a = seg_a_ref[...]
    qn = aq * lax.rsqrt(_group_sumsq(aq, seg_a) * (1.0 / SWA_HEAD_DIM) + EPS) * gaq_ref[...]
    kn = ak * lax.rsqrt(_group_sumsq(ak, seg_a[:SWA_KV, :SWA_KV]) * (1.0 / SWA_HEAD_DIM) + EPS) * gak_ref[...]
    qn = qn * jnp.concatenate([ca, ca], axis=1) + _swap_pairs(qn, SWA_HEAD_DIM // 4) * jnp.concatenate([sa, sa], axis=1)
    kn = kn * ca + _swap_pairs(kn, SWA_HEAD_DIM // 4) * sa
    qa_ref[rows, :] = (qn * SWA_HEAD_DIM ** -0.5).astype(BF16)
    kva_ref[rows, :] = jnp.concatenate([kn, av], axis=1).astype(BF16)

    cq = pc[:, 0:MLA_Q_RANK]
    ckv = pc[:, MLA_Q_RANK:MLA_Q_RANK + MLA_KV_RANK]
    cqn = cq * lax.rsqrt(jnp.mean(cq * cq, axis=-1, keepdims=True) + EPS) * gcq_ref[...]
    ckvn = (ckv * lax.rsqrt(jnp.mean(ckv * ckv, axis=-1, keepdims=True) + EPS) * gckv_ref[...]).astype(BF16)
    q = jnp.dot(cqn.astype(BF16), wuq_ref[...], preferred_element_type=F32)
    k = jnp.dot(ckvn, wuk_ref[...], preferred_element_type=F32)
    v = jnp.dot(ckvn, wuv_ref[...], preferred_element_type=F32)
    lane = lax.broadcasted_iota(jnp.int32, (1, LANE), 1)
    kpe = jnp.where((lane >= MLA_NOPE) & (lane < MLA_QK), pltpu.roll(small, MLA_NOPE, axis=1), 0.0)
    k = k + jnp.concatenate([kpe] * MLA_HEADS, axis=1)
    seg_c = seg_c_ref[...]
    q = q * lax.rsqrt(_group_sumsq(q, seg_c) * (1.0 / MLA_QK) + EPS) * gmq_ref[...]
    k = k * lax.rsqrt(_group_sumsq(k, seg_c) * (1.0 / MLA_QK) + EPS) * gmk_ref[...]
    cc = jnp.concatenate([rc_c_ref[rows, :]] * MLA_HEADS, axis=1)
    sc = jnp.concatenate([rc_s_ref[rows, :]] * MLA_HEADS, axis=1)
    q = q * cc + _swap_pairs(q, MLA_ROPE // 4) * sc
    k = k * cc + _swap_pairs(k, MLA_ROPE // 4) * sc
    qm_ref[rows, :] = (q * MLA_QK ** -0.5).astype(BF16)
    km_ref[rows, :] = k.astype(BF16)
    vm_ref[rows, :] = v.astype(BF16)


def _in_projection(x_flat, gain, mod_l, w_bf, tabs, consts, n_batch, seq):
    n_tok, d = x_flat.shape
    tm = TOK_TILE
    tps = seq // tm
    n_lat_tiles = n_batch * tps
    row = lambda i: (i, 0)
    const = lambda i: (0, 0)
    tab = lambda i: (jnp.where(i < n_lat_tiles, i % tps, tps), 0)
    wide = MLA_HEADS * MLA_HEAD_PAD
    in_specs = [pl.BlockSpec((tm, d), row),
                pl.BlockSpec((1, d), const),
                pl.BlockSpec((1, N_MOD, d), lambda i: (_group_of_tile(i, tps, n_batch), 0, 0)),
                pl.BlockSpec(w_bf.shape, const)]
    in_specs += [pl.BlockSpec((tm, LANE), tab)] * 4
    in_specs += [pl.BlockSpec(a.shape, const) for a in consts]
    widths = (4 * GDN_W, LANE, SWA_Q, 2 * SWA_KV, wide, wide, wide)
    dtypes = (F32, F32, BF16, BF16, BF16, BF16, BF16)
    return pl.pallas_call(
        _inproj_kernel,
        out_shape=tuple(jax.ShapeDtypeStruct((n_tok, w), t) for w, t in zip(widths, dtypes)),
        grid=(n_tok // tm,),
        in_specs=in_specs,
        out_specs=tuple(pl.BlockSpec((tm, w), row) for w in widths),
        compiler_params=_cparams("arbitrary"),
        name="in_projection",
    )(x_flat, gain.reshape(1, d), mod_l, w_bf, *tabs, *consts)


def _rope_tables(seq):
    t = jnp.arange(seq)
    row = (t // GRID_W).astype(F32)
    col = (t % GRID_W).astype(F32)

    def cos_sin(rot_dim):
        n_freq = rot_dim // 4
        freq = ROPE_BASE ** (-jnp.arange(n_freq, dtype=F32) / n_freq)
        ar, ac = row[:, None] * freq, col[:, None] * freq
        c = jnp.concatenate([jnp.cos(ar), jnp.cos(ar), jnp.cos(ac), jnp.cos(ac)], axis=1)
        s = jnp.concatenate([-jnp.sin(ar), jnp.sin(ar), -jnp.sin(ac), jnp.sin(ac)], axis=1)
        return c, s

    ca, sa = cos_sin(SWA_HEAD_DIM)
    ca, sa = jnp.tile(ca, (1, 2)), jnp.tile(sa, (1, 2))
    cc, sc = cos_sin(MLA_ROPE)
    ones_l = jnp.ones((seq, MLA_NOPE), F32)
    ones_r = jnp.ones((seq, MLA_HEAD_PAD - MLA_QK), F32)
    cc = jnp.concatenate([ones_l, cc, ones_r], axis=1)
    sc = jnp.concatenate([0 * ones_l, sc, 0 * ones_r], axis=1)
    ident_c = jnp.ones((TOK_TILE, LANE), F32)
    ident_s = jnp.zeros((TOK_TILE, LANE), F32)
    return tuple(jnp.concatenate([a, i], axis=0) for a, i in ((ca, ident_c), (sa, ident_s), (cc, ident_c), (sc, ident_s)))


def _block_diag_ones(n, blk):
    i = np.arange(n) // blk
    return jnp.asarray((i[:, None] == i[None, :]).astype(np.float32), BF16)


def _dot_nt(a, b):
    return lax.dot_general(a, b, (((1,), (1,)), ((), ())), preferred_element_type=F32)


def _swa_heads(sink_ref, q, k_list, v_list, mask_list, o_ref):
    lane = lax.broadcasted_iota(jnp.int32, (1, LANE), 1)
    lower = lane < SWA_HEAD_DIM
    cols = []
    for cgrp in range(2):
        qc = q[:, cgrp * LANE:(cgrp + 1) * LANE]
        halves = []
        for half in range(2):
            head = SWA_HEAD_ORDER[2 * cgrp + half]
            qh = jnp.where(lower if half == 0 else ~lower, qc, jnp.zeros_like(qc))
            sink = sink_ref[head]
            scores = []
            m = jnp.full((q.shape[0], 1), sink, F32)
            for kb, mk in zip(k_list, mask_list):
                s = _dot_nt(qh, kb)
                if mk is not None:
                    s = jnp.where(mk, s, -jnp.inf)
                scores.append(s)
                m = jnp.maximum(m, jnp.max(s, axis=-1, keepdims=True))
            l = jnp.exp(sink - m)
            acc = jnp.zeros((q.shape[0], LANE), F32)
            for s, vb in zip(scores, v_list):
                p = jnp.exp(s - m)
                l = l + jnp.sum(p, axis=-1, keepdims=True)
                acc = acc + jnp.dot(p.astype(BF16), vb, preferred_element_type=F32)
            halves.append(acc * (1.0 / l))
        cols.append(jnp.where(lower, halves[0], halves[1]))
    o_ref[...] = jnp.concatenate(cols, axis=1).astype(o_ref.dtype)


SWA_Q_BLOCKS = 2
SWA_KEY_OFFSETS = tuple(range(-1, SWA_Q_BLOCKS + 1))


def _swa_kernel(sink_ref, q_ref, *refs, n_blocks):
    kv_refs, kvc_ref, o_ref = refs[:len(SWA_KEY_OFFSETS)], refs[-2], refs[-1]
    j = pl.program_id(1)
    n_steps = n_blocks // SWA_Q_BLOCKS
    kc, vc = kvc_ref[:, 0:SWA_KV], kvc_ref[:, SWA_KV:]

    @pl.when(j < n_steps)
    def _():
        rows = SWA_Q_BLOCKS * ATTN_BLOCK
        r = lax.broadcasted_iota(jnp.int32, (rows, ATTN_BLOCK), 0)
        c = lax.broadcasted_iota(jnp.int32, (rows, ATTN_BLOCK), 1)
        masks = []
        for off in SWA_KEY_OFFSETS:
            blk = j * SWA_Q_BLOCKS + off
            in_seq = (blk >= 0) & (blk < n_blocks)
            masks.append(jnp.abs(off * ATTN_BLOCK + c - r) <= jnp.where(in_seq, SWA_WINDOW, -1))
        ks = [ref[:, 0:SWA_KV] for ref in kv_refs] + [kc]
        vs = [ref[:, SWA_KV:] for ref in kv_refs] + [vc]
        _swa_heads(sink_ref, q_ref[...], ks, vs, masks + [None], o_ref)

    @pl.when(j >= n_steps)
    def _():
        _swa_heads(sink_ref, q_ref[...], [kc], [vc], [None], o_ref)


def _swa_attention(sink, qa, kva, n_batch, seq, ctx_len, with_ctx):
    rows = SWA_Q_BLOCKS * ATTN_BLOCK
    nb = seq // ATTN_BLOCK
    nq = seq // rows
    ncq = ctx_len // rows
    n_lat = n_batch * seq
    steps = nq + (ncq if with_ctx else 0)
    n_tok = n_lat + (n_batch * ctx_len if with_ctx else 0)

    def q_map(b, j, s):
        return (jnp.where(j < nq, b * nq + j, n_lat // rows + b * ncq + (j - nq)), 0)

    def kv_map(off):
        def f(b, j, s):
            jj = jnp.clip(jnp.minimum(j, nq - 1) * SWA_Q_BLOCKS + off, 0, nb - 1)
            return (b * nb + jj, 0)
        return f

    return pl.pallas_call(
        functools.partial(_swa_kernel, n_blocks=nb),
        out_shape=jax.ShapeDtypeStruct((n_tok, SWA_Q), BF16),
        grid_spec=pltpu.PrefetchScalarGridSpec(
            num_scalar_prefetch=1,
            grid=(n_batch, steps),
            in_specs=[pl.BlockSpec((rows, SWA_Q), q_map)]
            + [pl.BlockSpec((ATTN_BLOCK, 2 * SWA_KV), kv_map(off)) for off in SWA_KEY_OFFSETS]
            + [pl.BlockSpec((ctx_len, 2 * SWA_KV), lambda b, j, s: (n_lat // ctx_len + b, 0))],
            out_specs=pl.BlockSpec((rows, SWA_Q), q_map)),
        compiler_params=_cparams("arbitrary", "arbitrary"),
        name="swa_attention",
    )(sink, qa, *([kva] * len(SWA_KEY_OFFSETS)), kva)


def _mla_heads(q_ref, k_refs, v_refs, o_ref):
    for h in range(MLA_HEADS):
        sl = slice(h * MLA_HEAD_PAD, (h + 1) * MLA_HEAD_PAD)
        qh = q_ref[:, sl]
        scores = [_dot_nt(qh, k_ref[:, sl]) for k_ref in k_refs]
        m = jnp.max(scores[0], axis=-1, keepdims=True)
        for s in scores[1:]:
            m = jnp.maximum(m, jnp.max(s, axis=-1, keepdims=True))
        l = jnp.zeros_like(m)
        acc = jnp.zeros((qh.shape[0], MLA_HEAD_PAD), F32)
        for s, v_ref in zip(scores, v_refs):
            p = jnp.exp(s - m)
            l = l + jnp.sum(p, axis=-1, keepdims=True)
            acc = acc + jnp.dot(p.astype(BF16), v_ref[:, sl], preferred_element_type=F32)
        o_ref[:, sl] = (acc * (1.0 / l)).astype(o_ref.dtype)


def _mla_kernel(q_ref, kl_ref, vl_ref, kc_ref, vc_ref, o_ref, *, n_lat_steps):
    j = pl.program_id(1)

    @pl.when(j < n_lat_steps)
    def _():
        _mla_heads(q_ref, [kl_ref, kc_ref], [vl_ref, vc_ref], o_ref)

    @pl.when(j >= n_lat_steps)
    def _():
        _mla_heads(q_ref, [kc_ref], [vc_ref], o_ref)


def _mla_attention(qm, km, vm, n_batch, seq, ctx_len, with_ctx):
    tq = MLA_Q_TILE
    nq = seq // tq
    ncq = ctx_len // tq
    n_lat = n_batch * seq
    steps = nq + (ncq if with_ctx else 0)
    n_tok = n_lat + (n_batch * ctx_len if with_ctx else 0)
    wide = MLA_HEADS * MLA_HEAD_PAD

    def q_map(b, j):
        return (jnp.where(j < nq, b * nq + j, n_lat // tq + b * ncq + (j - nq)), 0)

    lat_map = lambda b, j: (b, 0)
    ctx_map = lambda b, j: (n_lat // ctx_len + b, 0)
    return pl.pallas_call(
        functools.partial(_mla_kernel, n_lat_steps=nq),
        out_shape=jax.ShapeDtypeStruct((n_tok, wide), BF16),
        grid=(n_batch, steps),
        in_specs=[pl.BlockSpec((tq, wide), q_map),
                  pl.BlockSpec((seq, wide), lat_map),
                  pl.BlockSpec((seq, wide), lat_map),
                  pl.BlockSpec((ctx_len, wide), ctx_map),
                  pl.BlockSpec((ctx_len, wide), ctx_map)],
        out_specs=pl.BlockSpec((tq, wide), q_map),
        compiler_params=_cparams("arbitrary", "arbitrary"),
        name="mla_attention",
    )(qm, km, vm, km, vm)


def _outproj_kernel(oa_ref, of_ref, ob_ref, z_ref, oc_ref, x_ref, wa_ref, wb_ref, wc_ref, gg_ref, g_ref, m_ref,
                    xo_ref, h_ref):
    m = m_ref[0]
    groups = [pl.ds(i * ROW_GROUP, ROW_GROUP) for i in range(x_ref.shape[0] // ROW_GROUP)]
    mixed = []
    for rows in groups:
        ob = of_ref[rows, :] + ob_ref[rows, :]
        z = z_ref[rows, :]
        heads = []
        for h in range(GDN_HEADS):
            oh = ob[:, h * GDN_HEAD_DIM:(h + 1) * GDN_HEAD_DIM]
            heads.append(oh * lax.rsqrt(jnp.mean(oh * oh, axis=-1, keepdims=True) + EPS))
        gated = jnp.concatenate(heads, axis=1) * gg_ref[...] * (z * jax.nn.sigmoid(z))
        mixed.append(jnp.dot(oa_ref[rows, :], wa_ref[...], preferred_element_type=F32)
                     + jnp.dot(gated.astype(BF16), wb_ref[...], preferred_element_type=F32)
                     + jnp.dot(oc_ref[rows, :], wc_ref[...], preferred_element_type=F32))
    for rows, o in zip(groups, mixed):
        xn = x_ref[rows, :] + m[2:3] * o
        xo_ref[rows, :] = xn
        h_ref[rows, :] = _modulated_norm(xn, g_ref[...], m[3:4], m[4:5])


def _out_projection(o_a, o_fwd, o_bwd, p_flat, o_c, x_flat, wa, wb, wc, gdn_gain, gain2, mod_l, n_tok, n_batch, seq):
    d = x_flat.shape[1]
    tm = TOK_TILE
    tps = seq // tm
    row = lambda i: (i, 0)
    const = lambda i: (0, 0)
    return pl.pallas_call(
        _outproj_kernel,
        out_shape=(jax.ShapeDtypeStruct((n_tok, d), F32), jax.ShapeDtypeStruct((n_tok, d), F32)),
        grid=(n_tok // tm,),
        in_specs=[pl.BlockSpec((tm, o_a.shape[1]), row),
                  pl.BlockSpec((tm, GDN_W), row),
                  pl.BlockSpec((tm, GDN_W), row),
                  pl.BlockSpec((tm, GDN_W), lambda i: (i, 3)),
                  pl.BlockSpec((tm, o_c.shape[1]), row),
                  pl.BlockSpec((tm, d), row),
                  pl.BlockSpec(wa.shape, const),
                  pl.BlockSpec(wb.shape, const),
                  pl.BlockSpec(wc.shape, const),
                  pl.BlockSpec((1, GDN_W), const),
                  pl.BlockSpec((1, d), const),
                  pl.BlockSpec((1, N_MOD, d), lambda i: (_group_of_tile(i, tps, n_batch), 0, 0))],
        out_specs=(pl.BlockSpec((tm, d), row), pl.BlockSpec((tm, d), row)),
        compiler_params=_cparams("arbitrary"),
        name="out_projection",
    )(o_a, o_fwd, o_bwd, p_flat, o_c, x_flat, wa, wb, wc, gdn_gain, gain2.reshape(1, d), mod_l)


def _route_kernel(h_ref, rw_ref, rb_ref, idx_ref, gate_ref, cnt_ref, base_ref):
    step = pl.program_id(0)
    tm = h_ref.shape[0]

    @pl.when(step == 0)
    def _():
        base_ref[...] = jnp.zeros_like(base_ref)

    logits = lax.dot_general(rw_ref[...], h_ref[...].astype(BF16), (((1,), (1,)), ((), ())),
                             preferred_element_type=F32) + rb_ref[...]
    e_iota = lax.broadcasted_iota(jnp.int32, logits.shape, 0)
    work = logits
    tops, picks = [], []
    for _k in range(TOP_K):
        mx = jnp.max(work, axis=0, keepdims=True)
        pick = jnp.min(jnp.where(work == mx, e_iota, N_EXPERTS), axis=0, keepdims=True)
        work = jnp.where(e_iota == pick, -jnp.inf, work)
        tops.append(mx)
        picks.append(pick)
    exps = [jnp.exp(t - tops[0]) for t in tops]
    denom = exps[0] + exps[1] + exps[2] + exps[3]
    sel = jnp.zeros(logits.shape, F32)
    for pick in picks:
        sel = sel + (e_iota == pick).astype(F32)
    row = lax.broadcasted_iota(jnp.int32, (tm, tm), 0)
    col = lax.broadcasted_iota(jnp.int32, (tm, tm), 1)
    before = (row < col).astype(BF16)
    cnt = jnp.dot(sel.astype(BF16), before, preferred_element_type=F32) + base_ref[:, 0:1]
    ranks = [jnp.sum(jnp.where(e_iota == pick, cnt, 0.0), axis=0, keepdims=True) for pick in picks]
    idx_ref[0] = jnp.concatenate(picks + [r.astype(jnp.int32) for r in ranks], axis=0)
    gate_rows = jnp.concatenate([e / denom for e in exps] + [jnp.zeros((LANE - TOP_K, tm), F32)], axis=0)
    gate_ref[...] = jnp.transpose(gate_rows)
    base_ref[...] = base_ref[...] + jnp.sum(sel, axis=1, keepdims=True)
    cnt_ref[...] = base_ref[...]


def _route(h_flat, n_tok, rw_t_bf, rb):
    d = h_flat.shape[1]
    tm = ROUTE_TILE
    nt = n_tok // tm
    return pl.pallas_call(
        _route_kernel,
        out_shape=(jax.ShapeDtypeStruct((nt, 2 * TOP_K, tm), jnp.int32),
                   jax.ShapeDtypeStruct((n_tok, LANE), F32),
                   jax.ShapeDtypeStruct((N_EXPERTS, LANE), F32)),
        grid=(nt,),
        in_specs=[pl.BlockSpec((tm, d), lambda i: (i, 0)),
                  pl.BlockSpec((N_EXPERTS, d), lambda i: (0, 0)),
                  pl.BlockSpec((N_EXPERTS, 1), lambda i: (0, 0))],
        out_specs=(pl.BlockSpec((1, 2 * TOP_K, tm), lambda i: (i, 0, 0)),
                   pl.BlockSpec((tm, LANE), lambda i: (i, 0)),
                   pl.BlockSpec((N_EXPERTS, LANE), lambda i: (0, 0))),
        scratch_shapes=[pltpu.VMEM((N_EXPERTS, LANE), F32)],
        compiler_params=_cparams("arbitrary"),
        name="moe_route",
    )(h_flat, rw_t_bf, rb.reshape(N_EXPERTS, 1))


def _dispatch_kernel(lo_ref, hi_ref, slot_ref, h_ref, xs_out, zero_ref, sem, zero_sem):
    tm = slot_ref.shape[2]

    @pl.when(pl.program_id(0) == pl.num_programs(0) - 1)
    def _():
        zero_ref[...] = jnp.zeros_like(zero_ref)

        def zero_copy(r):
            return pltpu.make_async_copy(zero_ref.at[pl.ds(0, 1)], xs_out.at[pl.ds(r, 1)], zero_sem)

        def start(r, carry):
            zero_copy(r).start()
            return carry

        def wait(r, carry):
            zero_copy(r).wait()
            return carry

        def start_expert(e, carry):
            return lax.fori_loop(lo_ref[e], hi_ref[e], start, carry)

        def wait_expert(e, carry):
            return lax.fori_loop(lo_ref[e], hi_ref[e], wait, carry)

        lax.fori_loop(0, N_EXPERTS, start_expert, 0)
        lax.fori_loop(0, N_EXPERTS, wait_expert, 0)

    def row_copy(t, k):
        return pltpu.make_async_copy(h_ref.at[pl.ds(t, 1)], xs_out.at[pl.ds(slot_ref[0, k, t], 1)], sem)

    def issue(i, carry):
        for u in range(ROW_UNROLL):
            for k in range(TOP_K):
                row_copy(i * ROW_UNROLL + u, k).start(priority=k % 2)
        return carry

    lax.fori_loop(0, tm // ROW_UNROLL, issue, 0)

    def drain(i, carry):
        for u in range(ROW_UNROLL):
            for k in range(TOP_K):
                row_copy(i * ROW_UNROLL + u, k).wait()
        return carry

    lax.fori_loop(0, tm // ROW_UNROLL, drain, 0)


def _dispatch(pad_lo, pad_hi, slots, h_flat, n_tok, cap):
    d = h_flat.shape[1]
    tm = slots.shape[2]
    return pl.pallas_call(
        _dispatch_kernel,
        out_shape=jax.ShapeDtypeStruct((cap, d), F32),
        grid_spec=pltpu.PrefetchScalarGridSpec(
            num_scalar_prefetch=2,
            grid=(n_tok // tm,),
            in_specs=[pl.BlockSpec((1, TOP_K, tm), lambda i, lo, hi: (i, 0, 0), memory_space=pltpu.SMEM),
                      pl.BlockSpec((tm, d), lambda i, lo, hi: (i, 0))],
            out_specs=pl.BlockSpec(memory_space=pl.ANY),
            scratch_shapes=[pltpu.VMEM((SUBLANE, d), F32), pltpu.SemaphoreType.DMA(()), pltpu.SemaphoreType.DMA(())]),
        compiler_params=_cparams("arbitrary"),
        name="moe_dispatch",
    )(pad_lo, pad_hi, slots, h_flat)


SPLIT_COLS = 1024


def _split_kernel(w_ref, sel_ref, wg_ref, wl_ref):
    for j in range(w_ref.shape[2] // (2 * LANE)):
        blk = w_ref[0, :, 2 * LANE * j:2 * LANE * (j + 1)].astype(BF16)
        r = jnp.dot(blk, sel_ref[...], preferred_element_type=F32)
        wg_ref[0, :, LANE * j:LANE * (j + 1)] = r[:, :LANE].astype(BF16)
        wl_ref[0, :, LANE * j:LANE * (j + 1)] = r[:, LANE:].astype(BF16)


def _split_glu_columns(w_gu):
    nl, ne, d, f2 = w_gu.shape
    src = np.arange(2 * LANE)
    dst = np.where(src % 2 == 0, src // 2, LANE + src // 2)
    sel = jnp.asarray((dst[:, None] == np.arange(2 * LANE)[None, :]).astype(np.float32), BF16)
    tn = SPLIT_COLS
    out = jax.ShapeDtypeStruct((nl * ne, d, f2 // 2), BF16)
    wg, wl = pl.pallas_call(
        _split_kernel,
        out_shape=(out, out),
        grid=(nl * ne, f2 // tn),
        in_specs=[pl.BlockSpec((1, d, tn), lambda e, j: (e, 0, j)),
                  pl.BlockSpec((2 * LANE, 2 * LANE), lambda e, j: (0, 0))],
        out_specs=(pl.BlockSpec((1, d, tn // 2), lambda e, j: (e, 0, j)),
                   pl.BlockSpec((1, d, tn // 2), lambda e, j: (e, 0, j))),
        compiler_params=_cparams("arbitrary", "arbitrary"),
        name="split_glu_columns",
    )(w_gu.reshape(nl * ne, d, f2), sel)
    return wg, wl


def _expert_kernel(be_ref, nu_ref, xs_ref, wg_ref, wl_ref, bg_ref, bl_ref, wd_ref, bd_ref, ys_ref, wd_bf_ref):
    i = pl.program_id(0)
    last = nu_ref[0] - 1
    expert = be_ref[jnp.minimum(i, last)]
    prev_expert = be_ref[jnp.minimum(jnp.maximum(i - 1, 0), last)]

    @pl.when(jnp.logical_or(i == 0, expert != prev_expert))
    def _():
        wd_bf_ref[...] = wd_ref[0].astype(BF16)

    @pl.when(i < nu_ref[0])
    def _():
        x = xs_ref[...].astype(BF16)
        g = jnp.dot(x, wg_ref[0], preferred_element_type=F32) + bg_ref[0]
        u = jnp.dot(x, wl_ref[0], preferred_element_type=F32) + bl_ref[0]
        g = jnp.minimum(g, SWIGLU_LIMIT)
        u = jnp.clip(u, -SWIGLU_LIMIT, SWIGLU_LIMIT)
        act = g * jax.nn.sigmoid(SWIGLU_ALPHA * g) * (u + 1.0)
        ys_ref[...] = jnp.dot(act.astype(BF16), wd_bf_ref[...], preferred_element_type=F32) + bd_ref[0]

    @pl.when(pl.program_id(0) >= nu_ref[0])
    def _():
        ys_ref[...] = jnp.zeros_like(ys_ref)


def _experts(block_e, n_used, xs, wg, wl, bg, bl, wd, bd):
    cap, d = xs.shape
    f = wg.shape[2]
    tm = MOE_TILE
    nb = cap // tm

    def row_map(i, be, nu):
        return (jnp.minimum(i, nu[0] - 1), 0)

    def w_map(i, be, nu):
        return (be[jnp.minimum(i, nu[0] - 1)], 0, 0)

    return pl.pallas_call(
        _expert_kernel,
        out_shape=jax.ShapeDtypeStruct((cap, d), F32),
        grid_spec=pltpu.PrefetchScalarGridSpec(
            num_scalar_prefetch=2,
            grid=(nb,),
            in_specs=[pl.BlockSpec((tm, d), row_map),
                      pl.BlockSpec((1, d, f), w_map),
                      pl.BlockSpec((1, d, f), w_map),
                      pl.BlockSpec((1, 1, f), w_map),
                      pl.BlockSpec((1, 1, f), w_map),
                      pl.BlockSpec((1, f, d), w_map),
                      pl.BlockSpec((1, 1, d), w_map)],
            out_specs=pl.BlockSpec((tm, d), lambda i, be, nu: (i, 0)),
            scratch_shapes=[pltpu.VMEM((f, d), BF16)]),
        compiler_params=_cparams("arbitrary"),
        name="moe_experts",
    )(block_e, n_used, xs, wg, wl, bg, bl, wd, bd)


def _combine_kernel(slot_ref, ys_hbm, x_ref, gate_ref, m_ref, o_ref, buf, sem):
    tm = x_ref.shape[0]

    def row_copy(t, k):
        return pltpu.make_async_copy(ys_hbm.at[pl.ds(slot_ref[0, k, t], 1)], buf.at[k, pl.ds(t, 1)], sem)

    def issue(i, carry):
        for u in range(ROW_UNROLL):
            for k in range(TOP_K):
                row_copy(i * ROW_UNROLL + u, k).start(priority=k % 2)
        return carry

    lax.fori_loop(0, tm // ROW_UNROLL, issue, 0)

    def drain(i, carry):
        for u in range(ROW_UNROLL):
            for k in range(TOP_K):
                row_copy(i * ROW_UNROLL + u, k).wait()
        return carry

    lax.fori_loop(0, tm // ROW_UNROLL, drain, 0)
    gates = gate_ref[...]
    y = buf[0] * gates[:, 0:1]
    for k in range(1, TOP_K):
        y = y + buf[k] * gates[:, k:k + 1]
    o_ref[...] = x_ref[...] + m_ref[0][5:6] * y


def _combine(slots, ys, x_flat, gates, mod_l, n_tok, n_batch, seq):
    d = x_flat.shape[1]
    tm = slots.shape[2]
    tps = seq // tm
    return pl.pallas_call(
        _combine_kernel,
        out_shape=jax.ShapeDtypeStruct((n_tok, d), F32),
        grid=(n_tok // tm,),
        in_specs=[pl.BlockSpec((1, TOP_K, tm), lambda i: (i, 0, 0), memory_space=pltpu.SMEM),
                  pl.BlockSpec(memory_space=pl.ANY),
                  pl.BlockSpec((tm, d), lambda i: (i, 0)),
                  pl.BlockSpec((tm, LANE), lambda i: (i, 0)),
                  pl.BlockSpec((1, N_MOD, d), lambda i: (_group_of_tile(i, tps, n_batch), 0, 0))],
        out_specs=pl.BlockSpec((tm, d), lambda i: (i, 0)),
        scratch_shapes=[pltpu.VMEM((TOP_K, tm, d), F32), pltpu.SemaphoreType.DMA(())],
        compiler_params=_cparams("arbitrary"),
        name="moe_combine",
    )(slots, ys, x_flat, gates, mod_l)


def _retile_slots(slots, tm):
    nt, k, t = slots.shape
    return slots.reshape(nt, k, t // tm, tm).transpose(0, 2, 1, 3).reshape(nt * (t // tm), k, tm)


def _moe(h_flat, x_flat, n_tok, mod_l, rw_t_bf, rb, layer, wg, wl, bg, bl, wd, bd, n_batch, seq):
    idx, gates, counts = _route(h_flat, n_tok, rw_t_bf, rb)
    counts = counts[:, 0].astype(jnp.int32)
    padded = (counts + MOE_TILE - 1) // MOE_TILE * MOE_TILE
    pad_end = jnp.cumsum(padded)
    pad_start = pad_end - padded
    n_blocks = -(-(n_tok * TOP_K) // MOE_TILE) + N_EXPERTS
    cap = n_blocks * MOE_TILE
    block_row = jnp.arange(n_blocks, dtype=jnp.int32) * MOE_TILE
    block_e = jnp.minimum(jnp.sum(block_row[:, None] >= pad_end[None, :], axis=1), N_EXPERTS - 1).astype(jnp.int32)
    block_e = block_e + layer * N_EXPERTS
    n_used = (pad_end[-1:] // MOE_TILE).astype(jnp.int32)
    experts = idx[:, :TOP_K, :]
    start = jnp.sum(jnp.where(experts[..., None] == jnp.arange(N_EXPERTS), pad_start, 0), axis=-1)
    slots = start + idx[:, TOP_K:, :]
    slots = _retile_slots(slots, MOVE_TILE)
    xs = _dispatch(pad_start + counts, pad_end, slots, h_flat, n_tok, cap)
    ys = _experts(block_e, n_used, xs, wg, wl, bg, bl, wd, bd)
    return _combine(slots, ys, x_flat, gates, mod_l, n_tok, n_batch, seq)


GDN_TILE = 256
HALO = SUBLANE


def _gdn_prep_kernel(q_ref, k_ref, v_ref, qp_ref, kp_ref, vp_ref, qn_ref, kn_ref, vn_ref, small_ref, w_ref,
                     a_ref, bias_ref, qkv_ref, gb_ref, xe_ref, *, lat_tiles, lat_tps, ctx_tps):
    i = pl.program_id(0)
    tm = q_ref.shape[0]
    in_lat = i < lat_tiles
    pos = jnp.where(in_lat, i % lat_tps, (i - lat_tiles) % ctx_tps)
    last = jnp.where(in_lat, lat_tps - 1, ctx_tps - 1)
    keep_prev = jnp.where(pos > 0, 1.0, 0.0)
    keep_next = jnp.where(pos < last, 1.0, 0.0)
    half = GDN_CONV // 2
    for g, (x_ref, p_ref, n_ref) in enumerate(((q_ref, qp_ref, qn_ref), (k_ref, kp_ref, kn_ref),
                                               (v_ref, vp_ref, vn_ref))):
        xe_ref[g, 0:HALO, :] = p_ref[...] * keep_prev
        xe_ref[g, HALO:HALO + tm, :] = x_ref[...]
        xe_ref[g, HALO + tm:2 * HALO + tm, :] = n_ref[...] * keep_next
        w = w_ref[:, g * GDN_W:(g + 1) * GDN_W]
        y = xe_ref[g, HALO - half:HALO - half + tm, :] * w[0:1]
        for j in range(1, GDN_CONV):
            y = y + xe_ref[g, HALO - half + j:HALO - half + j + tm, :] * w[j:j + 1]
        y = y * jax.nn.sigmoid(y)
        if g < 2:
            heads = []
            for h in range(GDN_HEADS):
                yh = y[:, h * GDN_HEAD_DIM:(h + 1) * GDN_HEAD_DIM]
                yh = yh * lax.rsqrt(jnp.sum(yh * yh, axis=-1, keepdims=True) + EPS)
                heads.append(yh * GDN_HEAD_DIM ** -0.5 if g == 0 else yh)
            y = jnp.concatenate(heads, axis=1)
        qkv_ref[:, g * GDN_W:(g + 1) * GDN_W] = y.astype(BF16)
    sm = small_ref[...]
    z = sm + bias_ref[...]
    softplus = jnp.maximum(z, 0.0) + jnp.log(1.0 + jnp.exp(-jnp.abs(z)))
    lane = lax.broadcasted_iota(jnp.int32, (1, LANE), 1)
    is_g = (lane >= SMALL_GA) & (lane < SMALL_GB)
    is_b = (lane >= SMALL_GB) & (lane < SMALL_GB + 2 * GDN_HEADS)
    gb_ref[...] = jnp.where(is_g, a_ref[...] * softplus, jnp.where(is_b, jax.nn.sigmoid(sm), 0.0))


def _gdn_prep(pg, small, conv_w, neg_a, dt_bias, n_batch, seq, ctx_len):
    n_tok = pg.shape[0]
    tm = GDN_TILE
    lat_tiles = n_batch * seq // tm
    per_tile = tm // HALO
    n_halo_blocks = n_tok // HALO

    def main(g):
        return pl.BlockSpec((tm, GDN_W), lambda i: (i, g))

    def prev(g):
        return pl.BlockSpec((HALO, GDN_W), lambda i: (jnp.maximum(i * per_tile - 1, 0), g))

    def nxt(g):
        return pl.BlockSpec((HALO, GDN_W), lambda i: (jnp.minimum((i + 1) * per_tile, n_halo_blocks - 1), g))

    const = lambda i: (0, 0)
    return pl.pallas_call(
        functools.partial(_gdn_prep_kernel, lat_tiles=lat_tiles, lat_tps=seq // tm, ctx_tps=ctx_len // tm),
        out_shape=(jax.ShapeDtypeStruct((n_tok, 3 * GDN_W), BF16), jax.ShapeDtypeStruct((n_tok, LANE), F32)),
        grid=(n_tok // tm,),
        in_specs=[main(0), main(1), main(2), prev(0), prev(1), prev(2), nxt(0), nxt(1), nxt(2),
                  pl.BlockSpec((tm, LANE), lambda i: (i, 0)),
                  pl.BlockSpec(conv_w.shape, const), pl.BlockSpec((1, LANE), const), pl.BlockSpec((1, LANE), const)],
        out_specs=(pl.BlockSpec((tm, 3 * GDN_W), lambda i: (i, 0)), pl.BlockSpec((tm, LANE), lambda i: (i, 0))),
        scratch_shapes=[pltpu.VMEM((3, tm + 2 * HALO, GDN_W), F32)],
        compiler_params=_cparams("arbitrary"),
        name="gdn_prep",
    )(*([pg] * 9), small, conv_w, neg_a, dt_bias)


GDN_SUB = 16
GDN_CHUNKS_PER_STEP = 4


def _bdot(a, b):
    return jnp.dot(a.astype(BF16), b.astype(BF16), preferred_element_type=F32)


def _dot_tn(a, b):
    return lax.dot_general(a, b, (((0,), (0,)), ((), ())), preferred_element_type=F32)


def _each(f, *lists):
    return [f(*args) for args in zip(*lists)]


def _unit_tri_solve(n, rhs, same_blk, eye):
    nd = _each(lambda a: jnp.where(same_blk, a, 0.0).astype(BF16), n)
    nl = _each(lambda a: jnp.where(same_blk, 0.0, a).astype(BF16), n)
    nd2 = _each(lambda a: jnp.dot(a, a, preferred_element_type=F32).astype(BF16), nd)
    nd4 = _each(lambda a: jnp.dot(a, a, preferred_element_type=F32).astype(BF16), nd2)
    nd8 = _each(lambda a: jnp.dot(a, a, preferred_element_type=F32).astype(BF16), nd4)
    m = _each(lambda a: eye - a.astype(F32), nd)
    m = _each(lambda a, b: a + _bdot(a, b), m, nd2)
    m = _each(lambda a, b: a + _bdot(a, b), m, nd4)
    dinv = _each(lambda a, b: (a + _bdot(a, b)).astype(BF16), m, nd8)
    p = _each(lambda a, b: jnp.dot(a, b, preferred_element_type=F32).astype(BF16), dinv, nl)
    p2 = _each(lambda a: jnp.dot(a, a, preferred_element_type=F32).astype(BF16), p)
    y = _each(lambda a, b: _bdot(a, b), dinv, rhs)
    y = _each(lambda a, b: a + _bdot(b, a), y, p2)
    return _each(lambda a, b: a - _bdot(b, a), y, p)


def _gdn_scan_kernel(x0_ref, gb0_ref, x1_ref, gb1_ref, o0_ref, o1_ref, state_ref, *, n_sub):
    assert GDN_CHUNK == 4 * GDN_SUB

    @pl.when(pl.program_id(1) == 0)
    def _():
        state_ref[...] = jnp.zeros_like(state_ref)

    n = GDN_CHUNK
    r = lax.broadcasted_iota(jnp.int32, (n, n), 0)
    c = lax.broadcasted_iota(jnp.int32, (n, n), 1)
    same_blk = (r // GDN_SUB) == (c // GDN_SUB)
    eye = (r == c).astype(F32)
    dirs = ((x0_ref, gb0_ref, o0_ref), (x1_ref, gb1_ref, o1_ref))
    heads = [(d, h) for d in range(2) for h in range(GDN_HEADS)]
    chunk_at = [list(range(n_sub)), list(range(n_sub - 1, -1, -1))]
    probs = [(d, h, chunk_at[d][s]) for s in range(n_sub) for d, h in heads]
    incl = [r >= c, r <= c]
    strict = [r > c, r < c]
    end_row = [n - 1, 0]

    gbv, gc_all, gc_t = {}, {}, {}
    for d in range(2):
        tri = incl[d].astype(BF16)
        for ch in range(n_sub):
            g = dirs[d][1][ch * n:(ch + 1) * n, :]
            hi = g.astype(BF16)
            rem = g - hi.astype(F32)
            mid = rem.astype(BF16)
            lo = (rem - mid.astype(F32)).astype(BF16)
            gc = (jnp.dot(tri, hi, preferred_element_type=F32) + jnp.dot(tri, mid, preferred_element_type=F32)
                  + jnp.dot(tri, lo, preferred_element_type=F32))
            gbv[d, ch], gc_all[d, ch], gc_t[d, ch] = g, gc, jnp.transpose(gc)

    def head_cols(d, part, h, ch):
        lo_col = part * GDN_W + h * GDN_HEAD_DIM
        return dirs[d][0][ch * n:(ch + 1) * n, lo_col:lo_col + GDN_HEAD_DIM]

    lane_g = [SMALL_GA + GDN_HEADS * d + h for d, h, ch in probs]
    lane_b = [SMALL_GB + GDN_HEADS * d + h for d, h, ch in probs]
    gcol = [gc_all[d, ch][:, lg:lg + 1] for (d, h, ch), lg in zip(probs, lane_g)]
    grow = [gc_t[d, ch][lg:lg + 1, :] for (d, h, ch), lg in zip(probs, lane_g)]
    gend = [gc_all[d, ch][end_row[d]:end_row[d] + 1, lg:lg + 1] for (d, h, ch), lg in zip(probs, lane_g)]
    beta = [gbv[d, ch][:, lb:lb + 1] for (d, h, ch), lb in zip(probs, lane_b)]
    q = [head_cols(d, 0, h, ch) for d, h, ch in probs]
    k = [head_cols(d, 1, h, ch) for d, h, ch in probs]
    v = [head_cols(d, 2, h, ch) for d, h, ch in probs]
    kf = _each(lambda a: a.astype(F32), k)
    dec = [jnp.where(incl[d], jnp.exp(jnp.where(incl[d], gc_ - gr_, 0.0)), 0.0)
           for (d, h, ch), gc_, gr_ in zip(probs, gcol, grow)]
    qk_kk = _each(lambda q_, k_: _dot_nt(jnp.concatenate([q_, k_], axis=0), k_), q, k)
    qkd = _each(lambda a, dec_: (a[:n] * dec_).astype(BF16), qk_kk, dec)
    nmat = [jnp.where(strict[d], b_ * a[n:] * dec_, 0.0) for (d, h, ch), b_, a, dec_ in zip(probs, beta, qk_kk, dec)]
    egc = _each(jnp.exp, gcol)
    rhs = _each(lambda v_, kf_, b_, e_: jnp.concatenate([v_.astype(F32) * b_, kf_ * (b_ * e_)], axis=1).astype(BF16),
                v, kf, beta, egc)
    uw = _unit_tri_solve(nmat, rhs, same_blk, eye)
    wq = _each(lambda uw_, q_, e_: jnp.concatenate([uw_[:, GDN_HEAD_DIM:], q_.astype(F32) * e_], axis=0).astype(BF16),
               uw, q, egc)
    k_dec = _each(lambda kf_, ge_, gc_: (kf_ * jnp.exp(ge_ - gc_)).astype(BF16), kf, gend, gcol)
    g_end = _each(jnp.exp, gend)

    state = [state_ref[i] for i in range(len(heads))]
    for s in range(n_sub):
        sl = slice(s * len(heads), (s + 1) * len(heads))
        s_bf = _each(lambda a: a.astype(BF16), state)
        ws_qs = _each(lambda a, s_: jnp.dot(a, s_, preferred_element_type=F32), wq[sl], s_bf)
        v_new = _each(lambda uw_, a: (uw_[:, :GDN_HEAD_DIM] - a[:n]).astype(BF16), uw[sl], ws_qs)
        o_local = _each(lambda a, vn_: jnp.dot(a, vn_, preferred_element_type=F32), qkd[sl], v_new)
        s_add = _each(_dot_tn, k_dec[sl], v_new)
        state = _each(lambda ge_, st_, add_: ge_ * st_ + add_, g_end[sl], state, s_add)
        for (d, h, ch), a, ol in zip(probs[sl], ws_qs, o_local):
            dirs[d][2][ch * n:(ch + 1) * n, h * GDN_HEAD_DIM:(h + 1) * GDN_HEAD_DIM] = a[n:] + ol
    for i in range(len(heads)):
        state_ref[i] = state[i]


def _gdn_scan(qkv, gbeta, n_batch, seq, ctx_len):
    n_tok = qkv.shape[0]
    n = GDN_CHUNK * GDN_CHUNKS_PER_STEP
    lat_c, ctx_c = seq // n, ctx_len // n
    ctx0 = n_batch * seq // n

    def fwd(b, j):
        return (jnp.where(j < ctx_c, ctx0 + b * ctx_c + j, b * lat_c + (j - ctx_c)), 0)

    def bwd(b, j):
        return (jnp.where(j < ctx_c, ctx0 + b * ctx_c + (ctx_c - 1 - j), b * lat_c + (lat_c - 1 - (j - ctx_c))), 0)

    return pl.pallas_call(
        functools.partial(_gdn_scan_kernel, n_sub=GDN_CHUNKS_PER_STEP),
        out_shape=(jax.ShapeDtypeStruct((n_tok, GDN_W), F32), jax.ShapeDtypeStruct((n_tok, GDN_W), F32)),
        grid=(n_batch, ctx_c + lat_c),
        in_specs=[pl.BlockSpec((n, 3 * GDN_W), fwd), pl.BlockSpec((n, LANE), fwd),
                  pl.BlockSpec((n, 3 * GDN_W), bwd), pl.BlockSpec((n, LANE), bwd)],
        out_specs=(pl.BlockSpec((n, GDN_W), fwd), pl.BlockSpec((n, GDN_W), bwd)),
        scratch_shapes=[pltpu.VMEM((2 * GDN_HEADS, GDN_HEAD_DIM, GDN_HEAD_DIM), F32)],
        compiler_params=_cparams("arbitrary", "arbitrary"),
        name="gdn_scan",
    )(qkv, gbeta, qkv, gbeta)


def _pad_heads(w, n_heads, width, start, take):
    lead = w.shape[:-1]
    w = w.reshape(lead + (n_heads, width))[..., start:start + take]
    w = jnp.pad(w, [(0, 0)] * len(lead) + [(0, 0), (0, MLA_HEAD_PAD - take)])
    return w.reshape(lead + (n_heads * MLA_HEAD_PAD,))


def kernel(x, c, ctx, c_ctx, w_mod, b_mod, norm1, norm2, w_in, w_out, swa_q_norm, swa_k_norm, swa_sink, gdn_conv,
           gdn_a_log, gdn_dt_bias, gdn_out_norm, mla_q_a_norm, mla_w_uq, mla_kv_a_norm, mla_w_ukv, mla_q_norm,
           mla_k_norm, router_w, router_b, exp_w_gu, exp_b_gu, exp_w_dn, exp_b_dn):
    b, s, d = x.shape
    cl = ctx.shape[1]
    depth = w_mod.shape[0]
    n_lat, n_ctx = b * s, b * cl
    assert s % TOK_TILE == 0 and n_ctx % TOK_TILE == 0 and s % MOVE_TILE == 0 and n_ctx % ROUTE_TILE == 0
    assert cl % MLA_Q_TILE == 0 and cl % ATTN_BLOCK == 0 and n_lat % cl == 0

    n_rows = -(-(b + 1) // SUBLANE) * SUBLANE
    c_rows = jnp.concatenate([c, c_ctx[None, :], jnp.zeros((n_rows - b - 1, d), F32)], axis=0)
    mod_all = _modulation(c_rows, w_mod, b_mod)

    perm = _in_proj_perm()
    tabs = _rope_tables(s)
    seg_a = _block_diag_ones(SWA_Q, SWA_HEAD_DIM)
    seg_c = _block_diag_ones(MLA_HEADS * MLA_HEAD_PAD, MLA_HEAD_PAD)
    head_rows = np.concatenate([h * SWA_HEAD_DIM + np.arange(SWA_HEAD_DIM) for h in SWA_HEAD_ORDER])

    wg_all, wl_all = _split_glu_columns(exp_w_gu)
    n_le = depth * N_EXPERTS
    bg_all = exp_b_gu[..., 0::2].reshape(n_le, 1, -1)
    bl_all = exp_b_gu[..., 1::2].reshape(n_le, 1, -1)
    wd_all = exp_w_dn.reshape((n_le,) + exp_w_dn.shape[2:])
    bd_all = exp_b_dn.reshape(n_le, 1, d)
    x_flat = jnp.concatenate([x.reshape(n_lat, d), ctx.reshape(n_ctx, d)], axis=0)
    for l in range(depth):
        with_ctx = l < depth - 1
        n_tok = n_lat + n_ctx if with_ctx else n_lat
        mod_l = mod_all[l]
        w_in_bf = jnp.pad(w_in[l], ((0, 0), (0, 1)))[:, perm].astype(BF16)
        consts = (seg_a, seg_c,
                  jnp.tile(swa_q_norm[l], SWA_HEADS)[None], jnp.tile(swa_k_norm[l], SWA_KV_HEADS)[None],
                  mla_q_a_norm[l][None], mla_kv_a_norm[l][None],
                  _pad_heads(mla_w_uq[l], MLA_HEADS, MLA_QK, 0, MLA_QK).astype(BF16),
                  _pad_heads(mla_w_ukv[l], MLA_HEADS, MLA_NOPE + MLA_V, 0, MLA_NOPE).astype(BF16),
                  _pad_heads(mla_w_ukv[l], MLA_HEADS, MLA_NOPE + MLA_V, MLA_NOPE, MLA_V).astype(BF16),
                  _pad_heads(jnp.tile(mla_q_norm[l], MLA_HEADS), MLA_HEADS, MLA_QK, 0, MLA_QK)[None],
                  _pad_heads(jnp.tile(mla_k_norm[l], MLA_HEADS), MLA_HEADS, MLA_QK, 0, MLA_QK)[None])
        pg, small, qa, kva, qm, km, vm = _in_projection(x_flat, norm1[l], mod_l, w_in_bf, tabs, consts, b, s)
        o_a = _swa_attention(swa_sink[l], qa, kva, b, s, cl, with_ctx)
        o_c = _mla_attention(qm, km, vm, b, s, cl, with_ctx)
        lanes_g = slice(SMALL_GA, SMALL_GA + 2 * GDN_HEADS)
        neg_a = jnp.zeros((1, LANE), F32).at[0, lanes_g].set(-jnp.exp(gdn_a_log[l]).reshape(-1))
        dt_bias = jnp.zeros((1, LANE), F32).at[0, lanes_g].set(gdn_dt_bias[l].reshape(-1))
        qkv, gbeta = _gdn_prep(pg, small, gdn_conv[l], neg_a, dt_bias, b, s, cl)
        o_fwd, o_bwd = _gdn_scan(qkv, gbeta, b, s, cl)

        wa = w_out[l][:SWA_Q][head_rows].astype(BF16)
        wb = w_out[l][SWA_Q:SWA_Q + GDN_W].astype(BF16)
        wc = w_out[l][SWA_Q + GDN_W:].reshape(MLA_HEADS, MLA_V, d)
        wc = jnp.pad(wc, ((0, 0), (0, MLA_HEAD_PAD - MLA_V), (0, 0))).reshape(MLA_HEADS * MLA_HEAD_PAD, d).astype(BF16)
        x_mid, h2 = _out_projection(o_a, o_fwd, o_bwd, pg, o_c, x_flat, wa, wb, wc,
                                    jnp.tile(gdn_out_norm[l], GDN_HEADS)[None], norm2[l], mod_l, n_tok, b, s)

        x_flat = _moe(h2, x_mid, n_tok, mod_l, router_w[l].T.astype(BF16), router_b[l], l, wg_all, wl_all,
                      bg_all, bl_all, wd_all, bd_all, b, s)
    return x_flat[:n_lat].reshape(b, s, d)
```

```python
import functools

import jax
import jax.numpy as jnp
import numpy as np
from jax import lax
from jax.experimental import pallas as pl
from jax.experimental.pallas import tpu as pltpu

F32 = jnp.float32
BF16 = jnp.bfloat16

GRID_W = 64
N_MOD = 6
EPS = 1e-6
ROPE_BASE = 10000.0
ATTN_BLOCK = 128

SWA_HEADS = 4
SWA_KV_HEADS = 2
SWA_HEAD_DIM = 64
SWA_WINDOW = 128

GDN_HEADS = 4
GDN_HEAD_DIM = 128
GDN_CONV = 5
GDN_CHUNK = 64

MLA_HEADS = 4
MLA_Q_RANK = 256
MLA_KV_RANK = 128
MLA_NOPE = 64
MLA_ROPE = 32
MLA_V = 64
MLA_QK = MLA_NOPE + MLA_ROPE

N_EXPERTS = 32
TOP_K = 4
SWIGLU_LIMIT = 7.0
SWIGLU_ALPHA = 1.702

SWA_Q = SWA_HEADS * SWA_HEAD_DIM
SWA_KV = SWA_KV_HEADS * SWA_HEAD_DIM
GDN_W = GDN_HEADS * GDN_HEAD_DIM
MLA_O = MLA_HEADS * MLA_V
D_MIX = SWA_Q + GDN_W + MLA_O
IN_SPLITS = (SWA_Q, SWA_KV, SWA_KV, GDN_W, GDN_W, GDN_W, GDN_W, 2 * GDN_HEADS, 2 * GDN_HEADS, MLA_Q_RANK,
             MLA_KV_RANK, MLA_ROPE)
N_IN = sum(IN_SPLITS)

LANE = 128
SUBLANE = 8
VMEM_LIMIT = 56 * 1024 * 1024

TOK_TILE = 512
MOE_TILE = 512
ROUTE_TILE = 512
MOVE_TILE = 512
ROW_UNROLL = 4
ROW_GROUP = 256
MLA_Q_TILE = 256

COL_AQ, COL_AK, COL_AV = 0, 256, 384
COL_G = 512
COL_CQ, COL_CKV, COL_SMALL = 2560, 2816, 2944
N_IN_PAD = 3072
SMALL_KR, SMALL_GA, SMALL_GB = 0, MLA_ROPE, MLA_ROPE + 2 * GDN_HEADS
SWA_HEAD_ORDER = (0, 2, 1, 3)
MLA_HEAD_PAD = 128


def _cparams(*sem):
    return pltpu.CompilerParams(dimension_semantics=sem, vmem_limit_bytes=VMEM_LIMIT)


def _in_proj_perm():
    old = np.cumsum((0,) + IN_SPLITS)
    o_aq, o_ak, o_av, o_gq, o_gk, o_gv, o_gz, o_ga, o_gb, o_cq, o_ckv, o_ckr = old[:-1]
    perm = np.full((N_IN_PAD,), N_IN, np.int32)
    perm[COL_AQ:COL_AQ + SWA_Q] = np.concatenate(
        [o_aq + h * SWA_HEAD_DIM + np.arange(SWA_HEAD_DIM) for h in SWA_HEAD_ORDER])
    perm[COL_AK:COL_G + 4 * GDN_W] = np.arange(o_ak, o_ga)
    perm[COL_CQ:COL_CQ + MLA_Q_RANK] = o_cq + np.arange(MLA_Q_RANK)
    perm[COL_CKV:COL_CKV + MLA_KV_RANK] = o_ckv + np.arange(MLA_KV_RANK)
    perm[COL_SMALL + SMALL_KR:COL_SMALL + SMALL_KR + MLA_ROPE] = o_ckr + np.arange(MLA_ROPE)
    perm[COL_SMALL + SMALL_GA:COL_SMALL + SMALL_GA + 2 * GDN_HEADS] = o_ga + np.arange(2 * GDN_HEADS)
    perm[COL_SMALL + SMALL_GB:COL_SMALL + SMALL_GB + 2 * GDN_HEADS] = o_gb + np.arange(2 * GDN_HEADS)
    return perm


def _mod_kernel(c_ref, w_ref, b_ref, o_ref):
    a = c_ref[...]
    a = a * jax.nn.sigmoid(a)
    o_ref[0] = jnp.dot(a.astype(BF16), w_ref[0].astype(BF16), preferred_element_type=F32) + b_ref[0]


def _modulation(c_rows, w_mod, b_mod):
    nl, d, n = w_mod.shape
    r = c_rows.shape[0]
    tn = 1536
    out = pl.pallas_call(
        _mod_kernel,
        out_shape=jax.ShapeDtypeStruct((nl, r, n), F32),
        grid=(nl, n // tn),
        in_specs=[pl.BlockSpec((r, d), lambda l, j: (0, 0)),
                  pl.BlockSpec((1, d, tn), lambda l, j: (l, 0, j)),
                  pl.BlockSpec((1, 1, tn), lambda l, j: (l, 0, j))],
        out_specs=pl.BlockSpec((1, r, tn), lambda l, j: (l, 0, j)),
        compiler_params=_cparams("arbitrary", "arbitrary"),
        name="modulation",
    )(c_rows, w_mod, b_mod.reshape(nl, 1, n))
    return out.reshape(nl, r, N_MOD, d)


def _modulated_norm(x, gain, shift, scale):
    y = x * lax.rsqrt(jnp.mean(x * x, axis=-1, keepdims=True) + EPS)
    return (y * gain) * (1.0 + scale) + shift


def _group_of_tile(i, tiles_per_seq, n_batch):
    return jnp.minimum(i // tiles_per_seq, n_batch)


def _group_sumsq(x, seg):
    x2 = x * x
    hi = x2.astype(BF16)
    lo = (x2 - hi.astype(F32)).astype(BF16)
    return jnp.dot(hi, seg, preferred_element_type=F32) + jnp.dot(lo, seg, preferred_element_type=F32)


def _swap_pairs(x, half):
    w = x.shape[1]
    lane = lax.broadcasted_iota(jnp.int32, (1, w), 1)
    first = (lane % (2 * half)) < half
    return jnp.where(first, pltpu.roll(x, w - half, axis=1), pltpu.roll(x, half, axis=1))


def _inproj_kernel(xl_ref, xc_ref, g_ref, m_ref, w_ref, ra_c_ref, ra_s_ref, rc_c_ref, rc_s_ref, seg_a_ref, seg_c_ref,
                   gaq_ref, gak_ref, gcq_ref, gckv_ref, wuq_ref, wuk_ref, wuv_ref, gmq_ref, gmk_ref,
                   pg_ref, small_ref, qa_ref, kva_ref, qm_ref, km_ref, vm_ref, *, n_lat_tiles):
    m = m_ref[0]
    groups = [pl.ds(i * ROW_GROUP, ROW_GROUP) for i in range(xl_ref.shape[0] // ROW_GROUP)]
    projected = []
    for rows in groups:
        x = _tile_rows(xl_ref, xc_ref, rows, n_lat_tiles)
        h = _modulated_norm(x, g_ref[...], m[0:1], m[1:2]).astype(BF16)
        pa = jnp.dot(h, w_ref[:, COL_AQ:COL_G], preferred_element_type=F32)
        pg_ref[rows, :] = jnp.dot(h, w_ref[:, COL_G:COL_CQ], preferred_element_type=F32)
        pc = jnp.dot(h, w_ref[:, COL_CQ:], preferred_element_type=F32)
        small_ref[rows, :] = pc[:, MLA_Q_RANK + MLA_KV_RANK:]
        projected.append((pa, pc))
    for rows, (pa, pc) in zip(groups, projected):
        _attention_prep(rows, pa, pc, ra_c_ref, ra_s_ref, rc_c_ref, rc_s_ref, seg_a_ref, seg_c_ref, gaq_ref, gak_ref,
                        gcq_ref, gckv_ref, wuq_ref, wuk_ref, wuv_ref, gmq_ref, gmk_ref,
                        qa_ref, kva_ref, qm_ref, km_ref, vm_ref)


def _attention_prep(rows, pa, pc, ra_c_ref, ra_s_ref, rc_c_ref, rc_s_ref, seg_a_ref, seg_c_ref, gaq_ref, gak_ref,
                    gcq_ref, gckv_ref, wuq_ref, wuk_ref, wuv_ref, gmq_ref, gmk_ref,
                    qa_ref, kva_ref, qm_ref, km_ref, vm_ref):
    small = pc[:, MLA_Q_RANK + MLA_KV_RANK:]

    aq = pa[:, 0:SWA_Q]
    ak = pa[:, SWA_Q:SWA_Q + SWA_KV]
    av = pa[:, SWA_Q + SWA_KV:SWA_Q + 2 * SWA_KV]
    ca, sa = ra_c_ref[rows, :], ra_s_ref[rows, :]
    seg_a = seg_a_ref[...]
    qn = aq * lax.rsqrt(_group_sumsq(aq, seg_a) * (1.0 / SWA_HEAD_DIM) + EPS) * gaq_ref[...]
    kn = ak * lax.rsqrt(_group_sumsq(ak, seg_a[:SWA_KV, :SWA_KV]) * (1.0 / SWA_HEAD_DIM) + EPS) * gak_ref[...]
    qn = qn * jnp.concatenate([ca, ca], axis=1) + _swap_pairs(qn, SWA_HEAD_DIM // 4) * jnp.concatenate([sa, sa], axis=1)
    kn = kn * ca + _swap_pairs(kn, SWA_HEAD_DIM // 4) * sa
    qa_ref[rows, :] = (qn * SWA_HEAD_DIM ** -0.5).astype(BF16)
    kva_ref[rows, :] = jnp.concatenate([kn, av], axis=1).astype(BF16)

    cq = pc[:, 0:MLA_Q_RANK]
    ckv = pc[:, MLA_Q_RANK:MLA_Q_RANK + MLA_KV_RANK]
    cqn = cq * lax.rsqrt(jnp.mean(cq * cq, axis=-1, keepdims=True) + EPS) * gcq_ref[...]
    ckvn = (ckv * lax.rsqrt(jnp.mean(ckv * ckv, axis=-1, keepdims=True) + EPS) * gckv_ref[...]).astype(BF16)
    q = jnp.dot(cqn.astype(BF16), wuq_ref[...], preferred_element_type=F32)
    k = jnp.dot(ckvn, wuk_ref[...], preferred_element_type=F32)
    v = jnp.dot(ckvn, wuv_ref[...], preferred_element_type=F32)
    lane = lax.broadcasted_iota(jnp.int32, (1, LANE), 1)
    kpe = jnp.where((lane >= MLA_NOPE) & (lane < MLA_QK), pltpu.roll(small, MLA_NOPE, axis=1), 0.0)
    k = k + jnp.concatenate([kpe] * MLA_HEADS, axis=1)
    seg_c = seg_c_ref[...]
    q = q * lax.rsqrt(_group_sumsq(q, seg_c) * (1.0 / MLA_QK) + EPS) * gmq_ref[...]
    k = k * lax.rsqrt(_group_sumsq(k, seg_c) * (1.0 / MLA_QK) + EPS) * gmk_ref[...]
    cc = jnp.concatenate([rc_c_ref[rows, :]] * MLA_HEADS, axis=1)
    sc = jnp.concatenate([rc_s_ref[rows, :]] * MLA_HEADS, axis=1)
    q = q * cc + _swap_pairs(q, MLA_ROPE // 4) * sc
    k = k * cc + _swap_pairs(k, MLA_ROPE // 4) * sc
    qm_ref[rows, :] = (q * MLA_QK ** -0.5).astype(BF16)
    km_ref[rows, :] = k.astype(BF16)
    vm_ref[rows, :] = v.astype(BF16)


def _token_sources(x_lat, x_ctx, ctx_tile0, n_lat_tiles, tm):
    d = x_lat.shape[1]
    return [pl.BlockSpec((tm, d), lambda i: (jnp.minimum(i, n_lat_tiles - 1), 0)),
            pl.BlockSpec((tm, d), lambda i: (ctx_tile0 + jnp.maximum(i - n_lat_tiles, 0), 0))]


def _tile_rows(xl_ref, xc_ref, rows, n_lat_tiles):
    tile = jnp.zeros((ROW_GROUP, 1), jnp.int32) + pl.program_id(0)
    return jnp.where(tile < n_lat_tiles, xl_ref[rows, :], xc_ref[rows, :])


def _in_projection(x_lat, x_ctx, ctx_tile0, n_tok, gain, mod_l, w_bf, tabs, consts, n_batch, seq):
    d = x_lat.shape[1]
    tm = TOK_TILE
    tps = seq // tm
    n_lat_tiles = n_batch * tps
    row = lambda i: (i, 0)
    const = lambda i: (0, 0)
    tab = lambda i: (jnp.where(i < n_lat_tiles, i % tps, tps), 0)
    wide = MLA_HEADS * MLA_HEAD_PAD
    in_specs = _token_sources(x_lat, x_ctx, ctx_tile0, n_lat_tiles, tm)
    in_specs += [pl.BlockSpec((1, d), const),
                pl.BlockSpec((1, N_MOD, d), lambda i: (_group_of_tile(i, tps, n_batch), 0, 0)),
                pl.BlockSpec(w_bf.shape, const)]
    in_specs += [pl.BlockSpec((tm, LANE), tab)] * 4
    in_specs += [pl.BlockSpec(a.shape, const) for a in consts]
    widths = (4 * GDN_W, LANE, SWA_Q, 2 * SWA_KV, wide, wide, wide)
    dtypes = (F32, F32, BF16, BF16, BF16, BF16, BF16)
    return pl.pallas_call(
        functools.partial(_inproj_kernel, n_lat_tiles=n_lat_tiles),
        out_shape=tuple(jax.ShapeDtypeStruct((n_tok, w), t) for w, t in zip(widths, dtypes)),
        grid=(n_tok // tm,),
        in_specs=in_specs,
        out_specs=tuple(pl.BlockSpec((tm, w), row) for w in widths),
        compiler_params=_cparams("arbitrary"),
        name="in_projection",
    )(x_lat, x_ctx, gain.reshape(1, d), mod_l, w_bf, *tabs, *consts)


def _rope_tables(seq):
    t = jnp.arange(seq)
    row = (t // GRID_W).astype(F32)
    col = (t % GRID_W).astype(F32)

    def cos_sin(rot_dim):
        n_freq = rot_dim // 4
        freq = ROPE_BASE ** (-jnp.arange(n_freq, dtype=F32) / n_freq)
        ar, ac = row[:, None] * freq, col[:, None] * freq
        c = jnp.concatenate([jnp.cos(ar), jnp.cos(ar), jnp.cos(ac), jnp.cos(ac)], axis=1)
        s = jnp.concatenate([-jnp.sin(ar), jnp.sin(ar), -jnp.sin(ac), jnp.sin(ac)], axis=1)
        return c, s

    ca, sa = cos_sin(SWA_HEAD_DIM)
    ca, sa = jnp.tile(ca, (1, 2)), jnp.tile(sa, (1, 2))
    cc, sc = cos_sin(MLA_ROPE)
    ones_l = jnp.ones((seq, MLA_NOPE), F32)
    ones_r = jnp.ones((seq, MLA_HEAD_PAD - MLA_QK), F32)
    cc = jnp.concatenate([ones_l, cc, ones_r], axis=1)
    sc = jnp.concatenate([0 * ones_l, sc, 0 * ones_r], axis=1)
    ident_c = jnp.ones((TOK_TILE, LANE), F32)
    ident_s = jnp.zeros((TOK_TILE, LANE), F32)
    return tuple(jnp.concatenate([a, i], axis=0) for a, i in ((ca, ident_c), (sa, ident_s), (cc, ident_c), (sc, ident_s)))


def _block_diag_ones(n, blk):
    i = np.arange(n) // blk
    return jnp.asarray((i[:, None] == i[None, :]).astype(np.float32), BF16)


def _dot_nt(a, b):
    return lax.dot_general(a, b, (((1,), (1,)), ((), ())), preferred_element_type=F32)


def _swa_heads(sink_ref, q, k_list, v_list, mask_list, o_ref):
    lane = lax.broadcasted_iota(jnp.int32, (1, LANE), 1)
    lower = lane < SWA_HEAD_DIM
    cols = []
    for cgrp in range(2):
        qc = q[:, cgrp * LANE:(cgrp + 1) * LANE]
        halves = []
        for half in range(2):
            head = SWA_HEAD_ORDER[2 * cgrp + half]
            qh = jnp.where(lower if half == 0 else ~lower, qc, jnp.zeros_like(qc))
            sink = sink_ref[head]
            scores = []
            m = jnp.full((q.shape[0], 1), sink, F32)
            for kb, mk in zip(k_list, mask_list):
                s = _dot_nt(qh, kb)
                if mk is not None:
                    s = jnp.where(mk, s, -jnp.inf)
                scores.append(s)
                m = jnp.maximum(m, jnp.max(s, axis=-1, keepdims=True))
            l = jnp.exp(sink - m)
            acc = jnp.zeros((q.shape[0], LANE), F32)
            for s, vb in zip(scores, v_list):
                p = jnp.exp(s - m)
                l = l + jnp.sum(p, axis=-1, keepdims=True)
                acc = acc + jnp.dot(p.astype(BF16), vb, preferred_element_type=F32)
            halves.append(acc * (1.0 / l))
        cols.append(jnp.where(lower, halves[0], halves[1]))
    o_ref[...] = jnp.concatenate(cols, axis=1).astype(o_ref.dtype)


SWA_Q_BLOCKS = 2
SWA_KEY_OFFSETS = tuple(range(-1, SWA_Q_BLOCKS + 1))


def _swa_kernel(sink_ref, q_ref, *refs, n_blocks):
    kv_refs, kvc_ref, o_ref = refs[:len(SWA_KEY_OFFSETS)], refs[-2], refs[-1]
    j = pl.program_id(1)
    n_steps = n_blocks // SWA_Q_BLOCKS
    kc, vc = kvc_ref[:, 0:SWA_KV], kvc_ref[:, SWA_KV:]

    @pl.when(j < n_steps)
    def _():
        rows = SWA_Q_BLOCKS * ATTN_BLOCK
        r = lax.broadcasted_iota(jnp.int32, (rows, ATTN_BLOCK), 0)
        c = lax.broadcasted_iota(jnp.int32, (rows, ATTN_BLOCK), 1)
        masks = []
        for off in SWA_KEY_OFFSETS:
            blk = j * SWA_Q_BLOCKS + off
            in_seq = (blk >= 0) & (blk < n_blocks)
            masks.append(jnp.abs(off * ATTN_BLOCK + c - r) <= jnp.where(in_seq, SWA_WINDOW, -1))
        ks = [ref[:, 0:SWA_KV] for ref in kv_refs] + [kc]
        vs = [ref[:, SWA_KV:] for ref in kv_refs] + [vc]
        _swa_heads(sink_ref, q_ref[...], ks, vs, masks + [None], o_ref)

    @pl.when(j >= n_steps)
    def _():
        _swa_heads(sink_ref, q_ref[...], [kc], [vc], [None], o_ref)


def _swa_attention(sink, qa, kva, n_batch, seq, ctx_len, with_ctx):
    rows = SWA_Q_BLOCKS * ATTN_BLOCK
    nb = seq // ATTN_BLOCK
    nq = seq // rows
    ncq = ctx_len // rows
    n_lat = n_batch * seq
    steps = nq + (ncq if with_ctx else 0)
    n_tok = n_lat + (n_batch * ctx_len if with_ctx else 0)

    def q_map(b, j, s):
        return (jnp.where(j < nq, b * nq + j, n_lat // rows + b * ncq + (j - nq)), 0)

    def kv_map(off):
        def f(b, j, s):
            jj = jnp.clip(jnp.minimum(j, nq - 1) * SWA_Q_BLOCKS + off, 0, nb - 1)
            return (b * nb + jj, 0)
        return f

    return pl.pallas_call(
        functools.partial(_swa_kernel, n_blocks=nb),
        out_shape=jax.ShapeDtypeStruct((n_tok, SWA_Q), BF16),
        grid_spec=pltpu.PrefetchScalarGridSpec(
            num_scalar_prefetch=1,
            grid=(n_batch, steps),
            in_specs=[pl.BlockSpec((rows, SWA_Q), q_map)]
            + [pl.BlockSpec((ATTN_BLOCK, 2 * SWA_KV), kv_map(off)) for off in SWA_KEY_OFFSETS]
            + [pl.BlockSpec((ctx_len, 2 * SWA_KV), lambda b, j, s: (n_lat // ctx_len + b, 0))],
            out_specs=pl.BlockSpec((rows, SWA_Q), q_map)),
        compiler_params=_cparams("arbitrary", "arbitrary"),
        name="swa_attention",
    )(sink, qa, *([kva] * len(SWA_KEY_OFFSETS)), kva)


def _mla_heads(q_ref, k_refs, v_refs, o_ref):
    for h in range(MLA_HEADS):
        sl = slice(h * MLA_HEAD_PAD, (h + 1) * MLA_HEAD_PAD)
        qh = q_ref[:, sl]
        scores = [_dot_nt(qh, k_ref[:, sl]) for k_ref in k_refs]
        m = jnp.max(scores[0], axis=-1, keepdims=True)
        for s in scores[1:]:
            m = jnp.maximum(m, jnp.max(s, axis=-1, keepdims=True))
        l = jnp.zeros_like(m)
        acc = jnp.zeros((qh.shape[0], MLA_HEAD_PAD), F32)
        for s, v_ref in zip(scores, v_refs):
            p = jnp.exp(s - m)
            l = l + jnp.sum(p, axis=-1, keepdims=True)
            acc = acc + jnp.dot(p.astype(BF16), v_ref[:, sl], preferred_element_type=F32)
        o_ref[:, sl] = (acc * (1.0 / l)).astype(o_ref.dtype)


def _mla_kernel(q_ref, kl_ref, vl_ref, kc_ref, vc_ref, o_ref, *, n_lat_steps):
    j = pl.program_id(1)

    @pl.when(j < n_lat_steps)
    def _():
        _mla_heads(q_ref, [kl_ref, kc_ref], [vl_ref, vc_ref], o_ref)

    @pl.when(j >= n_lat_steps)
    def _():
        _mla_heads(q_ref, [kc_ref], [vc_ref], o_ref)


def _mla_attention(qm, km, vm, n_batch, seq, ctx_len, with_ctx):
    tq = MLA_Q_TILE
    nq = seq // tq
    ncq = ctx_len // tq
    n_lat = n_batch * seq
    steps = nq + (ncq if with_ctx else 0)
    n_tok = n_lat + (n_batch * ctx_len if with_ctx else 0)
    wide = MLA_HEADS * MLA_HEAD_PAD

    def q_map(b, j):
        return (jnp.where(j < nq, b * nq + j, n_lat // tq + b * ncq + (j - nq)), 0)

    lat_map = lambda b, j: (b, 0)
    ctx_map = lambda b, j: (n_lat // ctx_len + b, 0)
    return pl.pallas_call(
        functools.partial(_mla_kernel, n_lat_steps=nq),
        out_shape=jax.ShapeDtypeStruct((n_tok, wide), BF16),
        grid=(n_batch, steps),
        in_specs=[pl.BlockSpec((tq, wide), q_map),
                  pl.BlockSpec((seq, wide), lat_map),
                  pl.BlockSpec((seq, wide), lat_map),
                  pl.BlockSpec((ctx_len, wide), ctx_map),
                  pl.BlockSpec((ctx_len, wide), ctx_map)],
        out_specs=pl.BlockSpec((tq, wide), q_map),
        compiler_params=_cparams("arbitrary", "arbitrary"),
        name="mla_attention",
    )(qm, km, vm, km, vm)


def _outproj_kernel(oa_ref, of_ref, ob_ref, z_ref, oc_ref, xl_ref, xc_ref, wa_ref, wb_ref, wc_ref, gg_ref, g_ref, m_ref,
                    xo_ref, h_ref, *, n_lat_tiles):
    m = m_ref[0]
    groups = [pl.ds(i * ROW_GROUP, ROW_GROUP) for i in range(xl_ref.shape[0] // ROW_GROUP)]
    mixed = []
    for rows in groups:
        ob = of_ref[rows, :] + ob_ref[rows, :]
        z = z_ref[rows, :]
        heads = []
        for h in range(GDN_HEADS):
            oh = ob[:, h * GDN_HEAD_DIM:(h + 1) * GDN_HEAD_DIM]
            heads.append(oh * lax.rsqrt(jnp.mean(oh * oh, axis=-1, keepdims=True) + EPS))
        gated = jnp.concatenate(heads, axis=1) * gg_ref[...] * (z * jax.nn.sigmoid(z))
        mixed.append(jnp.dot(oa_ref[rows, :], wa_ref[...], preferred_element_type=F32)
                     + jnp.dot(gated.astype(BF16), wb_ref[...], preferred_element_type=F32)
                     + jnp.dot(oc_ref[rows, :], wc_ref[...], preferred_element_type=F32))
    for rows, o in zip(groups, mixed):
        xn = _tile_rows(xl_ref, xc_ref, rows, n_lat_tiles) + m[2:3] * o
        xo_ref[rows, :] = xn
        h_ref[rows, :] = _modulated_norm(xn, g_ref[...], m[3:4], m[4:5])


def _out_projection(o_a, o_fwd, o_bwd, pg, o_c, x_lat, x_ctx, ctx_tile0, wa, wb, wc, gdn_gain, gain2, mod_l, n_tok,
                    n_batch, seq):
    d = x_lat.shape[1]
    tm = TOK_TILE
    tps = seq // tm
    n_lat_tiles = n_batch * tps
    row = lambda i: (i, 0)
    const = lambda i: (0, 0)
    return pl.pallas_call(
        functools.partial(_outproj_kernel, n_lat_tiles=n_lat_tiles),
        out_shape=(jax.ShapeDtypeStruct((n_tok, d), F32), jax.ShapeDtypeStruct((n_tok, d), F32)),
        grid=(n_tok // tm,),
        in_specs=[pl.BlockSpec((tm, o_a.shape[1]), row),
                  pl.BlockSpec((tm, GDN_W), row),
                  pl.BlockSpec((tm, GDN_W), row),
                  pl.BlockSpec((tm, GDN_W), lambda i: (i, 3)),
                  pl.BlockSpec((tm, o_c.shape[1]), row)]
        + _token_sources(x_lat, x_ctx, ctx_tile0, n_lat_tiles, tm)
        + [pl.BlockSpec(wa.shape, const),
                  pl.BlockSpec(wb.shape, const),
                  pl.BlockSpec(wc.shape, const),
                  pl.BlockSpec((1, GDN_W), const),
                  pl.BlockSpec((1, d), const),
                  pl.BlockSpec((1, N_MOD, d), lambda i: (_group_of_tile(i, tps, n_batch), 0, 0))],
        out_specs=(pl.BlockSpec((tm, d), row), pl.BlockSpec((tm, d), row)),
        compiler_params=_cparams("arbitrary"),
        name="out_projection",
    )(o_a, o_fwd, o_bwd, pg, o_c, x_lat, x_ctx, wa, wb, wc, gdn_gain, gain2.reshape(1, d), mod_l)


def _route_kernel(h_ref, rw_ref, rb_ref, idx_ref, gate_ref, cnt_ref, base_ref):
    step = pl.program_id(0)
    tm = h_ref.shape[0]

    @pl.when(step == 0)
    def _():
        base_ref[...] = jnp.zeros_like(base_ref)

    logits = lax.dot_general(rw_ref[...], h_ref[...].astype(BF16), (((1,), (1,)), ((), ())),
                             preferred_element_type=F32) + rb_ref[...]
    e_iota = lax.broadcasted_iota(jnp.int32, logits.shape, 0)
    work = logits
    tops, picks = [], []
    for _k in range(TOP_K):
        mx = jnp.max(work, axis=0, keepdims=True)
        pick = jnp.min(jnp.where(work == mx, e_iota, N_EXPERTS), axis=0, keepdims=True)
        work = jnp.where(e_iota == pick, -jnp.inf, work)
        tops.append(mx)
        picks.append(pick)
    exps = [jnp.exp(t - tops[0]) for t in tops]
    denom = exps[0] + exps[1] + exps[2] + exps[3]
    sel = jnp.zeros(logits.shape, F32)
    for pick in picks:
        sel = sel + (e_iota == pick).astype(F32)
    row = lax.broadcasted_iota(jnp.int32, (tm, tm), 0)
    col = lax.broadcasted_iota(jnp.int32, (tm, tm), 1)
    before = (row < col).astype(BF16)
    cnt = jnp.dot(sel.astype(BF16), before, preferred_element_type=F32) + base_ref[:, 0:1]
    ranks = [jnp.sum(jnp.where(e_iota == pick, cnt, 0.0), axis=0, keepdims=True) for pick in picks]
    idx_ref[0] = jnp.concatenate(picks + [r.astype(jnp.int32) for r in ranks], axis=0)
    gate_rows = jnp.concatenate([e / denom for e in exps] + [jnp.zeros((LANE - TOP_K, tm), F32)], axis=0)
    gate_ref[...] = jnp.transpose(gate_rows)
    base_ref[...] = base_ref[...] + jnp.sum(sel, axis=1, keepdims=True)
    cnt_ref[...] = base_ref[...]


def _route(h_flat, n_tok, rw_t_bf, rb):
    d = h_flat.shape[1]
    tm = ROUTE_TILE
    nt = n_tok // tm
    return pl.pallas_call(
        _route_kernel,
        out_shape=(jax.ShapeDtypeStruct((nt, 2 * TOP_K, tm), jnp.int32),
                   jax.ShapeDtypeStruct((n_tok, LANE), F32),
                   jax.ShapeDtypeStruct((N_EXPERTS, LANE), F32)),
        grid=(nt,),
        in_specs=[pl.BlockSpec((tm, d), lambda i: (i, 0)),
                  pl.BlockSpec((N_EXPERTS, d), lambda i: (0, 0)),
                  pl.BlockSpec((N_EXPERTS, 1), lambda i: (0, 0))],
        out_specs=(pl.BlockSpec((1, 2 * TOP_K, tm), lambda i: (i, 0, 0)),
                   pl.BlockSpec((tm, LANE), lambda i: (i, 0)),
                   pl.BlockSpec((N_EXPERTS, LANE), lambda i: (0, 0))),
        scratch_shapes=[pltpu.VMEM((N_EXPERTS, LANE), F32)],
        compiler_params=_cparams("arbitrary"),
        name="moe_route",
    )(h_flat, rw_t_bf, rb.reshape(N_EXPERTS, 1))


def _dispatch_kernel(lo_ref, hi_ref, slot_ref, h_ref, xs_out, zero_ref, sem, zero_sem, tile_sem):
    tm = slot_ref.shape[2]

    @pl.when(pl.program_id(0) == pl.num_programs(0) - 1)
    def _():
        zero_ref[...] = jnp.zeros_like(zero_ref)

        def row_copy0(r):
            return pltpu.make_async_copy(zero_ref.at[pl.ds(0, 1)], xs_out.at[pl.ds(r, 1)], zero_sem)

        def tile_copy0(g):
            dst = xs_out.at[pl.ds(pl.multiple_of(g * SUBLANE, SUBLANE), SUBLANE)]
            return pltpu.make_async_copy(zero_ref, dst, tile_sem)

        def bounds(e):
            lo, hi = lo_ref[e], hi_ref[e]
            mid = jnp.minimum((lo + SUBLANE - 1) // SUBLANE * SUBLANE, hi)
            return lo, mid, hi

        def start_expert(e, carry):
            lo, mid, hi = bounds(e)
            lax.fori_loop(lo, mid, lambda r, c: (row_copy0(r).start(), c)[1], carry)
            return lax.fori_loop(mid // SUBLANE, hi // SUBLANE, lambda g, c: (tile_copy0(g).start(), c)[1], carry)

        def wait_expert(e, carry):
            lo, mid, hi = bounds(e)
            lax.fori_loop(lo, mid, lambda r, c: (row_copy0(r).wait(), c)[1], carry)
            return lax.fori_loop(mid // SUBLANE, hi // SUBLANE, lambda g, c: (tile_copy0(g).wait(), c)[1], carry)

        lax.fori_loop(0, N_EXPERTS, start_expert, 0)
        lax.fori_loop(0, N_EXPERTS, wait_expert, 0)

    def row_copy(t, k):
        return pltpu.make_async_copy(h_ref.at[pl.ds(t, 1)], xs_out.at[pl.ds(slot_ref[0, k, t], 1)], sem)

    def issue(i, carry):
        for u in range(ROW_UNROLL):
            for k in range(TOP_K):
                row_copy(i * ROW_UNROLL + u, k).start(priority=k % 2)
        return carry

    lax.fori_loop(0, tm // ROW_UNROLL, issue, 0)

    def drain(i, carry):
        for u in range(ROW_UNROLL):
            for k in range(TOP_K):
                row_copy(i * ROW_UNROLL + u, k).wait()
        return carry

    lax.fori_loop(0, tm // ROW_UNROLL, drain, 0)


def _dispatch(pad_lo, pad_hi, slots, h_flat, n_tok, cap):
    d = h_flat.shape[1]
    tm = slots.shape[2]
    return pl.pallas_call(
        _dispatch_kernel,
        out_shape=jax.ShapeDtypeStruct((cap, d), F32),
        grid_spec=pltpu.PrefetchScalarGridSpec(
            num_scalar_prefetch=2,
            grid=(n_tok // tm,),
            in_specs=[pl.BlockSpec((1, TOP_K, tm), lambda i, lo, hi: (i, 0, 0), memory_space=pltpu.SMEM),
                      pl.BlockSpec((tm, d), lambda i, lo, hi: (i, 0))],
            out_specs=pl.BlockSpec(memory_space=pl.ANY),
            scratch_shapes=[pltpu.VMEM((SUBLANE, d), F32)] + [pltpu.SemaphoreType.DMA(())] * 3),
        compiler_params=_cparams("arbitrary"),
        name="moe_dispatch",
    )(pad_lo, pad_hi, slots, h_flat)


SPLIT_COLS = 1024


def _split_kernel(w_ref, sel_ref, wg_ref, wl_ref):
    for j in range(w_ref.shape[2] // (2 * LANE)):
        blk = w_ref[0, :, 2 * LANE * j:2 * LANE * (j + 1)].astype(BF16)
        r = jnp.dot(blk, sel_ref[...], preferred_element_type=F32)
        wg_ref[0, :, LANE * j:LANE * (j + 1)] = r[:, :LANE].astype(BF16)
        wl_ref[0, :, LANE * j:LANE * (j + 1)] = r[:, LANE:].astype(BF16)


def _split_glu_columns(w_gu):
    nl, ne, d, f2 = w_gu.shape
    src = np.arange(2 * LANE)
    dst = np.where(src % 2 == 0, src // 2, LANE + src // 2)
    sel = jnp.asarray((dst[:, None] == np.arange(2 * LANE)[None, :]).astype(np.float32), BF16)
    tn = SPLIT_COLS
    out = jax.ShapeDtypeStruct((nl * ne, d, f2 // 2), BF16)
    wg, wl = pl.pallas_call(
        _split_kernel,
        out_shape=(out, out),
        grid=(nl * ne, f2 // tn),
        in_specs=[pl.BlockSpec((1, d, tn), lambda e, j: (e, 0, j)),
                  pl.BlockSpec((2 * LANE, 2 * LANE), lambda e, j: (0, 0))],
        out_specs=(pl.BlockSpec((1, d, tn // 2), lambda e, j: (e, 0, j)),
                   pl.BlockSpec((1, d, tn // 2), lambda e, j: (e, 0, j))),
        compiler_params=_cparams("arbitrary", "arbitrary"),
        name="split_glu_columns",
    )(w_gu.reshape(nl * ne, d, f2), sel)
    return wg, wl


def _expert_kernel(be_ref, nu_ref, xs_ref, wg_ref, wl_ref, bg_ref, bl_ref, wd_ref, bd_ref, ys_ref, wd_bf_ref):
    i = pl.program_id(0)
    last = nu_ref[0] - 1
    expert = be_ref[jnp.minimum(i, last)]
    prev_expert = be_ref[jnp.minimum(jnp.maximum(i - 1, 0), last)]

    @pl.when(jnp.logical_or(i == 0, expert != prev_expert))
    def _():
        wd_bf_ref[...] = wd_ref[0].astype(BF16)

    @pl.when(i < nu_ref[0])
    def _():
        x = xs_ref[...].astype(BF16)
        g = jnp.dot(x, wg_ref[0], preferred_element_type=F32) + bg_ref[0]
        u = jnp.dot(x, wl_ref[0], preferred_element_type=F32) + bl_ref[0]
        g = jnp.minimum(g, SWIGLU_LIMIT)
        u = jnp.clip(u, -SWIGLU_LIMIT, SWIGLU_LIMIT)
        act = g * jax.nn.sigmoid(SWIGLU_ALPHA * g) * (u + 1.0)
        ys_ref[...] = jnp.dot(act.astype(BF16), wd_bf_ref[...], preferred_element_type=F32) + bd_ref[0]

    @pl.when(pl.program_id(0) >= nu_ref[0])
    def _():
        ys_ref[...] = jnp.zeros_like(ys_ref)


def _experts(block_e, n_used, xs, wg, wl, bg, bl, wd, bd):
    cap, d = xs.shape
    f = wg.shape[2]
    tm = MOE_TILE
    nb = cap // tm

    def row_map(i, be, nu):
        return (jnp.minimum(i, nu[0] - 1), 0)

    def w_map(i, be, nu):
        return (be[jnp.minimum(i, nu[0] - 1)], 0, 0)

    return pl.pallas_call(
        _expert_kernel,
        out_shape=jax.ShapeDtypeStruct((cap, d), F32),
        grid_spec=pltpu.PrefetchScalarGridSpec(
            num_scalar_prefetch=2,
            grid=(nb,),
            in_specs=[pl.BlockSpec((tm, d), row_map),
                      pl.BlockSpec((1, d, f), w_map),
                      pl.BlockSpec((1, d, f), w_map),
                      pl.BlockSpec((1, 1, f), w_map),
                      pl.BlockSpec((1, 1, f), w_map),
                      pl.BlockSpec((1, f, d), w_map),
                      pl.BlockSpec((1, 1, d), w_map)],
            out_specs=pl.BlockSpec((tm, d), lambda i, be, nu: (i, 0)),
            scratch_shapes=[pltpu.VMEM((f, d), BF16)]),
        compiler_params=_cparams("arbitrary"),
        name="moe_experts",
    )(block_e, n_used, xs, wg, wl, bg, bl, wd, bd)


def _combine_kernel(slot_ref, ys_hbm, x_ref, gate_ref, m_ref, o_ref, buf, sem):
    tm = x_ref.shape[0]

    def row_copy(t, k):
        return pltpu.make_async_copy(ys_hbm.at[pl.ds(slot_ref[0, k, t], 1)], buf.at[k, pl.ds(t, 1)], sem)

    def issue(i, carry):
        for u in range(ROW_UNROLL):
            for k in range(TOP_K):
                row_copy(i * ROW_UNROLL + u, k).start(priority=k % 2)
        return carry

    lax.fori_loop(0, tm // ROW_UNROLL, issue, 0)

    def drain(i, carry):
        for u in range(ROW_UNROLL):
            for k in range(TOP_K):
                row_copy(i * ROW_UNROLL + u, k).wait()
        return carry

    lax.fori_loop(0, tm // ROW_UNROLL, drain, 0)
    gates = gate_ref[...]
    y = buf[0] * gates[:, 0:1]
    for k in range(1, TOP_K):
        y = y + buf[k] * gates[:, k:k + 1]
    o_ref[...] = x_ref[...] + m_ref[0][5:6] * y


def _combine(slots, ys, x_flat, gates, mod_l, n_tok, n_batch, seq):
    d = x_flat.shape[1]
    tm = slots.shape[2]
    tps = seq // tm
    return pl.pallas_call(
        _combine_kernel,
        out_shape=jax.ShapeDtypeStruct((n_tok, d), F32),
        grid=(n_tok // tm,),
        in_specs=[pl.BlockSpec((1, TOP_K, tm), lambda i: (i, 0, 0), memory_space=pltpu.SMEM),
                  pl.BlockSpec(memory_space=pl.ANY),
                  pl.BlockSpec((tm, d), lambda i: (i, 0)),
                  pl.BlockSpec((tm, LANE), lambda i: (i, 0)),
                  pl.BlockSpec((1, N_MOD, d), lambda i: (_group_of_tile(i, tps, n_batch), 0, 0))],
        out_specs=pl.BlockSpec((tm, d), lambda i: (i, 0)),
        scratch_shapes=[pltpu.VMEM((TOP_K, tm, d), F32), pltpu.SemaphoreType.DMA(())],
        compiler_params=_cparams("arbitrary"),
        name="moe_combine",
    )(slots, ys, x_flat, gates, mod_l)


def _retile_slots(slots, tm):
    nt, k, t = slots.shape
    return slots.reshape(nt, k, t // tm, tm).transpose(0, 2, 1, 3).reshape(nt * (t // tm), k, tm)


def _moe(h_flat, x_flat, n_tok, mod_l, rw_t_bf, rb, layer, wg, wl, bg, bl, wd, bd, n_batch, seq):
    idx, gates, counts = _route(h_flat, n_tok, rw_t_bf, rb)
    counts = counts[:, 0].astype(jnp.int32)
    padded = (counts + MOE_TILE - 1) // MOE_TILE * MOE_TILE
    pad_end = jnp.cumsum(padded)
    pad_start = pad_end - padded
    n_blocks = -(-(n_tok * TOP_K) // MOE_TILE) + N_EXPERTS
    cap = n_blocks * MOE_TILE
    block_row = jnp.arange(n_blocks, dtype=jnp.int32) * MOE_TILE
    block_e = jnp.minimum(jnp.sum(block_row[:, None] >= pad_end[None, :], axis=1), N_EXPERTS - 1).astype(jnp.int32)
    block_e = block_e + layer * N_EXPERTS
    n_used = (pad_end[-1:] // MOE_TILE).astype(jnp.int32)
    experts = idx[:, :TOP_K, :]
    start = jnp.sum(jnp.where(experts[..., None] == jnp.arange(N_EXPERTS), pad_start, 0), axis=-1)
    slots = start + idx[:, TOP_K:, :]
    slots = _retile_slots(slots, MOVE_TILE)
    xs = _dispatch(pad_start + counts, pad_end, slots, h_flat, n_tok, cap)
    ys = _experts(block_e, n_used, xs, wg, wl, bg, bl, wd, bd)
    return _combine(slots, ys, x_flat, gates, mod_l, n_tok, n_batch, seq)


GDN_TILE = 256
HALO = SUBLANE


def _gdn_prep_kernel(q_ref, k_ref, v_ref, qp_ref, kp_ref, vp_ref, qn_ref, kn_ref, vn_ref, small_ref, w_ref,
                     a_ref, bias_ref, qkv_ref, gb_ref, xe_ref, *, lat_tiles, lat_tps, ctx_tps):
    i = pl.program_id(0)
    tm = q_ref.shape[0]
    in_lat = i < lat_tiles
    pos = jnp.where(in_lat, i % lat_tps, (i - lat_tiles) % ctx_tps)
    last = jnp.where(in_lat, lat_tps - 1, ctx_tps - 1)
    keep_prev = jnp.where(pos > 0, 1.0, 0.0)
    keep_next = jnp.where(pos < last, 1.0, 0.0)
    half = GDN_CONV // 2
    for g, (x_ref, p_ref, n_ref) in enumerate(((q_ref, qp_ref, qn_ref), (k_ref, kp_ref, kn_ref),
                                               (v_ref, vp_ref, vn_ref))):
        xe_ref[g, 0:HALO, :] = p_ref[...] * keep_prev
        xe_ref[g, HALO:HALO + tm, :] = x_ref[...]
        xe_ref[g, HALO + tm:2 * HALO + tm, :] = n_ref[...] * keep_next
        w = w_ref[:, g * GDN_W:(g + 1) * GDN_W]
        y = xe_ref[g, HALO - half:HALO - half + tm, :] * w[0:1]
        for j in range(1, GDN_CONV):
            y = y + xe_ref[g, HALO - half + j:HALO - half + j + tm, :] * w[j:j + 1]
        y = y * jax.nn.sigmoid(y)
        if g < 2:
            heads = []
            for h in range(GDN_HEADS):
                yh = y[:, h * GDN_HEAD_DIM:(h + 1) * GDN_HEAD_DIM]
                yh = yh * lax.rsqrt(jnp.sum(yh * yh, axis=-1, keepdims=True) + EPS)
                heads.append(yh * GDN_HEAD_DIM ** -0.5 if g == 0 else yh)
            y = jnp.concatenate(heads, axis=1)
        qkv_ref[:, g * GDN_W:(g + 1) * GDN_W] = y.astype(BF16)
    sm = small_ref[...]
    z = sm + bias_ref[...]
    softplus = jnp.maximum(z, 0.0) + jnp.log(1.0 + jnp.exp(-jnp.abs(z)))
    lane = lax.broadcasted_iota(jnp.int32, (1, LANE), 1)
    is_g = (lane >= SMALL_GA) & (lane < SMALL_GB)
    is_b = (lane >= SMALL_GB) & (lane < SMALL_GB + 2 * GDN_HEADS)
    gb_ref[...] = jnp.where(is_g, a_ref[...] * softplus, jnp.where(is_b, jax.nn.sigmoid(sm), 0.0))


def _gdn_prep(pg, small, conv_w, neg_a, dt_bias, n_batch, seq, ctx_len):
    n_tok = pg.shape[0]
    tm = GDN_TILE
    lat_tiles = n_batch * seq // tm
    per_tile = tm // HALO
    n_halo_blocks = n_tok // HALO

    def main(g):
        return pl.BlockSpec((tm, GDN_W), lambda i: (i, g))

    def prev(g):
        return pl.BlockSpec((HALO, GDN_W), lambda i: (jnp.maximum(i * per_tile - 1, 0), g))

    def nxt(g):
        return pl.BlockSpec((HALO, GDN_W), lambda i: (jnp.minimum((i + 1) * per_tile, n_halo_blocks - 1), g))

    const = lambda i: (0, 0)
    return pl.pallas_call(
        functools.partial(_gdn_prep_kernel, lat_tiles=lat_tiles, lat_tps=seq // tm, ctx_tps=ctx_len // tm),
        out_shape=(jax.ShapeDtypeStruct((n_tok, 3 * GDN_W), BF16), jax.ShapeDtypeStruct((n_tok, LANE), F32)),
        grid=(n_tok // tm,),
        in_specs=[main(0), main(1), main(2), prev(0), prev(1), prev(2), nxt(0), nxt(1), nxt(2),
                  pl.BlockSpec((tm, LANE), lambda i: (i, 0)),
                  pl.BlockSpec(conv_w.shape, const), pl.BlockSpec((1, LANE), const), pl.BlockSpec((1, LANE), const)],
        out_specs=(pl.BlockSpec((tm, 3 * GDN_W), lambda i: (i, 0)), pl.BlockSpec((tm, LANE), lambda i: (i, 0))),
        scratch_shapes=[pltpu.VMEM((3, tm + 2 * HALO, GDN_W), F32)],
        compiler_params=_cparams("arbitrary"),
        name="gdn_prep",
    )(*([pg] * 9), small, conv_w, neg_a, dt_bias)


GDN_SUB = 16
GDN_CHUNKS_PER_STEP = 4


def _bdot(a, b):
    return jnp.dot(a.astype(BF16), b.astype(BF16), preferred_element_type=F32)


def _dot_tn(a, b):
    return lax.dot_general(a, b, (((0,), (0,)), ((), ())), preferred_element_type=F32)


def _each(f, *lists):
    return [f(*args) for args in zip(*lists)]


def _unit_tri_solve(n, rhs, same_blk, eye):
    nd = _each(lambda a: jnp.where(same_blk, a, 0.0).astype(BF16), n)
    nl = _each(lambda a: jnp.where(same_blk, 0.0, a).astype(BF16), n)
    nd2 = _each(lambda a: jnp.dot(a, a, preferred_element_type=F32).astype(BF16), nd)
    nd4 = _each(lambda a: jnp.dot(a, a, preferred_element_type=F32).astype(BF16), nd2)
    nd8 = _each(lambda a: jnp.dot(a, a, preferred_element_type=F32).astype(BF16), nd4)
    m = _each(lambda a: eye - a.astype(F32), nd)
    m = _each(lambda a, b: a + _bdot(a, b), m, nd2)
    m = _each(lambda a, b: a + _bdot(a, b), m, nd4)
    dinv = _each(lambda a, b: (a + _bdot(a, b)).astype(BF16), m, nd8)
    p = _each(lambda a, b: jnp.dot(a, b, preferred_element_type=F32).astype(BF16), dinv, nl)
    p2 = _each(lambda a: jnp.dot(a, a, preferred_element_type=F32).astype(BF16), p)
    y = _each(lambda a, b: _bdot(a, b), dinv, rhs)
    y = _each(lambda a, b: a + _bdot(b, a), y, p2)
    return _each(lambda a, b: a - _bdot(b, a), y, p)


def _gdn_scan_kernel(x0_ref, gb0_ref, x1_ref, gb1_ref, o0_ref, o1_ref, state_ref, *, n_sub):
    assert GDN_CHUNK == 4 * GDN_SUB

    @pl.when(pl.program_id(1) == 0)
    def _():
        state_ref[...] = jnp.zeros_like(state_ref)

    n = GDN_CHUNK
    r = lax.broadcasted_iota(jnp.int32, (n, n), 0)
    c = lax.broadcasted_iota(jnp.int32, (n, n), 1)
    same_blk = (r // GDN_SUB) == (c // GDN_SUB)
    eye = (r == c).astype(F32)
    dirs = ((x0_ref, gb0_ref, o0_ref), (x1_ref, gb1_ref, o1_ref))
    heads = [(d, h) for d in range(2) for h in range(GDN_HEADS)]
    chunk_at = [list(range(n_sub)), list(range(n_sub - 1, -1, -1))]
    probs = [(d, h, chunk_at[d][s]) for s in range(n_sub) for d, h in heads]
    incl = [r >= c, r <= c]
    strict = [r > c, r < c]
    end_row = [n - 1, 0]

    gbv, gc_all, gc_t = {}, {}, {}
    for d in range(2):
        tri = incl[d].astype(BF16)
        for ch in range(n_sub):
            g = dirs[d][1][ch * n:(ch + 1) * n, :]
            hi = g.astype(BF16)
            rem = g - hi.astype(F32)
            mid = rem.astype(BF16)
            lo = (rem - mid.astype(F32)).astype(BF16)
            gc = (jnp.dot(tri, hi, preferred_element_type=F32) + jnp.dot(tri, mid, preferred_element_type=F32)
                  + jnp.dot(tri, lo, preferred_element_type=F32))
            gbv[d, ch], gc_all[d, ch], gc_t[d, ch] = g, gc, jnp.transpose(gc)

    def head_cols(d, part, h, ch):
        lo_col = part * GDN_W + h * GDN_HEAD_DIM
        return dirs[d][0][ch * n:(ch + 1) * n, lo_col:lo_col + GDN_HEAD_DIM]

    lane_g = [SMALL_GA + GDN_HEADS * d + h for d, h, ch in probs]
    lane_b = [SMALL_GB + GDN_HEADS * d + h for d, h, ch in probs]
    gcol = [gc_all[d, ch][:, lg:lg + 1] for (d, h, ch), lg in zip(probs, lane_g)]
    grow = [gc_t[d, ch][lg:lg + 1, :] for (d, h, ch), lg in zip(probs, lane_g)]
    gend = [gc_all[d, ch][end_row[d]:end_row[d] + 1, lg:lg + 1] for (d, h, ch), lg in zip(probs, lane_g)]
    beta = [gbv[d, ch][:, lb:lb + 1] for (d, h, ch), lb in zip(probs, lane_b)]
    q = [head_cols(d, 0, h, ch) for d, h, ch in probs]
    k = [head_cols(d, 1, h, ch) for d, h, ch in probs]
    v = [head_cols(d, 2, h, ch) for d, h, ch in probs]
    kf = _each(lambda a: a.astype(F32), k)
    dec = [jnp.where(incl[d], jnp.exp(jnp.where(incl[d], gc_ - gr_, 0.0)), 0.0)
           for (d, h, ch), gc_, gr_ in zip(probs, gcol, grow)]
    qk_kk = _each(lambda q_, k_: _dot_nt(jnp.concatenate([q_, k_], axis=0), k_), q, k)
    qkd = _each(lambda a, dec_: (a[:n] * dec_).astype(BF16), qk_kk, dec)
    nmat = [jnp.where(strict[d], b_ * a[n:] * dec_, 0.0) for (d, h, ch), b_, a, dec_ in zip(probs, beta, qk_kk, dec)]
    egc = _each(jnp.exp, gcol)
    rhs = _each(lambda v_, kf_, b_, e_: jnp.concatenate([v_.astype(F32) * b_, kf_ * (b_ * e_)], axis=1).astype(BF16),
                v, kf, beta, egc)
    uw = _unit_tri_solve(nmat, rhs, same_blk, eye)
    wq = _each(lambda uw_, q_, e_: jnp.concatenate([uw_[:, GDN_HEAD_DIM:], q_.astype(F32) * e_], axis=0).astype(BF16),
               uw, q, egc)
    k_dec = _each(lambda kf_, ge_, gc_: (kf_ * jnp.exp(ge_ - gc_)).astype(BF16), kf, gend, gcol)
    g_end = _each(jnp.exp, gend)

    state = [state_ref[i] for i in range(len(heads))]
    for s in range(n_sub):
        sl = slice(s * len(heads), (s + 1) * len(heads))
        s_bf = _each(lambda a: a.astype(BF16), state)
        ws_qs = _each(lambda a, s_: jnp.dot(a, s_, preferred_element_type=F32), wq[sl], s_bf)
        v_new = _each(lambda uw_, a: (uw_[:, :GDN_HEAD_DIM] - a[:n]).astype(BF16), uw[sl], ws_qs)
        o_local = _each(lambda a, vn_: jnp.dot(a, vn_, preferred_element_type=F32), qkd[sl], v_new)
        s_add = _each(_dot_tn, k_dec[sl], v_new)
        state = _each(lambda ge_, st_, add_: ge_ * st_ + add_, g_end[sl], state, s_add)
        for (d, h, ch), a, ol in zip(probs[sl], ws_qs, o_local):
            dirs[d][2][ch * n:(ch + 1) * n, h * GDN_HEAD_DIM:(h + 1) * GDN_HEAD_DIM] = a[n:] + ol
    for i in range(len(heads)):
        state_ref[i] = state[i]


def _gdn_scan(qkv, gbeta, n_batch, seq, ctx_len):
    n_tok = qkv.shape[0]
    n = GDN_CHUNK * GDN_CHUNKS_PER_STEP
    lat_c, ctx_c = seq // n, ctx_len // n
    ctx0 = n_batch * seq // n

    def fwd(b, j):
        return (jnp.where(j < ctx_c, ctx0 + b * ctx_c + j, b * lat_c + (j - ctx_c)), 0)

    def bwd(b, j):
        return (jnp.where(j < ctx_c, ctx0 + b * ctx_c + (ctx_c - 1 - j), b * lat_c + (lat_c - 1 - (j - ctx_c))), 0)

    return pl.pallas_call(
        functools.partial(_gdn_scan_kernel, n_sub=GDN_CHUNKS_PER_STEP),
        out_shape=(jax.ShapeDtypeStruct((n_tok, GDN_W), F32), jax.ShapeDtypeStruct((n_tok, GDN_W), F32)),
        grid=(n_batch, ctx_c + lat_c),
        in_specs=[pl.BlockSpec((n, 3 * GDN_W), fwd), pl.BlockSpec((n, LANE), fwd),
                  pl.BlockSpec((n, 3 * GDN_W), bwd), pl.BlockSpec((n, LANE), bwd)],
        out_specs=(pl.BlockSpec((n, GDN_W), fwd), pl.BlockSpec((n, GDN_W), bwd)),
        scratch_shapes=[pltpu.VMEM((2 * GDN_HEADS, GDN_HEAD_DIM, GDN_HEAD_DIM), F32)],
        compiler_params=_cparams("arbitrary", "arbitrary"),
        name="gdn_scan",
    )(qkv, gbeta, qkv, gbeta)


def _pad_heads(w, n_heads, width, start, take):
    lead = w.shape[:-1]
    w = w.reshape(lead + (n_heads, width))[..., start:start + take]
    w = jnp.pad(w, [(0, 0)] * len(lead) + [(0, 0), (0, MLA_HEAD_PAD - take)])
    return w.reshape(lead + (n_heads * MLA_HEAD_PAD,))


def kernel(x, c, ctx, c_ctx, w_mod, b_mod, norm1, norm2, w_in, w_out, swa_q_norm, swa_k_norm, swa_sink, gdn_conv,
           gdn_a_log, gdn_dt_bias, gdn_out_norm, mla_q_a_norm, mla_w_uq, mla_kv_a_norm, mla_w_ukv, mla_q_norm,
           mla_k_norm, router_w, router_b, exp_w_gu, exp_b_gu, exp_w_dn, exp_b_dn):
    b, s, d = x.shape
    cl = ctx.shape[1]
    depth = w_mod.shape[0]
    n_lat, n_ctx = b * s, b * cl
    assert s % TOK_TILE == 0 and n_ctx % TOK_TILE == 0 and s % MOVE_TILE == 0 and n_ctx % ROUTE_TILE == 0
    assert cl % MLA_Q_TILE == 0 and cl % ATTN_BLOCK == 0 and n_lat % cl == 0

    n_rows = -(-(b + 1) // SUBLANE) * SUBLANE
    c_rows = jnp.concatenate([c, c_ctx[None, :], jnp.zeros((n_rows - b - 1, d), F32)], axis=0)
    mod_all = _modulation(c_rows, w_mod, b_mod)

    perm = _in_proj_perm()
    tabs = _rope_tables(s)
    seg_a = _block_diag_ones(SWA_Q, SWA_HEAD_DIM)
    seg_c = _block_diag_ones(MLA_HEADS * MLA_HEAD_PAD, MLA_HEAD_PAD)
    head_rows = np.concatenate([h * SWA_HEAD_DIM + np.arange(SWA_HEAD_DIM) for h in SWA_HEAD_ORDER])

    wg_all, wl_all = _split_glu_columns(exp_w_gu)
    n_le = depth * N_EXPERTS
    bg_all = exp_b_gu[..., 0::2].reshape(n_le, 1, -1)
    bl_all = exp_b_gu[..., 1::2].reshape(n_le, 1, -1)
    wd_all = exp_w_dn.reshape((n_le,) + exp_w_dn.shape[2:])
    bd_all = exp_b_dn.reshape(n_le, 1, d)
    x_lat, x_ctx, ctx_tile0 = x.reshape(n_lat, d), ctx.reshape(n_ctx, d), 0
    for l in range(depth):
        with_ctx = l < depth - 1
        n_tok = n_lat + n_ctx if with_ctx else n_lat
        mod_l = mod_all[l]
        w_in_bf = jnp.pad(w_in[l], ((0, 0), (0, 1)))[:, perm].astype(BF16)
        consts = (seg_a, seg_c,
                  jnp.tile(swa_q_norm[l], SWA_HEADS)[None], jnp.tile(swa_k_norm[l], SWA_KV_HEADS)[None],
                  mla_q_a_norm[l][None], mla_kv_a_norm[l][None],
                  _pad_heads(mla_w_uq[l], MLA_HEADS, MLA_QK, 0, MLA_QK).astype(BF16),
                  _pad_heads(mla_w_ukv[l], MLA_HEADS, MLA_NOPE + MLA_V, 0, MLA_NOPE).astype(BF16),
                  _pad_heads(mla_w_ukv[l], MLA_HEADS, MLA_NOPE + MLA_V, MLA_NOPE, MLA_V).astype(BF16),
                  _pad_heads(jnp.tile(mla_q_norm[l], MLA_HEADS), MLA_HEADS, MLA_QK, 0, MLA_QK)[None],
                  _pad_heads(jnp.tile(mla_k_norm[l], MLA_HEADS), MLA_HEADS, MLA_QK, 0, MLA_QK)[None])
        pg, small, qa, kva, qm, km, vm = _in_projection(x_lat, x_ctx, ctx_tile0, n_lat + n_ctx, norm1[l], mod_l,
                                                        w_in_bf, tabs, consts, b, s)
        o_a = _swa_attention(swa_sink[l], qa, kva, b, s, cl, with_ctx)
        o_c = _mla_attention(qm, km, vm, b, s, cl, with_ctx)
        lanes_g = slice(SMALL_GA, SMALL_GA + 2 * GDN_HEADS)
        neg_a = jnp.zeros((1, LANE), F32).at[0, lanes_g].set(-jnp.exp(gdn_a_log[l]).reshape(-1))
        dt_bias = jnp.zeros((1, LANE), F32).at[0, lanes_g].set(gdn_dt_bias[l].reshape(-1))
        qkv, gbeta = _gdn_prep(pg, small, gdn_conv[l], neg_a, dt_bias, b, s, cl)
        o_fwd, o_bwd = _gdn_scan(qkv, gbeta, b, s, cl)

        wa = w_out[l][:SWA_Q][head_rows].astype(BF16)
        wb = w_out[l][SWA_Q:SWA_Q + GDN_W].astype(BF16)
        wc = w_out[l][SWA_Q + GDN_W:].reshape(MLA_HEADS, MLA_V, d)
        wc = jnp.pad(wc, ((0, 0), (0, MLA_HEAD_PAD - MLA_V), (0, 0))).reshape(MLA_HEADS * MLA_HEAD_PAD, d).astype(BF16)
        x_mid, h2 = _out_projection(o_a, o_fwd, o_bwd, pg, o_c, x_lat, x_ctx, ctx_tile0, wa, wb, wc,
                                    jnp.tile(gdn_out_norm[l], GDN_HEADS)[None], norm2[l], mod_l, n_tok, b, s)

        x_next = _moe(h2, x_mid, n_tok, mod_l, router_w[l].T.astype(BF16), router_b[l], l, wg_all, wl_all,
                      bg_all, bl_all, wd_all, bd_all, b, s)
        x_lat, x_ctx, ctx_tile0 = x_next, x_next, n_lat // TOK_TILE
    return x_lat[:n_lat].reshape(b, s, d)
```

```python
import functools

import jax
import jax.numpy as jnp
import numpy as np
from jax import lax
from jax.experimental import pallas as pl
from jax.experimental.pallas import tpu as pltpu

F32 = jnp.float32
BF16 = jnp.bfloat16

GRID_W = 64
N_MOD = 6
EPS = 1e-6
ROPE_BASE = 10000.0
ATTN_BLOCK = 128

SWA_HEADS = 4
SWA_KV_HEADS = 2
SWA_HEAD_DIM = 64
SWA_WINDOW = 128

GDN_HEADS = 4
GDN_HEAD_DIM = 128
GDN_CONV = 5
GDN_CHUNK = 64

MLA_HEADS = 4
MLA_Q_RANK = 256
MLA_KV_RANK = 128
MLA_NOPE = 64
MLA_ROPE = 32
MLA_V = 64
MLA_QK = MLA_NOPE + MLA_ROPE

N_EXPERTS = 32
TOP_K = 4
SWIGLU_LIMIT = 7.0
SWIGLU_ALPHA = 1.702

SWA_Q = SWA_HEADS * SWA_HEAD_DIM
SWA_KV = SWA_KV_HEADS * SWA_HEAD_DIM
GDN_W = GDN_HEADS * GDN_HEAD_DIM
MLA_O = MLA_HEADS * MLA_V
D_MIX = SWA_Q + GDN_W + MLA_O
IN_SPLITS = (SWA_Q, SWA_KV, SWA_KV, GDN_W, GDN_W, GDN_W, GDN_W, 2 * GDN_HEADS, 2 * GDN_HEADS, MLA_Q_RANK,
             MLA_KV_RANK, MLA_ROPE)
N_IN = sum(IN_SPLITS)

LANE = 128
SUBLANE = 8
VMEM_LIMIT = 56 * 1024 * 1024

TOK_TILE = 512
MOE_TILE = 512
ROUTE_TILE = 512
MOVE_TILE = 512
ROW_UNROLL = 4
ROW_GROUP = 256
MLA_Q_TILE = 256
MOD_COL_TILE = 1536

COL_AQ, COL_AK, COL_AV = 0, 256, 384
COL_G = 512
COL_CQ, COL_CKV, COL_SMALL = 2560, 2816, 2944
N_IN_PAD = 3072
SMALL_KR, SMALL_GA, SMALL_GB = 0, MLA_ROPE, MLA_ROPE + 2 * GDN_HEADS
SWA_HEAD_ORDER = (0, 2, 1, 3)
MLA_HEAD_PAD = 128


def _cparams(*sem):
    return pltpu.CompilerParams(dimension_semantics=sem, vmem_limit_bytes=VMEM_LIMIT)


def _in_proj_perm():
    old = np.cumsum((0,) + IN_SPLITS)
    o_aq, o_ak, o_av, o_gq, o_gk, o_gv, o_gz, o_ga, o_gb, o_cq, o_ckv, o_ckr = old[:-1]
    perm = np.full((N_IN_PAD,), N_IN, np.int32)
    perm[COL_AQ:COL_AQ + SWA_Q] = np.concatenate(
        [o_aq + h * SWA_HEAD_DIM + np.arange(SWA_HEAD_DIM) for h in SWA_HEAD_ORDER])
    perm[COL_AK:COL_G + 4 * GDN_W] = np.arange(o_ak, o_ga)
    perm[COL_CQ:COL_CQ + MLA_Q_RANK] = o_cq + np.arange(MLA_Q_RANK)
    perm[COL_CKV:COL_CKV + MLA_KV_RANK] = o_ckv + np.arange(MLA_KV_RANK)
    perm[COL_SMALL + SMALL_KR:COL_SMALL + SMALL_KR + MLA_ROPE] = o_ckr + np.arange(MLA_ROPE)
    perm[COL_SMALL + SMALL_GA:COL_SMALL + SMALL_GA + 2 * GDN_HEADS] = o_ga + np.arange(2 * GDN_HEADS)
    perm[COL_SMALL + SMALL_GB:COL_SMALL + SMALL_GB + 2 * GDN_HEADS] = o_gb + np.arange(2 * GDN_HEADS)
    return perm


def _mod_kernel(c_ref, w_ref, b_ref, o_ref):
    a = c_ref[...]
    a = a * jax.nn.sigmoid(a)
    o_ref[0] = jnp.dot(a.astype(BF16), w_ref[0].astype(BF16), preferred_element_type=F32) + b_ref[0]


def _modulation(c_rows, w_mod, b_mod):
    nl, d, n = w_mod.shape
    r = c_rows.shape[0]
    tn = MOD_COL_TILE
    out = pl.pallas_call(
        _mod_kernel,
        out_shape=jax.ShapeDtypeStruct((nl, r, n), F32),
        grid=(nl, n // tn),
        in_specs=[pl.BlockSpec((r, d), lambda l, j: (0, 0)),
                  pl.BlockSpec((1, d, tn), lambda l, j: (l, 0, j)),
                  pl.BlockSpec((1, 1, tn), lambda l, j: (l, 0, j))],
        out_specs=pl.BlockSpec((1, r, tn), lambda l, j: (l, 0, j)),
        compiler_params=_cparams("arbitrary", "arbitrary"),
        name="modulation",
    )(c_rows, w_mod, b_mod.reshape(nl, 1, n))
    return out.reshape(nl, r, N_MOD, d)


def _modulated_norm(x, gain, shift, scale):
    y = x * lax.rsqrt(jnp.mean(x * x, axis=-1, keepdims=True) + EPS)
    return (y * gain) * (1.0 + scale) + shift


def _group_of_tile(i, tiles_per_seq, n_batch):
    return jnp.minimum(i // tiles_per_seq, n_batch)


def _group_sumsq(x, seg):
    x2 = x * x
    hi = x2.astype(BF16)
    lo = (x2 - hi.astype(F32)).astype(BF16)
    return jnp.dot(hi, seg, preferred_element_type=F32) + jnp.dot(lo, seg, preferred_element_type=F32)


def _swap_pairs(x, half):
    w = x.shape[1]
    lane = lax.broadcasted_iota(jnp.int32, (1, w), 1)
    first = (lane % (2 * half)) < half
    return jnp.where(first, pltpu.roll(x, w - half, axis=1), pltpu.roll(x, half, axis=1))


def _inproj_kernel(xl_ref, xc_ref, g_ref, m_ref, w_ref, ra_c_ref, ra_s_ref, rc_c_ref, rc_s_ref, seg_a_ref, seg_c_ref,
                   gaq_ref, gak_ref, gcq_ref, gckv_ref, wuq_ref, wuk_ref, wuv_ref, gmq_ref, gmk_ref,
                   pg_ref, small_ref, qa_ref, kva_ref, qm_ref, km_ref, vm_ref, *, n_lat_tiles):
    m = m_ref[0]
    for i in range(xl_ref.shape[0] // ROW_GROUP):
        rows = pl.ds(i * ROW_GROUP, ROW_GROUP)
        x = _tile_rows(xl_ref, xc_ref, rows, n_lat_tiles)
        h = _modulated_norm(x, g_ref[...], m[0:1], m[1:2]).astype(BF16)
        pa = jnp.dot(h, w_ref[:, COL_AQ:COL_G], preferred_element_type=F32)
        pc = jnp.dot(h, w_ref[:, COL_CQ:], preferred_element_type=F32)
        small_ref[rows, :] = pc[:, MLA_Q_RANK + MLA_KV_RANK:]
        _attention_prep(rows, pa, pc, ra_c_ref, ra_s_ref, rc_c_ref, rc_s_ref, seg_a_ref, seg_c_ref, gaq_ref, gak_ref,
                        gcq_ref, gckv_ref, wuq_ref, wuk_ref, wuv_ref, gmq_ref, gmk_ref,
                        qa_ref, kva_ref, qm_ref, km_ref, vm_ref)
        pg_ref[rows, :] = jnp.dot(h, w_ref[:, COL_G:COL_CQ], preferred_element_type=F32)


def _attention_prep(rows, pa, pc, ra_c_ref, ra_s_ref, rc_c_ref, rc_s_ref, seg_a_ref, seg_c_ref, gaq_ref, gak_ref,
                    gcq_ref, gckv_ref, wuq_ref, wuk_ref, wuv_ref, gmq_ref, gmk_ref,
                    qa_ref, kva_ref, qm_ref, km_ref, vm_ref):
    small = pc[:, MLA_Q_RANK + MLA_KV_RANK:]

    aq = pa[:, 0:SWA_Q]
    ak = pa[:, SWA_Q:SWA_Q + SWA_KV]
    av = pa[:, SWA_Q + SWA_KV:SWA_Q + 2 * SWA_KV]
    ca, sa = ra_c_ref[rows, :], ra_s_ref[rows, :]
    seg_a = seg_a_ref[...]
    qn = aq * lax.rsqrt(_group_sumsq(aq, seg_a) * (1.0 / SWA_HEAD_DIM) + EPS) * gaq_ref[...]
    kn = ak * lax.rsqrt(_group_sumsq(ak, seg_a[:SWA_KV, :SWA_KV]) * (1.0 / SWA_HEAD_DIM) + EPS) * gak_ref[...]
    qn = qn * jnp.concatenate([ca, ca], axis=1) + _swap_pairs(qn, SWA_HEAD_DIM // 4) * jnp.concatenate([sa, sa], axis=1)
    kn = kn * ca + _swap_pairs(kn, SWA_HEAD_DIM // 4) * sa
    qa_ref[rows, :] = (qn * SWA_HEAD_DIM ** -0.5).astype(BF16)
    kva_ref[rows, :] = jnp.concatenate([kn, av], axis=1).astype(BF16)

    cq = pc[:, 0:MLA_Q_RANK]
    ckv = pc[:, MLA_Q_RANK:MLA_Q_RANK + MLA_KV_RANK]
    cqn = cq * lax.rsqrt(jnp.mean(cq * cq, axis=-1, keepdims=True) + EPS) * gcq_ref[...]
    ckvn = (ckv * lax.rsqrt(jnp.mean(ckv * ckv, axis=-1, keepdims=True) + EPS) * gckv_ref[...]).astype(BF16)
    q = jnp.dot(cqn.astype(BF16), wuq_ref[...], preferred_element_type=F32)
    k = jnp.dot(ckvn, wuk_ref[...], preferred_element_type=F32)
    v = jnp.dot(ckvn, wuv_ref[...], preferred_element_type=F32)
    lane = lax.broadcasted_iota(jnp.int32, (1, LANE), 1)
    kpe = jnp.where((lane >= MLA_NOPE) & (lane < MLA_QK), pltpu.roll(small, MLA_NOPE, axis=1), 0.0)
    k = k + jnp.concatenate([kpe] * MLA_HEADS, axis=1)
    seg_c = seg_c_ref[...]
    q = q * lax.rsqrt(_group_sumsq(q, seg_c) * (1.0 / MLA_QK) + EPS) * gmq_ref[...]
    k = k * lax.rsqrt(_group_sumsq(k, seg_c) * (1.0 / MLA_QK) + EPS) * gmk_ref[...]
    cc = jnp.concatenate([rc_c_ref[rows, :]] * MLA_HEADS, axis=1)
    sc = jnp.concatenate([rc_s_ref[rows, :]] * MLA_HEADS, axis=1)
    q = q * cc + _swap_pairs(q, MLA_ROPE // 4) * sc
    k = k * cc + _swap_pairs(k, MLA_ROPE // 4) * sc
    qm_ref[rows, :] = (q * MLA_QK ** -0.5).astype(BF16)
    km_ref[rows, :] = k.astype(BF16)
    vm_ref[rows, :] = v.astype(BF16)


def _token_sources(x_lat, x_ctx, ctx_tile0, n_lat_tiles, tm):
    d = x_lat.shape[1]
    return [pl.BlockSpec((tm, d), lambda i: (jnp.minimum(i, n_lat_tiles - 1), 0)),
            pl.BlockSpec((tm, d), lambda i: (ctx_tile0 + jnp.maximum(i - n_lat_tiles, 0), 0))]


def _tile_rows(xl_ref, xc_ref, rows, n_lat_tiles):
    tile = jnp.zeros((ROW_GROUP, 1), jnp.int32) + pl.program_id(0)
    return jnp.where(tile < n_lat_tiles, xl_ref[rows, :], xc_ref[rows, :])


def _in_projection(x_lat, x_ctx, ctx_tile0, n_tok, gain, mod_l, w_bf, tabs, consts, n_batch, seq):
    d = x_lat.shape[1]
    tm = TOK_TILE
    tps = seq // tm
    n_lat_tiles = n_batch * tps
    row = lambda i: (i, 0)
    const = lambda i: (0, 0)
    tab = lambda i: (jnp.where(i < n_lat_tiles, i % tps, tps), 0)
    wide = MLA_HEADS * MLA_HEAD_PAD
    in_specs = _token_sources(x_lat, x_ctx, ctx_tile0, n_lat_tiles, tm)
    in_specs += [pl.BlockSpec((1, d), const),
                pl.BlockSpec((1, N_MOD, d), lambda i: (_group_of_tile(i, tps, n_batch), 0, 0)),
                pl.BlockSpec(w_bf.shape, const)]
    in_specs += [pl.BlockSpec((tm, LANE), tab)] * 4
    in_specs += [pl.BlockSpec(a.shape, const) for a in consts]
    widths = (4 * GDN_W, LANE, SWA_Q, 2 * SWA_KV, wide, wide, wide)
    dtypes = (F32, F32, BF16, BF16, BF16, BF16, BF16)
    return pl.pallas_call(
        functools.partial(_inproj_kernel, n_lat_tiles=n_lat_tiles),
        out_shape=tuple(jax.ShapeDtypeStruct((n_tok, w), t) for w, t in zip(widths, dtypes)),
        grid=(n_tok // tm,),
        in_specs=in_specs,
        out_specs=tuple(pl.BlockSpec((tm, w), row) for w in widths),
        compiler_params=_cparams("arbitrary"),
        name="in_projection",
    )(x_lat, x_ctx, gain.reshape(1, d), mod_l, w_bf, *tabs, *consts)


def _rope_tables(seq):
    t = jnp.arange(seq)
    row = (t // GRID_W).astype(F32)
    col = (t % GRID_W).astype(F32)

    def cos_sin(rot_dim):
        n_freq = rot_dim // 4
        freq = ROPE_BASE ** (-jnp.arange(n_freq, dtype=F32) / n_freq)
        ar, ac = row[:, None] * freq, col[:, None] * freq
        c = jnp.concatenate([jnp.cos(ar), jnp.cos(ar), jnp.cos(ac), jnp.cos(ac)], axis=1)
        s = jnp.concatenate([-jnp.sin(ar), jnp.sin(ar), -jnp.sin(ac), jnp.sin(ac)], axis=1)
        return c, s

    ca, sa = cos_sin(SWA_HEAD_DIM)
    ca, sa = jnp.tile(ca, (1, 2)), jnp.tile(sa, (1, 2))
    cc, sc = cos_sin(MLA_ROPE)
    ones_l = jnp.ones((seq, MLA_NOPE), F32)
    ones_r = jnp.ones((seq, MLA_HEAD_PAD - MLA_QK), F32)
    cc = jnp.concatenate([ones_l, cc, ones_r], axis=1)
    sc = jnp.concatenate([0 * ones_l, sc, 0 * ones_r], axis=1)
    ident_c = jnp.ones((TOK_TILE, LANE), F32)
    ident_s = jnp.zeros((TOK_TILE, LANE), F32)
    return tuple(jnp.concatenate([a, i], axis=0) for a, i in ((ca, ident_c), (sa, ident_s), (cc, ident_c), (sc, ident_s)))


def _block_diag_ones(n, blk):
    i = np.arange(n) // blk
    return jnp.asarray((i[:, None] == i[None, :]).astype(np.float32), BF16)


def _dot_nt(a, b):
    return lax.dot_general(a, b, (((1,), (1,)), ((), ())), preferred_element_type=F32)


def _swa_heads(sink_ref, q, k_list, v_list, mask_list, o_ref):
    lane = lax.broadcasted_iota(jnp.int32, (1, LANE), 1)
    lower = lane < SWA_HEAD_DIM
    cols = []
    for cgrp in range(2):
        qc = q[:, cgrp * LANE:(cgrp + 1) * LANE]
        halves = []
        for half in range(2):
            head = SWA_HEAD_ORDER[2 * cgrp + half]
            qh = jnp.where(lower if half == 0 else ~lower, qc, jnp.zeros_like(qc))
            sink = sink_ref[head]
            scores = []
            m = jnp.full((q.shape[0], 1), sink, F32)
            for kb, mk in zip(k_list, mask_list):
                s = _dot_nt(qh, kb)
                if mk is not None:
                    s = jnp.where(mk, s, -jnp.inf)
                scores.append(s)
                m = jnp.maximum(m, jnp.max(s, axis=-1, keepdims=True))
            l = jnp.exp(sink - m)
            acc = jnp.zeros((q.shape[0], LANE), F32)
            for s, vb in zip(scores, v_list):
                p = jnp.exp(s - m)
                l = l + jnp.sum(p, axis=-1, keepdims=True)
                acc = acc + jnp.dot(p.astype(BF16), vb, preferred_element_type=F32)
            halves.append(acc * (1.0 / l))
        cols.append(jnp.where(lower, halves[0], halves[1]))
    o_ref[...] = jnp.concatenate(cols, axis=1).astype(o_ref.dtype)


SWA_Q_BLOCKS = 2
SWA_KEY_OFFSETS = tuple(range(-1, SWA_Q_BLOCKS + 1))


def _swa_kernel(sink_ref, q_ref, *refs, n_blocks):
    kv_refs, kvc_ref, o_ref = refs[:len(SWA_KEY_OFFSETS)], refs[-2], refs[-1]
    j = pl.program_id(1)
    n_steps = n_blocks // SWA_Q_BLOCKS
    kc, vc = kvc_ref[:, 0:SWA_KV], kvc_ref[:, SWA_KV:]

    @pl.when(j < n_steps)
    def _():
        rows = SWA_Q_BLOCKS * ATTN_BLOCK
        r = lax.broadcasted_iota(jnp.int32, (rows, ATTN_BLOCK), 0)
        c = lax.broadcasted_iota(jnp.int32, (rows, ATTN_BLOCK), 1)
        masks = []
        for off in SWA_KEY_OFFSETS:
            blk = j * SWA_Q_BLOCKS + off
            in_seq = (blk >= 0) & (blk < n_blocks)
            masks.append(jnp.abs(off * ATTN_BLOCK + c - r) <= jnp.where(in_seq, SWA_WINDOW, -1))
        ks = [ref[:, 0:SWA_KV] for ref in kv_refs] + [kc]
        vs = [ref[:, SWA_KV:] for ref in kv_refs] + [vc]
        _swa_heads(sink_ref, q_ref[...], ks, vs, masks + [None], o_ref)

    @pl.when(j >= n_steps)
    def _():
        _swa_heads(sink_ref, q_ref[...], [kc], [vc], [None], o_ref)


def _swa_attention(sink, qa, kva, n_batch, seq, ctx_len, with_ctx):
    rows = SWA_Q_BLOCKS * ATTN_BLOCK
    nb = seq // ATTN_BLOCK
    nq = seq // rows
    ncq = ctx_len // rows
    n_lat = n_batch * seq
    steps = nq + (ncq if with_ctx else 0)
    n_tok = n_lat + (n_batch * ctx_len if with_ctx else 0)

    def q_map(b, j, s):
        return (jnp.where(j < nq, b * nq + j, n_lat // rows + b * ncq + (j - nq)), 0)

    def kv_map(off):
        def f(b, j, s):
            jj = jnp.clip(jnp.minimum(j, nq - 1) * SWA_Q_BLOCKS + off, 0, nb - 1)
            return (b * nb + jj, 0)
        return f

    return pl.pallas_call(
        functools.partial(_swa_kernel, n_blocks=nb),
        out_shape=jax.ShapeDtypeStruct((n_tok, SWA_Q), BF16),
        grid_spec=pltpu.PrefetchScalarGridSpec(
            num_scalar_prefetch=1,
            grid=(n_batch, steps),
            in_specs=[pl.BlockSpec((rows, SWA_Q), q_map)]
            + [pl.BlockSpec((ATTN_BLOCK, 2 * SWA_KV), kv_map(off)) for off in SWA_KEY_OFFSETS]
            + [pl.BlockSpec((ctx_len, 2 * SWA_KV), lambda b, j, s: (n_lat // ctx_len + b, 0))],
            out_specs=pl.BlockSpec((rows, SWA_Q), q_map)),
        compiler_params=_cparams("arbitrary", "arbitrary"),
        name="swa_attention",
    )(sink, qa, *([kva] * len(SWA_KEY_OFFSETS)), kva)


def _mla_heads(q_ref, k_refs, v_refs, o_ref):
    for h in range(MLA_HEADS):
        sl = slice(h * MLA_HEAD_PAD, (h + 1) * MLA_HEAD_PAD)
        qh = q_ref[:, sl]
        scores = [_dot_nt(qh, k_ref[:, sl]) for k_ref in k_refs]
        m = jnp.max(scores[0], axis=-1, keepdims=True)
        for s in scores[1:]:
            m = jnp.maximum(m, jnp.max(s, axis=-1, keepdims=True))
        l = jnp.zeros_like(m)
        acc = jnp.zeros((qh.shape[0], MLA_HEAD_PAD), F32)
        for s, v_ref in zip(scores, v_refs):
            p = jnp.exp(s - m)
            l = l + jnp.sum(p, axis=-1, keepdims=True)
            acc = acc + jnp.dot(p.astype(BF16), v_ref[:, sl], preferred_element_type=F32)
        o_ref[:, sl] = (acc * (1.0 / l)).astype(o_ref.dtype)


def _mla_kernel(q_ref, kl_ref, vl_ref, kc_ref, vc_ref, o_ref, *, n_lat_steps):
    j = pl.program_id(1)

    @pl.when(j < n_lat_steps)
    def _():
        _mla_heads(q_ref, [kl_ref, kc_ref], [vl_ref, vc_ref], o_ref)

    @pl.when(j >= n_lat_steps)
    def _():
        _mla_heads(q_ref, [kc_ref], [vc_ref], o_ref)


def _mla_attention(qm, km, vm, n_batch, seq, ctx_len, with_ctx):
    tq = MLA_Q_TILE
    nq = seq // tq
    ncq = ctx_len // tq
    n_lat = n_batch * seq
    steps = nq + (ncq if with_ctx else 0)
    n_tok = n_lat + (n_batch * ctx_len if with_ctx else 0)
    wide = MLA_HEADS * MLA_HEAD_PAD

    def q_map(b, j):
        return (jnp.where(j < nq, b * nq + j, n_lat // tq + b * ncq + (j - nq)), 0)

    lat_map = lambda b, j: (b, 0)
    ctx_map = lambda b, j: (n_lat // ctx_len + b, 0)
    return pl.pallas_call(
        functools.partial(_mla_kernel, n_lat_steps=nq),
        out_shape=jax.ShapeDtypeStruct((n_tok, wide), BF16),
        grid=(n_batch, steps),
        in_specs=[pl.BlockSpec((tq, wide), q_map),
                  pl.BlockSpec((seq, wide), lat_map),
                  pl.BlockSpec((seq, wide), lat_map),
                  pl.BlockSpec((ctx_len, wide), ctx_map),
                  pl.BlockSpec((ctx_len, wide), ctx_map)],
        out_specs=pl.BlockSpec((tq, wide), q_map),
        compiler_params=_cparams("arbitrary", "arbitrary"),
        name="mla_attention",
    )(qm, km, vm, km, vm)


def _outproj_kernel(oa_ref, of_ref, ob_ref, z_ref, oc_ref, xl_ref, xc_ref, wa_ref, wb_ref, wc_ref, gg_ref, g_ref, m_ref,
                    xo_ref, h_ref, *, n_lat_tiles):
    m = m_ref[0]
    groups = [pl.ds(i * ROW_GROUP, ROW_GROUP) for i in range(xl_ref.shape[0] // ROW_GROUP)]
    mixed = []
    for rows in groups:
        ob = of_ref[rows, :] + ob_ref[rows, :]
        z = z_ref[rows, :]
        heads = []
        for h in range(GDN_HEADS):
            oh = ob[:, h * GDN_HEAD_DIM:(h + 1) * GDN_HEAD_DIM]
            heads.append(oh * lax.rsqrt(jnp.mean(oh * oh, axis=-1, keepdims=True) + EPS))
        gated = jnp.concatenate(heads, axis=1) * gg_ref[...] * (z * jax.nn.sigmoid(z))
        mixed.append(jnp.dot(oa_ref[rows, :], wa_ref[...], preferred_element_type=F32)
                     + jnp.dot(gated.astype(BF16), wb_ref[...], preferred_element_type=F32)
                     + jnp.dot(oc_ref[rows, :], wc_ref[...], preferred_element_type=F32))
    for rows, o in zip(groups, mixed):
        xn = _tile_rows(xl_ref, xc_ref, rows, n_lat_tiles) + m[2:3] * o
        xo_ref[rows, :] = xn
        h_ref[rows, :] = _modulated_norm(xn, g_ref[...], m[3:4], m[4:5])


def _out_projection(o_a, o_fwd, o_bwd, pg, o_c, x_lat, x_ctx, ctx_tile0, wa, wb, wc, gdn_gain, gain2, mod_l, n_tok,
                    n_batch, seq):
    d = x_lat.shape[1]
    tm = TOK_TILE
    tps = seq // tm
    n_lat_tiles = n_batch * tps
    row = lambda i: (i, 0)
    const = lambda i: (0, 0)
    return pl.pallas_call(
        functools.partial(_outproj_kernel, n_lat_tiles=n_lat_tiles),
        out_shape=(jax.ShapeDtypeStruct((n_tok, d), F32), jax.ShapeDtypeStruct((n_tok, d), F32)),
        grid=(n_tok // tm,),
        in_specs=[pl.BlockSpec((tm, o_a.shape[1]), row),
                  pl.BlockSpec((tm, GDN_W), row),
                  pl.BlockSpec((tm, GDN_W), row),
                  pl.BlockSpec((tm, GDN_W), lambda i: (i, 3)),
                  pl.BlockSpec((tm, o_c.shape[1]), row)]
        + _token_sources(x_lat, x_ctx, ctx_tile0, n_lat_tiles, tm)
        + [pl.BlockSpec(wa.shape, const),
                  pl.BlockSpec(wb.shape, const),
                  pl.BlockSpec(wc.shape, const),
                  pl.BlockSpec((1, GDN_W), const),
                  pl.BlockSpec((1, d), const),
                  pl.BlockSpec((1, N_MOD, d), lambda i: (_group_of_tile(i, tps, n_batch), 0, 0))],
        out_specs=(pl.BlockSpec((tm, d), row), pl.BlockSpec((tm, d), row)),
        compiler_params=_cparams("arbitrary"),
        name="out_projection",
    )(o_a, o_fwd, o_bwd, pg, o_c, x_lat, x_ctx, wa, wb, wc, gdn_gain, gain2.reshape(1, d), mod_l)


def _route_kernel(h_ref, rw_ref, rb_ref, idx_ref, gate_ref, cnt_ref, base_ref):
    step = pl.program_id(0)
    tm = h_ref.shape[0]

    @pl.when(step == 0)
    def _():
        base_ref[...] = jnp.zeros_like(base_ref)

    logits = lax.dot_general(rw_ref[...], h_ref[...].astype(BF16), (((1,), (1,)), ((), ())),
                             preferred_element_type=F32) + rb_ref[...]
    e_iota = lax.broadcasted_iota(jnp.int32, logits.shape, 0)
    work = logits
    tops, picks = [], []
    for _k in range(TOP_K):
        mx = jnp.max(work, axis=0, keepdims=True)
        pick = jnp.min(jnp.where(work == mx, e_iota, N_EXPERTS), axis=0, keepdims=True)
        work = jnp.where(e_iota == pick, -jnp.inf, work)
        tops.append(mx)
        picks.append(pick)
    exps = [jnp.exp(t - tops[0]) for t in tops]
    denom = exps[0] + exps[1] + exps[2] + exps[3]
    sel = jnp.zeros(logits.shape, F32)
    for pick in picks:
        sel = sel + (e_iota == pick).astype(F32)
    row = lax.broadcasted_iota(jnp.int32, (tm, tm), 0)
    col = lax.broadcasted_iota(jnp.int32, (tm, tm), 1)
    before = (row < col).astype(BF16)
    cnt = jnp.dot(sel.astype(BF16), before, preferred_element_type=F32) + base_ref[:, 0:1]
    ranks = [jnp.sum(jnp.where(e_iota == pick, cnt, 0.0), axis=0, keepdims=True) for pick in picks]
    idx_ref[0] = jnp.concatenate(picks + [r.astype(jnp.int32) for r in ranks], axis=0)
    gate_rows = jnp.concatenate([e / denom for e in exps] + [jnp.zeros((LANE - TOP_K, tm), F32)], axis=0)
    gate_ref[...] = jnp.transpose(gate_rows)
    base_ref[...] = base_ref[...] + jnp.sum(sel, axis=1, keepdims=True)
    cnt_ref[...] = base_ref[...]


def _route(h_flat, n_tok, rw_t_bf, rb):
    d = h_flat.shape[1]
    tm = ROUTE_TILE
    nt = n_tok // tm
    return pl.pallas_call(
        _route_kernel,
        out_shape=(jax.ShapeDtypeStruct((nt, 2 * TOP_K, tm), jnp.int32),
                   jax.ShapeDtypeStruct((n_tok, LANE), F32),
                   jax.ShapeDtypeStruct((N_EXPERTS, LANE), F32)),
        grid=(nt,),
        in_specs=[pl.BlockSpec((tm, d), lambda i: (i, 0)),
                  pl.BlockSpec((N_EXPERTS, d), lambda i: (0, 0)),
                  pl.BlockSpec((N_EXPERTS, 1), lambda i: (0, 0))],
        out_specs=(pl.BlockSpec((1, 2 * TOP_K, tm), lambda i: (i, 0, 0)),
                   pl.BlockSpec((tm, LANE), lambda i: (i, 0)),
                   pl.BlockSpec((N_EXPERTS, LANE), lambda i: (0, 0))),
        scratch_shapes=[pltpu.VMEM((N_EXPERTS, LANE), F32)],
        compiler_params=_cparams("arbitrary"),
        name="moe_route",
    )(h_flat, rw_t_bf, rb.reshape(N_EXPERTS, 1))


def _dispatch_kernel(lo_ref, hi_ref, slot_ref, h_ref, xs_out, zero_ref, sem, zero_sem, tile_sem):
    tm = slot_ref.shape[2]

    @pl.when(pl.program_id(0) == pl.num_programs(0) - 1)
    def _():
        zero_ref[...] = jnp.zeros_like(zero_ref)

        def row_copy0(r):
            return pltpu.make_async_copy(zero_ref.at[pl.ds(0, 1)], xs_out.at[pl.ds(r, 1)], zero_sem)

        def tile_copy0(g):
            dst = xs_out.at[pl.ds(pl.multiple_of(g * SUBLANE, SUBLANE), SUBLANE)]
            return pltpu.make_async_copy(zero_ref, dst, tile_sem)

        def bounds(e):
            lo, hi = lo_ref[e], hi_ref[e]
            mid = jnp.minimum((lo + SUBLANE - 1) // SUBLANE * SUBLANE, hi)
            return lo, mid, hi

        def start_expert(e, carry):
            lo, mid, hi = bounds(e)
            lax.fori_loop(lo, mid, lambda r, c: (row_copy0(r).start(), c)[1], carry)
            return lax.fori_loop(mid // SUBLANE, hi // SUBLANE, lambda g, c: (tile_copy0(g).start(), c)[1], carry)

        def wait_expert(e, carry):
            lo, mid, hi = bounds(e)
            lax.fori_loop(lo, mid, lambda r, c: (row_copy0(r).wait(), c)[1], carry)
            return lax.fori_loop(mid // SUBLANE, hi // SUBLANE, lambda g, c: (tile_copy0(g).wait(), c)[1], carry)

        lax.fori_loop(0, lo_ref.shape[0], start_expert, 0)
        lax.fori_loop(0, lo_ref.shape[0], wait_expert, 0)

    def row_copy(t, k):
        return pltpu.make_async_copy(h_ref.at[pl.ds(t, 1)], xs_out.at[pl.ds(slot_ref[0, k, t], 1)], sem)

    def issue(i, carry):
        for u in range(ROW_UNROLL):
            for k in range(TOP_K):
                row_copy(i * ROW_UNROLL + u, k).start(priority=k % 2)
        return carry

    lax.fori_loop(0, tm // ROW_UNROLL, issue, 0)

    def drain(i, carry):
        for u in range(ROW_UNROLL):
            for k in range(TOP_K):
                row_copy(i * ROW_UNROLL + u, k).wait()
        return carry

    lax.fori_loop(0, tm // ROW_UNROLL, drain, 0)


def _dispatch(pad_lo, pad_hi, slots, h_flat, n_tok, cap):
    d = h_flat.shape[1]
    tm = slots.shape[2]
    return pl.pallas_call(
        _dispatch_kernel,
        out_shape=jax.ShapeDtypeStruct((cap, d), F32),
        grid_spec=pltpu.PrefetchScalarGridSpec(
            num_scalar_prefetch=2,
            grid=(n_tok // tm,),
            in_specs=[pl.BlockSpec((1, TOP_K, tm), lambda i, lo, hi: (i, 0, 0), memory_space=pltpu.SMEM),
                      pl.BlockSpec((tm, d), lambda i, lo, hi: (i, 0))],
            out_specs=pl.BlockSpec(memory_space=pl.ANY),
            scratch_shapes=[pltpu.VMEM((SUBLANE, d), F32)] + [pltpu.SemaphoreType.DMA(())] * 3),
        compiler_params=_cparams("arbitrary"),
        name="moe_dispatch",
    )(pad_lo, pad_hi, slots, h_flat)


SPLIT_COLS = 1024


def _split_kernel(w_ref, sel_ref, wg_ref, wl_ref):
    for j in range(w_ref.shape[2] // (2 * LANE)):
        blk = w_ref[0, :, 2 * LANE * j:2 * LANE * (j + 1)].astype(BF16)
        r = jnp.dot(blk, sel_ref[...], preferred_element_type=F32)
        wg_ref[0, :, LANE * j:LANE * (j + 1)] = r[:, :LANE].astype(BF16)
        wl_ref[0, :, LANE * j:LANE * (j + 1)] = r[:, LANE:].astype(BF16)


def _split_glu_columns(w_gu):
    nl, ne, d, f2 = w_gu.shape
    src = np.arange(2 * LANE)
    dst = np.where(src % 2 == 0, src // 2, LANE + src // 2)
    sel = jnp.asarray((dst[:, None] == np.arange(2 * LANE)[None, :]).astype(np.float32), BF16)
    tn = SPLIT_COLS
    out = jax.ShapeDtypeStruct((nl * ne, d, f2 // 2), BF16)
    wg, wl = pl.pallas_call(
        _split_kernel,
        out_shape=(out, out),
        grid=(nl * ne, f2 // tn),
        in_specs=[pl.BlockSpec((1, d, tn), lambda e, j: (e, 0, j)),
                  pl.BlockSpec((2 * LANE, 2 * LANE), lambda e, j: (0, 0))],
        out_specs=(pl.BlockSpec((1, d, tn // 2), lambda e, j: (e, 0, j)),
                   pl.BlockSpec((1, d, tn // 2), lambda e, j: (e, 0, j))),
        compiler_params=_cparams("arbitrary", "arbitrary"),
        name="split_glu_columns",
    )(w_gu.reshape(nl * ne, d, f2), sel)
    return wg, wl


def _expert_kernel(be_ref, nu_ref, xs_ref, wg_ref, wl_ref, bg_ref, bl_ref, wd_ref, bd_ref, ys_ref, wd_bf_ref):
    i = pl.program_id(0)
    last = nu_ref[0] - 1
    expert = be_ref[jnp.minimum(i, last)]
    prev_expert = be_ref[jnp.minimum(jnp.maximum(i - 1, 0), last)]

    @pl.when(jnp.logical_or(i == 0, expert != prev_expert))
    def _():
        wd_bf_ref[...] = wd_ref[0].astype(BF16)

    @pl.when(i < nu_ref[0])
    def _():
        x = xs_ref[...].astype(BF16)
        g = jnp.dot(x, wg_ref[0], preferred_element_type=F32) + bg_ref[0]
        u = jnp.dot(x, wl_ref[0], preferred_element_type=F32) + bl_ref[0]
        g = jnp.minimum(g, SWIGLU_LIMIT)
        u = jnp.clip(u, -SWIGLU_LIMIT, SWIGLU_LIMIT)
        act = g * jax.nn.sigmoid(SWIGLU_ALPHA * g) * (u + 1.0)
        ys_ref[...] = jnp.dot(act.astype(BF16), wd_bf_ref[...], preferred_element_type=F32) + bd_ref[0]

    @pl.when(pl.program_id(0) >= nu_ref[0])
    def _():
        ys_ref[...] = jnp.zeros_like(ys_ref)


def _experts(block_e, n_used, xs, wg, wl, bg, bl, wd, bd):
    cap, d = xs.shape
    f = wg.shape[2]
    tm = MOE_TILE
    nb = cap // tm

    def row_map(i, be, nu):
        return (jnp.minimum(i, nu[0] - 1), 0)

    def w_map(i, be, nu):
        return (be[jnp.minimum(i, nu[0] - 1)], 0, 0)

    return pl.pallas_call(
        _expert_kernel,
        out_shape=jax.ShapeDtypeStruct((cap, d), F32),
        grid_spec=pltpu.PrefetchScalarGridSpec(
            num_scalar_prefetch=2,
            grid=(nb,),
            in_specs=[pl.BlockSpec((tm, d), row_map),
                      pl.BlockSpec((1, d, f), w_map),
                      pl.BlockSpec((1, d, f), w_map),
                      pl.BlockSpec((1, 1, f), w_map),
                      pl.BlockSpec((1, 1, f), w_map),
                      pl.BlockSpec((1, f, d), w_map),
                      pl.BlockSpec((1, 1, d), w_map)],
            out_specs=pl.BlockSpec((tm, d), lambda i, be, nu: (i, 0)),
            scratch_shapes=[pltpu.VMEM((f, d), BF16)]),
        compiler_params=_cparams("arbitrary"),
        name="moe_experts",
    )(block_e, n_used, xs, wg, wl, bg, bl, wd, bd)


def _combine_kernel(slot_ref, ys_hbm, x_ref, gate_ref, m_ref, o_ref, buf, sem):
    tm = x_ref.shape[0]

    def row_copy(t, k):
        return pltpu.make_async_copy(ys_hbm.at[pl.ds(slot_ref[0, k, t], 1)], buf.at[k, pl.ds(t, 1)], sem)

    def issue(i, carry):
        for u in range(ROW_UNROLL):
            for k in range(TOP_K):
                row_copy(i * ROW_UNROLL + u, k).start(priority=k % 2)
        return carry

    lax.fori_loop(0, tm // ROW_UNROLL, issue, 0)

    def drain(i, carry):
        for u in range(ROW_UNROLL):
            for k in range(TOP_K):
                row_copy(i * ROW_UNROLL + u, k).wait()
        return carry

    lax.fori_loop(0, tm // ROW_UNROLL, drain, 0)
    gates = gate_ref[...]
    y = buf[0] * gates[:, 0:1]
    for k in range(1, TOP_K):
        y = y + buf[k] * gates[:, k:k + 1]
    o_ref[...] = x_ref[...] + m_ref[0][5:6] * y


def _combine(slots, ys, x_flat, gates, mod_l, n_tok, n_batch, seq):
    d = x_flat.shape[1]
    tm = slots.shape[2]
    tps = seq // tm
    return pl.pallas_call(
        _combine_kernel,
        out_shape=jax.ShapeDtypeStruct((n_tok, d), F32),
        grid=(n_tok // tm,),
        in_specs=[pl.BlockSpec((1, TOP_K, tm), lambda i: (i, 0, 0), memory_space=pltpu.SMEM),
                  pl.BlockSpec(memory_space=pl.ANY),
                  pl.BlockSpec((tm, d), lambda i: (i, 0)),
                  pl.BlockSpec((tm, LANE), lambda i: (i, 0)),
                  pl.BlockSpec((1, N_MOD, d), lambda i: (_group_of_tile(i, tps, n_batch), 0, 0))],
        out_specs=pl.BlockSpec((tm, d), lambda i: (i, 0)),
        scratch_shapes=[pltpu.VMEM((TOP_K, tm, d), F32), pltpu.SemaphoreType.DMA(())],
        compiler_params=_cparams("arbitrary"),
        name="moe_combine",
    )(slots, ys, x_flat, gates, mod_l)


def _retile_slots(slots, tm):
    nt, k, t = slots.shape
    return slots.reshape(nt, k, t // tm, tm).transpose(0, 2, 1, 3).reshape(nt * (t // tm), k, tm)


def _moe(h_flat, x_flat, n_tok, mod_l, rw_t_bf, rb, layer, wg, wl, bg, bl, wd, bd, n_batch, seq):
    idx, gates, counts = _route(h_flat, n_tok, rw_t_bf, rb)
    counts = counts[:, 0].astype(jnp.int32)
    padded = (counts + MOE_TILE - 1) // MOE_TILE * MOE_TILE
    pad_end = jnp.cumsum(padded)
    pad_start = pad_end - padded
    n_blocks = -(-(n_tok * TOP_K) // MOE_TILE) + N_EXPERTS
    cap = n_blocks * MOE_TILE
    block_row = jnp.arange(n_blocks, dtype=jnp.int32) * MOE_TILE
    block_e = jnp.minimum(jnp.sum(block_row[:, None] >= pad_end[None, :], axis=1), N_EXPERTS - 1).astype(jnp.int32)
    block_e = block_e + layer * N_EXPERTS
    n_used = (pad_end[-1:] // MOE_TILE).astype(jnp.int32)
    experts = idx[:, :TOP_K, :]
    start = jnp.sum(jnp.where(experts[..., None] == jnp.arange(N_EXPERTS), pad_start, 0), axis=-1)
    slots = start + idx[:, TOP_K:, :]
    slots = _retile_slots(slots, MOVE_TILE)
    zero_lo = jnp.concatenate([pad_start + counts, pad_end[-1:]]).astype(jnp.int32)
    zero_hi = jnp.concatenate([pad_end, jnp.full((1,), cap)]).astype(jnp.int32)
    xs = _dispatch(zero_lo, zero_hi, slots, h_flat, n_tok, cap)
    ys = _experts(block_e, n_used, xs, wg, wl, bg, bl, wd, bd)
    return _combine(slots, ys, x_flat, gates, mod_l, n_tok, n_batch, seq)


GDN_TILE = 256
HALO = SUBLANE


def _gdn_prep_kernel(q_ref, k_ref, v_ref, qp_ref, kp_ref, vp_ref, qn_ref, kn_ref, vn_ref, small_ref, w_ref,
                     a_ref, bias_ref, qkv_ref, gb_ref, xe_ref, *, lat_tiles, lat_tps, ctx_tps):
    i = pl.program_id(0)
    tm = q_ref.shape[0]
    in_lat = i < lat_tiles
    pos = jnp.where(in_lat, i % lat_tps, (i - lat_tiles) % ctx_tps)
    last = jnp.where(in_lat, lat_tps - 1, ctx_tps - 1)
    keep_prev = jnp.where(pos > 0, 1.0, 0.0)
    keep_next = jnp.where(pos < last, 1.0, 0.0)
    half = GDN_CONV // 2
    for g, (x_ref, p_ref, n_ref) in enumerate(((q_ref, qp_ref, qn_ref), (k_ref, kp_ref, kn_ref),
                                               (v_ref, vp_ref, vn_ref))):
        xe_ref[g, 0:HALO, :] = p_ref[...] * keep_prev
        xe_ref[g, HALO:HALO + tm, :] = x_ref[...]
        xe_ref[g, HALO + tm:2 * HALO + tm, :] = n_ref[...] * keep_next
        w = w_ref[:, g * GDN_W:(g + 1) * GDN_W]
        y = xe_ref[g, HALO - half:HALO - half + tm, :] * w[0:1]
        for j in range(1, GDN_CONV):
            y = y + xe_ref[g, HALO - half + j:HALO - half + j + tm, :] * w[j:j + 1]
        y = y * jax.nn.sigmoid(y)
        if g < 2:
            heads = []
            for h in range(GDN_HEADS):
                yh = y[:, h * GDN_HEAD_DIM:(h + 1) * GDN_HEAD_DIM]
                yh = yh * lax.rsqrt(jnp.sum(yh * yh, axis=-1, keepdims=True) + EPS)
                heads.append(yh * GDN_HEAD_DIM ** -0.5 if g == 0 else yh)
            y = jnp.concatenate(heads, axis=1)
        qkv_ref[:, g * GDN_W:(g + 1) * GDN_W] = y.astype(BF16)
    sm = small_ref[...]
    z = sm + bias_ref[...]
    softplus = jnp.maximum(z, 0.0) + jnp.log(1.0 + jnp.exp(-jnp.abs(z)))
    lane = lax.broadcasted_iota(jnp.int32, (1, LANE), 1)
    is_g = (lane >= SMALL_GA) & (lane < SMALL_GB)
    is_b = (lane >= SMALL_GB) & (lane < SMALL_GB + 2 * GDN_HEADS)
    gb_ref[...] = jnp.where(is_g, a_ref[...] * softplus, jnp.where(is_b, jax.nn.sigmoid(sm), 0.0))


def _gdn_prep(pg, small, conv_w, neg_a, dt_bias, n_batch, seq, ctx_len):
    n_tok = pg.shape[0]
    tm = GDN_TILE
    lat_tiles = n_batch * seq // tm
    per_tile = tm // HALO
    n_halo_blocks = n_tok // HALO

    def main(g):
        return pl.BlockSpec((tm, GDN_W), lambda i: (i, g))

    def prev(g):
        return pl.BlockSpec((HALO, GDN_W), lambda i: (jnp.maximum(i * per_tile - 1, 0), g))

    def nxt(g):
        return pl.BlockSpec((HALO, GDN_W), lambda i: (jnp.minimum((i + 1) * per_tile, n_halo_blocks - 1), g))

    const = lambda i: (0, 0)
    return pl.pallas_call(
        functools.partial(_gdn_prep_kernel, lat_tiles=lat_tiles, lat_tps=seq // tm, ctx_tps=ctx_len // tm),
        out_shape=(jax.ShapeDtypeStruct((n_tok, 3 * GDN_W), BF16), jax.ShapeDtypeStruct((n_tok, LANE), F32)),
        grid=(n_tok // tm,),
        in_specs=[main(0), main(1), main(2), prev(0), prev(1), prev(2), nxt(0), nxt(1), nxt(2),
                  pl.BlockSpec((tm, LANE), lambda i: (i, 0)),
                  pl.BlockSpec(conv_w.shape, const), pl.BlockSpec((1, LANE), const), pl.BlockSpec((1, LANE), const)],
        out_specs=(pl.BlockSpec((tm, 3 * GDN_W), lambda i: (i, 0)), pl.BlockSpec((tm, LANE), lambda i: (i, 0))),
        scratch_shapes=[pltpu.VMEM((3, tm + 2 * HALO, GDN_W), F32)],
        compiler_params=_cparams("arbitrary"),
        name="gdn_prep",
    )(*([pg] * 9), small, conv_w, neg_a, dt_bias)


GDN_SUB = 16
GDN_CHUNKS_PER_STEP = 4


def _bdot(a, b):
    return jnp.dot(a.astype(BF16), b.astype(BF16), preferred_element_type=F32)


def _dot_tn(a, b):
    return lax.dot_general(a, b, (((0,), (0,)), ((), ())), preferred_element_type=F32)


def _each(f, *lists):
    return [f(*args) for args in zip(*lists)]


def _unit_tri_solve(n, rhs, same_blk, eye):
    nd = _each(lambda a: jnp.where(same_blk, a, 0.0).astype(BF16), n)
    nl = _each(lambda a: jnp.where(same_blk, 0.0, a).astype(BF16), n)
    nd2 = _each(lambda a: jnp.dot(a, a, preferred_element_type=F32).astype(BF16), nd)
    nd4 = _each(lambda a: jnp.dot(a, a, preferred_element_type=F32).astype(BF16), nd2)
    nd8 = _each(lambda a: jnp.dot(a, a, preferred_element_type=F32).astype(BF16), nd4)
    m = _each(lambda a: eye - a.astype(F32), nd)
    m = _each(lambda a, b: a + _bdot(a, b), m, nd2)
    m = _each(lambda a, b: a + _bdot(a, b), m, nd4)
    dinv = _each(lambda a, b: (a + _bdot(a, b)).astype(BF16), m, nd8)
    p = _each(lambda a, b: jnp.dot(a, b, preferred_element_type=F32).astype(BF16), dinv, nl)
    p2 = _each(lambda a: jnp.dot(a, a, preferred_element_type=F32).astype(BF16), p)
    y = _each(lambda a, b: _bdot(a, b), dinv, rhs)
    y = _each(lambda a, b: a + _bdot(b, a), y, p2)
    return _each(lambda a, b: a - _bdot(b, a), y, p)


def _gdn_scan_kernel(x0_ref, gb0_ref, x1_ref, gb1_ref, o0_ref, o1_ref, state_ref, *, n_sub):
    assert GDN_CHUNK == 4 * GDN_SUB

    @pl.when(pl.program_id(1) == 0)
    def _():
        state_ref[...] = jnp.zeros_like(state_ref)

    n = GDN_CHUNK
    r = lax.broadcasted_iota(jnp.int32, (n, n), 0)
    c = lax.broadcasted_iota(jnp.int32, (n, n), 1)
    same_blk = (r // GDN_SUB) == (c // GDN_SUB)
    eye = (r == c).astype(F32)
    dirs = ((x0_ref, gb0_ref, o0_ref), (x1_ref, gb1_ref, o1_ref))
    heads = [(d, h) for d in range(2) for h in range(GDN_HEADS)]
    chunk_at = [list(range(n_sub)), list(range(n_sub - 1, -1, -1))]
    probs = [(d, h, chunk_at[d][s]) for s in range(n_sub) for d, h in heads]
    incl = [r >= c, r <= c]
    strict = [r > c, r < c]
    end_row = [n - 1, 0]

    gbv, gc_all, gc_t = {}, {}, {}
    for d in range(2):
        tri = incl[d].astype(BF16)
        for ch in range(n_sub):
            g = dirs[d][1][ch * n:(ch + 1) * n, :]
            hi = g.astype(BF16)
            rem = g - hi.astype(F32)
            mid = rem.astype(BF16)
            lo = (rem - mid.astype(F32)).astype(BF16)
            gc = (jnp.dot(tri, hi, preferred_element_type=F32) + jnp.dot(tri, mid, preferred_element_type=F32)
                  + jnp.dot(tri, lo, preferred_element_type=F32))
            gbv[d, ch], gc_all[d, ch], gc_t[d, ch] = g, gc, jnp.transpose(gc)

    def head_cols(d, part, h, ch):
        lo_col = part * GDN_W + h * GDN_HEAD_DIM
        return dirs[d][0][ch * n:(ch + 1) * n, lo_col:lo_col + GDN_HEAD_DIM]

    lane_g = [SMALL_GA + GDN_HEADS * d + h for d, h, ch in probs]
    lane_b = [SMALL_GB + GDN_HEADS * d + h for d, h, ch in probs]
    gcol = [gc_all[d, ch][:, lg:lg + 1] for (d, h, ch), lg in zip(probs, lane_g)]
    grow = [gc_t[d, ch][lg:lg + 1, :] for (d, h, ch), lg in zip(probs, lane_g)]
    gend = [gc_all[d, ch][end_row[d]:end_row[d] + 1, lg:lg + 1] for (d, h, ch), lg in zip(probs, lane_g)]
    beta = [gbv[d, ch][:, lb:lb + 1] for (d, h, ch), lb in zip(probs, lane_b)]
    q = [head_cols(d, 0, h, ch) for d, h, ch in probs]
    k = [head_cols(d, 1, h, ch) for d, h, ch in probs]
    v = [head_cols(d, 2, h, ch) for d, h, ch in probs]
    kf = _each(lambda a: a.astype(F32), k)
    dec = [jnp.where(incl[d], jnp.exp(jnp.where(incl[d], gc_ - gr_, 0.0)), 0.0)
           for (d, h, ch), gc_, gr_ in zip(probs, gcol, grow)]
    qk_kk = _each(lambda q_, k_: _dot_nt(jnp.concatenate([q_, k_], axis=0), k_), q, k)
    qkd = _each(lambda a, dec_: (a[:n] * dec_).astype(BF16), qk_kk, dec)
    nmat = [jnp.where(strict[d], b_ * a[n:] * dec_, 0.0) for (d, h, ch), b_, a, dec_ in zip(probs, beta, qk_kk, dec)]
    egc = _each(jnp.exp, gcol)
    rhs = _each(lambda v_, kf_, b_, e_: jnp.concatenate([v_.astype(F32) * b_, kf_ * (b_ * e_)], axis=1).astype(BF16),
                v, kf, beta, egc)
    uw = _unit_tri_solve(nmat, rhs, same_blk, eye)
    wq = _each(lambda uw_, q_, e_: jnp.concatenate([uw_[:, GDN_HEAD_DIM:], q_.astype(F32) * e_], axis=0).astype(BF16),
               uw, q, egc)
    k_dec = _each(lambda kf_, ge_, gc_: (kf_ * jnp.exp(ge_ - gc_)).astype(BF16), kf, gend, gcol)
    g_end = _each(jnp.exp, gend)

    state = [state_ref[i] for i in range(len(heads))]
    for s in range(n_sub):
        sl = slice(s * len(heads), (s + 1) * len(heads))
        s_bf = _each(lambda a: a.astype(BF16), state)
        ws_qs = _each(lambda a, s_: jnp.dot(a, s_, preferred_element_type=F32), wq[sl], s_bf)
        v_new = _each(lambda uw_, a: (uw_[:, :GDN_HEAD_DIM] - a[:n]).astype(BF16), uw[sl], ws_qs)
        o_local = _each(lambda a, vn_: jnp.dot(a, vn_, preferred_element_type=F32), qkd[sl], v_new)
        s_add = _each(_dot_tn, k_dec[sl], v_new)
        state = _each(lambda ge_, st_, add_: ge_ * st_ + add_, g_end[sl], state, s_add)
        for (d, h, ch), a, ol in zip(probs[sl], ws_qs, o_local):
            dirs[d][2][ch * n:(ch + 1) * n, h * GDN_HEAD_DIM:(h + 1) * GDN_HEAD_DIM] = a[n:] + ol
    for i in range(len(heads)):
        state_ref[i] = state[i]


def _gdn_scan(qkv, gbeta, n_batch, seq, ctx_len):
    n_tok = qkv.shape[0]
    n = GDN_CHUNK * GDN_CHUNKS_PER_STEP
    lat_c, ctx_c = seq // n, ctx_len // n
    ctx0 = n_batch * seq // n

    def fwd(b, j):
        return (jnp.where(j < ctx_c, ctx0 + b * ctx_c + j, b * lat_c + (j - ctx_c)), 0)

    def bwd(b, j):
        return (jnp.where(j < ctx_c, ctx0 + b * ctx_c + (ctx_c - 1 - j), b * lat_c + (lat_c - 1 - (j - ctx_c))), 0)

    return pl.pallas_call(
        functools.partial(_gdn_scan_kernel, n_sub=GDN_CHUNKS_PER_STEP),
        out_shape=(jax.ShapeDtypeStruct((n_tok, GDN_W), F32), jax.ShapeDtypeStruct((n_tok, GDN_W), F32)),
        grid=(n_batch, ctx_c + lat_c),
        in_specs=[pl.BlockSpec((n, 3 * GDN_W), fwd), pl.BlockSpec((n, LANE), fwd),
                  pl.BlockSpec((n, 3 * GDN_W), bwd), pl.BlockSpec((n, LANE), bwd)],
        out_specs=(pl.BlockSpec((n, GDN_W), fwd), pl.BlockSpec((n, GDN_W), bwd)),
        scratch_shapes=[pltpu.VMEM((2 * GDN_HEADS, GDN_HEAD_DIM, GDN_HEAD_DIM), F32)],
        compiler_params=_cparams("arbitrary", "arbitrary"),
        name="gdn_scan",
    )(qkv, gbeta, qkv, gbeta)


def _pad_heads(w, n_heads, width, start, take):
    lead = w.shape[:-1]
    w = w.reshape(lead + (n_heads, width))[..., start:start + take]
    w = jnp.pad(w, [(0, 0)] * len(lead) + [(0, 0), (0, MLA_HEAD_PAD - take)])
    return w.reshape(lead + (n_heads * MLA_HEAD_PAD,))


def kernel(x, c, ctx, c_ctx, w_mod, b_mod, norm1, norm2, w_in, w_out, swa_q_norm, swa_k_norm, swa_sink, gdn_conv,
           gdn_a_log, gdn_dt_bias, gdn_out_norm, mla_q_a_norm, mla_w_uq, mla_kv_a_norm, mla_w_ukv, mla_q_norm,
           mla_k_norm, router_w, router_b, exp_w_gu, exp_b_gu, exp_w_dn, exp_b_dn):
    b, s, d = x.shape
    cl = ctx.shape[1]
    depth = w_mod.shape[0]
    n_lat, n_ctx = b * s, b * cl
    assert s % TOK_TILE == 0 and n_ctx % TOK_TILE == 0 and s % MOVE_TILE == 0 and n_ctx % ROUTE_TILE == 0
    assert cl % MLA_Q_TILE == 0 and cl % ATTN_BLOCK == 0 and n_lat % cl == 0

    n_rows = -(-(b + 1) // SUBLANE) * SUBLANE
    c_rows = jnp.concatenate([c, c_ctx[None, :], jnp.zeros((n_rows - b - 1, d), F32)], axis=0)
    mod_all = _modulation(c_rows, w_mod, b_mod)

    perm = _in_proj_perm()
    tabs = _rope_tables(s)
    seg_a = _block_diag_ones(SWA_Q, SWA_HEAD_DIM)
    seg_c = _block_diag_ones(MLA_HEADS * MLA_HEAD_PAD, MLA_HEAD_PAD)
    head_rows = np.concatenate([h * SWA_HEAD_DIM + np.arange(SWA_HEAD_DIM) for h in SWA_HEAD_ORDER])

    wg_all, wl_all = _split_glu_columns(exp_w_gu)
    n_le = depth * N_EXPERTS
    bg_all = exp_b_gu[..., 0::2].reshape(n_le, 1, -1)
    bl_all = exp_b_gu[..., 1::2].reshape(n_le, 1, -1)
    wd_all = exp_w_dn.reshape((n_le,) + exp_w_dn.shape[2:])
    bd_all = exp_b_dn.reshape(n_le, 1, d)
    x_lat, x_ctx, ctx_tile0 = x.reshape(n_lat, d), ctx.reshape(n_ctx, d), 0
    for l in range(depth):
        with_ctx = l < depth - 1
        n_tok = n_lat + n_ctx if with_ctx else n_lat
        mod_l = mod_all[l]
        w_in_bf = jnp.pad(w_in[l], ((0, 0), (0, 1)))[:, perm].astype(BF16)
        consts = (seg_a, seg_c,
                  jnp.tile(swa_q_norm[l], SWA_HEADS)[None], jnp.tile(swa_k_norm[l], SWA_KV_HEADS)[None],
                  mla_q_a_norm[l][None], mla_kv_a_norm[l][None],
                  _pad_heads(mla_w_uq[l], MLA_HEADS, MLA_QK, 0, MLA_QK).astype(BF16),
                  _pad_heads(mla_w_ukv[l], MLA_HEADS, MLA_NOPE + MLA_V, 0, MLA_NOPE).astype(BF16),
                  _pad_heads(mla_w_ukv[l], MLA_HEADS, MLA_NOPE + MLA_V, MLA_NOPE, MLA_V).astype(BF16),
                  _pad_heads(jnp.tile(mla_q_norm[l], MLA_HEADS), MLA_HEADS, MLA_QK, 0, MLA_QK)[None],
                  _pad_heads(jnp.tile(mla_k_norm[l], MLA_HEADS), MLA_HEADS, MLA_QK, 0, MLA_QK)[None])
        pg, small, qa, kva, qm, km, vm = _in_projection(x_lat, x_ctx, ctx_tile0, n_lat + n_ctx, norm1[l], mod_l,
                                                        w_in_bf, tabs, consts, b, s)
        o_a = _swa_attention(swa_sink[l], qa, kva, b, s, cl, with_ctx)
        o_c = _mla_attention(qm, km, vm, b, s, cl, with_ctx)
        lanes_g = slice(SMALL_GA, SMALL_GA + 2 * GDN_HEADS)
        neg_a = jnp.zeros((1, LANE), F32).at[0, lanes_g].set(-jnp.exp(gdn_a_log[l]).reshape(-1))
        dt_bias = jnp.zeros((1, LANE), F32).at[0, lanes_g].set(gdn_dt_bias[l].reshape(-1))
        qkv, gbeta = _gdn_prep(pg, small, gdn_conv[l], neg_a, dt_bias, b, s, cl)
        o_fwd, o_bwd = _gdn_scan(qkv, gbeta, b, s, cl)

        wa = w_out[l][:SWA_Q][head_rows].astype(BF16)
        wb = w_out[l][SWA_Q:SWA_Q + GDN_W].astype(BF16)
        wc = w_out[l][SWA_Q + GDN_W:].reshape(MLA_HEADS, MLA_V, d)
        wc = jnp.pad(wc, ((0, 0), (0, MLA_HEAD_PAD - MLA_V), (0, 0))).reshape(MLA_HEADS * MLA_HEAD_PAD, d).astype(BF16)
        x_mid, h2 = _out_projection(o_a, o_fwd, o_bwd, pg, o_c, x_lat, x_ctx, ctx_tile0, wa, wb, wc,
                                    jnp.tile(gdn_out_norm[l], GDN_HEADS)[None], norm2[l], mod_l, n_tok, b, s)

        x_next = _moe(h2, x_mid, n_tok, mod_l, router_w[l].T.astype(BF16), router_b[l], l, wg_all, wl_all,
                      bg_all, bl_all, wd_all, bd_all, b, s)
        x_lat, x_ctx, ctx_tile0 = x_next, x_next, n_lat // TOK_TILE
    return x_lat[:n_lat].reshape(b, s, d)
```

```python
import functools

import jax
import jax.numpy as jnp
import numpy as np
from jax import lax
from jax.experimental import pallas as pl
from jax.experimental.pallas import tpu as pltpu

F32 = jnp.float32
BF16 = jnp.bfloat16

GRID_W = 64
N_MOD = 6
EPS = 1e-6
ROPE_BASE = 10000.0
ATTN_BLOCK = 128

SWA_HEADS = 4
SWA_KV_HEADS = 2
SWA_HEAD_DIM = 64
SWA_WINDOW = 128

GDN_HEADS = 4
GDN_HEAD_DIM = 128
GDN_CONV = 5
GDN_CHUNK = 64

MLA_HEADS = 4
MLA_Q_RANK = 256
MLA_KV_RANK = 128
MLA_NOPE = 64
MLA_ROPE = 32
MLA_V = 64
MLA_QK = MLA_NOPE + MLA_ROPE

N_EXPERTS = 32
TOP_K = 4
SWIGLU_LIMIT = 7.0
SWIGLU_ALPHA = 1.702

SWA_Q = SWA_HEADS * SWA_HEAD_DIM
SWA_KV = SWA_KV_HEADS * SWA_HEAD_DIM
GDN_W = GDN_HEADS * GDN_HEAD_DIM
MLA_O = MLA_HEADS * MLA_V
D_MIX = SWA_Q + GDN_W + MLA_O
IN_SPLITS = (SWA_Q, SWA_KV, SWA_KV, GDN_W, GDN_W, GDN_W, GDN_W, 2 * GDN_HEADS, 2 * GDN_HEADS, MLA_Q_RANK,
             MLA_KV_RANK, MLA_ROPE)
N_IN = sum(IN_SPLITS)

LANE = 128
SUBLANE = 8
VMEM_LIMIT = 56 * 1024 * 1024

TOK_TILE = 512
MOE_TILE = 512
ROUTE_TILE = 512
MOVE_TILE = 512
ROW_UNROLL = 4
ROW_GROUP = 256
MLA_Q_TILE = 256
MOD_COL_TILE = 1536

COL_AQ, COL_AK, COL_AV = 0, 256, 384
COL_G = 512
COL_CQ, COL_CKV, COL_SMALL = 2560, 2816, 2944
N_IN_PAD = 3072
SMALL_KR, SMALL_GA, SMALL_GB = 0, MLA_ROPE, MLA_ROPE + 2 * GDN_HEADS
SWA_HEAD_ORDER = (0, 2, 1, 3)
MLA_HEAD_PAD = 128


def _cparams(*sem):
    return pltpu.CompilerParams(dimension_semantics=sem, vmem_limit_bytes=VMEM_LIMIT)


def _in_proj_perm():
    old = np.cumsum((0,) + IN_SPLITS)
    o_aq, o_ak, o_av, o_gq, o_gk, o_gv, o_gz, o_ga, o_gb, o_cq, o_ckv, o_ckr = old[:-1]
    perm = np.full((N_IN_PAD,), N_IN, np.int32)
    perm[COL_AQ:COL_AQ + SWA_Q] = np.concatenate(
        [o_aq + h * SWA_HEAD_DIM + np.arange(SWA_HEAD_DIM) for h in SWA_HEAD_ORDER])
    perm[COL_AK:COL_G + 4 * GDN_W] = np.arange(o_ak, o_ga)
    perm[COL_CQ:COL_CQ + MLA_Q_RANK] = o_cq + np.arange(MLA_Q_RANK)
    perm[COL_CKV:COL_CKV + MLA_KV_RANK] = o_ckv + np.arange(MLA_KV_RANK)
    perm[COL_SMALL + SMALL_KR:COL_SMALL + SMALL_KR + MLA_ROPE] = o_ckr + np.arange(MLA_ROPE)
    perm[COL_SMALL + SMALL_GA:COL_SMALL + SMALL_GA + 2 * GDN_HEADS] = o_ga + np.arange(2 * GDN_HEADS)
    perm[COL_SMALL + SMALL_GB:COL_SMALL + SMALL_GB + 2 * GDN_HEADS] = o_gb + np.arange(2 * GDN_HEADS)
    return perm


def _mod_kernel(c_ref, w_ref, b_ref, o_ref):
    a = c_ref[...]
    a = a * jax.nn.sigmoid(a)
    o_ref[0] = jnp.dot(a.astype(BF16), w_ref[0].astype(BF16), preferred_element_type=F32) + b_ref[0]


def _modulation(c_rows, w_mod, b_mod):
    nl, d, n = w_mod.shape
    r = c_rows.shape[0]
    tn = MOD_COL_TILE
    out = pl.pallas_call(
        _mod_kernel,
        out_shape=jax.ShapeDtypeStruct((nl, r, n), F32),
        grid=(nl, n // tn),
        in_specs=[pl.BlockSpec((r, d), lambda l, j: (0, 0)),
                  pl.BlockSpec((1, d, tn), lambda l, j: (l, 0, j)),
                  pl.BlockSpec((1, 1, tn), lambda l, j: (l, 0, j))],
        out_specs=pl.BlockSpec((1, r, tn), lambda l, j: (l, 0, j)),
        compiler_params=_cparams("arbitrary", "arbitrary"),
        name="modulation",
    )(c_rows, w_mod, b_mod.reshape(nl, 1, n))
    return out.reshape(nl, r, N_MOD, d)


def _modulated_norm(x, gain, shift, scale):
    y = x * lax.rsqrt(jnp.mean(x * x, axis=-1, keepdims=True) + EPS)
    return (y * gain) * (1.0 + scale) + shift


def _group_of_tile(i, tiles_per_seq, n_batch):
    return jnp.minimum(i // tiles_per_seq, n_batch)


def _group_sumsq(x, group):
    assert group in (LANE, LANE // 2)
    x2 = x * x
    lane = lax.broadcasted_iota(jnp.int32, (1, LANE), 1)
    lower = lane < LANE // 2
    cols = []
    for c in range(x.shape[1] // LANE):
        blk = x2[:, c * LANE:(c + 1) * LANE]
        if group == LANE:
            cols.append(jnp.broadcast_to(jnp.sum(blk, axis=-1, keepdims=True), blk.shape))
        else:
            s_lo = jnp.sum(jnp.where(lower, blk, 0.0), axis=-1, keepdims=True)
            s_hi = jnp.sum(jnp.where(lower, 0.0, blk), axis=-1, keepdims=True)
            cols.append(jnp.where(lower, s_lo, s_hi))
    return cols[0] if len(cols) == 1 else jnp.concatenate(cols, axis=1)


def _swap_pairs(x, half):
    w = x.shape[1]
    lane = lax.broadcasted_iota(jnp.int32, (1, w), 1)
    first = (lane % (2 * half)) < half
    return jnp.where(first, pltpu.roll(x, w - half, axis=1), pltpu.roll(x, half, axis=1))


def _inproj_kernel(xl_ref, xc_ref, g_ref, m_ref, w_ref, ra_c_ref, ra_s_ref, rc_c_ref, rc_s_ref,
                   gaq_ref, gak_ref, gcq_ref, gckv_ref, wuq_ref, wuk_ref, wuv_ref, gmq_ref, gmk_ref,
                   pg_ref, small_ref, qa_ref, kva_ref, qm_ref, km_ref, vm_ref, *, n_lat_tiles):
    m = m_ref[0]
    for i in range(xl_ref.shape[0] // ROW_GROUP):
        rows = pl.ds(i * ROW_GROUP, ROW_GROUP)
        x = _tile_rows(xl_ref, xc_ref, rows, n_lat_tiles)
        h = _modulated_norm(x, g_ref[...], m[0:1], m[1:2]).astype(BF16)
        pa = jnp.dot(h, w_ref[:, COL_AQ:COL_G], preferred_element_type=F32)
        pc = jnp.dot(h, w_ref[:, COL_CQ:], preferred_element_type=F32)
        small_ref[rows, :] = pc[:, MLA_Q_RANK + MLA_KV_RANK:]
        _attention_prep(rows, pa, pc, ra_c_ref, ra_s_ref, rc_c_ref, rc_s_ref, gaq_ref, gak_ref,
                        gcq_ref, gckv_ref, wuq_ref, wuk_ref, wuv_ref, gmq_ref, gmk_ref,
                        qa_ref, kva_ref, qm_ref, km_ref, vm_ref)
        pg_ref[rows, :] = jnp.dot(h, w_ref[:, COL_G:COL_CQ], preferred_element_type=F32)


def _attention_prep(rows, pa, pc, ra_c_ref, ra_s_ref, rc_c_ref, rc_s_ref, gaq_ref, gak_ref,
                    gcq_ref, gckv_ref, wuq_ref, wuk_ref, wuv_ref, gmq_ref, gmk_ref,
                    qa_ref, kva_ref, qm_ref, km_ref, vm_ref):
    small = pc[:, MLA_Q_RANK + MLA_KV_RANK:]

    aq = pa[:, 0:SWA_Q]
    ak = pa[:, SWA_Q:SWA_Q + SWA_KV]
    av = pa[:, SWA_Q + SWA_KV:SWA_Q + 2 * SWA_KV]
    ca, sa = ra_c_ref[rows, :], ra_s_ref[rows, :]
    qn = aq * lax.rsqrt(_group_sumsq(aq, SWA_HEAD_DIM) * (1.0 / SWA_HEAD_DIM) + EPS) * gaq_ref[...]
    kn = ak * lax.rsqrt(_group_sumsq(ak, SWA_HEAD_DIM) * (1.0 / SWA_HEAD_DIM) + EPS) * gak_ref[...]
    qn = qn * jnp.concatenate([ca, ca], axis=1) + _swap_pairs(qn, SWA_HEAD_DIM // 4) * jnp.concatenate([sa, sa], axis=1)
    kn = kn * ca + _swap_pairs(kn, SWA_HEAD_DIM // 4) * sa
    qa_ref[rows, :] = (qn * SWA_HEAD_DIM ** -0.5).astype(BF16)
    kva_ref[rows, :] = jnp.concatenate([kn, av], axis=1).astype(BF16)

    cq = pc[:, 0:MLA_Q_RANK]
    ckv = pc[:, MLA_Q_RANK:MLA_Q_RANK + MLA_KV_RANK]
    cqn = cq * lax.rsqrt(jnp.mean(cq * cq, axis=-1, keepdims=True) + EPS) * gcq_ref[...]
    ckvn = (ckv * lax.rsqrt(jnp.mean(ckv * ckv, axis=-1, keepdims=True) + EPS) * gckv_ref[...]).astype(BF16)
    q = jnp.dot(cqn.astype(BF16), wuq_ref[...], preferred_element_type=F32)
    k = jnp.dot(ckvn, wuk_ref[...], preferred_element_type=F32)
    v = jnp.dot(ckvn, wuv_ref[...], preferred_element_type=F32)
    lane = lax.broadcasted_iota(jnp.int32, (1, LANE), 1)
    kpe = jnp.where((lane >= MLA_NOPE) & (lane < MLA_QK), pltpu.roll(small, MLA_NOPE, axis=1), 0.0)
    k = k + jnp.concatenate([kpe] * MLA_HEADS, axis=1)
    q = q * lax.rsqrt(_group_sumsq(q, MLA_HEAD_PAD) * (1.0 / MLA_QK) + EPS) * gmq_ref[...]
    k = k * lax.rsqrt(_group_sumsq(k, MLA_HEAD_PAD) * (1.0 / MLA_QK) + EPS) * gmk_ref[...]
    cc = jnp.concatenate([rc_c_ref[rows, :]] * MLA_HEADS, axis=1)
    sc = jnp.concatenate([rc_s_ref[rows, :]] * MLA_HEADS, axis=1)
    q = q * cc + _swap_pairs(q, MLA_ROPE // 4) * sc
    k = k * cc + _swap_pairs(k, MLA_ROPE // 4) * sc
    qm_ref[rows, :] = (q * MLA_QK ** -0.5).astype(BF16)
    km_ref[rows, :] = k.astype(BF16)
    vm_ref[rows, :] = v.astype(BF16)


def _token_sources(x_lat, x_ctx, ctx_tile0, n_lat_tiles, tm):
    d = x_lat.shape[1]
    return [pl.BlockSpec((tm, d), lambda i: (jnp.minimum(i, n_lat_tiles - 1), 0)),
            pl.BlockSpec((tm, d), lambda i: (ctx_tile0 + jnp.maximum(i - n_lat_tiles, 0), 0))]


def _tile_rows(xl_ref, xc_ref, rows, n_lat_tiles):
    tile = jnp.zeros((ROW_GROUP, 1), jnp.int32) + pl.program_id(0)
    return jnp.where(tile < n_lat_tiles, xl_ref[rows, :], xc_ref[rows, :])


def _in_projection(x_lat, x_ctx, ctx_tile0, n_tok, gain, mod_l, w_bf, tabs, consts, n_batch, seq):
    d = x_lat.shape[1]
    tm = TOK_TILE
    tps = seq // tm
    n_lat_tiles = n_batch * tps
    row = lambda i: (i, 0)
    const = lambda i: (0, 0)
    tab = lambda i: (jnp.where(i < n_lat_tiles, i % tps, tps), 0)
    wide = MLA_HEADS * MLA_HEAD_PAD
    in_specs = _token_sources(x_lat, x_ctx, ctx_tile0, n_lat_tiles, tm)
    in_specs += [pl.BlockSpec((1, d), const),
                pl.BlockSpec((1, N_MOD, d), lambda i: (_group_of_tile(i, tps, n_batch), 0, 0)),
                pl.BlockSpec(w_bf.shape, const)]
    in_specs += [pl.BlockSpec((tm, LANE), tab)] * 4
    in_specs += [pl.BlockSpec(a.shape, const) for a in consts]
    widths = (4 * GDN_W, LANE, SWA_Q, 2 * SWA_KV, wide, wide, wide)
    dtypes = (F32, F32, BF16, BF16, BF16, BF16, BF16)
    return pl.pallas_call(
        functools.partial(_inproj_kernel, n_lat_tiles=n_lat_tiles),
        out_shape=tuple(jax.ShapeDtypeStruct((n_tok, w), t) for w, t in zip(widths, dtypes)),
        grid=(n_tok // tm,),
        in_specs=in_specs,
        out_specs=tuple(pl.BlockSpec((tm, w), row) for w in widths),
        compiler_params=_cparams("arbitrary"),
        name="in_projection",
    )(x_lat, x_ctx, gain.reshape(1, d), mod_l, w_bf, *tabs, *consts)


def _rope_tables(seq):
    t = jnp.arange(seq)
    row = (t // GRID_W).astype(F32)
    col = (t % GRID_W).astype(F32)

    def cos_sin(rot_dim):
        n_freq = rot_dim // 4
        freq = ROPE_BASE ** (-jnp.arange(n_freq, dtype=F32) / n_freq)
        ar, ac = row[:, None] * freq, col[:, None] * freq
        c = jnp.concatenate([jnp.cos(ar), jnp.cos(ar), jnp.cos(ac), jnp.cos(ac)], axis=1)
        s = jnp.concatenate([-jnp.sin(ar), jnp.sin(ar), -jnp.sin(ac), jnp.sin(ac)], axis=1)
        return c, s

    ca, sa = cos_sin(SWA_HEAD_DIM)
    ca, sa = jnp.tile(ca, (1, 2)), jnp.tile(sa, (1, 2))
    cc, sc = cos_sin(MLA_ROPE)
    ones_l = jnp.ones((seq, MLA_NOPE), F32)
    ones_r = jnp.ones((seq, MLA_HEAD_PAD - MLA_QK), F32)
    cc = jnp.concatenate([ones_l, cc, ones_r], axis=1)
    sc = jnp.concatenate([0 * ones_l, sc, 0 * ones_r], axis=1)
    ident_c = jnp.ones((TOK_TILE, LANE), F32)
    ident_s = jnp.zeros((TOK_TILE, LANE), F32)
    return tuple(jnp.concatenate([a, i], axis=0) for a, i in ((ca, ident_c), (sa, ident_s), (cc, ident_c), (sc, ident_s)))


def _dot_nt(a, b):
    return lax.dot_general(a, b, (((1,), (1,)), ((), ())), preferred_element_type=F32)


def _swa_heads(sink_ref, q, k_list, v_list, mask_list, o_ref):
    lane = lax.broadcasted_iota(jnp.int32, (1, LANE), 1)
    lower = lane < SWA_HEAD_DIM
    cols = []
    for cgrp in range(2):
        qc = q[:, cgrp * LANE:(cgrp + 1) * LANE]
        halves = []
        for half in range(2):
            head = SWA_HEAD_ORDER[2 * cgrp + half]
            qh = jnp.where(lower if half == 0 else ~lower, qc, jnp.zeros_like(qc))
            sink = sink_ref[head]
            scores = []
            m = jnp.full((q.shape[0], 1), sink, F32)
            for kb, mk in zip(k_list, mask_list):
                s = _dot_nt(qh, kb)
                if mk is not None:
                    s = jnp.where(mk, s, -jnp.inf)
                scores.append(s)
                m = jnp.maximum(m, jnp.max(s, axis=-1, keepdims=True))
            l = jnp.exp(sink - m)
            acc = jnp.zeros((q.shape[0], LANE), F32)
            for s, vb in zip(scores, v_list):
                p = jnp.exp(s - m)
                l = l + jnp.sum(p, axis=-1, keepdims=True)
                acc = acc + jnp.dot(p.astype(BF16), vb, preferred_element_type=F32)
            halves.append(acc * (1.0 / l))
        cols.append(jnp.where(lower, halves[0], halves[1]))
    o_ref[...] = jnp.concatenate(cols, axis=1).astype(o_ref.dtype)


SWA_Q_BLOCKS = 2
SWA_KEY_OFFSETS = tuple(range(-1, SWA_Q_BLOCKS + 1))


def _swa_kernel(sink_ref, q_ref, *refs, n_blocks):
    kv_refs, kvc_ref, o_ref = refs[:len(SWA_KEY_OFFSETS)], refs[-2], refs[-1]
    j = pl.program_id(1)
    n_steps = n_blocks // SWA_Q_BLOCKS
    kc, vc = kvc_ref[:, 0:SWA_KV], kvc_ref[:, SWA_KV:]

    @pl.when(j < n_steps)
    def _():
        rows = SWA_Q_BLOCKS * ATTN_BLOCK
        r = lax.broadcasted_iota(jnp.int32, (rows, ATTN_BLOCK), 0)
        c = lax.broadcasted_iota(jnp.int32, (rows, ATTN_BLOCK), 1)
        masks = []
        for off in SWA_KEY_OFFSETS:
            blk = j * SWA_Q_BLOCKS + off
            in_seq = (blk >= 0) & (blk < n_blocks)
            masks.append(jnp.abs(off * ATTN_BLOCK + c - r) <= jnp.where(in_seq, SWA_WINDOW, -1))
        ks = [ref[:, 0:SWA_KV] for ref in kv_refs] + [kc]
        vs = [ref[:, SWA_KV:] for ref in kv_refs] + [vc]
        _swa_heads(sink_ref, q_ref[...], ks, vs, masks + [None], o_ref)

    @pl.when(j >= n_steps)
    def _():
        _swa_heads(sink_ref, q_ref[...], [kc], [vc], [None], o_ref)


def _swa_attention(sink, qa, kva, n_batch, seq, ctx_len, with_ctx):
    rows = SWA_Q_BLOCKS * ATTN_BLOCK
    nb = seq // ATTN_BLOCK
    nq = seq // rows
    ncq = ctx_len // rows
    n_lat = n_batch * seq
    steps = nq + (ncq if with_ctx else 0)
    n_tok = n_lat + (n_batch * ctx_len if with_ctx else 0)

    def q_map(b, j, s):
        return (jnp.where(j < nq, b * nq + j, n_lat // rows + b * ncq + (j - nq)), 0)

    def kv_map(off):
        def f(b, j, s):
            jj = jnp.clip(jnp.minimum(j, nq - 1) * SWA_Q_BLOCKS + off, 0, nb - 1)
            return (b * nb + jj, 0)
        return f

    return pl.pallas_call(
        functools.partial(_swa_kernel, n_blocks=nb),
        out_shape=jax.ShapeDtypeStruct((n_tok, SWA_Q), BF16),
        grid_spec=pltpu.PrefetchScalarGridSpec(
            num_scalar_prefetch=1,
            grid=(n_batch, steps),
            in_specs=[pl.BlockSpec((rows, SWA_Q), q_map)]
            + [pl.BlockSpec((ATTN_BLOCK, 2 * SWA_KV), kv_map(off)) for off in SWA_KEY_OFFSETS]
            + [pl.BlockSpec((ctx_len, 2 * SWA_KV), lambda b, j, s: (n_lat // ctx_len + b, 0))],
            out_specs=pl.BlockSpec((rows, SWA_Q), q_map)),
        compiler_params=_cparams("arbitrary", "arbitrary"),
        name="swa_attention",
    )(sink, qa, *([kva] * len(SWA_KEY_OFFSETS)), kva)


def _mla_heads(q_ref, k_refs, v_refs, o_ref):
    for h in range(MLA_HEADS):
        sl = slice(h * MLA_HEAD_PAD, (h + 1) * MLA_HEAD_PAD)
        qh = q_ref[:, sl]
        scores = [_dot_nt(qh, k_ref[:, sl]) for k_ref in k_refs]
        m = jnp.max(scores[0], axis=-1, keepdims=True)
        for s in scores[1:]:
            m = jnp.maximum(m, jnp.max(s, axis=-1, keepdims=True))
        l = jnp.zeros_like(m)
        acc = jnp.zeros((qh.shape[0], MLA_HEAD_PAD), F32)
        for s, v_ref in zip(scores, v_refs):
            p = jnp.exp(s - m)
            l = l + jnp.sum(p, axis=-1, keepdims=True)
            acc = acc + jnp.dot(p.astype(BF16), v_ref[:, sl], preferred_element_type=F32)
        o_ref[:, sl] = (acc * (1.0 / l)).astype(o_ref.dtype)


def _mla_kernel(q_ref, kl_ref, vl_ref, kc_ref, vc_ref, o_ref, *, n_lat_steps):
    j = pl.program_id(1)

    @pl.when(j < n_lat_steps)
    def _():
        _mla_heads(q_ref, [kl_ref, kc_ref], [vl_ref, vc_ref], o_ref)

    @pl.when(j >= n_lat_steps)
    def _():
        _mla_heads(q_ref, [kc_ref], [vc_ref], o_ref)


def _mla_attention(qm, km, vm, n_batch, seq, ctx_len, with_ctx):
    tq = MLA_Q_TILE
    nq = seq // tq
    ncq = ctx_len // tq
    n_lat = n_batch * seq
    steps = nq + (ncq if with_ctx else 0)
    n_tok = n_lat + (n_batch * ctx_len if with_ctx else 0)
    wide = MLA_HEADS * MLA_HEAD_PAD

    def q_map(b, j):
        return (jnp.where(j < nq, b * nq + j, n_lat // tq + b * ncq + (j - nq)), 0)

    lat_map = lambda b, j: (b, 0)
    ctx_map = lambda b, j: (n_lat // ctx_len + b, 0)
    return pl.pallas_call(
        functools.partial(_mla_kernel, n_lat_steps=nq),
        out_shape=jax.ShapeDtypeStruct((n_tok, wide), BF16),
        grid=(n_batch, steps),
        in_specs=[pl.BlockSpec((tq, wide), q_map),
                  pl.BlockSpec((seq, wide), lat_map),
                  pl.BlockSpec((seq, wide), lat_map),
                  pl.BlockSpec((ctx_len, wide), ctx_map),
                  pl.BlockSpec((ctx_len, wide), ctx_map)],
        out_specs=pl.BlockSpec((tq, wide), q_map),
        compiler_params=_cparams("arbitrary", "arbitrary"),
        name="mla_attention",
    )(qm, km, vm, km, vm)


def _outproj_kernel(oa_ref, of_ref, ob_ref, z_ref, oc_ref, xl_ref, xc_ref, wa_ref, wb_ref, wc_ref, gg_ref, g_ref, m_ref,
                    xo_ref, h_ref, *, n_lat_tiles):
    m = m_ref[0]
    groups = [pl.ds(i * ROW_GROUP, ROW_GROUP) for i in range(xl_ref.shape[0] // ROW_GROUP)]
    mixed = []
    for rows in groups:
        ob = of_ref[rows, :] + ob_ref[rows, :]
        z = z_ref[rows, :]
        heads = []
        for h in range(GDN_HEADS):
            oh = ob[:, h * GDN_HEAD_DIM:(h + 1) * GDN_HEAD_DIM]
            heads.append(oh * lax.rsqrt(jnp.mean(oh * oh, axis=-1, keepdims=True) + EPS))
        gated = jnp.concatenate(heads, axis=1) * gg_ref[...] * (z * jax.nn.sigmoid(z))
        mixed.append(jnp.dot(oa_ref[rows, :], wa_ref[...], preferred_element_type=F32)
                     + jnp.dot(gated.astype(BF16), wb_ref[...], preferred_element_type=F32)
                     + jnp.dot(oc_ref[rows, :], wc_ref[...], preferred_element_type=F32))
    for rows, o in zip(groups, mixed):
        xn = _tile_rows(xl_ref, xc_ref, rows, n_lat_tiles) + m[2:3] * o
        xo_ref[rows, :] = xn
        h_ref[rows, :] = _modulated_norm(xn, g_ref[...], m[3:4], m[4:5])


def _out_projection(o_a, o_fwd, o_bwd, pg, o_c, x_lat, x_ctx, ctx_tile0, wa, wb, wc, gdn_gain, gain2, mod_l, n_tok,
                    n_batch, seq):
    d = x_lat.shape[1]
    tm = TOK_TILE
    tps = seq // tm
    n_lat_tiles = n_batch * tps
    row = lambda i: (i, 0)
    const = lambda i: (0, 0)
    return pl.pallas_call(
        functools.partial(_outproj_kernel, n_lat_tiles=n_lat_tiles),
        out_shape=(jax.ShapeDtypeStruct((n_tok, d), F32), jax.ShapeDtypeStruct((n_tok, d), F32)),
        grid=(n_tok // tm,),
        in_specs=[pl.BlockSpec((tm, o_a.shape[1]), row),
                  pl.BlockSpec((tm, GDN_W), row),
                  pl.BlockSpec((tm, GDN_W), row),
                  pl.BlockSpec((tm, GDN_W), lambda i: (i, 3)),
                  pl.BlockSpec((tm, o_c.shape[1]), row)]
        + _token_sources(x_lat, x_ctx, ctx_tile0, n_lat_tiles, tm)
        + [pl.BlockSpec(wa.shape, const),
                  pl.BlockSpec(wb.shape, const),
                  pl.BlockSpec(wc.shape, const),
                  pl.BlockSpec((1, GDN_W), const),
                  pl.BlockSpec((1, d), const),
                  pl.BlockSpec((1, N_MOD, d), lambda i: (_group_of_tile(i, tps, n_batch), 0, 0))],
        out_specs=(pl.BlockSpec((tm, d), row), pl.BlockSpec((tm, d), row)),
        compiler_params=_cparams("arbitrary"),
        name="out_projection",
    )(o_a, o_fwd, o_bwd, pg, o_c, x_lat, x_ctx, wa, wb, wc, gdn_gain, gain2.reshape(1, d), mod_l)


def _route_kernel(h_ref, rw_ref, rb_ref, idx_ref, gate_ref, cnt_ref, base_ref):
    step = pl.program_id(0)
    tm = h_ref.shape[0]

    @pl.when(step == 0)
    def _():
        base_ref[...] = jnp.zeros_like(base_ref)

    logits = lax.dot_general(rw_ref[...], h_ref[...].astype(BF16), (((1,), (1,)), ((), ())),
                             preferred_element_type=F32) + rb_ref[...]
    e_iota = lax.broadcasted_iota(jnp.int32, logits.shape, 0)
    work = logits
    tops, picks = [], []
    for _k in range(TOP_K):
        mx = jnp.max(work, axis=0, keepdims=True)
        pick = jnp.min(jnp.where(work == mx, e_iota, N_EXPERTS), axis=0, keepdims=True)
        work = jnp.where(e_iota == pick, -jnp.inf, work)
        tops.append(mx)
        picks.append(pick)
    exps = [jnp.exp(t - tops[0]) for t in tops]
    denom = exps[0] + exps[1] + exps[2] + exps[3]
    sel = jnp.zeros(logits.shape, F32)
    for pick in picks:
        sel = sel + (e_iota == pick).astype(F32)
    row = lax.broadcasted_iota(jnp.int32, (tm, tm), 0)
    col = lax.broadcasted_iota(jnp.int32, (tm, tm), 1)
    before = (row < col).astype(BF16)
    cnt = jnp.dot(sel.astype(BF16), before, preferred_element_type=F32) + base_ref[:, 0:1]
    ranks = [jnp.sum(jnp.where(e_iota == pick, cnt, 0.0), axis=0, keepdims=True) for pick in picks]
    idx_ref[0] = jnp.concatenate(picks + [r.astype(jnp.int32) for r in ranks], axis=0)
    gate_rows = jnp.concatenate([e / denom for e in exps] + [jnp.zeros((LANE - TOP_K, tm), F32)], axis=0)
    gate_ref[...] = jnp.transpose(gate_rows)
    base_ref[...] = base_ref[...] + jnp.sum(sel, axis=1, keepdims=True)
    cnt_ref[...] = base_ref[...]


def _route(h_flat, n_tok, rw_t_bf, rb):
    d = h_flat.shape[1]
    tm = ROUTE_TILE
    nt = n_tok // tm
    return pl.pallas_call(
        _route_kernel,
        out_shape=(jax.ShapeDtypeStruct((nt, 2 * TOP_K, tm), jnp.int32),
                   jax.ShapeDtypeStruct((n_tok, LANE), F32),
                   jax.ShapeDtypeStruct((N_EXPERTS, LANE), F32)),
        grid=(nt,),
        in_specs=[pl.BlockSpec((tm, d), lambda i: (i, 0)),
                  pl.BlockSpec((N_EXPERTS, d), lambda i: (0, 0)),
                  pl.BlockSpec((N_EXPERTS, 1), lambda i: (0, 0))],
        out_specs=(pl.BlockSpec((1, 2 * TOP_K, tm), lambda i: (i, 0, 0)),
                   pl.BlockSpec((tm, LANE), lambda i: (i, 0)),
                   pl.BlockSpec((N_EXPERTS, LANE), lambda i: (0, 0))),
        scratch_shapes=[pltpu.VMEM((N_EXPERTS, LANE), F32)],
        compiler_params=_cparams("arbitrary"),
        name="moe_route",
    )(h_flat, rw_t_bf, rb.reshape(N_EXPERTS, 1))


def _dispatch_kernel(lo_ref, hi_ref, slot_ref, h_ref, xs_out, zero_ref, sem, zero_sem, tile_sem):
    tm = slot_ref.shape[2]

    @pl.when(pl.program_id(0) == pl.num_programs(0) - 1)
    def _():
        zero_ref[...] = jnp.zeros_like(zero_ref)

        def row_copy0(r):
            return pltpu.make_async_copy(zero_ref.at[pl.ds(0, 1)], xs_out.at[pl.ds(r, 1)], zero_sem)

        def tile_copy0(g):
            dst = xs_out.at[pl.ds(pl.multiple_of(g * SUBLANE, SUBLANE), SUBLANE)]
            return pltpu.make_async_copy(zero_ref, dst, tile_sem)

        def bounds(e):
            lo, hi = lo_ref[e], hi_ref[e]
            mid = jnp.minimum((lo + SUBLANE - 1) // SUBLANE * SUBLANE, hi)
            return lo, mid, hi

        def start_expert(e, carry):
            lo, mid, hi = bounds(e)
            lax.fori_loop(lo, mid, lambda r, c: (row_copy0(r).start(), c)[1], carry)
            return lax.fori_loop(mid // SUBLANE, hi // SUBLANE, lambda g, c: (tile_copy0(g).start(), c)[1], carry)

        def wait_expert(e, carry):
            lo, mid, hi = bounds(e)
            lax.fori_loop(lo, mid, lambda r, c: (row_copy0(r).wait(), c)[1], carry)
            return lax.fori_loop(mid // SUBLANE, hi // SUBLANE, lambda g, c: (tile_copy0(g).wait(), c)[1], carry)

        lax.fori_loop(0, lo_ref.shape[0], start_expert, 0)
        lax.fori_loop(0, lo_ref.shape[0], wait_expert, 0)

    def row_copy(t, k):
        return pltpu.make_async_copy(h_ref.at[pl.ds(t, 1)], xs_out.at[pl.ds(slot_ref[0, k, t], 1)], sem)

    def issue(i, carry):
        for u in range(ROW_UNROLL):
            for k in range(TOP_K):
                row_copy(i * ROW_UNROLL + u, k).start(priority=k % 2)
        return carry

    lax.fori_loop(0, tm // ROW_UNROLL, issue, 0)

    def drain(i, carry):
        for u in range(ROW_UNROLL):
            for k in range(TOP_K):
                row_copy(i * ROW_UNROLL + u, k).wait()
        return carry

    lax.fori_loop(0, tm // ROW_UNROLL, drain, 0)


def _dispatch(pad_lo, pad_hi, slots, h_flat, n_tok, cap):
    d = h_flat.shape[1]
    tm = slots.shape[2]
    return pl.pallas_call(
        _dispatch_kernel,
        out_shape=jax.ShapeDtypeStruct((cap, d), F32),
        grid_spec=pltpu.PrefetchScalarGridSpec(
            num_scalar_prefetch=2,
            grid=(n_tok // tm,),
            in_specs=[pl.BlockSpec((1, TOP_K, tm), lambda i, lo, hi: (i, 0, 0), memory_space=pltpu.SMEM),
                      pl.BlockSpec((tm, d), lambda i, lo, hi: (i, 0))],
            out_specs=pl.BlockSpec(memory_space=pl.ANY),
            scratch_shapes=[pltpu.VMEM((SUBLANE, d), F32)] + [pltpu.SemaphoreType.DMA(())] * 3),
        compiler_params=_cparams("arbitrary"),
        name="moe_dispatch",
    )(pad_lo, pad_hi, slots, h_flat)


SPLIT_COLS = 1024


def _split_kernel(w_ref, sel_ref, wg_ref, wl_ref):
    for j in range(w_ref.shape[2] // (2 * LANE)):
        blk = w_ref[0, :, 2 * LANE * j:2 * LANE * (j + 1)].astype(BF16)
        r = jnp.dot(blk, sel_ref[...], preferred_element_type=F32)
        wg_ref[0, :, LANE * j:LANE * (j + 1)] = r[:, :LANE].astype(BF16)
        wl_ref[0, :, LANE * j:LANE * (j + 1)] = r[:, LANE:].astype(BF16)


def _split_glu_columns(w_gu):
    nl, ne, d, f2 = w_gu.shape
    src = np.arange(2 * LANE)
    dst = np.where(src % 2 == 0, src // 2, LANE + src // 2)
    sel = jnp.asarray((dst[:, None] == np.arange(2 * LANE)[None, :]).astype(np.float32), BF16)
    tn = SPLIT_COLS
    out = jax.ShapeDtypeStruct((nl * ne, d, f2 // 2), BF16)
    wg, wl = pl.pallas_call(
        _split_kernel,
        out_shape=(out, out),
        grid=(nl * ne, f2 // tn),
        in_specs=[pl.BlockSpec((1, d, tn), lambda e, j: (e, 0, j)),
                  pl.BlockSpec((2 * LANE, 2 * LANE), lambda e, j: (0, 0))],
        out_specs=(pl.BlockSpec((1, d, tn // 2), lambda e, j: (e, 0, j)),
                   pl.BlockSpec((1, d, tn // 2), lambda e, j: (e, 0, j))),
        compiler_params=_cparams("arbitrary", "arbitrary"),
        name="split_glu_columns",
    )(w_gu.reshape(nl * ne, d, f2), sel)
    return wg, wl


def _expert_kernel(be_ref, nu_ref, xs_ref, wg_ref, wl_ref, bg_ref, bl_ref, wd_ref, bd_ref, ys_ref, wd_bf_ref):
    i = pl.program_id(0)
    last = nu_ref[0] - 1
    expert = be_ref[jnp.minimum(i, last)]
    prev_expert = be_ref[jnp.minimum(jnp.maximum(i - 1, 0), last)]

    @pl.when(jnp.logical_or(i == 0, expert != prev_expert))
    def _():
        wd_bf_ref[...] = wd_ref[0].astype(BF16)

    @pl.when(i < nu_ref[0])
    def _():
        x = xs_ref[...].astype(BF16)
        g = jnp.dot(x, wg_ref[0], preferred_element_type=F32) + bg_ref[0]
        u = jnp.dot(x, wl_ref[0], preferred_element_type=F32) + bl_ref[0]
        g = jnp.minimum(g, SWIGLU_LIMIT)
        u = jnp.clip(u, -SWIGLU_LIMIT, SWIGLU_LIMIT)
        act = g * jax.nn.sigmoid(SWIGLU_ALPHA * g) * (u + 1.0)
        ys_ref[...] = jnp.dot(act.astype(BF16), wd_bf_ref[...], preferred_element_type=F32) + bd_ref[0]

    @pl.when(pl.program_id(0) >= nu_ref[0])
    def _():
        ys_ref[...] = jnp.zeros_like(ys_ref)


def _experts(block_e, n_used, xs, wg, wl, bg, bl, wd, bd):
    cap, d = xs.shape
    f = wg.shape[2]
    tm = MOE_TILE
    nb = cap // tm

    def row_map(i, be, nu):
        return (jnp.minimum(i, nu[0] - 1), 0)

    def w_map(i, be, nu):
        return (be[jnp.minimum(i, nu[0] - 1)], 0, 0)

    return pl.pallas_call(
        _expert_kernel,
        out_shape=jax.ShapeDtypeStruct((cap, d), F32),
        grid_spec=pltpu.PrefetchScalarGridSpec(
            num_scalar_prefetch=2,
            grid=(nb,),
            in_specs=[pl.BlockSpec((tm, d), row_map),
                      pl.BlockSpec((1, d, f), w_map),
                      pl.BlockSpec((1, d, f), w_map),
                      pl.BlockSpec((1, 1, f), w_map),
                      pl.BlockSpec((1, 1, f), w_map),
                      pl.BlockSpec((1, f, d), w_map),
                      pl.BlockSpec((1, 1, d), w_map)],
            out_specs=pl.BlockSpec((tm, d), lambda i, be, nu: (i, 0)),
            scratch_shapes=[pltpu.VMEM((f, d), BF16)]),
        compiler_params=_cparams("arbitrary"),
        name="moe_experts",
    )(block_e, n_used, xs, wg, wl, bg, bl, wd, bd)


def _combine_kernel(slot_ref, ys_hbm, x_ref, gate_ref, m_ref, o_ref, buf, sem):
    tm = x_ref.shape[0]

    def row_copy(t, k):
        return pltpu.make_async_copy(ys_hbm.at[pl.ds(slot_ref[0, k, t], 1)], buf.at[k, pl.ds(t, 1)], sem)

    def issue(i, carry):
        for u in range(ROW_UNROLL):
            for k in range(TOP_K):
                row_copy(i * ROW_UNROLL + u, k).start(priority=k % 2)
        return carry

    lax.fori_loop(0, tm // ROW_UNROLL, issue, 0)

    def drain(i, carry):
        for u in range(ROW_UNROLL):
            for k in range(TOP_K):
                row_copy(i * ROW_UNROLL + u, k).wait()
        return carry

    lax.fori_loop(0, tm // ROW_UNROLL, drain, 0)
    gates = gate_ref[...]
    y = buf[0] * gates[:, 0:1]
    for k in range(1, TOP_K):
        y = y + buf[k] * gates[:, k:k + 1]
    o_ref[...] = x_ref[...] + m_ref[0][5:6] * y


def _combine(slots, ys, x_flat, gates, mod_l, n_tok, n_batch, seq):
    d = x_flat.shape[1]
    tm = slots.shape[2]
    tps = seq // tm
    return pl.pallas_call(
        _combine_kernel,
        out_shape=jax.ShapeDtypeStruct((n_tok, d), F32),
        grid=(n_tok // tm,),
        in_specs=[pl.BlockSpec((1, TOP_K, tm), lambda i: (i, 0, 0), memory_space=pltpu.SMEM),
                  pl.BlockSpec(memory_space=pl.ANY),
                  pl.BlockSpec((tm, d), lambda i: (i, 0)),
                  pl.BlockSpec((tm, LANE), lambda i: (i, 0)),
                  pl.BlockSpec((1, N_MOD, d), lambda i: (_group_of_tile(i, tps, n_batch), 0, 0))],
        out_specs=pl.BlockSpec((tm, d), lambda i: (i, 0)),
        scratch_shapes=[pltpu.VMEM((TOP_K, tm, d), F32), pltpu.SemaphoreType.DMA(())],
        compiler_params=_cparams("arbitrary"),
        name="moe_combine",
    )(slots, ys, x_flat, gates, mod_l)


def _retile_slots(slots, tm):
    nt, k, t = slots.shape
    return slots.reshape(nt, k, t // tm, tm).transpose(0, 2, 1, 3).reshape(nt * (t // tm), k, tm)


def _moe(h_flat, x_flat, n_tok, mod_l, rw_t_bf, rb, layer, wg, wl, bg, bl, wd, bd, n_batch, seq):
    idx, gates, counts = _route(h_flat, n_tok, rw_t_bf, rb)
    counts = counts[:, 0].astype(jnp.int32)
    padded = (counts + MOE_TILE - 1) // MOE_TILE * MOE_TILE
    pad_end = jnp.cumsum(padded)
    pad_start = pad_end - padded
    n_blocks = -(-(n_tok * TOP_K) // MOE_TILE) + N_EXPERTS
    cap = n_blocks * MOE_TILE
    block_row = jnp.arange(n_blocks, dtype=jnp.int32) * MOE_TILE
    block_e = jnp.minimum(jnp.sum(block_row[:, None] >= pad_end[None, :], axis=1), N_EXPERTS - 1).astype(jnp.int32)
    block_e = block_e + layer * N_EXPERTS
    n_used = (pad_end[-1:] // MOE_TILE).astype(jnp.int32)
    experts = idx[:, :TOP_K, :]
    start = jnp.sum(jnp.where(experts[..., None] == jnp.arange(N_EXPERTS), pad_start, 0), axis=-1)
    slots = start + idx[:, TOP_K:, :]
    slots = _retile_slots(slots, MOVE_TILE)
    zero_lo = jnp.concatenate([pad_start + counts, pad_end[-1:]]).astype(jnp.int32)
    zero_hi = jnp.concatenate([pad_end, jnp.full((1,), cap)]).astype(jnp.int32)
    xs = _dispatch(zero_lo, zero_hi, slots, h_flat, n_tok, cap)
    ys = _experts(block_e, n_used, xs, wg, wl, bg, bl, wd, bd)
    return _combine(slots, ys, x_flat, gates, mod_l, n_tok, n_batch, seq)


GDN_TILE = 256
HALO = SUBLANE


def _gdn_prep_kernel(q_ref, k_ref, v_ref, qp_ref, kp_ref, vp_ref, qn_ref, kn_ref, vn_ref, small_ref, w_ref,
                     a_ref, bias_ref, qkv_ref, gb_ref, xe_ref, *, lat_tiles, lat_tps, ctx_tps):
    i = pl.program_id(0)
    tm = q_ref.shape[0]
    in_lat = i < lat_tiles
    pos = jnp.where(in_lat, i % lat_tps, (i - lat_tiles) % ctx_tps)
    last = jnp.where(in_lat, lat_tps - 1, ctx_tps - 1)
    keep_prev = jnp.where(pos > 0, 1.0, 0.0)
    keep_next = jnp.where(pos < last, 1.0, 0.0)
    half = GDN_CONV // 2
    for g, (x_ref, p_ref, n_ref) in enumerate(((q_ref, qp_ref, qn_ref), (k_ref, kp_ref, kn_ref),
                                               (v_ref, vp_ref, vn_ref))):
        xe_ref[g, 0:HALO, :] = p_ref[...] * keep_prev
        xe_ref[g, HALO:HALO + tm, :] = x_ref[...]
        xe_ref[g, HALO + tm:2 * HALO + tm, :] = n_ref[...] * keep_next
        w = w_ref[:, g * GDN_W:(g + 1) * GDN_W]
        y = xe_ref[g, HALO - half:HALO - half + tm, :] * w[0:1]
        for j in range(1, GDN_CONV):
            y = y + xe_ref[g, HALO - half + j:HALO - half + j + tm, :] * w[j:j + 1]
        y = y * jax.nn.sigmoid(y)
        if g < 2:
            heads = []
            for h in range(GDN_HEADS):
                yh = y[:, h * GDN_HEAD_DIM:(h + 1) * GDN_HEAD_DIM]
                yh = yh * lax.rsqrt(jnp.sum(yh * yh, axis=-1, keepdims=True) + EPS)
                heads.append(yh * GDN_HEAD_DIM ** -0.5 if g == 0 else yh)
            y = jnp.concatenate(heads, axis=1)
        qkv_ref[:, g * GDN_W:(g + 1) * GDN_W] = y.astype(BF16)
    sm = small_ref[...]
    z = sm + bias_ref[...]
    softplus = jnp.maximum(z, 0.0) + jnp.log(1.0 + jnp.exp(-jnp.abs(z)))
    lane = lax.broadcasted_iota(jnp.int32, (1, LANE), 1)
    is_g = (lane >= SMALL_GA) & (lane < SMALL_GB)
    is_b = (lane >= SMALL_GB) & (lane < SMALL_GB + 2 * GDN_HEADS)
    gb_ref[...] = jnp.where(is_g, a_ref[...] * softplus, jnp.where(is_b, jax.nn.sigmoid(sm), 0.0))


def _gdn_prep(pg, small, conv_w, neg_a, dt_bias, n_batch, seq, ctx_len):
    n_tok = pg.shape[0]
    tm = GDN_TILE
    lat_tiles = n_batch * seq // tm
    per_tile = tm // HALO
    n_halo_blocks = n_tok // HALO

    def main(g):
        return pl.BlockSpec((tm, GDN_W), lambda i: (i, g))

    def prev(g):
        return pl.BlockSpec((HALO, GDN_W), lambda i: (jnp.maximum(i * per_tile - 1, 0), g))

    def nxt(g):
        return pl.BlockSpec((HALO, GDN_W), lambda i: (jnp.minimum((i + 1) * per_tile, n_halo_blocks - 1), g))

    const = lambda i: (0, 0)
    return pl.pallas_call(
        functools.partial(_gdn_prep_kernel, lat_tiles=lat_tiles, lat_tps=seq // tm, ctx_tps=ctx_len // tm),
        out_shape=(jax.ShapeDtypeStruct((n_tok, 3 * GDN_W), BF16), jax.ShapeDtypeStruct((n_tok, LANE), F32)),
        grid=(n_tok // tm,),
        in_specs=[main(0), main(1), main(2), prev(0), prev(1), prev(2), nxt(0), nxt(1), nxt(2),
                  pl.BlockSpec((tm, LANE), lambda i: (i, 0)),
                  pl.BlockSpec(conv_w.shape, const), pl.BlockSpec((1, LANE), const), pl.BlockSpec((1, LANE), const)],
        out_specs=(pl.BlockSpec((tm, 3 * GDN_W), lambda i: (i, 0)), pl.BlockSpec((tm, LANE), lambda i: (i, 0))),
        scratch_shapes=[pltpu.VMEM((3, tm + 2 * HALO, GDN_W), F32)],
        compiler_params=_cparams("arbitrary"),
        name="gdn_prep",
    )(*([pg] * 9), small, conv_w, neg_a, dt_bias)


GDN_SUB = 16
GDN_CHUNKS_PER_STEP = 4


def _bdot(a, b):
    return jnp.dot(a.astype(BF16), b.astype(BF16), preferred_element_type=F32)


def _dot_tn(a, b):
    return lax.dot_general(a, b, (((0,), (0,)), ((), ())), preferred_element_type=F32)


def _each(f, *lists):
    return [f(*args) for args in zip(*lists)]


def _unit_tri_solve(n, rhs, same_blk, eye):
    nd = _each(lambda a: jnp.where(same_blk, a, 0.0).astype(BF16), n)
    nl = _each(lambda a: jnp.where(same_blk, 0.0, a).astype(BF16), n)
    nd2 = _each(lambda a: jnp.dot(a, a, preferred_element_type=F32).astype(BF16), nd)
    nd4 = _each(lambda a: jnp.dot(a, a, preferred_element_type=F32).astype(BF16), nd2)
    nd8 = _each(lambda a: jnp.dot(a, a, preferred_element_type=F32).astype(BF16), nd4)
    m = _each(lambda a: eye - a.astype(F32), nd)
    m = _each(lambda a, b: a + _bdot(a, b), m, nd2)
    m = _each(lambda a, b: a + _bdot(a, b), m, nd4)
    dinv = _each(lambda a, b: (a + _bdot(a, b)).astype(BF16), m, nd8)
    p = _each(lambda a, b: jnp.dot(a, b, preferred_element_type=F32).astype(BF16), dinv, nl)
    p2 = _each(lambda a: jnp.dot(a, a, preferred_element_type=F32).astype(BF16), p)
    y = _each(lambda a, b: _bdot(a, b), dinv, rhs)
    y = _each(lambda a, b: a + _bdot(b, a), y, p2)
    return _each(lambda a, b: a - _bdot(b, a), y, p)


def _gdn_scan_kernel(x0_ref, gb0_ref, x1_ref, gb1_ref, o0_ref, o1_ref, state_ref, *, n_sub):
    assert GDN_CHUNK == 4 * GDN_SUB

    @pl.when(pl.program_id(1) == 0)
    def _():
        state_ref[...] = jnp.zeros_like(state_ref)

    n = GDN_CHUNK
    r = lax.broadcasted_iota(jnp.int32, (n, n), 0)
    c = lax.broadcasted_iota(jnp.int32, (n, n), 1)
    same_blk = (r // GDN_SUB) == (c // GDN_SUB)
    eye = (r == c).astype(F32)
    dirs = ((x0_ref, gb0_ref, o0_ref), (x1_ref, gb1_ref, o1_ref))
    heads = [(d, h) for d in range(2) for h in range(GDN_HEADS)]
    chunk_at = [list(range(n_sub)), list(range(n_sub - 1, -1, -1))]
    probs = [(d, h, chunk_at[d][s]) for s in range(n_sub) for d, h in heads]
    incl = [r >= c, r <= c]
    strict = [r > c, r < c]
    end_row = [n - 1, 0]

    gbv, gc_all, gc_t = {}, {}, {}
    for d in range(2):
        tri = incl[d].astype(BF16)
        for ch in range(n_sub):
            g = dirs[d][1][ch * n:(ch + 1) * n, :]
            hi = g.astype(BF16)
            rem = g - hi.astype(F32)
            mid = rem.astype(BF16)
            lo = (rem - mid.astype(F32)).astype(BF16)
            gc = (jnp.dot(tri, hi, preferred_element_type=F32) + jnp.dot(tri, mid, preferred_element_type=F32)
                  + jnp.dot(tri, lo, preferred_element_type=F32))
            gbv[d, ch], gc_all[d, ch], gc_t[d, ch] = g, gc, jnp.transpose(gc)

    def head_cols(d, part, h, ch):
        lo_col = part * GDN_W + h * GDN_HEAD_DIM
        return dirs[d][0][ch * n:(ch + 1) * n, lo_col:lo_col + GDN_HEAD_DIM]

    lane_g = [SMALL_GA + GDN_HEADS * d + h for d, h, ch in probs]
    lane_b = [SMALL_GB + GDN_HEADS * d + h for d, h, ch in probs]
    gcol = [gc_all[d, ch][:, lg:lg + 1] for (d, h, ch), lg in zip(probs, lane_g)]
    grow = [gc_t[d, ch][lg:lg + 1, :] for (d, h, ch), lg in zip(probs, lane_g)]
    gend = [gc_all[d, ch][end_row[d]:end_row[d] + 1, lg:lg + 1] for (d, h, ch), lg in zip(probs, lane_g)]
    beta = [gbv[d, ch][:, lb:lb + 1] for (d, h, ch), lb in zip(probs, lane_b)]
    q = [head_cols(d, 0, h, ch) for d, h, ch in probs]
    k = [head_cols(d, 1, h, ch) for d, h, ch in probs]
    v = [head_cols(d, 2, h, ch) for d, h, ch in probs]
    kf = _each(lambda a: a.astype(F32), k)
    dec = [jnp.where(incl[d], jnp.exp(jnp.where(incl[d], gc_ - gr_, 0.0)), 0.0)
           for (d, h, ch), gc_, gr_ in zip(probs, gcol, grow)]
    qk_kk = _each(lambda q_, k_: _dot_nt(jnp.concatenate([q_, k_], axis=0), k_), q, k)
    qkd = _each(lambda a, dec_: (a[:n] * dec_).astype(BF16), qk_kk, dec)
    nmat = [jnp.where(strict[d], b_ * a[n:] * dec_, 0.0) for (d, h, ch), b_, a, dec_ in zip(probs, beta, qk_kk, dec)]
    egc = _each(jnp.exp, gcol)
    rhs = _each(lambda v_, kf_, b_, e_: jnp.concatenate([v_.astype(F32) * b_, kf_ * (b_ * e_)], axis=1).astype(BF16),
                v, kf, beta, egc)
    uw = _unit_tri_solve(nmat, rhs, same_blk, eye)
    wq = _each(lambda uw_, q_, e_: jnp.concatenate([uw_[:, GDN_HEAD_DIM:], q_.astype(F32) * e_], axis=0).astype(BF16),
               uw, q, egc)
    k_dec = _each(lambda kf_, ge_, gc_: (kf_ * jnp.exp(ge_ - gc_)).astype(BF16), kf, gend, gcol)
    g_end = _each(jnp.exp, gend)

    state = [state_ref[i] for i in range(len(heads))]
    for s in range(n_sub):
        sl = slice(s * len(heads), (s + 1) * len(heads))
        s_bf = _each(lambda a: a.astype(BF16), state)
        ws_qs = _each(lambda a, s_: jnp.dot(a, s_, preferred_element_type=F32), wq[sl], s_bf)
        v_new = _each(lambda uw_, a: (uw_[:, :GDN_HEAD_DIM] - a[:n]).astype(BF16), uw[sl], ws_qs)
        o_local = _each(lambda a, vn_: jnp.dot(a, vn_, preferred_element_type=F32), qkd[sl], v_new)
        s_add = _each(_dot_tn, k_dec[sl], v_new)
        state = _each(lambda ge_, st_, add_: ge_ * st_ + add_, g_end[sl], state, s_add)
        for (d, h, ch), a, ol in zip(probs[sl], ws_qs, o_local):
            dirs[d][2][ch * n:(ch + 1) * n, h * GDN_HEAD_DIM:(h + 1) * GDN_HEAD_DIM] = a[n:] + ol
    for i in range(len(heads)):
        state_ref[i] = state[i]


def _gdn_scan(qkv, gbeta, n_batch, seq, ctx_len):
    n_tok = qkv.shape[0]
    n = GDN_CHUNK * GDN_CHUNKS_PER_STEP
    lat_c, ctx_c = seq // n, ctx_len // n
    ctx0 = n_batch * seq // n

    def fwd(b, j):
        return (jnp.where(j < ctx_c, ctx0 + b * ctx_c + j, b * lat_c + (j - ctx_c)), 0)

    def bwd(b, j):
        return (jnp.where(j < ctx_c, ctx0 + b * ctx_c + (ctx_c - 1 - j), b * lat_c + (lat_c - 1 - (j - ctx_c))), 0)

    return pl.pallas_call(
        functools.partial(_gdn_scan_kernel, n_sub=GDN_CHUNKS_PER_STEP),
        out_shape=(jax.ShapeDtypeStruct((n_tok, GDN_W), F32), jax.ShapeDtypeStruct((n_tok, GDN_W), F32)),
        grid=(n_batch, ctx_c + lat_c),
        in_specs=[pl.BlockSpec((n, 3 * GDN_W), fwd), pl.BlockSpec((n, LANE), fwd),
                  pl.BlockSpec((n, 3 * GDN_W), bwd), pl.BlockSpec((n, LANE), bwd)],
        out_specs=(pl.BlockSpec((n, GDN_W), fwd), pl.BlockSpec((n, GDN_W), bwd)),
        scratch_shapes=[pltpu.VMEM((2 * GDN_HEADS, GDN_HEAD_DIM, GDN_HEAD_DIM), F32)],
        compiler_params=_cparams("arbitrary", "arbitrary"),
        name="gdn_scan",
    )(qkv, gbeta, qkv, gbeta)


def _pad_heads(w, n_heads, width, start, take):
    lead = w.shape[:-1]
    w = w.reshape(lead + (n_heads, width))[..., start:start + take]
    w = jnp.pad(w, [(0, 0)] * len(lead) + [(0, 0), (0, MLA_HEAD_PAD - take)])
    return w.reshape(lead + (n_heads * MLA_HEAD_PAD,))


def kernel(x, c, ctx, c_ctx, w_mod, b_mod, norm1, norm2, w_in, w_out, swa_q_norm, swa_k_norm, swa_sink, gdn_conv,
           gdn_a_log, gdn_dt_bias, gdn_out_norm, mla_q_a_norm, mla_w_uq, mla_kv_a_norm, mla_w_ukv, mla_q_norm,
           mla_k_norm, router_w, router_b, exp_w_gu, exp_b_gu, exp_w_dn, exp_b_dn):
    b, s, d = x.shape
    cl = ctx.shape[1]
    depth = w_mod.shape[0]
    n_lat, n_ctx = b * s, b * cl
    assert s % TOK_TILE == 0 and n_ctx % TOK_TILE == 0 and s % MOVE_TILE == 0 and n_ctx % ROUTE_TILE == 0
    assert cl % MLA_Q_TILE == 0 and cl % ATTN_BLOCK == 0 and n_lat % cl == 0

    n_rows = -(-(b + 1) // SUBLANE) * SUBLANE
    c_rows = jnp.concatenate([c, c_ctx[None, :], jnp.zeros((n_rows - b - 1, d), F32)], axis=0)
    mod_all = _modulation(c_rows, w_mod, b_mod)

    perm = _in_proj_perm()
    tabs = _rope_tables(s)
    head_rows = np.concatenate([h * SWA_HEAD_DIM + np.arange(SWA_HEAD_DIM) for h in SWA_HEAD_ORDER])

    wg_all, wl_all = _split_glu_columns(exp_w_gu)
    n_le = depth * N_EXPERTS
    bg_all = exp_b_gu[..., 0::2].reshape(n_le, 1, -1)
    bl_all = exp_b_gu[..., 1::2].reshape(n_le, 1, -1)
    wd_all = exp_w_dn.reshape((n_le,) + exp_w_dn.shape[2:])
    bd_all = exp_b_dn.reshape(n_le, 1, d)
    x_lat, x_ctx, ctx_tile0 = x.reshape(n_lat, d), ctx.reshape(n_ctx, d), 0
    for l in range(depth):
        with_ctx = l < depth - 1
        n_tok = n_lat + n_ctx if with_ctx else n_lat
        mod_l = mod_all[l]
        w_in_bf = jnp.pad(w_in[l], ((0, 0), (0, 1)))[:, perm].astype(BF16)
        consts = (jnp.tile(swa_q_norm[l], SWA_HEADS)[None], jnp.tile(swa_k_norm[l], SWA_KV_HEADS)[None],
                  mla_q_a_norm[l][None], mla_kv_a_norm[l][None],
                  _pad_heads(mla_w_uq[l], MLA_HEADS, MLA_QK, 0, MLA_QK).astype(BF16),
                  _pad_heads(mla_w_ukv[l], MLA_HEADS, MLA_NOPE + MLA_V, 0, MLA_NOPE).astype(BF16),
                  _pad_heads(mla_w_ukv[l], MLA_HEADS, MLA_NOPE + MLA_V, MLA_NOPE, MLA_V).astype(BF16),
                  _pad_heads(jnp.tile(mla_q_norm[l], MLA_HEADS), MLA_HEADS, MLA_QK, 0, MLA_QK)[None],
                  _pad_heads(jnp.tile(mla_k_norm[l], MLA_HEADS), MLA_HEADS, MLA_QK, 0, MLA_QK)[None])
        pg, small, qa, kva, qm, km, vm = _in_projection(x_lat, x_ctx, ctx_tile0, n_lat + n_ctx, norm1[l], mod_l,
                                                        w_in_bf, tabs, consts, b, s)
        o_a = _swa_attention(swa_sink[l], qa, kva, b, s, cl, with_ctx)
        o_c = _mla_attention(qm, km, vm, b, s, cl, with_ctx)
        lanes_g = slice(SMALL_GA, SMALL_GA + 2 * GDN_HEADS)
        neg_a = jnp.zeros((1, LANE), F32).at[0, lanes_g].set(-jnp.exp(gdn_a_log[l]).reshape(-1))
        dt_bias = jnp.zeros((1, LANE), F32).at[0, lanes_g].set(gdn_dt_bias[l].reshape(-1))
        qkv, gbeta = _gdn_prep(pg, small, gdn_conv[l], neg_a, dt_bias, b, s, cl)
        o_fwd, o_bwd = _gdn_scan(qkv, gbeta, b, s, cl)

        wa = w_out[l][:SWA_Q][head_rows].astype(BF16)
        wb = w_out[l][SWA_Q:SWA_Q + GDN_W].astype(BF16)
        wc = w_out[l][SWA_Q + GDN_W:].reshape(MLA_HEADS, MLA_V, d)
        wc = jnp.pad(wc, ((0, 0), (0, MLA_HEAD_PAD - MLA_V), (0, 0))).reshape(MLA_HEADS * MLA_HEAD_PAD, d).astype(BF16)
        x_mid, h2 = _out_projection(o_a, o_fwd, o_bwd, pg, o_c, x_lat, x_ctx, ctx_tile0, wa, wb, wc,
                                    jnp.tile(gdn_out_norm[l], GDN_HEADS)[None], norm2[l], mod_l, n_tok, b, s)

        x_next = _moe(h2, x_mid, n_tok, mod_l, router_w[l].T.astype(BF16), router_b[l], l, wg_all, wl_all,
                      bg_all, bl_all, wd_all, bd_all, b, s)
        x_lat, x_ctx, ctx_tile0 = x_next, x_next, n_lat // TOK_TILE
    return x_lat[:n_lat].reshape(b, s, d)
```

```python
import functools

import jax
import jax.numpy as jnp
import numpy as np
from jax import lax
from jax.experimental import pallas as pl
from jax.experimental.pallas import tpu as pltpu

F32 = jnp.float32
BF16 = jnp.bfloat16

GRID_W = 64
N_MOD = 6
EPS = 1e-6
ROPE_BASE = 10000.0
ATTN_BLOCK = 128

SWA_HEADS = 4
SWA_KV_HEADS = 2
SWA_HEAD_DIM = 64
SWA_WINDOW = 128

GDN_HEADS = 4
GDN_HEAD_DIM = 128
GDN_CONV = 5
GDN_CHUNK = 64

MLA_HEADS = 4
MLA_Q_RANK = 256
MLA_KV_RANK = 128
MLA_NOPE = 64
MLA_ROPE = 32
MLA_V = 64
MLA_QK = MLA_NOPE + MLA_ROPE

N_EXPERTS = 32
TOP_K = 4
SWIGLU_LIMIT = 7.0
SWIGLU_ALPHA = 1.702

SWA_Q = SWA_HEADS * SWA_HEAD_DIM
SWA_KV = SWA_KV_HEADS * SWA_HEAD_DIM
GDN_W = GDN_HEADS * GDN_HEAD_DIM
MLA_O = MLA_HEADS * MLA_V
D_MIX = SWA_Q + GDN_W + MLA_O
IN_SPLITS = (SWA_Q, SWA_KV, SWA_KV, GDN_W, GDN_W, GDN_W, GDN_W, 2 * GDN_HEADS, 2 * GDN_HEADS, MLA_Q_RANK,
             MLA_KV_RANK, MLA_ROPE)
N_IN = sum(IN_SPLITS)

LANE = 128
SUBLANE = 8
VMEM_LIMIT = 56 * 1024 * 1024

TOK_TILE = 512
MOE_TILE = 512
ROUTE_TILE = 512
MOVE_TILE = 512
ROW_UNROLL = 4
ROW_GROUP = 256
MLA_Q_TILE = 256
MOD_COL_TILE = 1536

COL_AQ, COL_AK, COL_AV = 0, 256, 384
COL_G = 512
COL_CQ, COL_CKV, COL_SMALL = 2560, 2816, 2944
N_IN_PAD = 3072
SMALL_KR, SMALL_GA, SMALL_GB = 0, MLA_ROPE, MLA_ROPE + 2 * GDN_HEADS
SWA_HEAD_ORDER = (0, 2, 1, 3)
MLA_HEAD_PAD = 128


def _cparams(*sem):
    return pltpu.CompilerParams(dimension_semantics=sem, vmem_limit_bytes=VMEM_LIMIT)


def _in_proj_perm():
    old = np.cumsum((0,) + IN_SPLITS)
    o_aq, o_ak, o_av, o_gq, o_gk, o_gv, o_gz, o_ga, o_gb, o_cq, o_ckv, o_ckr = old[:-1]
    perm = np.full((N_IN_PAD,), N_IN, np.int32)
    perm[COL_AQ:COL_AQ + SWA_Q] = np.concatenate(
        [o_aq + h * SWA_HEAD_DIM + np.arange(SWA_HEAD_DIM) for h in SWA_HEAD_ORDER])
    perm[COL_AK:COL_G + 4 * GDN_W] = np.arange(o_ak, o_ga)
    perm[COL_CQ:COL_CQ + MLA_Q_RANK] = o_cq + np.arange(MLA_Q_RANK)
    perm[COL_CKV:COL_CKV + MLA_KV_RANK] = o_ckv + np.arange(MLA_KV_RANK)
    perm[COL_SMALL + SMALL_KR:COL_SMALL + SMALL_KR + MLA_ROPE] = o_ckr + np.arange(MLA_ROPE)
    perm[COL_SMALL + SMALL_GA:COL_SMALL + SMALL_GA + 2 * GDN_HEADS] = o_ga + np.arange(2 * GDN_HEADS)
    perm[COL_SMALL + SMALL_GB:COL_SMALL + SMALL_GB + 2 * GDN_HEADS] = o_gb + np.arange(2 * GDN_HEADS)
    return perm


def _mod_kernel(c_ref, w_ref, b_ref, o_ref):
    a = c_ref[...]
    a = a * jax.nn.sigmoid(a)
    o_ref[0] = jnp.dot(a.astype(BF16), w_ref[0].astype(BF16), preferred_element_type=F32) + b_ref[0]


def _modulation(c_rows, w_mod, b_mod):
    nl, d, n = w_mod.shape
    r = c_rows.shape[0]
    tn = MOD_COL_TILE
    out = pl.pallas_call(
        _mod_kernel,
        out_shape=jax.ShapeDtypeStruct((nl, r, n), F32),
        grid=(nl, n // tn),
        in_specs=[pl.BlockSpec((r, d), lambda l, j: (0, 0)),
                  pl.BlockSpec((1, d, tn), lambda l, j: (l, 0, j)),
                  pl.BlockSpec((1, 1, tn), lambda l, j: (l, 0, j))],
        out_specs=pl.BlockSpec((1, r, tn), lambda l, j: (l, 0, j)),
        compiler_params=_cparams("arbitrary", "arbitrary"),
        name="modulation",
    )(c_rows, w_mod, b_mod.reshape(nl, 1, n))
    return out.reshape(nl, r, N_MOD, d)


def _modulated_norm(x, gain, shift, scale):
    y = x * lax.rsqrt(jnp.mean(x * x, axis=-1, keepdims=True) + EPS)
    return (y * gain) * (1.0 + scale) + shift


def _group_of_tile(i, tiles_per_seq, n_batch):
    return jnp.minimum(i // tiles_per_seq, n_batch)


def _group_sumsq(x, group):
    assert group in (LANE, LANE // 2)
    x2 = x * x
    lane = lax.broadcasted_iota(jnp.int32, (1, LANE), 1)
    lower = lane < LANE // 2
    cols = []
    for c in range(x.shape[1] // LANE):
        blk = x2[:, c * LANE:(c + 1) * LANE]
        if group == LANE:
            cols.append(jnp.broadcast_to(jnp.sum(blk, axis=-1, keepdims=True), blk.shape))
        else:
            s_lo = jnp.sum(jnp.where(lower, blk, 0.0), axis=-1, keepdims=True)
            s_hi = jnp.sum(jnp.where(lower, 0.0, blk), axis=-1, keepdims=True)
            cols.append(jnp.where(lower, s_lo, s_hi))
    return cols[0] if len(cols) == 1 else jnp.concatenate(cols, axis=1)


def _swap_pairs(x, half):
    w = x.shape[1]
    lane = lax.broadcasted_iota(jnp.int32, (1, w), 1)
    first = (lane % (2 * half)) < half
    return jnp.where(first, pltpu.roll(x, w - half, axis=1), pltpu.roll(x, half, axis=1))


def _inproj_kernel(xl_ref, xc_ref, g_ref, m_ref, w_ref, ra_c_ref, ra_s_ref, rc_c_ref, rc_s_ref,
                   gaq_ref, gak_ref, gcq_ref, gckv_ref, wuq_ref, wuk_ref, wuv_ref, gmq_ref, gmk_ref,
                   pg_ref, small_ref, qa_ref, kva_ref, qm_ref, km_ref, vm_ref, *, n_lat_tiles):
    m = m_ref[0]
    for i in range(xl_ref.shape[0] // ROW_GROUP):
        rows = pl.ds(i * ROW_GROUP, ROW_GROUP)
        x = _tile_rows(xl_ref, xc_ref, rows, n_lat_tiles)
        h = _modulated_norm(x, g_ref[...], m[0:1], m[1:2]).astype(BF16)
        pa = jnp.dot(h, w_ref[:, COL_AQ:COL_G], preferred_element_type=F32)
        pc = jnp.dot(h, w_ref[:, COL_CQ:], preferred_element_type=F32)
        small_ref[rows, :] = pc[:, MLA_Q_RANK + MLA_KV_RANK:]
        _attention_prep(rows, pa, pc, ra_c_ref, ra_s_ref, rc_c_ref, rc_s_ref, gaq_ref, gak_ref,
                        gcq_ref, gckv_ref, wuq_ref, wuk_ref, wuv_ref, gmq_ref, gmk_ref,
                        qa_ref, kva_ref, qm_ref, km_ref, vm_ref)
        pg_ref[rows, :] = jnp.dot(h, w_ref[:, COL_G:COL_CQ], preferred_element_type=F32)


def _attention_prep(rows, pa, pc, ra_c_ref, ra_s_ref, rc_c_ref, rc_s_ref, gaq_ref, gak_ref,
                    gcq_ref, gckv_ref, wuq_ref, wuk_ref, wuv_ref, gmq_ref, gmk_ref,
                    qa_ref, kva_ref, qm_ref, km_ref, vm_ref):
    small = pc[:, MLA_Q_RANK + MLA_KV_RANK:]

    aq = pa[:, 0:SWA_Q]
    ak = pa[:, SWA_Q:SWA_Q + SWA_KV]
    av = pa[:, SWA_Q + SWA_KV:SWA_Q + 2 * SWA_KV]
    ca, sa = ra_c_ref[rows, :], ra_s_ref[rows, :]
    qn = aq * lax.rsqrt(_group_sumsq(aq, SWA_HEAD_DIM) * (1.0 / SWA_HEAD_DIM) + EPS) * gaq_ref[...]
    kn = ak * lax.rsqrt(_group_sumsq(ak, SWA_HEAD_DIM) * (1.0 / SWA_HEAD_DIM) + EPS) * gak_ref[...]
    qn = qn * jnp.concatenate([ca, ca], axis=1) + _swap_pairs(qn, SWA_HEAD_DIM // 4) * jnp.concatenate([sa, sa], axis=1)
    kn = kn * ca + _swap_pairs(kn, SWA_HEAD_DIM // 4) * sa
    qa_ref[rows, :] = (qn * SWA_HEAD_DIM ** -0.5).astype(BF16)
    kva_ref[rows, :] = jnp.concatenate([kn, av], axis=1).astype(BF16)

    cq = pc[:, 0:MLA_Q_RANK]
    ckv = pc[:, MLA_Q_RANK:MLA_Q_RANK + MLA_KV_RANK]
    cqn = cq * lax.rsqrt(jnp.mean(cq * cq, axis=-1, keepdims=True) + EPS) * gcq_ref[...]
    ckvn = (ckv * lax.rsqrt(jnp.mean(ckv * ckv, axis=-1, keepdims=True) + EPS) * gckv_ref[...]).astype(BF16)
    q = jnp.dot(cqn.astype(BF16), wuq_ref[...], preferred_element_type=F32)
    k = jnp.dot(ckvn, wuk_ref[...], preferred_element_type=F32)
    v = jnp.dot(ckvn, wuv_ref[...], preferred_element_type=F32)
    lane = lax.broadcasted_iota(jnp.int32, (1, LANE), 1)
    kpe = jnp.where((lane >= MLA_NOPE) & (lane < MLA_QK), pltpu.roll(small, MLA_NOPE, axis=1), 0.0)
    k = k + jnp.concatenate([kpe] * MLA_HEADS, axis=1)
    q = q * lax.rsqrt(_group_sumsq(q, MLA_HEAD_PAD) * (1.0 / MLA_QK) + EPS) * gmq_ref[...]
    k = k * lax.rsqrt(_group_sumsq(k, MLA_HEAD_PAD) * (1.0 / MLA_QK) + EPS) * gmk_ref[...]
    cc = jnp.concatenate([rc_c_ref[rows, :]] * MLA_HEADS, axis=1)
    sc = jnp.concatenate([rc_s_ref[rows, :]] * MLA_HEADS, axis=1)
    q = q * cc + _swap_pairs(q, MLA_ROPE // 4) * sc
    k = k * cc + _swap_pairs(k, MLA_ROPE // 4) * sc
    qm_ref[rows, :] = (q * MLA_QK ** -0.5).astype(BF16)
    km_ref[rows, :] = k.astype(BF16)
    vm_ref[rows, :] = v.astype(BF16)


def _token_sources(x_lat, x_ctx, ctx_tile0, n_lat_tiles, tm):
    d = x_lat.shape[1]
    return [pl.BlockSpec((tm, d), lambda i: (jnp.minimum(i, n_lat_tiles - 1), 0)),
            pl.BlockSpec((tm, d), lambda i: (ctx_tile0 + jnp.maximum(i - n_lat_tiles, 0), 0))]


def _tile_rows(xl_ref, xc_ref, rows, n_lat_tiles):
    tile = jnp.zeros((ROW_GROUP, 1), jnp.int32) + pl.program_id(0)
    return jnp.where(tile < n_lat_tiles, xl_ref[rows, :], xc_ref[rows, :])


def _in_projection(x_lat, x_ctx, ctx_tile0, n_tok, gain, mod_l, w_bf, tabs, consts, n_batch, seq):
    d = x_lat.shape[1]
    tm = TOK_TILE
    tps = seq // tm
    n_lat_tiles = n_batch * tps
    row = lambda i: (i, 0)
    const = lambda i: (0, 0)
    tab = lambda i: (jnp.where(i < n_lat_tiles, i % tps, tps), 0)
    wide = MLA_HEADS * MLA_HEAD_PAD
    in_specs = _token_sources(x_lat, x_ctx, ctx_tile0, n_lat_tiles, tm)
    in_specs += [pl.BlockSpec((1, d), const),
                pl.BlockSpec((1, N_MOD, d), lambda i: (_group_of_tile(i, tps, n_batch), 0, 0)),
                pl.BlockSpec(w_bf.shape, const)]
    in_specs += [pl.BlockSpec((tm, LANE), tab)] * 4
    in_specs += [pl.BlockSpec(a.shape, const) for a in consts]
    widths = (4 * GDN_W, LANE, SWA_Q, 2 * SWA_KV, wide, wide, wide)
    dtypes = (F32, F32, BF16, BF16, BF16, BF16, BF16)
    return pl.pallas_call(
        functools.partial(_inproj_kernel, n_lat_tiles=n_lat_tiles),
        out_shape=tuple(jax.ShapeDtypeStruct((n_tok, w), t) for w, t in zip(widths, dtypes)),
        grid=(n_tok // tm,),
        in_specs=in_specs,
        out_specs=tuple(pl.BlockSpec((tm, w), row) for w in widths),
        compiler_params=_cparams("arbitrary"),
        name="in_projection",
    )(x_lat, x_ctx, gain.reshape(1, d), mod_l, w_bf, *tabs, *consts)


def _rope_tables(seq):
    t = jnp.arange(seq)
    row = (t // GRID_W).astype(F32)
    col = (t % GRID_W).astype(F32)

    def cos_sin(rot_dim):
        n_freq = rot_dim // 4
        freq = ROPE_BASE ** (-jnp.arange(n_freq, dtype=F32) / n_freq)
        ar, ac = row[:, None] * freq, col[:, None] * freq
        c = jnp.concatenate([jnp.cos(ar), jnp.cos(ar), jnp.cos(ac), jnp.cos(ac)], axis=1)
        s = jnp.concatenate([-jnp.sin(ar), jnp.sin(ar), -jnp.sin(ac), jnp.sin(ac)], axis=1)
        return c, s

    ca, sa = cos_sin(SWA_HEAD_DIM)
    ca, sa = jnp.tile(ca, (1, 2)), jnp.tile(sa, (1, 2))
    cc, sc = cos_sin(MLA_ROPE)
    ones_l = jnp.ones((seq, MLA_NOPE), F32)
    ones_r = jnp.ones((seq, MLA_HEAD_PAD - MLA_QK), F32)
    cc = jnp.concatenate([ones_l, cc, ones_r], axis=1)
    sc = jnp.concatenate([0 * ones_l, sc, 0 * ones_r], axis=1)
    ident_c = jnp.ones((TOK_TILE, LANE), F32)
    ident_s = jnp.zeros((TOK_TILE, LANE), F32)
    return tuple(jnp.concatenate([a, i], axis=0) for a, i in ((ca, ident_c), (sa, ident_s), (cc, ident_c), (sc, ident_s)))


def _dot_nt(a, b):
    return lax.dot_general(a, b, (((1,), (1,)), ((), ())), preferred_element_type=F32)


def _swa_heads(sink_ref, q, k_list, v_list, mask_list, o_ref):
    lane = lax.broadcasted_iota(jnp.int32, (1, LANE), 1)
    lower = lane < SWA_HEAD_DIM

    def scores(i):
        cgrp, half = divmod(i, 2)
        qc = q[:, cgrp * LANE:(cgrp + 1) * LANE]
        qh = jnp.where(lower if half == 0 else ~lower, qc, jnp.zeros_like(qc))
        out = []
        for kb, mk in zip(k_list, mask_list):
            s = _dot_nt(qh, kb)
            out.append(s if mk is None else jnp.where(mk, s, -jnp.inf))
        return out

    results = []
    pending = scores(0)
    for i in range(SWA_HEADS):
        upcoming = scores(i + 1) if i + 1 < SWA_HEADS else None
        sink = sink_ref[SWA_HEAD_ORDER[i]]
        m = jnp.full((q.shape[0], 1), sink, F32)
        for s in pending:
            m = jnp.maximum(m, jnp.max(s, axis=-1, keepdims=True))
        l = jnp.exp(sink - m)
        acc = jnp.zeros((q.shape[0], LANE), F32)
        for s, vb in zip(pending, v_list):
            p = jnp.exp(s - m)
            l = l + jnp.sum(p, axis=-1, keepdims=True)
            acc = acc + jnp.dot(p.astype(BF16), vb, preferred_element_type=F32)
        results.append(acc * (1.0 / l))
        pending = upcoming
    cols = [jnp.where(lower, results[2 * c], results[2 * c + 1]) for c in range(2)]
    o_ref[...] = jnp.concatenate(cols, axis=1).astype(o_ref.dtype)


SWA_Q_BLOCKS = 2
SWA_KEY_OFFSETS = tuple(range(-1, SWA_Q_BLOCKS + 1))


def _swa_kernel(sink_ref, q_ref, *refs, n_blocks):
    kv_refs, kvc_ref, o_ref = refs[:len(SWA_KEY_OFFSETS)], refs[-2], refs[-1]
    j = pl.program_id(1)
    n_steps = n_blocks // SWA_Q_BLOCKS
    kc, vc = kvc_ref[:, 0:SWA_KV], kvc_ref[:, SWA_KV:]

    @pl.when(j < n_steps)
    def _():
        rows = SWA_Q_BLOCKS * ATTN_BLOCK
        r = lax.broadcasted_iota(jnp.int32, (rows, ATTN_BLOCK), 0)
        c = lax.broadcasted_iota(jnp.int32, (rows, ATTN_BLOCK), 1)
        masks = []
        for off in SWA_KEY_OFFSETS:
            blk = j * SWA_Q_BLOCKS + off
            in_seq = (blk >= 0) & (blk < n_blocks)
            masks.append(jnp.abs(off * ATTN_BLOCK + c - r) <= jnp.where(in_seq, SWA_WINDOW, -1))
        ks = [ref[:, 0:SWA_KV] for ref in kv_refs] + [kc]
        vs = [ref[:, SWA_KV:] for ref in kv_refs] + [vc]
        _swa_heads(sink_ref, q_ref[...], ks, vs, masks + [None], o_ref)

    @pl.when(j >= n_steps)
    def _():
        _swa_heads(sink_ref, q_ref[...], [kc], [vc], [None], o_ref)


def _swa_attention(sink, qa, kva, n_batch, seq, ctx_len, with_ctx):
    rows = SWA_Q_BLOCKS * ATTN_BLOCK
    nb = seq // ATTN_BLOCK
    nq = seq // rows
    ncq = ctx_len // rows
    n_lat = n_batch * seq
    steps = nq + (ncq if with_ctx else 0)
    n_tok = n_lat + (n_batch * ctx_len if with_ctx else 0)

    def q_map(b, j, s):
        return (jnp.where(j < nq, b * nq + j, n_lat // rows + b * ncq + (j - nq)), 0)

    def kv_map(off):
        def f(b, j, s):
            jj = jnp.clip(jnp.minimum(j, nq - 1) * SWA_Q_BLOCKS + off, 0, nb - 1)
            return (b * nb + jj, 0)
        return f

    return pl.pallas_call(
        functools.partial(_swa_kernel, n_blocks=nb),
        out_shape=jax.ShapeDtypeStruct((n_tok, SWA_Q), BF16),
        grid_spec=pltpu.PrefetchScalarGridSpec(
            num_scalar_prefetch=1,
            grid=(n_batch, steps),
            in_specs=[pl.BlockSpec((rows, SWA_Q), q_map)]
            + [pl.BlockSpec((ATTN_BLOCK, 2 * SWA_KV), kv_map(off)) for off in SWA_KEY_OFFSETS]
            + [pl.BlockSpec((ctx_len, 2 * SWA_KV), lambda b, j, s: (n_lat // ctx_len + b, 0))],
            out_specs=pl.BlockSpec((rows, SWA_Q), q_map)),
        compiler_params=_cparams("arbitrary", "arbitrary"),
        name="swa_attention",
    )(sink, qa, *([kva] * len(SWA_KEY_OFFSETS)), kva)


def _mla_heads(q_ref, k_refs, v_refs, o_ref):
    def lanes(h):
        return slice(h * MLA_HEAD_PAD, (h + 1) * MLA_HEAD_PAD)

    def scores(h):
        return [_dot_nt(q_ref[:, lanes(h)], k_ref[:, lanes(h)]) for k_ref in k_refs]

    pending = scores(0)
    for h in range(MLA_HEADS):
        upcoming = scores(h + 1) if h + 1 < MLA_HEADS else None
        m = jnp.max(pending[0], axis=-1, keepdims=True)
        for s in pending[1:]:
            m = jnp.maximum(m, jnp.max(s, axis=-1, keepdims=True))
        l = jnp.zeros_like(m)
        acc = jnp.zeros((m.shape[0], MLA_HEAD_PAD), F32)
        for s, v_ref in zip(pending, v_refs):
            p = jnp.exp(s - m)
            l = l + jnp.sum(p, axis=-1, keepdims=True)
            acc = acc + jnp.dot(p.astype(BF16), v_ref[:, lanes(h)], preferred_element_type=F32)
        o_ref[:, lanes(h)] = (acc * (1.0 / l)).astype(o_ref.dtype)
        pending = upcoming


def _mla_kernel(q_ref, kl_ref, vl_ref, kc_ref, vc_ref, o_ref, *, n_lat_steps):
    j = pl.program_id(1)

    @pl.when(j < n_lat_steps)
    def _():
        _mla_heads(q_ref, [kl_ref, kc_ref], [vl_ref, vc_ref], o_ref)

    @pl.when(j >= n_lat_steps)
    def _():
        _mla_heads(q_ref, [kc_ref], [vc_ref], o_ref)


def _mla_attention(qm, km, vm, n_batch, seq, ctx_len, with_ctx):
    tq = MLA_Q_TILE
    nq = seq // tq
    ncq = ctx_len // tq
    n_lat = n_batch * seq
    steps = nq + (ncq if with_ctx else 0)
    n_tok = n_lat + (n_batch * ctx_len if with_ctx else 0)
    wide = MLA_HEADS * MLA_HEAD_PAD

    def q_map(b, j):
        return (jnp.where(j < nq, b * nq + j, n_lat // tq + b * ncq + (j - nq)), 0)

    lat_map = lambda b, j: (b, 0)
    ctx_map = lambda b, j: (n_lat // ctx_len + b, 0)
    return pl.pallas_call(
        functools.partial(_mla_kernel, n_lat_steps=nq),
        out_shape=jax.ShapeDtypeStruct((n_tok, wide), BF16),
        grid=(n_batch, steps),
        in_specs=[pl.BlockSpec((tq, wide), q_map),
                  pl.BlockSpec((seq, wide), lat_map),
                  pl.BlockSpec((seq, wide), lat_map),
                  pl.BlockSpec((ctx_len, wide), ctx_map),
                  pl.BlockSpec((ctx_len, wide), ctx_map)],
        out_specs=pl.BlockSpec((tq, wide), q_map),
        compiler_params=_cparams("arbitrary", "arbitrary"),
        name="mla_attention",
    )(qm, km, vm, km, vm)


def _outproj_kernel(oa_ref, of_ref, ob_ref, z_ref, oc_ref, xl_ref, xc_ref, wa_ref, wb_ref, wc_ref, gg_ref, g_ref, m_ref,
                    xo_ref, h_ref, *, n_lat_tiles):
    m = m_ref[0]
    groups = [pl.ds(i * ROW_GROUP, ROW_GROUP) for i in range(xl_ref.shape[0] // ROW_GROUP)]
    mixed = []
    for rows in groups:
        ob = of_ref[rows, :] + ob_ref[rows, :]
        z = z_ref[rows, :]
        heads = []
        for h in range(GDN_HEADS):
            oh = ob[:, h * GDN_HEAD_DIM:(h + 1) * GDN_HEAD_DIM]
            heads.append(oh * lax.rsqrt(jnp.mean(oh * oh, axis=-1, keepdims=True) + EPS))
        gated = jnp.concatenate(heads, axis=1) * gg_ref[...] * (z * jax.nn.sigmoid(z))
        mixed.append(jnp.dot(oa_ref[rows, :], wa_ref[...], preferred_element_type=F32)
                     + jnp.dot(gated.astype(BF16), wb_ref[...], preferred_element_type=F32)
                     + jnp.dot(oc_ref[rows, :], wc_ref[...], preferred_element_type=F32))
    for rows, o in zip(groups, mixed):
        xn = _tile_rows(xl_ref, xc_ref, rows, n_lat_tiles) + m[2:3] * o
        xo_ref[rows, :] = xn
        h_ref[rows, :] = _modulated_norm(xn, g_ref[...], m[3:4], m[4:5])


def _out_projection(o_a, o_fwd, o_bwd, pg, o_c, x_lat, x_ctx, ctx_tile0, wa, wb, wc, gdn_gain, gain2, mod_l, n_tok,
                    n_batch, seq):
    d = x_lat.shape[1]
    tm = TOK_TILE
    tps = seq // tm
    n_lat_tiles = n_batch * tps
    row = lambda i: (i, 0)
    const = lambda i: (0, 0)
    return pl.pallas_call(
        functools.partial(_outproj_kernel, n_lat_tiles=n_lat_tiles),
        out_shape=(jax.ShapeDtypeStruct((n_tok, d), F32), jax.ShapeDtypeStruct((n_tok, d), F32)),
        grid=(n_tok // tm,),
        in_specs=[pl.BlockSpec((tm, o_a.shape[1]), row),
                  pl.BlockSpec((tm, GDN_W), row),
                  pl.BlockSpec((tm, GDN_W), row),
                  pl.BlockSpec((tm, GDN_W), lambda i: (i, 3)),
                  pl.BlockSpec((tm, o_c.shape[1]), row)]
        + _token_sources(x_lat, x_ctx, ctx_tile0, n_lat_tiles, tm)
        + [pl.BlockSpec(wa.shape, const),
                  pl.BlockSpec(wb.shape, const),
                  pl.BlockSpec(wc.shape, const),
                  pl.BlockSpec((1, GDN_W), const),
                  pl.BlockSpec((1, d), const),
                  pl.BlockSpec((1, N_MOD, d), lambda i: (_group_of_tile(i, tps, n_batch), 0, 0))],
        out_specs=(pl.BlockSpec((tm, d), row), pl.BlockSpec((tm, d), row)),
        compiler_params=_cparams("arbitrary"),
        name="out_projection",
    )(o_a, o_fwd, o_bwd, pg, o_c, x_lat, x_ctx, wa, wb, wc, gdn_gain, gain2.reshape(1, d), mod_l)


def _route_kernel(h_ref, rw_ref, rb_ref, idx_ref, gate_ref, cnt_ref, base_ref):
    step = pl.program_id(0)
    tm = h_ref.shape[0]

    @pl.when(step == 0)
    def _():
        base_ref[...] = jnp.zeros_like(base_ref)

    logits = lax.dot_general(rw_ref[...], h_ref[...].astype(BF16), (((1,), (1,)), ((), ())),
                             preferred_element_type=F32) + rb_ref[...]
    e_iota = lax.broadcasted_iota(jnp.int32, logits.shape, 0)
    work = logits
    tops, picks = [], []
    for _k in range(TOP_K):
        mx = jnp.max(work, axis=0, keepdims=True)
        pick = jnp.min(jnp.where(work == mx, e_iota, N_EXPERTS), axis=0, keepdims=True)
        work = jnp.where(e_iota == pick, -jnp.inf, work)
        tops.append(mx)
        picks.append(pick)
    exps = [jnp.exp(t - tops[0]) for t in tops]
    denom = exps[0] + exps[1] + exps[2] + exps[3]
    sel = jnp.zeros(logits.shape, F32)
    for pick in picks:
        sel = sel + (e_iota == pick).astype(F32)
    row = lax.broadcasted_iota(jnp.int32, (tm, tm), 0)
    col = lax.broadcasted_iota(jnp.int32, (tm, tm), 1)
    before = (row < col).astype(BF16)
    cnt = jnp.dot(sel.astype(BF16), before, preferred_element_type=F32) + base_ref[:, 0:1]
    ranks = [jnp.sum(jnp.where(e_iota == pick, cnt, 0.0), axis=0, keepdims=True) for pick in picks]
    idx_ref[0] = jnp.concatenate(picks + [r.astype(jnp.int32) for r in ranks], axis=0)
    gate_rows = jnp.concatenate([e / denom for e in exps] + [jnp.zeros((LANE - TOP_K, tm), F32)], axis=0)
    gate_ref[...] = jnp.transpose(gate_rows)
    base_ref[...] = base_ref[...] + jnp.sum(sel, axis=1, keepdims=True)
    cnt_ref[...] = base_ref[...]


def _route(h_flat, n_tok, rw_t_bf, rb):
    d = h_flat.shape[1]
    tm = ROUTE_TILE
    nt = n_tok // tm
    return pl.pallas_call(
        _route_kernel,
        out_shape=(jax.ShapeDtypeStruct((nt, 2 * TOP_K, tm), jnp.int32),
                   jax.ShapeDtypeStruct((n_tok, LANE), F32),
                   jax.ShapeDtypeStruct((N_EXPERTS, LANE), F32)),
        grid=(nt,),
        in_specs=[pl.BlockSpec((tm, d), lambda i: (i, 0)),
                  pl.BlockSpec((N_EXPERTS, d), lambda i: (0, 0)),
                  pl.BlockSpec((N_EXPERTS, 1), lambda i: (0, 0))],
        out_specs=(pl.BlockSpec((1, 2 * TOP_K, tm), lambda i: (i, 0, 0)),
                   pl.BlockSpec((tm, LANE), lambda i: (i, 0)),
                   pl.BlockSpec((N_EXPERTS, LANE), lambda i: (0, 0))),
        scratch_shapes=[pltpu.VMEM((N_EXPERTS, LANE), F32)],
        compiler_params=_cparams("arbitrary"),
        name="moe_route",
    )(h_flat, rw_t_bf, rb.reshape(N_EXPERTS, 1))


def _dispatch_kernel(lo_ref, hi_ref, slot_ref, h_ref, xs_out, zero_ref, sem, zero_sem, tile_sem):
    tm = slot_ref.shape[2]

    @pl.when(pl.program_id(0) == pl.num_programs(0) - 1)
    def _():
        zero_ref[...] = jnp.zeros_like(zero_ref)

        def row_copy0(r):
            return pltpu.make_async_copy(zero_ref.at[pl.ds(0, 1)], xs_out.at[pl.ds(r, 1)], zero_sem)

        def tile_copy0(g):
            dst = xs_out.at[pl.ds(pl.multiple_of(g * SUBLANE, SUBLANE), SUBLANE)]
            return pltpu.make_async_copy(zero_ref, dst, tile_sem)

        def bounds(e):
            lo, hi = lo_ref[e], hi_ref[e]
            mid = jnp.minimum((lo + SUBLANE - 1) // SUBLANE * SUBLANE, hi)
            return lo, mid, hi

        def start_expert(e, carry):
            lo, mid, hi = bounds(e)
            lax.fori_loop(lo, mid, lambda r, c: (row_copy0(r).start(), c)[1], carry)
            return lax.fori_loop(mid // SUBLANE, hi // SUBLANE, lambda g, c: (tile_copy0(g).start(), c)[1], carry)

        def wait_expert(e, carry):
            lo, mid, hi = bounds(e)
            lax.fori_loop(lo, mid, lambda r, c: (row_copy0(r).wait(), c)[1], carry)
            return lax.fori_loop(mid // SUBLANE, hi // SUBLANE, lambda g, c: (tile_copy0(g).wait(), c)[1], carry)

        lax.fori_loop(0, lo_ref.shape[0], start_expert, 0)
        lax.fori_loop(0, lo_ref.shape[0], wait_expert, 0)

    def row_copy(t, k):
        return pltpu.make_async_copy(h_ref.at[pl.ds(t, 1)], xs_out.at[pl.ds(slot_ref[0, k, t], 1)], sem)

    def issue(i, carry):
        for u in range(ROW_UNROLL):
            for k in range(TOP_K):
                row_copy(i * ROW_UNROLL + u, k).start(priority=k % 2)
        return carry

    lax.fori_loop(0, tm // ROW_UNROLL, issue, 0)

    def drain(i, carry):
        for u in range(ROW_UNROLL):
            for k in range(TOP_K):
                row_copy(i * ROW_UNROLL + u, k).wait()
        return carry

    lax.fori_loop(0, tm // ROW_UNROLL, drain, 0)


def _dispatch(pad_lo, pad_hi, slots, h_flat, n_tok, cap):
    d = h_flat.shape[1]
    tm = slots.shape[2]
    return pl.pallas_call(
        _dispatch_kernel,
        out_shape=jax.ShapeDtypeStruct((cap, d), F32),
        grid_spec=pltpu.PrefetchScalarGridSpec(
            num_scalar_prefetch=2,
            grid=(n_tok // tm,),
            in_specs=[pl.BlockSpec((1, TOP_K, tm), lambda i, lo, hi: (i, 0, 0), memory_space=pltpu.SMEM),
                      pl.BlockSpec((tm, d), lambda i, lo, hi: (i, 0))],
            out_specs=pl.BlockSpec(memory_space=pl.ANY),
            scratch_shapes=[pltpu.VMEM((SUBLANE, d), F32)] + [pltpu.SemaphoreType.DMA(())] * 3),
        compiler_params=_cparams("arbitrary"),
        name="moe_dispatch",
    )(pad_lo, pad_hi, slots, h_flat)


SPLIT_COLS = 1024


def _split_kernel(w_ref, sel_ref, wg_ref, wl_ref):
    for j in range(w_ref.shape[2] // (2 * LANE)):
        blk = w_ref[0, :, 2 * LANE * j:2 * LANE * (j + 1)].astype(BF16)
        r = jnp.dot(blk, sel_ref[...], preferred_element_type=F32)
        wg_ref[0, :, LANE * j:LANE * (j + 1)] = r[:, :LANE].astype(BF16)
        wl_ref[0, :, LANE * j:LANE * (j + 1)] = r[:, LANE:].astype(BF16)


def _split_glu_columns(w_gu):
    nl, ne, d, f2 = w_gu.shape
    src = np.arange(2 * LANE)
    dst = np.where(src % 2 == 0, src // 2, LANE + src // 2)
    sel = jnp.asarray((dst[:, None] == np.arange(2 * LANE)[None, :]).astype(np.float32), BF16)
    tn = SPLIT_COLS
    out = jax.ShapeDtypeStruct((nl * ne, d, f2 // 2), BF16)
    wg, wl = pl.pallas_call(
        _split_kernel,
        out_shape=(out, out),
        grid=(nl * ne, f2 // tn),
        in_specs=[pl.BlockSpec((1, d, tn), lambda e, j: (e, 0, j)),
                  pl.BlockSpec((2 * LANE, 2 * LANE), lambda e, j: (0, 0))],
        out_specs=(pl.BlockSpec((1, d, tn // 2), lambda e, j: (e, 0, j)),
                   pl.BlockSpec((1, d, tn // 2), lambda e, j: (e, 0, j))),
        compiler_params=_cparams("arbitrary", "arbitrary"),
        name="split_glu_columns",
    )(w_gu.reshape(nl * ne, d, f2), sel)
    return wg, wl


def _expert_kernel(be_ref, nu_ref, xs_ref, wg_ref, wl_ref, bg_ref, bl_ref, wd_ref, bd_ref, ys_ref, wd_bf_ref):
    i = pl.program_id(0)
    last = nu_ref[0] - 1
    expert = be_ref[jnp.minimum(i, last)]
    prev_expert = be_ref[jnp.minimum(jnp.maximum(i - 1, 0), last)]

    @pl.when(jnp.logical_or(i == 0, expert != prev_expert))
    def _():
        wd_bf_ref[...] = wd_ref[0].astype(BF16)

    @pl.when(i < nu_ref[0])
    def _():
        x = xs_ref[...].astype(BF16)
        g = jnp.dot(x, wg_ref[0], preferred_element_type=F32) + bg_ref[0]
        u = jnp.dot(x, wl_ref[0], preferred_element_type=F32) + bl_ref[0]
        g = jnp.minimum(g, SWIGLU_LIMIT)
        u = jnp.clip(u, -SWIGLU_LIMIT, SWIGLU_LIMIT)
        act = g * jax.nn.sigmoid(SWIGLU_ALPHA * g) * (u + 1.0)
        ys_ref[...] = jnp.dot(act.astype(BF16), wd_bf_ref[...], preferred_element_type=F32) + bd_ref[0]

    @pl.when(pl.program_id(0) >= nu_ref[0])
    def _():
        ys_ref[...] = jnp.zeros_like(ys_ref)


def _experts(block_e, n_used, xs, wg, wl, bg, bl, wd, bd):
    cap, d = xs.shape
    f = wg.shape[2]
    tm = MOE_TILE
    nb = cap // tm

    def row_map(i, be, nu):
        return (jnp.minimum(i, nu[0] - 1), 0)

    def w_map(i, be, nu):
        return (be[jnp.minimum(i, nu[0] - 1)], 0, 0)

    return pl.pallas_call(
        _expert_kernel,
        out_shape=jax.ShapeDtypeStruct((cap, d), F32),
        grid_spec=pltpu.PrefetchScalarGridSpec(
            num_scalar_prefetch=2,
            grid=(nb,),
            in_specs=[pl.BlockSpec((tm, d), row_map),
                      pl.BlockSpec((1, d, f), w_map),
                      pl.BlockSpec((1, d, f), w_map),
                      pl.BlockSpec((1, 1, f), w_map),
                      pl.BlockSpec((1, 1, f), w_map),
                      pl.BlockSpec((1, f, d), w_map),
                      pl.BlockSpec((1, 1, d), w_map)],
            out_specs=pl.BlockSpec((tm, d), lambda i, be, nu: (i, 0)),
            scratch_shapes=[pltpu.VMEM((f, d), BF16)]),
        compiler_params=_cparams("arbitrary"),
        name="moe_experts",
    )(block_e, n_used, xs, wg, wl, bg, bl, wd, bd)


def _combine_kernel(slot_ref, ys_hbm, x_ref, gate_ref, m_ref, o_ref, buf, sem):
    tm = x_ref.shape[0]

    def row_copy(t, k):
        return pltpu.make_async_copy(ys_hbm.at[pl.ds(slot_ref[0, k, t], 1)], buf.at[k, pl.ds(t, 1)], sem)

    def issue(i, carry):
        for u in range(ROW_UNROLL):
            for k in range(TOP_K):
                row_copy(i * ROW_UNROLL + u, k).start(priority=k % 2)
        return carry

    lax.fori_loop(0, tm // ROW_UNROLL, issue, 0)

    def drain(i, carry):
        for u in range(ROW_UNROLL):
            for k in range(TOP_K):
                row_copy(i * ROW_UNROLL + u, k).wait()
        return carry

    lax.fori_loop(0, tm // ROW_UNROLL, drain, 0)
    gates = gate_ref[...]
    y = buf[0] * gates[:, 0:1]
    for k in range(1, TOP_K):
        y = y + buf[k] * gates[:, k:k + 1]
    o_ref[...] = x_ref[...] + m_ref[0][5:6] * y


def _combine(slots, ys, x_flat, gates, mod_l, n_tok, n_batch, seq):
    d = x_flat.shape[1]
    tm = slots.shape[2]
    tps = seq // tm
    return pl.pallas_call(
        _combine_kernel,
        out_shape=jax.ShapeDtypeStruct((n_tok, d), F32),
        grid=(n_tok // tm,),
        in_specs=[pl.BlockSpec((1, TOP_K, tm), lambda i: (i, 0, 0), memory_space=pltpu.SMEM),
                  pl.BlockSpec(memory_space=pl.ANY),
                  pl.BlockSpec((tm, d), lambda i: (i, 0)),
                  pl.BlockSpec((tm, LANE), lambda i: (i, 0)),
                  pl.BlockSpec((1, N_MOD, d), lambda i: (_group_of_tile(i, tps, n_batch), 0, 0))],
        out_specs=pl.BlockSpec((tm, d), lambda i: (i, 0)),
        scratch_shapes=[pltpu.VMEM((TOP_K, tm, d), F32), pltpu.SemaphoreType.DMA(())],
        compiler_params=_cparams("arbitrary"),
        name="moe_combine",
    )(slots, ys, x_flat, gates, mod_l)


def _retile_slots(slots, tm):
    nt, k, t = slots.shape
    return slots.reshape(nt, k, t // tm, tm).transpose(0, 2, 1, 3).reshape(nt * (t // tm), k, tm)


def _moe(h_flat, x_flat, n_tok, mod_l, rw_t_bf, rb, layer, wg, wl, bg, bl, wd, bd, n_batch, seq):
    idx, gates, counts = _route(h_flat, n_tok, rw_t_bf, rb)
    counts = counts[:, 0].astype(jnp.int32)
    padded = (counts + MOE_TILE - 1) // MOE_TILE * MOE_TILE
    pad_end = jnp.cumsum(padded)
    pad_start = pad_end - padded
    n_blocks = -(-(n_tok * TOP_K) // MOE_TILE) + N_EXPERTS
    cap = n_blocks * MOE_TILE
    block_row = jnp.arange(n_blocks, dtype=jnp.int32) * MOE_TILE
    block_e = jnp.minimum(jnp.sum(block_row[:, None] >= pad_end[None, :], axis=1), N_EXPERTS - 1).astype(jnp.int32)
    block_e = block_e + layer * N_EXPERTS
    n_used = (pad_end[-1:] // MOE_TILE).astype(jnp.int32)
    experts = idx[:, :TOP_K, :]
    start = jnp.sum(jnp.where(experts[..., None] == jnp.arange(N_EXPERTS), pad_start, 0), axis=-1)
    slots = start + idx[:, TOP_K:, :]
    slots = _retile_slots(slots, MOVE_TILE)
    zero_lo = jnp.concatenate([pad_start + counts, pad_end[-1:]]).astype(jnp.int32)
    zero_hi = jnp.concatenate([pad_end, jnp.full((1,), cap)]).astype(jnp.int32)
    xs = _dispatch(zero_lo, zero_hi, slots, h_flat, n_tok, cap)
    ys = _experts(block_e, n_used, xs, wg, wl, bg, bl, wd, bd)
    return _combine(slots, ys, x_flat, gates, mod_l, n_tok, n_batch, seq)


GDN_TILE = 256
HALO = SUBLANE


def _gdn_prep_kernel(q_ref, k_ref, v_ref, qp_ref, kp_ref, vp_ref, qn_ref, kn_ref, vn_ref, small_ref, w_ref,
                     a_ref, bias_ref, qkv_ref, gb_ref, xe_ref, *, lat_tiles, lat_tps, ctx_tps):
    i = pl.program_id(0)
    tm = q_ref.shape[0]
    in_lat = i < lat_tiles
    pos = jnp.where(in_lat, i % lat_tps, (i - lat_tiles) % ctx_tps)
    last = jnp.where(in_lat, lat_tps - 1, ctx_tps - 1)
    keep_prev = jnp.where(pos > 0, 1.0, 0.0)
    keep_next = jnp.where(pos < last, 1.0, 0.0)
    half = GDN_CONV // 2
    for g, (x_ref, p_ref, n_ref) in enumerate(((q_ref, qp_ref, qn_ref), (k_ref, kp_ref, kn_ref),
                                               (v_ref, vp_ref, vn_ref))):
        xe_ref[g, 0:HALO, :] = p_ref[...] * keep_prev
        xe_ref[g, HALO:HALO + tm, :] = x_ref[...]
        xe_ref[g, HALO + tm:2 * HALO + tm, :] = n_ref[...] * keep_next
        w = w_ref[:, g * GDN_W:(g + 1) * GDN_W]
        y = xe_ref[g, HALO - half:HALO - half + tm, :] * w[0:1]
        for j in range(1, GDN_CONV):
            y = y + xe_ref[g, HALO - half + j:HALO - half + j + tm, :] * w[j:j + 1]
        y = y * jax.nn.sigmoid(y)
        if g < 2:
            heads = []
            for h in range(GDN_HEADS):
                yh = y[:, h * GDN_HEAD_DIM:(h + 1) * GDN_HEAD_DIM]
                yh = yh * lax.rsqrt(jnp.sum(yh * yh, axis=-1, keepdims=True) + EPS)
                heads.append(yh * GDN_HEAD_DIM ** -0.5 if g == 0 else yh)
            y = jnp.concatenate(heads, axis=1)
        qkv_ref[:, g * GDN_W:(g + 1) * GDN_W] = y.astype(BF16)
    sm = small_ref[...]
    z = sm + bias_ref[...]
    softplus = jnp.maximum(z, 0.0) + jnp.log(1.0 + jnp.exp(-jnp.abs(z)))
    lane = lax.broadcasted_iota(jnp.int32, (1, LANE), 1)
    is_g = (lane >= SMALL_GA) & (lane < SMALL_GB)
    is_b = (lane >= SMALL_GB) & (lane < SMALL_GB + 2 * GDN_HEADS)
    gb_ref[...] = jnp.where(is_g, a_ref[...] * softplus, jnp.where(is_b, jax.nn.sigmoid(sm), 0.0))


def _gdn_prep(pg, small, conv_w, neg_a, dt_bias, n_batch, seq, ctx_len):
    n_tok = pg.shape[0]
    tm = GDN_TILE
    lat_tiles = n_batch * seq // tm
    per_tile = tm // HALO
    n_halo_blocks = n_tok // HALO

    def main(g):
        return pl.BlockSpec((tm, GDN_W), lambda i: (i, g))

    def prev(g):
        return pl.BlockSpec((HALO, GDN_W), lambda i: (jnp.maximum(i * per_tile - 1, 0), g))

    def nxt(g):
        return pl.BlockSpec((HALO, GDN_W), lambda i: (jnp.minimum((i + 1) * per_tile, n_halo_blocks - 1), g))

    const = lambda i: (0, 0)
    return pl.pallas_call(
        functools.partial(_gdn_prep_kernel, lat_tiles=lat_tiles, lat_tps=seq // tm, ctx_tps=ctx_len // tm),
        out_shape=(jax.ShapeDtypeStruct((n_tok, 3 * GDN_W), BF16), jax.ShapeDtypeStruct((n_tok, LANE), F32)),
        grid=(n_tok // tm,),
        in_specs=[main(0), main(1), main(2), prev(0), prev(1), prev(2), nxt(0), nxt(1), nxt(2),
                  pl.BlockSpec((tm, LANE), lambda i: (i, 0)),
                  pl.BlockSpec(conv_w.shape, const), pl.BlockSpec((1, LANE), const), pl.BlockSpec((1, LANE), const)],
        out_specs=(pl.BlockSpec((tm, 3 * GDN_W), lambda i: (i, 0)), pl.BlockSpec((tm, LANE), lambda i: (i, 0))),
        scratch_shapes=[pltpu.VMEM((3, tm + 2 * HALO, GDN_W), F32)],
        compiler_params=_cparams("arbitrary"),
        name="gdn_prep",
    )(*([pg] * 9), small, conv_w, neg_a, dt_bias)


GDN_SUB = 16
GDN_CHUNKS_PER_STEP = 4


def _bdot(a, b):
    return jnp.dot(a.astype(BF16), b.astype(BF16), preferred_element_type=F32)


def _dot_tn(a, b):
    return lax.dot_general(a, b, (((0,), (0,)), ((), ())), preferred_element_type=F32)


def _each(f, *lists):
    return [f(*args) for args in zip(*lists)]


def _unit_tri_solve(n, rhs, same_blk, eye):
    nd = _each(lambda a: jnp.where(same_blk, a, 0.0).astype(BF16), n)
    nl = _each(lambda a: jnp.where(same_blk, 0.0, a).astype(BF16), n)
    nd2 = _each(lambda a: jnp.dot(a, a, preferred_element_type=F32).astype(BF16), nd)
    nd4 = _each(lambda a: jnp.dot(a, a, preferred_element_type=F32).astype(BF16), nd2)
    nd8 = _each(lambda a: jnp.dot(a, a, preferred_element_type=F32).astype(BF16), nd4)
    m = _each(lambda a: eye - a.astype(F32), nd)
    m = _each(lambda a, b: a + _bdot(a, b), m, nd2)
    m = _each(lambda a, b: a + _bdot(a, b), m, nd4)
    dinv = _each(lambda a, b: (a + _bdot(a, b)).astype(BF16), m, nd8)
    p = _each(lambda a, b: jnp.dot(a, b, preferred_element_type=F32).astype(BF16), dinv, nl)
    p2 = _each(lambda a: jnp.dot(a, a, preferred_element_type=F32).astype(BF16), p)
    y = _each(lambda a, b: _bdot(a, b), dinv, rhs)
    y = _each(lambda a, b: a + _bdot(b, a), y, p2)
    return _each(lambda a, b: a - _bdot(b, a), y, p)


def _gdn_scan_kernel(x0_ref, gb0_ref, x1_ref, gb1_ref, o0_ref, o1_ref, state_ref, *, n_sub):
    assert GDN_CHUNK == 4 * GDN_SUB

    @pl.when(pl.program_id(1) == 0)
    def _():
        state_ref[...] = jnp.zeros_like(state_ref)

    n = GDN_CHUNK
    r = lax.broadcasted_iota(jnp.int32, (n, n), 0)
    c = lax.broadcasted_iota(jnp.int32, (n, n), 1)
    same_blk = (r // GDN_SUB) == (c // GDN_SUB)
    eye = (r == c).astype(F32)
    dirs = ((x0_ref, gb0_ref, o0_ref), (x1_ref, gb1_ref, o1_ref))
    heads = [(d, h) for d in range(2) for h in range(GDN_HEADS)]
    chunk_at = [list(range(n_sub)), list(range(n_sub - 1, -1, -1))]
    probs = [(d, h, chunk_at[d][s]) for s in range(n_sub) for d, h in heads]
    incl = [r >= c, r <= c]
    strict = [r > c, r < c]
    end_row = [n - 1, 0]

    gbv, gc_all, gc_t = {}, {}, {}
    for d in range(2):
        tri = incl[d].astype(BF16)
        for ch in range(n_sub):
            g = dirs[d][1][ch * n:(ch + 1) * n, :]
            hi = g.astype(BF16)
            rem = g - hi.astype(F32)
            mid = rem.astype(BF16)
            lo = (rem - mid.astype(F32)).astype(BF16)
            gc = (jnp.dot(tri, hi, preferred_element_type=F32) + jnp.dot(tri, mid, preferred_element_type=F32)
                  + jnp.dot(tri, lo, preferred_element_type=F32))
            gbv[d, ch], gc_all[d, ch], gc_t[d, ch] = g, gc, jnp.transpose(gc)

    def head_cols(d, part, h, ch):
        lo_col = part * GDN_W + h * GDN_HEAD_DIM
        return dirs[d][0][ch * n:(ch + 1) * n, lo_col:lo_col + GDN_HEAD_DIM]

    lane_g = [SMALL_GA + GDN_HEADS * d + h for d, h, ch in probs]
    lane_b = [SMALL_GB + GDN_HEADS * d + h for d, h, ch in probs]
    gcol = [gc_all[d, ch][:, lg:lg + 1] for (d, h, ch), lg in zip(probs, lane_g)]
    grow = [gc_t[d, ch][lg:lg + 1, :] for (d, h, ch), lg in zip(probs, lane_g)]
    gend = [gc_all[d, ch][end_row[d]:end_row[d] + 1, lg:lg + 1] for (d, h, ch), lg in zip(probs, lane_g)]
    beta = [gbv[d, ch][:, lb:lb + 1] for (d, h, ch), lb in zip(probs, lane_b)]
    q = [head_cols(d, 0, h, ch) for d, h, ch in probs]
    k = [head_cols(d, 1, h, ch) for d, h, ch in probs]
    v = [head_cols(d, 2, h, ch) for d, h, ch in probs]
    kf = _each(lambda a: a.astype(F32), k)
    dec = [jnp.where(incl[d], jnp.exp(jnp.where(incl[d], gc_ - gr_, 0.0)), 0.0)
           for (d, h, ch), gc_, gr_ in zip(probs, gcol, grow)]
    qk_kk = _each(lambda q_, k_: _dot_nt(jnp.concatenate([q_, k_], axis=0), k_), q, k)
    qkd = _each(lambda a, dec_: (a[:n] * dec_).astype(BF16), qk_kk, dec)
    nmat = [jnp.where(strict[d], b_ * a[n:] * dec_, 0.0) for (d, h, ch), b_, a, dec_ in zip(probs, beta, qk_kk, dec)]
    egc = _each(jnp.exp, gcol)
    rhs = _each(lambda v_, kf_, b_, e_: jnp.concatenate([v_.astype(F32) * b_, kf_ * (b_ * e_)], axis=1).astype(BF16),
                v, kf, beta, egc)
    uw = _unit_tri_solve(nmat, rhs, same_blk, eye)
    wq = _each(lambda uw_, q_, e_: jnp.concatenate([uw_[:, GDN_HEAD_DIM:], q_.astype(F32) * e_], axis=0).astype(BF16),
               uw, q, egc)
    k_dec = _each(lambda kf_, ge_, gc_: (kf_ * jnp.exp(ge_ - gc_)).astype(BF16), kf, gend, gcol)
    g_end = _each(jnp.exp, gend)

    state = [state_ref[i] for i in range(len(heads))]
    for s in range(n_sub):
        sl = slice(s * len(heads), (s + 1) * len(heads))
        s_bf = _each(lambda a: a.astype(BF16), state)
        ws_qs = _each(lambda a, s_: jnp.dot(a, s_, preferred_element_type=F32), wq[sl], s_bf)
        v_new = _each(lambda uw_, a: (uw_[:, :GDN_HEAD_DIM] - a[:n]).astype(BF16), uw[sl], ws_qs)
        o_local = _each(lambda a, vn_: jnp.dot(a, vn_, preferred_element_type=F32), qkd[sl], v_new)
        s_add = _each(_dot_tn, k_dec[sl], v_new)
        state = _each(lambda ge_, st_, add_: ge_ * st_ + add_, g_end[sl], state, s_add)
        for (d, h, ch), a, ol in zip(probs[sl], ws_qs, o_local):
            dirs[d][2][ch * n:(ch + 1) * n, h * GDN_HEAD_DIM:(h + 1) * GDN_HEAD_DIM] = a[n:] + ol
    for i in range(len(heads)):
        state_ref[i] = state[i]


def _gdn_scan(qkv, gbeta, n_batch, seq, ctx_len):
    n_tok = qkv.shape[0]
    n = GDN_CHUNK * GDN_CHUNKS_PER_STEP
    lat_c, ctx_c = seq // n, ctx_len // n
    ctx0 = n_batch * seq // n

    def fwd(b, j):
        return (jnp.where(j < ctx_c, ctx0 + b * ctx_c + j, b * lat_c + (j - ctx_c)), 0)

    def bwd(b, j):
        return (jnp.where(j < ctx_c, ctx0 + b * ctx_c + (ctx_c - 1 - j), b * lat_c + (lat_c - 1 - (j - ctx_c))), 0)

    return pl.pallas_call(
        functools.partial(_gdn_scan_kernel, n_sub=GDN_CHUNKS_PER_STEP),
        out_shape=(jax.ShapeDtypeStruct((n_tok, GDN_W), F32), jax.ShapeDtypeStruct((n_tok, GDN_W), F32)),
        grid=(n_batch, ctx_c + lat_c),
        in_specs=[pl.BlockSpec((n, 3 * GDN_W), fwd), pl.BlockSpec((n, LANE), fwd),
                  pl.BlockSpec((n, 3 * GDN_W), bwd), pl.BlockSpec((n, LANE), bwd)],
        out_specs=(pl.BlockSpec((n, GDN_W), fwd), pl.BlockSpec((n, GDN_W), bwd)),
        scratch_shapes=[pltpu.VMEM((2 * GDN_HEADS, GDN_HEAD_DIM, GDN_HEAD_DIM), F32)],
        compiler_params=_cparams("arbitrary", "arbitrary"),
        name="gdn_scan",
    )(qkv, gbeta, qkv, gbeta)


def _pad_heads(w, n_heads, width, start, take):
    lead = w.shape[:-1]
    w = w.reshape(lead + (n_heads, width))[..., start:start + take]
    w = jnp.pad(w, [(0, 0)] * len(lead) + [(0, 0), (0, MLA_HEAD_PAD - take)])
    return w.reshape(lead + (n_heads * MLA_HEAD_PAD,))


def kernel(x, c, ctx, c_ctx, w_mod, b_mod, norm1, norm2, w_in, w_out, swa_q_norm, swa_k_norm, swa_sink, gdn_conv,
           gdn_a_log, gdn_dt_bias, gdn_out_norm, mla_q_a_norm, mla_w_uq, mla_kv_a_norm, mla_w_ukv, mla_q_norm,
           mla_k_norm, router_w, router_b, exp_w_gu, exp_b_gu, exp_w_dn, exp_b_dn):
    b, s, d = x.shape
    cl = ctx.shape[1]
    depth = w_mod.shape[0]
    n_lat, n_ctx = b * s, b * cl
    assert s % TOK_TILE == 0 and n_ctx % TOK_TILE == 0 and s % MOVE_TILE == 0 and n_ctx % ROUTE_TILE == 0
    assert cl % MLA_Q_TILE == 0 and cl % ATTN_BLOCK == 0 and n_lat % cl == 0

    n_rows = -(-(b + 1) // SUBLANE) * SUBLANE
    c_rows = jnp.concatenate([c, c_ctx[None, :], jnp.zeros((n_rows - b - 1, d), F32)], axis=0)
    mod_all = _modulation(c_rows, w_mod, b_mod)

    perm = _in_proj_perm()
    tabs = _rope_tables(s)
    head_rows = np.concatenate([h * SWA_HEAD_DIM + np.arange(SWA_HEAD_DIM) for h in SWA_HEAD_ORDER])

    wg_all, wl_all = _split_glu_columns(exp_w_gu)
    n_le = depth * N_EXPERTS
    bg_all = exp_b_gu[..., 0::2].reshape(n_le, 1, -1)
    bl_all = exp_b_gu[..., 1::2].reshape(n_le, 1, -1)
    wd_all = exp_w_dn.reshape((n_le,) + exp_w_dn.shape[2:])
    bd_all = exp_b_dn.reshape(n_le, 1, d)
    x_lat, x_ctx, ctx_tile0 = x.reshape(n_lat, d), ctx.reshape(n_ctx, d), 0
    for l in range(depth):
        with_ctx = l < depth - 1
        n_tok = n_lat + n_ctx if with_ctx else n_lat
        mod_l = mod_all[l]
        w_in_bf = jnp.pad(w_in[l], ((0, 0), (0, 1)))[:, perm].astype(BF16)
        consts = (jnp.tile(swa_q_norm[l], SWA_HEADS)[None], jnp.tile(swa_k_norm[l], SWA_KV_HEADS)[None],
                  mla_q_a_norm[l][None], mla_kv_a_norm[l][None],
                  _pad_heads(mla_w_uq[l], MLA_HEADS, MLA_QK, 0, MLA_QK).astype(BF16),
                  _pad_heads(mla_w_ukv[l], MLA_HEADS, MLA_NOPE + MLA_V, 0, MLA_NOPE).astype(BF16),
                  _pad_heads(mla_w_ukv[l], MLA_HEADS, MLA_NOPE + MLA_V, MLA_NOPE, MLA_V).astype(BF16),
                  _pad_heads(jnp.tile(mla_q_norm[l], MLA_HEADS), MLA_HEADS, MLA_QK, 0, MLA_QK)[None],
                  _pad_heads(jnp.tile(mla_k_norm[l], MLA_HEADS), MLA_HEADS, MLA_QK, 0, MLA_QK)[None])
        pg, small, qa, kva, qm, km, vm = _in_projection(x_lat, x_ctx, ctx_tile0, n_lat + n_ctx, norm1[l], mod_l,
                                                        w_in_bf, tabs, consts, b, s)
        o_a = _swa_attention(swa_sink[l], qa, kva, b, s, cl, with_ctx)
        o_c = _mla_attention(qm, km, vm, b, s, cl, with_ctx)
        lanes_g = slice(SMALL_GA, SMALL_GA + 2 * GDN_HEADS)
        neg_a = jnp.zeros((1, LANE), F32).at[0, lanes_g].set(-jnp.exp(gdn_a_log[l]).reshape(-1))
        dt_bias = jnp.zeros((1, LANE), F32).at[0, lanes_g].set(gdn_dt_bias[l].reshape(-1))
        qkv, gbeta = _gdn_prep(pg, small, gdn_conv[l], neg_a, dt_bias, b, s, cl)
        o_fwd, o_bwd = _gdn_scan(qkv, gbeta, b, s, cl)

        wa = w_out[l][:SWA_Q][head_rows].astype(BF16)
        wb = w_out[l][SWA_Q:SWA_Q + GDN_W].astype(BF16)
        wc = w_out[l][SWA_Q + GDN_W:].reshape(MLA_HEADS, MLA_V, d)
        wc = jnp.pad(wc, ((0, 0), (0, MLA_HEAD_PAD - MLA_V), (0, 0))).reshape(MLA_HEADS * MLA_HEAD_PAD, d).astype(BF16)
        x_mid, h2 = _out_projection(o_a, o_fwd, o_bwd, pg, o_c, x_lat, x_ctx, ctx_tile0, wa, wb, wc,
                                    jnp.tile(gdn_out_norm[l], GDN_HEADS)[None], norm2[l], mod_l, n_tok, b, s)

        x_next = _moe(h2, x_mid, n_tok, mod_l, router_w[l].T.astype(BF16), router_b[l], l, wg_all, wl_all,
                      bg_all, bl_all, wd_all, bd_all, b, s)
        x_lat, x_ctx, ctx_tile0 = x_next, x_next, n_lat // TOK_TILE
    return x_lat[:n_lat].reshape(b, s, d)
```
